```python
import jax, jax.numpy as jnp
from jax import lax
import numpy as np

D_MODEL = 1024
BATCH = 1
SEQ = 16384
DEPTH = 1

CHUNK = 64
MIX_WIDTH = D_MODEL
RWKV_WIDTH = 512
RWKV_HEAD = 64
RWKV_HEADS = RWKV_WIDTH // RWKV_HEAD
DECAY_LORA = 64
AAA_LORA = 64
GATE_LORA = 128
SGU_WIDTH = MIX_WIDTH - RWKV_WIDTH
SGU_GROUPS = 8
SGU_BLOCK = 2 * CHUNK
RWKV_IN = 3 * RWKV_WIDTH + DECAY_LORA + AAA_LORA + GATE_LORA
IN_WIDTH = RWKV_IN + 2 * SGU_WIDTH
MEM_LEN = 256
XA_HEADS = 4
XA_HEAD_DIM = D_MODEL // XA_HEADS
D_FF = ((8 * D_MODEL // 3 + 255) // 256) * 256
RMS_EPS = 1e-6
LN_EPS = 1e-5
LNX_EPS = 64e-5

kernel_name = "hybrid_rwkv7_gmlp_memxattn_block"


def rmsnorm(x, g):
    xf = x.astype(jnp.float32)
    y = xf * lax.rsqrt(jnp.mean(xf * xf, axis=-1, keepdims=True) + RMS_EPS)
    return (y * g.astype(jnp.float32)).astype(x.dtype)


def token_shift(h):
    return jnp.pad(h, ((0, 0), (1, 0), (0, 0)))[:, :-1]


def wkv7_scan(r, w, k, v, kk, a):
    B, T, H, N = r.shape

    def step(S, inp):
        r_t, w_t, k_t, v_t, kk_t, a_t = inp
        sa = jnp.einsum('bhvk,bhk->bhv', S, -kk_t)
        S = (S * w_t[:, :, None, :]
             + sa[..., None] * (kk_t * a_t)[:, :, None, :]
             + v_t[..., None] * k_t[:, :, None, :])
        y = jnp.einsum('bhvk,bhk->bhv', S, r_t)
        return S, y

    xs = tuple(jnp.moveaxis(t, 1, 0) for t in (r, w, k, v, kk, a))
    S0 = jnp.zeros((B, H, N, N), jnp.float32)
    _, ys = lax.scan(step, S0, xs)
    return jnp.moveaxis(ys, 0, 1)


def rwkv7_group(h, shift_mu, w0, w_lora_up, a0, a_lora_up, g_lora_up, k_k, k_a, r_k, lnx_w, lnx_b):
    B, T, _ = h.shape
    f32 = jnp.float32
    h = h + (token_shift(h) - h) * shift_mu
    c = np.cumsum([RWKV_WIDTH, RWKV_WIDTH, RWKV_WIDTH, DECAY_LORA, AAA_LORA])
    r, k, v, wd, ad, gd = jnp.split(h, [int(i) for i in c], axis=-1)
    w_log = -jax.nn.softplus(-(w0 + jnp.tanh(wd) @ w_lora_up).astype(f32)) - 0.5
    decay = jnp.exp(-jnp.exp(w_log))
    a = jax.nn.sigmoid((a0 + ad @ a_lora_up).astype(f32))
    g = (jax.nn.sigmoid(gd) @ g_lora_up).astype(f32)
    heads = lambda t: t.reshape(B, T, RWKV_HEADS, RWKV_HEAD)
    kk = heads((k * k_k).astype(f32))
    kk = kk / jnp.maximum(jnp.sqrt(jnp.sum(kk * kk, axis=-1, keepdims=True)), 1e-12)
    k = k.astype(f32) * (1.0 + (a - 1.0) * k_a.astype(f32))
    rh, kh, vh = heads(r.astype(f32)), heads(k), heads(v.astype(f32))
    y = wkv7_scan(rh, heads(decay), kh, vh, kk, heads(a))
    mu = jnp.mean(y, axis=-1, keepdims=True)
    var = jnp.mean(jnp.square(y - mu), axis=-1, keepdims=True)
    y = ((y - mu) * lax.rsqrt(var + LNX_EPS)).reshape(B, T, RWKV_WIDTH)
    y = y * lnx_w.astype(f32) + lnx_b.astype(f32)
    bonus = (jnp.sum(rh * kh * r_k.astype(f32), axis=-1, keepdims=True) * vh).reshape(B, T, RWKV_WIDTH)
    return ((y + bonus) * g).astype(h.dtype)


def sgu_group(h, ln_w, ln_b, w_spatial, b_spatial):
    B, T, _ = h.shape
    u, v = jnp.split(jax.nn.gelu(h), 2, axis=-1)
    vf = v.astype(jnp.float32)
    mu = jnp.mean(vf, axis=-1, keepdims=True)
    var = jnp.mean(jnp.square(vf - mu), axis=-1, keepdims=True)
    v = ((vf - mu) * lax.rsqrt(var + LN_EPS) * ln_w + ln_b).astype(h.dtype)
    vb = v.reshape(B, T // SGU_BLOCK, SGU_BLOCK, SGU_GROUPS, SGU_WIDTH // SGU_GROUPS)
    mask = jnp.tril(jnp.ones((SGU_BLOCK, SGU_BLOCK), dtype=bool))
    ws = jnp.where(mask[None], w_spatial, jnp.zeros_like(w_spatial))
    mixed = jnp.einsum('gij,bnjgc->bnigc', ws, vb) + b_spatial.T[None, None, :, :, None]
    return u * mixed.reshape(B, T, SGU_WIDTH)


def mem_cross_attn(h, mem, mem_g, wq, wk, wv, wo):
    B, T, _ = h.shape
    m = rmsnorm(mem, mem_g)
    q = (h @ wq).reshape(B, T, XA_HEADS, XA_HEAD_DIM)
    k = (m @ wk).reshape(B, MEM_LEN, XA_HEADS, XA_HEAD_DIM)
    v = (m @ wv).reshape(B, MEM_LEN, XA_HEADS, XA_HEAD_DIM)
    s = jnp.einsum('bshd,bmhd->bhsm', q, k).astype(jnp.float32) * (XA_HEAD_DIM ** -0.5)
    p = jax.nn.softmax(s, axis=-1).astype(v.dtype)
    o = jnp.einsum('bhsm,bmhd->bshd', p, v).reshape(B, T, D_MODEL)
    return o @ wo


def swiglu(h, w_gate, w_up, w_down):
    return (jax.nn.silu(h @ w_gate) * (h @ w_up)) @ w_down


def setup_inputs(seed: int = 0) -> dict:
    key = jax.random.key(seed)
    ks = jax.random.split(key, 32)
    L = DEPTH
    nrm = lambda k, shape, s: jax.random.normal(k, shape, jnp.float32) * s
    gain = lambda k, shape: 1.0 + 0.02 * jax.random.normal(k, shape, jnp.float32)
    return {
        "x": nrm(ks[0], (BATCH, SEQ, D_MODEL), 1.0),
        "mem": nrm(ks[1], (BATCH, MEM_LEN, D_MODEL), 1.0),
        "norm1_g": gain(ks[2], (L, D_MODEL)),
        "w_in": nrm(ks[3], (L, D_MODEL, IN_WIDTH), D_MODEL ** -0.5),
        "shift_mu": jax.random.uniform(ks[4], (L, RWKV_IN), jnp.float32),
        "w0": jax.random.uniform(ks[5], (L, RWKV_WIDTH), jnp.float32, -5.0, 1.0),
        "w_lora_up": nrm(ks[6], (L, DECAY_LORA, RWKV_WIDTH), 0.1 * DECAY_LORA ** -0.5),
        "a0": nrm(ks[7], (L, RWKV_WIDTH), 0.1),
        "a_lora_up": nrm(ks[8], (L, AAA_LORA, RWKV_WIDTH), 0.1 * AAA_LORA ** -0.5),
        "g_lora_up": nrm(ks[9], (L, GATE_LORA, RWKV_WIDTH), GATE_LORA ** -0.5),
        "k_k": 0.85 + 0.05 * jax.random.normal(ks[10], (L, RWKV_WIDTH), jnp.float32),
        "k_a": 1.0 + 0.05 * jax.random.normal(ks[11], (L, RWKV_WIDTH), jnp.float32),
        "r_k": nrm(ks[12], (L, RWKV_HEADS, RWKV_HEAD), 0.1),
        "lnx_w": gain(ks[13], (L, RWKV_WIDTH)),
        "lnx_b": nrm(ks[14], (L, RWKV_WIDTH), 0.02),
        "sgu_ln_w": gain(ks[15], (L, SGU_WIDTH)),
        "sgu_ln_b": nrm(ks[16], (L, SGU_WIDTH), 0.02),
        "w_spatial": nrm(ks[17], (L, SGU_GROUPS, SGU_BLOCK, SGU_BLOCK), SGU_BLOCK ** -0.5),
        "b_spatial": gain(ks[18], (L, SGU_GROUPS, SGU_BLOCK)),
        "w_out": nrm(ks[19], (L, MIX_WIDTH, D_MODEL), MIX_WIDTH ** -0.5),
        "norm2_g": gain(ks[20], (L, D_MODEL)),
        "mem_norm_g": gain(ks[21], (L, D_MODEL)),
        "wq_x": nrm(ks[22], (L, D_MODEL, D_MODEL), D_MODEL ** -0.5),
        "wk_x": nrm(ks[23], (L, D_MODEL, D_MODEL), D_MODEL ** -0.5),
        "wv_x": nrm(ks[24], (L, D_MODEL, D_MODEL), D_MODEL ** -0.5),
        "wo_x": nrm(ks[25], (L, D_MODEL, D_MODEL), D_MODEL ** -0.5),
        "norm3_g": gain(ks[26], (L, D_MODEL)),
        "w_gate": nrm(ks[27], (L, D_MODEL, D_FF), D_MODEL ** -0.5),
        "w_up": nrm(ks[28], (L, D_MODEL, D_FF), D_MODEL ** -0.5),
        "w_down": nrm(ks[29], (L, D_FF, D_MODEL), D_FF ** -0.5),
        "norm_f_g": gain(ks[30], (D_MODEL,)),
    }


def reference(x, mem, norm1_g, w_in, shift_mu, w0, w_lora_up, a0, a_lora_up, g_lora_up,
              k_k, k_a, r_k, lnx_w, lnx_b, sgu_ln_w, sgu_ln_b, w_spatial, b_spatial, w_out,
              norm2_g, mem_norm_g, wq_x, wk_x, wv_x, wo_x, norm3_g, w_gate, w_up, w_down,
              norm_f_g):
    for l in range(DEPTH):
        z = rmsnorm(x, norm1_g[l]) @ w_in[l]
        y_rwkv = rwkv7_group(z[..., :RWKV_IN], shift_mu[l], w0[l], w_lora_up[l], a0[l],
                             a_lora_up[l], g_lora_up[l], k_k[l], k_a[l], r_k[l],
                             lnx_w[l], lnx_b[l])
        y_sgu = sgu_group(z[..., RWKV_IN:], sgu_ln_w[l], sgu_ln_b[l], w_spatial[l], b_spatial[l])
        x = x + jnp.concatenate([y_rwkv, y_sgu], axis=-1) @ w_out[l]
        x = x + mem_cross_attn(rmsnorm(x, norm2_g[l]), mem, mem_norm_g[l],
                               wq_x[l], wk_x[l], wv_x[l], wo_x[l])
        x = x + swiglu(rmsnorm(x, norm3_g[l]), w_gate[l], w_up[l], w_down[l])
    return rmsnorm(x, norm_f_g)
```

```python
import functools

import jax
import jax.numpy as jnp
from jax import lax
from jax.experimental import pallas as pl
from jax.experimental.pallas import tpu as pltpu

F32 = jnp.float32
BF16 = jnp.bfloat16

D_MODEL = 1024
RWKV_WIDTH = 512
RWKV_HEAD = 64
LORA_WA = 128
GATE_LORA = 128
RWKV_IN = 3 * RWKV_WIDTH + LORA_WA + GATE_LORA
SGU_WIDTH = 512
SGU_GROUPS = 8
SGU_BLOCK = 128
IN_WIDTH = RWKV_IN + 2 * SGU_WIDTH
MEM_LEN = 256
XA_HEADS = 4
XA_HEAD_DIM = D_MODEL // XA_HEADS
D_FF = 2816
RMS_EPS = 1e-6
LN_EPS = 1e-5
LNX_EPS = 64e-5

CHUNK = 64
PAIR = 2 * RWKV_HEAD
N_PAIRS = RWKV_WIDTH // PAIR
VMEM_LIMIT = 56 * 1024 * 1024

_NN = (((1,), (0,)), ((), ()))
_NT = (((1,), (1,)), ((), ()))
_TN = (((0,), (0,)), ((), ()))


def _mm(a, b, dims=_NN):
    return lax.dot_general(a.astype(BF16), b.astype(BF16), dims, preferred_element_type=F32)


def _split_mm_rhs(l01, x, terms):
    acc = None
    rem = x
    for _ in range(terms):
        part = rem.astype(BF16)
        rem = rem - part.astype(F32)
        d = lax.dot_general(l01, part, _NN, preferred_element_type=F32)
        acc = d if acc is None else acc + d
    return acc


def _split_mm_lhs(x, r01, terms):
    acc = None
    rem = x
    for _ in range(terms):
        part = rem.astype(BF16)
        rem = rem - part.astype(F32)
        d = lax.dot_general(part, r01, _NN, preferred_element_type=F32)
        acc = d if acc is None else acc + d
    return acc


def _rmsnorm(x, g):
    return x * lax.rsqrt(jnp.mean(x * x, axis=-1, keepdims=True) + RMS_EPS) * g


def _full(shape):
    n = len(shape)
    return pl.BlockSpec(shape, lambda i: (0,) * n)


def _params(sem="arbitrary"):
    return pltpu.CompilerParams(dimension_semantics=(sem,), vmem_limit_bytes=VMEM_LIMIT)


def _mem_kv_kernel(mem_ref, g_ref, wk_ref, wv_ref, k_ref, v_ref):
    m = _rmsnorm(mem_ref[...], g_ref[...]).astype(BF16)
    k_ref[...] = jnp.dot(m, wk_ref[...], preferred_element_type=F32).astype(BF16)
    v_ref[...] = jnp.dot(m, wv_ref[...], preferred_element_type=F32).astype(BF16)


def _mem_kv(mem, g, wk, wv):
    return pl.pallas_call(
        _mem_kv_kernel,
        out_shape=(jax.ShapeDtypeStruct((MEM_LEN, D_MODEL), BF16),) * 2,
        grid=(1,),
        in_specs=[_full((MEM_LEN, D_MODEL)), _full((1, D_MODEL)),
                  _full((D_MODEL, D_MODEL)), _full((D_MODEL, D_MODEL))],
        out_specs=(_full((MEM_LEN, D_MODEL)),) * 2,
        compiler_params=_params(),
        name="mem_kv",
    )(mem, g, wk, wv)


def _in_proj_kernel(x_ref, g_ref, w_ref, zr_ref, zs_ref):
    h = _rmsnorm(x_ref[...], g_ref[...]).astype(BF16)
    z = jnp.dot(h, w_ref[...], preferred_element_type=F32)
    zr_ref[...] = z[:, :RWKV_IN]
    zs_ref[...] = z[:, RWKV_IN:]


def _in_proj(x, g, w, tm):
    t = x.shape[0]
    return pl.pallas_call(
        _in_proj_kernel,
        out_shape=(jax.ShapeDtypeStruct((t, RWKV_IN), F32),
                   jax.ShapeDtypeStruct((t, 2 * SGU_WIDTH), F32)),
        grid=(t // tm,),
        in_specs=[pl.BlockSpec((tm, D_MODEL), lambda i: (i, 0)), _full((1, D_MODEL)),
                  _full((D_MODEL, IN_WIDTH))],
        out_specs=(pl.BlockSpec((tm, RWKV_IN), lambda i: (i, 0)),
                   pl.BlockSpec((tm, 2 * SGU_WIDTH), lambda i: (i, 0))),
        compiler_params=_params("parallel"),
        name="in_proj",
    )(x, g, w)


def _pair_masks():
    t = lax.broadcasted_iota(jnp.int32, (CHUNK, PAIR), 0)
    j = lax.broadcasted_iota(jnp.int32, (CHUNK, PAIR), 1) & (CHUNK - 1)
    strict = j < t
    incl = j <= t
    blk16 = (t >> 4) == (j >> 4)
    blk32 = (t >> 5) == (j >> 5)
    return strict, incl, blk16, blk32


def _bd(x, bd_mask):
    return jnp.where(bd_mask, jnp.concatenate([x, x], axis=0), 0.0)


def _unit_lower_inverse_minus_identity(a, masks, bd_mask):
    _, _, blk16, blk32 = masks
    pm = lambda x, y: _mm(x, _bd(y, bd_mask))
    ad = jnp.where(blk16, a, 0.0)
    tp = ad
    ap = ad
    for _ in range(3):
        ap = pm(ap, ap)
        tp = tp + ap + pm(ap, tp)
    for inner, outer in ((blk16, blk32), (blk32, None)):
        off = jnp.where(inner, 0.0, a) if outer is None else jnp.where(outer & ~inner, a, 0.0)
        x = off + pm(tp, off)
        tp = tp + x + pm(x, tp)
    return tp


def _rwkv_kernel(z_ref, mu_ref, w0_ref, waup_ref, a0_ref, gup_ref, kk_ref, ka_ref, rk_ref,
                 lnw_ref, lnb_ref, o_ref, carry_ref, s_ref, y_ref, *, tb):
    @pl.when(pl.program_id(0) == 0)
    def _():
        carry_ref[...] = jnp.zeros_like(carry_ref)
        s_ref[...] = jnp.zeros_like(s_ref)

    z = z_ref[...]
    row = lax.broadcasted_iota(jnp.int32, (tb, 1), 0)
    zprev = jnp.where(row == 0, carry_ref[...], pltpu.roll(z, 1, axis=0))
    carry_ref[...] = z[tb - 1:tb, :]
    h = z + (zprev - z) * mu_ref[...]

    r = h[:, 0:RWKV_WIDTH]
    k = h[:, RWKV_WIDTH:2 * RWKV_WIDTH]
    v = h[:, 2 * RWKV_WIDTH:3 * RWKV_WIDTH]
    wa_in = h[:, 3 * RWKV_WIDTH:3 * RWKV_WIDTH + LORA_WA]
    gd = h[:, 3 * RWKV_WIDTH + LORA_WA:RWKV_IN]

    lane = lax.broadcasted_iota(jnp.int32, (1, LORA_WA), 1)
    wa_in = jnp.where(lane < LORA_WA // 2, jnp.tanh(wa_in), wa_in)
    wa = jnp.dot(wa_in.astype(BF16), waup_ref[...], preferred_element_type=F32)
    w_pre = w0_ref[...] + wa[:, :RWKV_WIDTH]
    a = jax.nn.sigmoid(a0_ref[...] + wa[:, RWKV_WIDTH:])
    logw = -jnp.exp(-jax.nn.softplus(-w_pre) - 0.5)
    g = jnp.dot(jax.nn.sigmoid(gd).astype(BF16), gup_ref[...], preferred_element_type=F32)

    li = lax.broadcasted_iota(jnp.int32, (256, 256), 0) >> 6
    lj = lax.broadcasted_iota(jnp.int32, (256, 256), 1) >> 6
    seg01 = (li == lj).astype(BF16)

    def head_sum(x):
        return jnp.concatenate(
            [_split_mm_lhs(x[:, 256 * q:256 * (q + 1)], seg01, 2) for q in range(RWKV_WIDTH // 256)],
            axis=1)

    kk = k * kk_ref[...]
    kk = kk / jnp.maximum(jnp.sqrt(head_sum(kk * kk)), 1e-12)
    kmod = k * (1.0 + (a - 1.0) * ka_ref[...])
    kka = kk * a

    masks = _pair_masks()
    strict, incl = masks[0], masks[1]
    bi = lax.broadcasted_iota(jnp.int32, (PAIR, PAIR), 0) >> 6
    bj = lax.broadcasted_iota(jnp.int32, (PAIR, PAIR), 1) >> 6
    bd1 = bi == bj
    bd2 = jnp.concatenate([bd1, bd1], axis=1)
    ti = lax.broadcasted_iota(jnp.int32, (CHUNK, CHUNK), 0)
    tj = lax.broadcasted_iota(jnp.int32, (CHUNK, CHUNK), 1)
    ltri01 = (tj <= ti).astype(BF16)

    for c in range(tb // CHUNK):
        rows = slice(c * CHUNK, (c + 1) * CHUNK)
        lw = logw[rows]
        cs = _split_mm_rhs(ltri01, lw, 3)
        cs_last = cs[CHUNK - 1:CHUNK, :]
        w_in = jnp.exp(cs)
        w_inv = jnp.exp(-cs)
        rt = r[rows] * w_in
        at = -kk[rows] * jnp.exp(cs - lw)
        bh = kka[rows] * w_inv
        kh = kmod[rows] * w_inv
        w_tail = jnp.exp(cs_last - cs)
        bc = kka[rows] * w_tail
        kc = kmod[rows] * w_tail
        w_last = jnp.exp(cs_last)
        vc = v[rows]
        for p in range(N_PAIRS):
            lanes = slice(p * PAIR, (p + 1) * PAIR)
            at_p, rt_p, v_p = at[:, lanes], rt[:, lanes], vc[:, lanes]
            lhs = jnp.concatenate([at_p, rt_p], axis=0)
            g_b = _mm(lhs, _bd(bh[:, lanes], bd1), _NT)
            g_k = _mm(lhs, _bd(kh[:, lanes], bd1), _NT)
            a_ab = jnp.where(strict, g_b[:CHUNK], 0.0)
            a_ak = jnp.where(strict, g_k[:CHUNK], 0.0)
            b_rb = jnp.where(incl, g_b[CHUNK:], 0.0)
            b_rk = jnp.where(incl, g_k[CHUNK:], 0.0)
            v_bd = _bd(v_p, bd1)
            rhs = jnp.concatenate([_mm(a_ak, v_bd), at_p], axis=1)
            tp = _unit_lower_inverse_minus_identity(a_ab, masks, bd1)
            x = rhs + _mm(tp, _bd(rhs, bd2))
            u_v, a_chk = x[:, :PAIR], x[:, PAIR:]
            s = s_ref[p]
            u = _mm(a_chk, s, _NT) + u_v
            y = _mm(rt_p, s, _NT) + _mm(b_rb, _bd(u, bd1)) + _mm(b_rk, v_bd)
            upd = _mm(u, bc[:, lanes], _TN) + _mm(v_p, kc[:, lanes], _TN)
            s_ref[p] = s * w_last[:, lanes] + jnp.where(bd1, upd, 0.0)
            y_ref[rows, lanes] = y

    y = y_ref[...]
    mean = head_sum(y) * (1.0 / RWKV_HEAD)
    d = y - mean
    var = head_sum(d * d) * (1.0 / RWKV_HEAD)
    yn = d * lax.rsqrt(var + LNX_EPS) * lnw_ref[...] + lnb_ref[...]
    bonus = head_sum(r * kmod * rk_ref[...]) * v
    o_ref[...] = (yn + bonus) * g


def _rwkv(z, mu, w0, waup, a0, gup, k_k, k_a, r_k, lnw, lnb, tb):
    t = z.shape[0]
    vec = _full((1, RWKV_WIDTH))
    return pl.pallas_call(
        functools.partial(_rwkv_kernel, tb=tb),
        out_shape=jax.ShapeDtypeStruct((t, RWKV_WIDTH), F32),
        grid=(t // tb,),
        in_specs=[pl.BlockSpec((tb, RWKV_IN), lambda i: (i, 0)), _full((1, RWKV_IN)), vec,
                  _full((LORA_WA, 2 * RWKV_WIDTH)), vec, _full((GATE_LORA, RWKV_WIDTH)),
                  vec, vec, vec, vec, vec],
        out_specs=pl.BlockSpec((tb, RWKV_WIDTH), lambda i: (i, 0)),
        scratch_shapes=[pltpu.VMEM((1, RWKV_IN), F32), pltpu.VMEM((N_PAIRS, PAIR, PAIR), F32),
                        pltpu.VMEM((tb, RWKV_WIDTH), F32)],
        compiler_params=_params(),
        name="rwkv",
    )(z, mu, w0, waup, a0, gup, k_k, k_a, r_k, lnw, lnb)


def _sgu_kernel(z_ref, lnw_ref, lnb_ref, ws_ref, bias_ref, o_ref, *, tb):
    hz = jax.nn.gelu(z_ref[...])
    u = hz[:, :SGU_WIDTH]
    vf = hz[:, SGU_WIDTH:]
    mu = jnp.mean(vf, axis=-1, keepdims=True)
    d = vf - mu
    var = jnp.mean(d * d, axis=-1, keepdims=True)
    vn = d * lax.rsqrt(var + LN_EPS) * lnw_ref[...] + lnb_ref[...]

    ti = lax.broadcasted_iota(jnp.int32, (SGU_BLOCK, SGU_BLOCK), 0)
    tj = lax.broadcasted_iota(jnp.int32, (SGU_BLOCK, SGU_BLOCK), 1)
    tril = tj <= ti
    bi = lax.broadcasted_iota(jnp.int32, (2 * SGU_BLOCK, PAIR), 0) >> 7
    bj = lax.broadcasted_iota(jnp.int32, (2 * SGU_BLOCK, PAIR), 1) >> 6
    sel = bi == bj
    for p in range(SGU_WIDTH // PAIR):
        lanes = slice(p * PAIR, (p + 1) * PAIR)
        w_cat = jnp.concatenate([jnp.where(tril, ws_ref[2 * p], 0.0),
                                 jnp.where(tril, ws_ref[2 * p + 1], 0.0)], axis=1)
        for b in range(tb // SGU_BLOCK):
            rows = slice(b * SGU_BLOCK, (b + 1) * SGU_BLOCK)
            vb = vn[rows, lanes]
            v_stack = jnp.where(sel, jnp.concatenate([vb, vb], axis=0), 0.0)
            mixed = _mm(w_cat, v_stack) + bias_ref[:, lanes]
            o_ref[rows, lanes] = u[rows, lanes] * mixed


def _sgu(z, lnw, lnb, ws, bias, tb):
    t = z.shape[0]
    return pl.pallas_call(
        functools.partial(_sgu_kernel, tb=tb),
        out_shape=jax.ShapeDtypeStruct((t, SGU_WIDTH), F32),
        grid=(t // tb,),
        in_specs=[pl.BlockSpec((tb, 2 * SGU_WIDTH), lambda i: (i, 0)),
                  _full((1, SGU_WIDTH)), _full((1, SGU_WIDTH)),
                  _full((SGU_GROUPS, SGU_BLOCK, SGU_BLOCK)), _full((SGU_BLOCK, SGU_WIDTH))],
        out_specs=pl.BlockSpec((tb, SGU_WIDTH), lambda i: (i, 0)),
        compiler_params=_params("parallel"),
        name="sgu",
    )(z, lnw, lnb, ws, bias)


def _mix_attn_kernel(x_ref, yr_ref, ys_ref, wo1_ref, wo2_ref, g2_ref, wq_ref, k_ref, v_ref, wo_ref,
                     o_ref):
    x1 = (x_ref[...]
          + jnp.dot(yr_ref[...].astype(BF16), wo1_ref[...], preferred_element_type=F32)
          + jnp.dot(ys_ref[...].astype(BF16), wo2_ref[...], preferred_element_type=F32))
    h = _rmsnorm(x1, g2_ref[...]).astype(BF16)
    q = jnp.dot(h, wq_ref[...], preferred_element_type=F32).astype(BF16)
    outs = []
    for hd in range(XA_HEADS):
        lanes = slice(hd * XA_HEAD_DIM, (hd + 1) * XA_HEAD_DIM)
        s = lax.dot_general(q[:, lanes], k_ref[:, lanes], _NT, preferred_element_type=F32)
        s = s * (XA_HEAD_DIM ** -0.5)
        m = jnp.max(s, axis=-1, keepdims=True)
        e = jnp.exp(s - m)
        p = e / jnp.sum(e, axis=-1, keepdims=True)
        outs.append(jnp.dot(p.astype(BF16), v_ref[:, lanes], preferred_element_type=F32))
    o = jnp.concatenate(outs, axis=1).astype(BF16)
    o_ref[...] = x1 + jnp.dot(o, wo_ref[...], preferred_element_type=F32)


def _mix_attn(x, yr, ys, wo1, wo2, g2, wq, k, v, wo, tm):
    t = x.shape[0]
    sq = _full((D_MODEL, D_MODEL))
    half = _full((RWKV_WIDTH, D_MODEL))
    return pl.pallas_call(
        _mix_attn_kernel,
        out_shape=jax.ShapeDtypeStruct((t, D_MODEL), F32),
        grid=(t // tm,),
        in_specs=[pl.BlockSpec((tm, D_MODEL), lambda i: (i, 0)),
                  pl.BlockSpec((tm, RWKV_WIDTH), lambda i: (i, 0)),
                  pl.BlockSpec((tm, SGU_WIDTH), lambda i: (i, 0)),
                  half, half, _full((1, D_MODEL)), sq,
                  _full((MEM_LEN, D_MODEL)), _full((MEM_LEN, D_MODEL)), sq],
        out_specs=pl.BlockSpec((tm, D_MODEL), lambda i: (i, 0)),
        compiler_params=_params("parallel"),
        name="mix_attn",
    )(x, yr, ys, wo1, wo2, g2, wq, k, v, wo)


def _ffn_kernel(x_ref, g3_ref, wg_ref, wu_ref, wd_ref, gf_ref, o_ref):
    x2 = x_ref[...]
    h = _rmsnorm(x2, g3_ref[...]).astype(BF16)
    gate = jnp.dot(h, wg_ref[...], preferred_element_type=F32)
    up = jnp.dot(h, wu_ref[...], preferred_element_type=F32)
    act = (jax.nn.silu(gate) * up).astype(BF16)
    x3 = x2 + jnp.dot(act, wd_ref[...], preferred_element_type=F32)
    o_ref[...] = _rmsnorm(x3, gf_ref[...])


def _ffn(x, g3, wg, wu, wd, gf, tm):
    t = x.shape[0]
    return pl.pallas_call(
        _ffn_kernel,
        out_shape=jax.ShapeDtypeStruct((t, D_MODEL), F32),
        grid=(t // tm,),
        in_specs=[pl.BlockSpec((tm, D_MODEL), lambda i: (i, 0)), _full((1, D_MODEL)),
                  _full((D_MODEL, D_FF)), _full((D_MODEL, D_FF)), _full((D_FF, D_MODEL)),
                  _full((1, D_MODEL))],
        out_specs=pl.BlockSpec((tm, D_MODEL), lambda i: (i, 0)),
        compiler_params=_params("parallel"),
        name="ffn",
    )(x, g3, wg, wu, wd, gf)


def kernel(x, mem, norm1_g, w_in, shift_mu, w0, w_lora_up, a0, a_lora_up, g_lora_up, k_k, k_a, r_k,
           lnx_w, lnx_b, sgu_ln_w, sgu_ln_b, w_spatial, b_spatial, w_out, norm2_g, mem_norm_g,
           wq_x, wk_x, wv_x, wo_x, norm3_g, w_gate, w_up, w_down, norm_f_g):
    b, t, _ = x.shape
    depth = w_in.shape[0]
    row = lambda p: p.reshape(1, -1)
    bf = lambda p: p.astype(BF16)
    outs = []
    for bi in range(b):
        xb = x[bi]
        for l in range(depth):
            lora = w_lora_up.shape[1]
            zeros = jnp.zeros((lora, RWKV_WIDTH), F32)
            waup = jnp.concatenate(
                [jnp.concatenate([w_lora_up[l], zeros], axis=1),
                 jnp.concatenate([zeros, a_lora_up[l]], axis=1)], axis=0)
            bias = jnp.repeat(b_spatial[l].T, SGU_WIDTH // SGU_GROUPS, axis=1)

            z_rwkv, z_sgu = _in_proj(xb, row(norm1_g[l]), bf(w_in[l]), 512)
            y_rwkv = _rwkv(z_rwkv, row(shift_mu[l]), row(w0[l]), bf(waup), row(a0[l]),
                           bf(g_lora_up[l]), row(k_k[l]), row(k_a[l]), row(r_k[l]), row(lnx_w[l]),
                           row(lnx_b[l]), 256)
            y_sgu = _sgu(z_sgu, row(sgu_ln_w[l]), row(sgu_ln_b[l]), w_spatial[l], bias, 512)
            k_mem, v_mem = _mem_kv(mem[bi], row(mem_norm_g[l]), bf(wk_x[l]), bf(wv_x[l]))
            x2 = _mix_attn(xb, y_rwkv, y_sgu, bf(w_out[l][:RWKV_WIDTH]), bf(w_out[l][RWKV_WIDTH:]),
                           row(norm2_g[l]), bf(wq_x[l]), k_mem, v_mem, bf(wo_x[l]), 512)
            assert depth == 1
            xb = _ffn(x2, row(norm3_g[l]), bf(w_gate[l]), bf(w_up[l]), bf(w_down[l]),
                      row(norm_f_g), 512)
        outs.append(xb)
    return jnp.stack(outs, axis=0)
```

```python
import functools

import jax
import jax.numpy as jnp
from jax import lax
from jax.experimental import pallas as pl
from jax.experimental.pallas import tpu as pltpu

F32 = jnp.float32
BF16 = jnp.bfloat16

D_MODEL = 1024
RWKV_WIDTH = 512
RWKV_HEAD = 64
LORA_WA = 128
GATE_LORA = 128
RWKV_IN = 3 * RWKV_WIDTH + LORA_WA + GATE_LORA
SGU_WIDTH = 512
SGU_GROUPS = 8
SGU_BLOCK = 128
IN_WIDTH = RWKV_IN + 2 * SGU_WIDTH
MEM_LEN = 256
XA_HEADS = 4
XA_HEAD_DIM = D_MODEL // XA_HEADS
D_FF = 2816
RMS_EPS = 1e-6
LN_EPS = 1e-5
LNX_EPS = 64e-5

CHUNK = 64
PAIR = 2 * RWKV_HEAD
N_PAIRS = RWKV_WIDTH // PAIR
VMEM_LIMIT = 56 * 1024 * 1024

_NN = (((1,), (0,)), ((), ()))
_NT = (((1,), (1,)), ((), ()))
_TN = (((0,), (0,)), ((), ()))


def _mm(a, b, dims=_NN):
    return lax.dot_general(a.astype(BF16), b.astype(BF16), dims, preferred_element_type=F32)


def _split_mm_rhs(l01, x, terms):
    acc = None
    rem = x
    for _ in range(terms):
        part = rem.astype(BF16)
        rem = rem - part.astype(F32)
        d = lax.dot_general(l01, part, _NN, preferred_element_type=F32)
        acc = d if acc is None else acc + d
    return acc


def _split_mm_lhs(x, r01, terms):
    acc = None
    rem = x
    for _ in range(terms):
        part = rem.astype(BF16)
        rem = rem - part.astype(F32)
        d = lax.dot_general(part, r01, _NN, preferred_element_type=F32)
        acc = d if acc is None else acc + d
    return acc


def _rmsnorm(x, g):
    return x * lax.rsqrt(jnp.mean(x * x, axis=-1, keepdims=True) + RMS_EPS) * g


def _full(shape):
    n = len(shape)
    return pl.BlockSpec(shape, lambda i: (0,) * n)


def _params(sem="arbitrary"):
    return pltpu.CompilerParams(dimension_semantics=(sem,), vmem_limit_bytes=VMEM_LIMIT)


def _mem_kv_kernel(mem_ref, g_ref, wk_ref, wv_ref, k_ref, v_ref):
    m = _rmsnorm(mem_ref[...], g_ref[...]).astype(BF16)
    k_ref[...] = jnp.dot(m, wk_ref[...], preferred_element_type=F32).astype(BF16)
    v_ref[...] = jnp.dot(m, wv_ref[...], preferred_element_type=F32).astype(BF16)


def _mem_kv(mem, g, wk, wv):
    return pl.pallas_call(
        _mem_kv_kernel,
        out_shape=(jax.ShapeDtypeStruct((MEM_LEN, D_MODEL), BF16),) * 2,
        grid=(1,),
        in_specs=[_full((MEM_LEN, D_MODEL)), _full((1, D_MODEL)),
                  _full((D_MODEL, D_MODEL)), _full((D_MODEL, D_MODEL))],
        out_specs=(_full((MEM_LEN, D_MODEL)),) * 2,
        compiler_params=_params(),
        name="mem_kv",
    )(mem, g, wk, wv)


def _in_proj_kernel(x_ref, g_ref, w_ref, zr_ref, zs_ref):
    h = _rmsnorm(x_ref[...], g_ref[...]).astype(BF16)
    z = jnp.dot(h, w_ref[...], preferred_element_type=F32)
    zr_ref[...] = z[:, :RWKV_IN]
    zs_ref[...] = z[:, RWKV_IN:]


def _in_proj(x, g, w, tm):
    t = x.shape[0]
    return pl.pallas_call(
        _in_proj_kernel,
        out_shape=(jax.ShapeDtypeStruct((t, RWKV_IN), F32),
                   jax.ShapeDtypeStruct((t, 2 * SGU_WIDTH), F32)),
        grid=(t // tm,),
        in_specs=[pl.BlockSpec((tm, D_MODEL), lambda i: (i, 0)), _full((1, D_MODEL)),
                  _full((D_MODEL, IN_WIDTH))],
        out_specs=(pl.BlockSpec((tm, RWKV_IN), lambda i: (i, 0)),
                   pl.BlockSpec((tm, 2 * SGU_WIDTH), lambda i: (i, 0))),
        compiler_params=_params("parallel"),
        name="in_proj",
    )(x, g, w)


def _pair_masks():
    t = lax.broadcasted_iota(jnp.int32, (CHUNK, PAIR), 0)
    j = lax.broadcasted_iota(jnp.int32, (CHUNK, PAIR), 1) & (CHUNK - 1)
    strict = j < t
    incl = j <= t
    blk16 = (t >> 4) == (j >> 4)
    blk32 = (t >> 5) == (j >> 5)
    return strict, incl, blk16, blk32


def _bd(x, bd_mask):
    x = x.astype(BF16)
    return jnp.where(bd_mask, jnp.concatenate([x, x], axis=0), 0.0).astype(BF16)


def _unit_lower_inverse_minus_identity(a_list, masks, bd_mask):
    _, _, blk16, blk32 = masks
    ad = [jnp.where(blk16, a, 0.0) for a in a_list]
    ap = [_mm(x, _bd(x, bd_mask)) for x in ad]
    tp = ad
    for _ in range(2):
        both = [_mm(p, jnp.concatenate([_bd(p, bd_mask), _bd(t, bd_mask)], axis=1))
                for p, t in zip(ap, tp)]
        tp = [t + p + b[:, PAIR:] for t, p, b in zip(tp, ap, both)]
        ap = [b[:, :PAIR] for b in both]
    last = [_mm(p, _bd(t, bd_mask)) for p, t in zip(ap, tp)]
    tp = [t + p + x for t, p, x in zip(tp, ap, last)]
    for off_mask in (blk32 & ~blk16, ~blk32):
        off = [jnp.where(off_mask, a, 0.0) for a in a_list]
        x = [o + _mm(t, _bd(o, bd_mask)) for o, t in zip(off, tp)]
        tp = [t + xx + _mm(xx, _bd(t, bd_mask)) for t, xx in zip(tp, x)]
    return tp


def _rwkv_kernel(z_ref, mu_ref, w0_ref, waup_ref, a0_ref, gup_ref, kk_ref, ka_ref, rk_ref,
                 lnw_ref, lnb_ref, o_ref, carry_ref, s_ref, y_ref, *, tb):
    @pl.when(pl.program_id(0) == 0)
    def _():
        carry_ref[...] = jnp.zeros_like(carry_ref)
        s_ref[...] = jnp.zeros_like(s_ref)

    z = z_ref[...]
    row = lax.broadcasted_iota(jnp.int32, (tb, 1), 0)
    zprev = jnp.where(row == 0, carry_ref[...], pltpu.roll(z, 1, axis=0))
    carry_ref[...] = z[tb - 1:tb, :]
    h = z + (zprev - z) * mu_ref[...]

    r = h[:, 0:RWKV_WIDTH]
    k = h[:, RWKV_WIDTH:2 * RWKV_WIDTH]
    v = h[:, 2 * RWKV_WIDTH:3 * RWKV_WIDTH]
    wa_in = h[:, 3 * RWKV_WIDTH:3 * RWKV_WIDTH + LORA_WA]
    gd = h[:, 3 * RWKV_WIDTH + LORA_WA:RWKV_IN]

    lane = lax.broadcasted_iota(jnp.int32, (1, LORA_WA), 1)
    wa_in = jnp.where(lane < LORA_WA // 2, jnp.tanh(wa_in), wa_in)
    wa = jnp.dot(wa_in.astype(BF16), waup_ref[...], preferred_element_type=F32)
    w_pre = w0_ref[...] + wa[:, :RWKV_WIDTH]
    a = jax.nn.sigmoid(a0_ref[...] + wa[:, RWKV_WIDTH:])
    logw = -jnp.exp(-jax.nn.softplus(-w_pre) - 0.5)
    g = jnp.dot(jax.nn.sigmoid(gd).astype(BF16), gup_ref[...], preferred_element_type=F32)

    li = lax.broadcasted_iota(jnp.int32, (256, 256), 0) >> 6
    lj = lax.broadcasted_iota(jnp.int32, (256, 256), 1) >> 6
    seg01 = (li == lj).astype(BF16)

    def head_sum(x):
        return jnp.concatenate(
            [_split_mm_lhs(x[:, 256 * q:256 * (q + 1)], seg01, 2) for q in range(RWKV_WIDTH // 256)],
            axis=1)

    kk = k * kk_ref[...]
    kk = kk / jnp.maximum(jnp.sqrt(head_sum(kk * kk)), 1e-12)
    kmod = k * (1.0 + (a - 1.0) * ka_ref[...])
    kka = kk * a

    masks = _pair_masks()
    strict, incl = masks[0], masks[1]
    bi = lax.broadcasted_iota(jnp.int32, (PAIR, PAIR), 0) >> 6
    bj = lax.broadcasted_iota(jnp.int32, (PAIR, PAIR), 1) >> 6
    bd1 = bi == bj
    bd2 = jnp.concatenate([bd1, bd1], axis=1)
    ti = lax.broadcasted_iota(jnp.int32, (CHUNK, CHUNK), 0)
    tj = lax.broadcasted_iota(jnp.int32, (CHUNK, CHUNK), 1)
    ltri01 = (tj <= ti).astype(BF16)

    n_chunks = tb // CHUNK
    rt, at, bh, kh, bc, kc, w_last = [], [], [], [], [], [], []
    for c in range(n_chunks):
        rows = slice(c * CHUNK, (c + 1) * CHUNK)
        lw = logw[rows]
        cs = _split_mm_rhs(ltri01, lw, 3)
        cs_last = cs[CHUNK - 1:CHUNK, :]
        w_inv = jnp.exp(-cs)
        w_tail = jnp.exp(cs_last - cs)
        rt.append(r[rows] * jnp.exp(cs))
        at.append(-kk[rows] * jnp.exp(cs - lw))
        bh.append(kka[rows] * w_inv)
        kh.append(kmod[rows] * w_inv)
        bc.append(kka[rows] * w_tail)
        kc.append(kmod[rows] * w_tail)
        w_last.append(jnp.exp(cs_last))

    probs = [(c, p) for c in range(n_chunks) for p in range(N_PAIRS)]
    cut = lambda arrs: [arrs[c][:, p * PAIR:(p + 1) * PAIR] for c, p in probs]
    rt_p, at_p, bh_p, kh_p, bc_p, kc_p = map(cut, (rt, at, bh, kh, bc, kc))
    v_p = [v[c * CHUNK:(c + 1) * CHUNK, p * PAIR:(p + 1) * PAIR] for c, p in probs]

    gram = [_mm(jnp.concatenate([a_, r_], axis=0),
                jnp.concatenate([_bd(b_, bd1), _bd(k_, bd1)], axis=0), _NT)
            for a_, r_, b_, k_ in zip(at_p, rt_p, bh_p, kh_p)]
    a_ab = [jnp.where(strict, g_[:CHUNK, :PAIR], 0.0) for g_ in gram]
    a_ak = [jnp.where(strict, g_[:CHUNK, PAIR:], 0.0) for g_ in gram]
    b_rb = [jnp.where(incl, g_[CHUNK:, :PAIR], 0.0) for g_ in gram]
    b_rk = [jnp.where(incl, g_[CHUNK:, PAIR:], 0.0) for g_ in gram]
    v_bd = [_bd(x_, bd1) for x_ in v_p]
    rhs = [jnp.concatenate([_mm(m_, x_), a_], axis=1) for m_, x_, a_ in zip(a_ak, v_bd, at_p)]
    tp = _unit_lower_inverse_minus_identity(a_ab, masks, bd1)
    sol = [x_ + _mm(t_, _bd(x_, bd2)) for x_, t_ in zip(rhs, tp)]
    u_v = [x_[:, :PAIR] for x_ in sol]
    a_chk = [x_[:, PAIR:] for x_ in sol]
    p_mat = [jnp.where(bd1, _mm(a_, b_, _TN), 0.0) for a_, b_ in zip(a_chk, bc_p)]
    q_mat = [jnp.where(bd1, _mm(jnp.concatenate([u_, x_], axis=0),
                                jnp.concatenate([b_, k_], axis=0), _TN), 0.0)
             for u_, x_, b_, k_ in zip(u_v, v_p, bc_p, kc_p)]
    r_chk = [r_ + _mm(m_, _bd(a_, bd1)) for r_, m_, a_ in zip(rt_p, b_rb, a_chk)]
    y_v = [_mm(jnp.concatenate([m1, m2], axis=1), jnp.concatenate([_bd(u_, bd1), x_], axis=0))
           for m1, m2, u_, x_ in zip(b_rb, b_rk, u_v, v_bd)]

    s = [s_ref[p] for p in range(N_PAIRS)]
    for c in range(n_chunks):
        idx = [c * N_PAIRS + p for p in range(N_PAIRS)]
        s_next = [s[p] * w_last[c][:, p * PAIR:(p + 1) * PAIR] + _mm(s[p], p_mat[i]) + q_mat[i]
                  for p, i in enumerate(idx)]
        for p, i in enumerate(idx):
            y_ref[c * CHUNK:(c + 1) * CHUNK, p * PAIR:(p + 1) * PAIR] = _mm(r_chk[i], s[p], _NT) + y_v[i]
        s = s_next
    for p in range(N_PAIRS):
        s_ref[p] = s[p]

    y = y_ref[...]
    mean = head_sum(y) * (1.0 / RWKV_HEAD)
    d = y - mean
    var = head_sum(d * d) * (1.0 / RWKV_HEAD)
    yn = d * lax.rsqrt(var + LNX_EPS) * lnw_ref[...] + lnb_ref[...]
    bonus = head_sum(r * kmod * rk_ref[...]) * v
    o_ref[...] = (yn + bonus) * g


def _rwkv(z, mu, w0, waup, a0, gup, k_k, k_a, r_k, lnw, lnb, tb):
    t = z.shape[0]
    vec = _full((1, RWKV_WIDTH))
    return pl.pallas_call(
        functools.partial(_rwkv_kernel, tb=tb),
        out_shape=jax.ShapeDtypeStruct((t, RWKV_WIDTH), F32),
        grid=(t // tb,),
        in_specs=[pl.BlockSpec((tb, RWKV_IN), lambda i: (i, 0)), _full((1, RWKV_IN)), vec,
                  _full((LORA_WA, 2 * RWKV_WIDTH)), vec, _full((GATE_LORA, RWKV_WIDTH)),
                  vec, vec, vec, vec, vec],
        out_specs=pl.BlockSpec((tb, RWKV_WIDTH), lambda i: (i, 0)),
        scratch_shapes=[pltpu.VMEM((1, RWKV_IN), F32), pltpu.VMEM((N_PAIRS, PAIR, PAIR), F32),
                        pltpu.VMEM((tb, RWKV_WIDTH), F32)],
        compiler_params=_params(),
        name="rwkv",
    )(z, mu, w0, waup, a0, gup, k_k, k_a, r_k, lnw, lnb)


def _sgu_kernel(z_ref, lnw_ref, lnb_ref, ws_ref, bias_ref, o_ref, *, tb):
    hz = jax.nn.gelu(z_ref[...])
    u = hz[:, :SGU_WIDTH]
    vf = hz[:, SGU_WIDTH:]
    mu = jnp.mean(vf, axis=-1, keepdims=True)
    d = vf - mu
    var = jnp.mean(d * d, axis=-1, keepdims=True)
    vn = d * lax.rsqrt(var + LN_EPS) * lnw_ref[...] + lnb_ref[...]

    ti = lax.broadcasted_iota(jnp.int32, (SGU_BLOCK, SGU_BLOCK), 0)
    tj = lax.broadcasted_iota(jnp.int32, (SGU_BLOCK, SGU_BLOCK), 1)
    tril = tj <= ti
    bi = lax.broadcasted_iota(jnp.int32, (2 * SGU_BLOCK, PAIR), 0) >> 7
    bj = lax.broadcasted_iota(jnp.int32, (2 * SGU_BLOCK, PAIR), 1) >> 6
    sel = bi == bj
    for p in range(SGU_WIDTH // PAIR):
        lanes = slice(p * PAIR, (p + 1) * PAIR)
        w_cat = jnp.concatenate([jnp.where(tril, ws_ref[2 * p], 0.0),
                                 jnp.where(tril, ws_ref[2 * p + 1], 0.0)], axis=1)
        for b in range(tb // SGU_BLOCK):
            rows = slice(b * SGU_BLOCK, (b + 1) * SGU_BLOCK)
            vb = vn[rows, lanes]
            v_stack = jnp.where(sel, jnp.concatenate([vb, vb], axis=0), 0.0)
            mixed = _mm(w_cat, v_stack) + bias_ref[:, lanes]
            o_ref[rows, lanes] = u[rows, lanes] * mixed


def _sgu(z, lnw, lnb, ws, bias, tb):
    t = z.shape[0]
    return pl.pallas_call(
        functools.partial(_sgu_kernel, tb=tb),
        out_shape=jax.ShapeDtypeStruct((t, SGU_WIDTH), F32),
        grid=(t // tb,),
        in_specs=[pl.BlockSpec((tb, 2 * SGU_WIDTH), lambda i: (i, 0)),
                  _full((1, SGU_WIDTH)), _full((1, SGU_WIDTH)),
                  _full((SGU_GROUPS, SGU_BLOCK, SGU_BLOCK)), _full((SGU_BLOCK, SGU_WIDTH))],
        out_specs=pl.BlockSpec((tb, SGU_WIDTH), lambda i: (i, 0)),
        compiler_params=_params("parallel"),
        name="sgu",
    )(z, lnw, lnb, ws, bias)


def _mix_attn_kernel(x_ref, yr_ref, ys_ref, wo1_ref, wo2_ref, g2_ref, wq_ref, k_ref, v_ref, wo_ref,
                     o_ref):
    x1 = (x_ref[...]
          + jnp.dot(yr_ref[...].astype(BF16), wo1_ref[...], preferred_element_type=F32)
          + jnp.dot(ys_ref[...].astype(BF16), wo2_ref[...], preferred_element_type=F32))
    h = _rmsnorm(x1, g2_ref[...]).astype(BF16)
    q = jnp.dot(h, wq_ref[...], preferred_element_type=F32).astype(BF16)
    outs = []
    for hd in range(XA_HEADS):
        lanes = slice(hd * XA_HEAD_DIM, (hd + 1) * XA_HEAD_DIM)
        s = lax.dot_general(q[:, lanes], k_ref[:, lanes], _NT, preferred_element_type=F32)
        s = s * (XA_HEAD_DIM ** -0.5)
        m = jnp.max(s, axis=-1, keepdims=True)
        e = jnp.exp(s - m)
        p = e / jnp.sum(e, axis=-1, keepdims=True)
        outs.append(jnp.dot(p.astype(BF16), v_ref[:, lanes], preferred_element_type=F32))
    o = jnp.concatenate(outs, axis=1).astype(BF16)
    o_ref[...] = x1 + jnp.dot(o, wo_ref[...], preferred_element_type=F32)


def _mix_attn(x, yr, ys, wo1, wo2, g2, wq, k, v, wo, tm):
    t = x.shape[0]
    sq = _full((D_MODEL, D_MODEL))
    half = _full((RWKV_WIDTH, D_MODEL))
    return pl.pallas_call(
        _mix_attn_kernel,
        out_shape=jax.ShapeDtypeStruct((t, D_MODEL), F32),
        grid=(t // tm,),
        in_specs=[pl.BlockSpec((tm, D_MODEL), lambda i: (i, 0)),
                  pl.BlockSpec((tm, RWKV_WIDTH), lambda i: (i, 0)),
                  pl.BlockSpec((tm, SGU_WIDTH), lambda i: (i, 0)),
                  half, half, _full((1, D_MODEL)), sq,
                  _full((MEM_LEN, D_MODEL)), _full((MEM_LEN, D_MODEL)), sq],
        out_specs=pl.BlockSpec((tm, D_MODEL), lambda i: (i, 0)),
        compiler_params=_params("parallel"),
        name="mix_attn",
    )(x, yr, ys, wo1, wo2, g2, wq, k, v, wo)


def _ffn_kernel(x_ref, g3_ref, wg_ref, wu_ref, wd_ref, gf_ref, o_ref):
    x2 = x_ref[...]
    h = _rmsnorm(x2, g3_ref[...]).astype(BF16)
    gate = jnp.dot(h, wg_ref[...], preferred_element_type=F32)
    up = jnp.dot(h, wu_ref[...], preferred_element_type=F32)
    act = (jax.nn.silu(gate) * up).astype(BF16)
    x3 = x2 + jnp.dot(act, wd_ref[...], preferred_element_type=F32)
    o_ref[...] = _rmsnorm(x3, gf_ref[...])


def _ffn(x, g3, wg, wu, wd, gf, tm):
    t = x.shape[0]
    return pl.pallas_call(
        _ffn_kernel,
        out_shape=jax.ShapeDtypeStruct((t, D_MODEL), F32),
        grid=(t // tm,),
        in_specs=[pl.BlockSpec((tm, D_MODEL), lambda i: (i, 0)), _full((1, D_MODEL)),
                  _full((D_MODEL, D_FF)), _full((D_MODEL, D_FF)), _full((D_FF, D_MODEL)),
                  _full((1, D_MODEL))],
        out_specs=pl.BlockSpec((tm, D_MODEL), lambda i: (i, 0)),
        compiler_params=_params("parallel"),
        name="ffn",
    )(x, g3, wg, wu, wd, gf)


def kernel(x, mem, norm1_g, w_in, shift_mu, w0, w_lora_up, a0, a_lora_up, g_lora_up, k_k, k_a, r_k,
           lnx_w, lnx_b, sgu_ln_w, sgu_ln_b, w_spatial, b_spatial, w_out, norm2_g, mem_norm_g,
           wq_x, wk_x, wv_x, wo_x, norm3_g, w_gate, w_up, w_down, norm_f_g):
    b, t, _ = x.shape
    depth = w_in.shape[0]
    row = lambda p: p.reshape(1, -1)
    bf = lambda p: p.astype(BF16)
    outs = []
    for bi in range(b):
        xb = x[bi]
        for l in range(depth):
            lora = w_lora_up.shape[1]
            zeros = jnp.zeros((lora, RWKV_WIDTH), F32)
            waup = jnp.concatenate(
                [jnp.concatenate([w_lora_up[l], zeros], axis=1),
                 jnp.concatenate([zeros, a_lora_up[l]], axis=1)], axis=0)
            bias = jnp.repeat(b_spatial[l].T, SGU_WIDTH // SGU_GROUPS, axis=1)

            z_rwkv, z_sgu = _in_proj(xb, row(norm1_g[l]), bf(w_in[l]), 512)
            y_rwkv = _rwkv(z_rwkv, row(shift_mu[l]), row(w0[l]), bf(waup), row(a0[l]),
                           bf(g_lora_up[l]), row(k_k[l]), row(k_a[l]), row(r_k[l]), row(lnx_w[l]),
                           row(lnx_b[l]), 256)
            y_sgu = _sgu(z_sgu, row(sgu_ln_w[l]), row(sgu_ln_b[l]), w_spatial[l], bias, 512)
            k_mem, v_mem = _mem_kv(mem[bi], row(mem_norm_g[l]), bf(wk_x[l]), bf(wv_x[l]))
            x2 = _mix_attn(xb, y_rwkv, y_sgu, bf(w_out[l][:RWKV_WIDTH]), bf(w_out[l][RWKV_WIDTH:]),
                           row(norm2_g[l]), bf(wq_x[l]), k_mem, v_mem, bf(wo_x[l]), 512)
            assert depth == 1
            xb = _ffn(x2, row(norm3_g[l]), bf(w_gate[l]), bf(w_up[l]), bf(w_down[l]),
                      row(norm_f_g), 512)
        outs.append(xb)
    return jnp.stack(outs, axis=0)
```

```python
import functools

import jax
import jax.numpy as jnp
from jax import lax
from jax.experimental import pallas as pl
from jax.experimental.pallas import tpu as pltpu

F32 = jnp.float32
BF16 = jnp.bfloat16

D_MODEL = 1024
RWKV_WIDTH = 512
RWKV_HEAD = 64
LORA_WA = 128
GATE_LORA = 128
RWKV_IN = 3 * RWKV_WIDTH + LORA_WA + GATE_LORA
SGU_WIDTH = 512
SGU_GROUPS = 8
SGU_BLOCK = 128
IN_WIDTH = RWKV_IN + 2 * SGU_WIDTH
MEM_LEN = 256
XA_HEADS = 4
XA_HEAD_DIM = D_MODEL // XA_HEADS
D_FF = 2816
RMS_EPS = 1e-6
LN_EPS = 1e-5
LNX_EPS = 64e-5
EXP_M05 = 0.6065306597126334

CHUNK = 64
PAIR = 2 * RWKV_HEAD
N_PAIRS = RWKV_WIDTH // PAIR
TM_DENSE = 512
TB_RWKV = 256
VMEM_LIMIT = 56 * 1024 * 1024

_NN = (((1,), (0,)), ((), ()))
_NT = (((1,), (1,)), ((), ()))
_TN = (((0,), (0,)), ((), ()))


def _mm(a, b, dims=_NN):
    return lax.dot_general(a.astype(BF16), b.astype(BF16), dims, preferred_element_type=F32)


def _split_bf16(x, terms):
    parts = []
    rem = x
    for _ in range(terms):
        part = rem.astype(BF16)
        rem = rem - part.astype(F32)
        parts.append(part)
    return parts


def _cumsum_rows(ltri01, parts):
    acc = None
    for part in parts:
        d = lax.dot_general(ltri01, part, _NN, preferred_element_type=F32)
        acc = d if acc is None else acc + d
    return acc


def _head_sum_parts(parts, seg01):
    cols = []
    for q in range(RWKV_WIDTH // 256):
        acc = None
        for part in parts:
            d = lax.dot_general(part[:, 256 * q:256 * (q + 1)], seg01, _NN,
                                preferred_element_type=F32)
            acc = d if acc is None else acc + d
        cols.append(acc)
    return jnp.concatenate(cols, axis=1)


def _rmsnorm(x, g):
    return x * lax.rsqrt(jnp.mean(x * x, axis=-1, keepdims=True) + RMS_EPS) * g


def _full(shape):
    n = len(shape)
    return pl.BlockSpec(shape, lambda i: (0,) * n)


def _params(sem="arbitrary"):
    return pltpu.CompilerParams(dimension_semantics=(sem,), vmem_limit_bytes=VMEM_LIMIT)


def _mem_kv_kernel(mem_ref, g_ref, wk_ref, wv_ref, k_ref, v_ref):
    m = _rmsnorm(mem_ref[...], g_ref[...]).astype(BF16)
    k_ref[...] = jnp.dot(m, wk_ref[...], preferred_element_type=F32).astype(BF16)
    v_ref[...] = jnp.dot(m, wv_ref[...], preferred_element_type=F32).astype(BF16)


def _mem_kv(mem, g, wk, wv):
    return pl.pallas_call(
        _mem_kv_kernel,
        out_shape=(jax.ShapeDtypeStruct((MEM_LEN, D_MODEL), BF16),) * 2,
        grid=(1,),
        in_specs=[_full((MEM_LEN, D_MODEL)), _full((1, D_MODEL)),
                  _full((D_MODEL, D_MODEL)), _full((D_MODEL, D_MODEL))],
        out_specs=(_full((MEM_LEN, D_MODEL)),) * 2,
        compiler_params=_params(),
        name="mem_kv",
    )(mem, g, wk, wv)


def _in_proj_kernel(x_ref, g_ref, w_ref, zr_ref, zs_ref):
    h = _rmsnorm(x_ref[...], g_ref[...]).astype(BF16)
    z = jnp.dot(h, w_ref[...], preferred_element_type=F32)
    zr_ref[...] = z[:, :RWKV_IN]
    zs_ref[...] = z[:, RWKV_IN:]


def _in_proj(x, g, w, tm):
    t = x.shape[0]
    return pl.pallas_call(
        _in_proj_kernel,
        out_shape=(jax.ShapeDtypeStruct((t, RWKV_IN), F32),
                   jax.ShapeDtypeStruct((t, 2 * SGU_WIDTH), F32)),
        grid=(t // tm,),
        in_specs=[pl.BlockSpec((tm, D_MODEL), lambda i: (i, 0)), _full((1, D_MODEL)),
                  _full((D_MODEL, IN_WIDTH))],
        out_specs=(pl.BlockSpec((tm, RWKV_IN), lambda i: (i, 0)),
                   pl.BlockSpec((tm, 2 * SGU_WIDTH), lambda i: (i, 0))),
        compiler_params=_params("parallel"),
        name="in_proj",
    )(x, g, w)


def _pair_masks():
    t = lax.broadcasted_iota(jnp.int32, (CHUNK, PAIR), 0)
    j = lax.broadcasted_iota(jnp.int32, (CHUNK, PAIR), 1) & (CHUNK - 1)
    strict = j < t
    incl = j <= t
    blk16 = (t >> 4) == (j >> 4)
    blk32 = (t >> 5) == (j >> 5)
    return strict, incl, blk16, blk32


def _bd(x, bd_mask):
    x = x.astype(BF16)
    return jnp.where(bd_mask, jnp.concatenate([x, x], axis=0), 0.0).astype(BF16)


def _unit_lower_inverse_minus_identity(a_list, masks, bd_mask):
    _, _, blk16, blk32 = masks
    ad = [jnp.where(blk16, a, 0.0) for a in a_list]
    ap = [_mm(x, _bd(x, bd_mask)) for x in ad]
    yield
    tp = ad
    for _ in range(2):
        both = [_mm(p, jnp.concatenate([_bd(p, bd_mask), _bd(t, bd_mask)], axis=1))
                for p, t in zip(ap, tp)]
        yield
        tp = [t + p + b[:, PAIR:] for t, p, b in zip(tp, ap, both)]
        ap = [b[:, :PAIR] for b in both]
    last = [_mm(p, _bd(t, bd_mask)) for p, t in zip(ap, tp)]
    yield
    tp = [t + p + x for t, p, x in zip(tp, ap, last)]
    for off_mask in (blk32 & ~blk16, ~blk32):
        off = [jnp.where(off_mask, a, 0.0) for a in a_list]
        x = [o + _mm(t, _bd(o, bd_mask)) for o, t in zip(off, tp)]
        yield
        tp = [t + xx + _mm(xx, _bd(t, bd_mask)) for t, xx in zip(tp, x)]
        yield
    return tp


_PREP_F32 = ("rt", "at", "gate", "bonus")
_PREP_BF16 = ("bh", "kh", "bc", "kc", "v")
_PREP_NAMES = _PREP_F32 + _PREP_BF16 + ("wl",)
_WL_ROWS = 8


def _head_sum(x, seg01):
    return _head_sum_parts(_split_bf16(x, 2), seg01)


def _seg01():
    li = lax.broadcasted_iota(jnp.int32, (256, 256), 0) >> 6
    lj = lax.broadcasted_iota(jnp.int32, (256, 256), 1) >> 6
    return (li == lj).astype(BF16)


def _rwkv_prep(z_ref, mu_ref, w0_ref, waup_ref, a0_ref, gup_ref, kk_ref, ka_ref, rk_ref, carry_ref,
               prep, *, tb):
    seg01 = _seg01()
    lane = lax.broadcasted_iota(jnp.int32, (1, LORA_WA), 1)
    row = lax.broadcasted_iota(jnp.int32, (CHUNK, 1), 0)
    ti = lax.broadcasted_iota(jnp.int32, (CHUNK, CHUNK), 0)
    tj = lax.broadcasted_iota(jnp.int32, (CHUNK, CHUNK), 1)
    ltri01 = (tj <= ti).astype(BF16)
    for c in range(tb // CHUNK):
        rows = slice(c * CHUNK, (c + 1) * CHUNK)
        z = z_ref[rows, :]
        before = carry_ref[...] if c == 0 else z_ref[c * CHUNK - 1:c * CHUNK, :]
        zprev = jnp.where(row == 0, before, pltpu.roll(z, 1, axis=0))
        h = z + (zprev - z) * mu_ref[...]
        r = h[:, 0:RWKV_WIDTH]
        k = h[:, RWKV_WIDTH:2 * RWKV_WIDTH]
        v = h[:, 2 * RWKV_WIDTH:3 * RWKV_WIDTH]
        wa_in = h[:, 3 * RWKV_WIDTH:3 * RWKV_WIDTH + LORA_WA]
        gd = h[:, 3 * RWKV_WIDTH + LORA_WA:RWKV_IN]
        wa_in = jnp.where(lane < LORA_WA // 2, jnp.tanh(wa_in), wa_in).astype(BF16)
        gd = jax.nn.sigmoid(gd).astype(BF16)
        prep["v"][rows] = v.astype(BF16)
        kk = k * kk_ref[...]
        kk_sq = _split_bf16(kk * kk, 2)
        yield
        wa = jnp.dot(wa_in, waup_ref[...], preferred_element_type=F32)
        prep["gate"][rows] = jnp.dot(gd, gup_ref[...], preferred_element_type=F32)
        kk_ss = _head_sum_parts(kk_sq, seg01)
        yield
        w_pre = w0_ref[...] + wa[:, :RWKV_WIDTH]
        a = jax.nn.sigmoid(a0_ref[...] + wa[:, RWKV_WIDTH:])
        lw = jax.nn.sigmoid(w_pre) * (-EXP_M05)
        lw_parts = _split_bf16(lw, 3)
        kk = kk * lax.rsqrt(jnp.maximum(kk_ss, 1e-24))
        kmod = k * (1.0 + (a - 1.0) * ka_ref[...])
        kka = kk * a
        rkk = _split_bf16(r * kmod * rk_ref[...], 2)
        yield
        cs = _cumsum_rows(ltri01, lw_parts)
        prep["bonus"][rows] = _head_sum_parts(rkk, seg01) * v
        yield
        cs_last = cs[CHUNK - 1:CHUNK, :]
        w_inv = jnp.exp(-cs)
        w_tail = jnp.exp(cs_last - cs)
        prep["rt"][rows] = r * jnp.exp(cs)
        prep["at"][rows] = -kk * jnp.exp(cs - lw)
        prep["bh"][rows] = (kka * w_inv).astype(BF16)
        prep["kh"][rows] = (kmod * w_inv).astype(BF16)
        prep["bc"][rows] = (kka * w_tail).astype(BF16)
        prep["kc"][rows] = (kmod * w_tail).astype(BF16)
        prep["wl"][c:c + 1, :] = jnp.exp(cs_last)
        yield
    carry_ref[...] = z_ref[tb - 1:tb, :]


def _rwkv_scan(prep, lnw_ref, lnb_ref, o_ref, s_ref, y_ref, *, tb):
    masks = _pair_masks()
    strict, incl = masks[0], masks[1]
    bi = lax.broadcasted_iota(jnp.int32, (PAIR, PAIR), 0) >> 6
    bj = lax.broadcasted_iota(jnp.int32, (PAIR, PAIR), 1) >> 6
    bd1 = bi == bj
    bd2 = jnp.concatenate([bd1, bd1], axis=1)

    n_chunks = tb // CHUNK
    probs = [(c, p) for c in range(n_chunks) for p in range(N_PAIRS)]
    cut = lambda name: [prep[name][c * CHUNK:(c + 1) * CHUNK, p * PAIR:(p + 1) * PAIR]
                        for c, p in probs]
    rt_p, at_p, bh_p, kh_p, bc_p, kc_p, v_p = map(cut, ("rt", "at", "bh", "kh", "bc", "kc", "v"))

    gram = [_mm(jnp.concatenate([a_, r_], axis=0),
                jnp.concatenate([_bd(b_, bd1), _bd(k_, bd1)], axis=0), _NT)
            for a_, r_, b_, k_ in zip(at_p, rt_p, bh_p, kh_p)]
    yield
    a_ab = [jnp.where(strict, g_[:CHUNK, :PAIR], 0.0) for g_ in gram]
    a_ak = [jnp.where(strict, g_[:CHUNK, PAIR:], 0.0) for g_ in gram]
    b_rb = [jnp.where(incl, g_[CHUNK:, :PAIR], 0.0) for g_ in gram]
    b_rk = [jnp.where(incl, g_[CHUNK:, PAIR:], 0.0) for g_ in gram]
    v_bd = [_bd(x_, bd1) for x_ in v_p]
    rhs = [jnp.concatenate([_mm(m_, x_), a_], axis=1) for m_, x_, a_ in zip(a_ak, v_bd, at_p)]
    yield
    tp = yield from _unit_lower_inverse_minus_identity(a_ab, masks, bd1)
    sol = [x_ + _mm(t_, _bd(x_, bd2)) for x_, t_ in zip(rhs, tp)]
    yield
    u_v = [x_[:, :PAIR] for x_ in sol]
    a_chk = [x_[:, PAIR:] for x_ in sol]
    p_mat = [jnp.where(bd1, _mm(a_, b_, _TN), 0.0) for a_, b_ in zip(a_chk, bc_p)]
    yield
    q_mat = [jnp.where(bd1, _mm(jnp.concatenate([u_.astype(BF16), x_], axis=0),
                                jnp.concatenate([b_, k_], axis=0), _TN), 0.0)
             for u_, x_, b_, k_ in zip(u_v, v_p, bc_p, kc_p)]
    yield
    r_chk = [r_ + _mm(m_, _bd(a_, bd1)) for r_, m_, a_ in zip(rt_p, b_rb, a_chk)]
    yield
    y_v = [_mm(jnp.concatenate([m1, m2], axis=1), jnp.concatenate([_bd(u_, bd1), x_], axis=0))
           for m1, m2, u_, x_ in zip(b_rb, b_rk, u_v, v_bd)]
    yield

    s = [s_ref[p] for p in range(N_PAIRS)]
    for c in range(n_chunks):
        idx = [c * N_PAIRS + p for p in range(N_PAIRS)]
        w_last = prep["wl"][c:c + 1, :]
        s_next = [s[p] * w_last[:, p * PAIR:(p + 1) * PAIR] + _mm(s[p], p_mat[i]) + q_mat[i]
                  for p, i in enumerate(idx)]
        for p, i in enumerate(idx):
            y_ref[c * CHUNK:(c + 1) * CHUNK, p * PAIR:(p + 1) * PAIR] = (
                _mm(r_chk[i], s[p], _NT) + y_v[i])
        s = s_next
        yield
    for p in range(N_PAIRS):
        s_ref[p] = s[p]

    seg01 = _seg01()
    y = y_ref[...]
    mean = _head_sum(y, seg01) * (1.0 / RWKV_HEAD)
    d = y - mean
    var = _head_sum(d * d, seg01) * (1.0 / RWKV_HEAD)
    yn = d * lax.rsqrt(var + LNX_EPS) * lnw_ref[...] + lnb_ref[...]
    o_ref[...] = (yn + prep["bonus"][...]) * prep["gate"][...]


def _rwkv_kernel(z_ref, mu_ref, w0_ref, waup_ref, a0_ref, gup_ref, kk_ref, ka_ref, rk_ref,
                 lnw_ref, lnb_ref, o_ref, carry_ref, s_ref, y_ref, *prep_refs, tb):
    n = len(_PREP_NAMES)
    slots = [dict(zip(_PREP_NAMES, prep_refs[k * n:(k + 1) * n])) for k in range(2)]
    i = pl.program_id(0)

    @pl.when(i == 0)
    def _():
        carry_ref[...] = jnp.zeros_like(carry_ref)
        s_ref[...] = jnp.zeros_like(s_ref)
        for ref in slots[1].values():
            ref[...] = jnp.zeros_like(ref)

    def step(write, read):
        stages = [_rwkv_scan(read, lnw_ref, lnb_ref, o_ref, s_ref, y_ref, tb=tb),
                  _rwkv_prep(z_ref, mu_ref, w0_ref, waup_ref, a0_ref, gup_ref, kk_ref, ka_ref,
                             rk_ref, carry_ref, write, tb=tb)]
        while stages:
            for gen in list(stages):
                if next(gen, StopIteration) is StopIteration:
                    stages.remove(gen)

    @pl.when((i & 1) == 0)
    def _():
        step(slots[0], slots[1])

    @pl.when((i & 1) == 1)
    def _():
        step(slots[1], slots[0])


def _rwkv(z, mu, w0, waup, a0, gup, k_k, k_a, r_k, lnw, lnb, tb):
    t = z.shape[0]
    n_tiles = t // tb
    assert tb // CHUNK <= _WL_ROWS
    vec = _full((1, RWKV_WIDTH))
    prep_shapes = ([pltpu.VMEM((tb, RWKV_WIDTH), F32)] * len(_PREP_F32)
                   + [pltpu.VMEM((tb, RWKV_WIDTH), BF16)] * len(_PREP_BF16)
                   + [pltpu.VMEM((_WL_ROWS, RWKV_WIDTH), F32)])
    return pl.pallas_call(
        functools.partial(_rwkv_kernel, tb=tb),
        out_shape=jax.ShapeDtypeStruct((t, RWKV_WIDTH), F32),
        grid=(n_tiles + 1,),
        in_specs=[pl.BlockSpec((tb, RWKV_IN), lambda i: (jnp.minimum(i, n_tiles - 1), 0)),
                  _full((1, RWKV_IN)), vec, _full((LORA_WA, 2 * RWKV_WIDTH)), vec,
                  _full((GATE_LORA, RWKV_WIDTH)), vec, vec, vec, vec, vec],
        out_specs=pl.BlockSpec((tb, RWKV_WIDTH), lambda i: (jnp.maximum(i - 1, 0), 0)),
        scratch_shapes=[pltpu.VMEM((1, RWKV_IN), F32), pltpu.VMEM((N_PAIRS, PAIR, PAIR), F32),
                        pltpu.VMEM((tb, RWKV_WIDTH), F32)] + prep_shapes * 2,
        compiler_params=_params(),
        name="rwkv",
    )(z, mu, w0, waup, a0, gup, k_k, k_a, r_k, lnw, lnb)


def _sgu_kernel(z_ref, lnw_ref, lnb_ref, ws_ref, bias_ref, o_ref, *, tb):
    hz = jax.nn.gelu(z_ref[...])
    u = hz[:, :SGU_WIDTH]
    vf = hz[:, SGU_WIDTH:]
    mu = jnp.mean(vf, axis=-1, keepdims=True)
    d = vf - mu
    var = jnp.mean(d * d, axis=-1, keepdims=True)
    vn = d * lax.rsqrt(var + LN_EPS) * lnw_ref[...] + lnb_ref[...]

    ti = lax.broadcasted_iota(jnp.int32, (SGU_BLOCK, SGU_BLOCK), 0)
    tj = lax.broadcasted_iota(jnp.int32, (SGU_BLOCK, SGU_BLOCK), 1)
    tril = tj <= ti
    bi = lax.broadcasted_iota(jnp.int32, (2 * SGU_BLOCK, PAIR), 0) >> 7
    bj = lax.broadcasted_iota(jnp.int32, (2 * SGU_BLOCK, PAIR), 1) >> 6
    sel = bi == bj
    for p in range(SGU_WIDTH // PAIR):
        lanes = slice(p * PAIR, (p + 1) * PAIR)
        w_cat = jnp.concatenate([jnp.where(tril, ws_ref[2 * p], 0.0),
                                 jnp.where(tril, ws_ref[2 * p + 1], 0.0)], axis=1)
        for b in range(tb // SGU_BLOCK):
            rows = slice(b * SGU_BLOCK, (b + 1) * SGU_BLOCK)
            vb = vn[rows, lanes]
            v_stack = jnp.where(sel, jnp.concatenate([vb, vb], axis=0), 0.0)
            mixed = _mm(w_cat, v_stack) + bias_ref[:, lanes]
            o_ref[rows, lanes] = u[rows, lanes] * mixed


def _sgu(z, lnw, lnb, ws, bias, tb):
    t = z.shape[0]
    return pl.pallas_call(
        functools.partial(_sgu_kernel, tb=tb),
        out_shape=jax.ShapeDtypeStruct((t, SGU_WIDTH), F32),
        grid=(t // tb,),
        in_specs=[pl.BlockSpec((tb, 2 * SGU_WIDTH), lambda i: (i, 0)),
                  _full((1, SGU_WIDTH)), _full((1, SGU_WIDTH)),
                  _full((SGU_GROUPS, SGU_BLOCK, SGU_BLOCK)), _full((SGU_BLOCK, SGU_WIDTH))],
        out_specs=pl.BlockSpec((tb, SGU_WIDTH), lambda i: (i, 0)),
        compiler_params=_params("parallel"),
        name="sgu",
    )(z, lnw, lnb, ws, bias)


def _mix_attn_kernel(x_ref, yr_ref, ys_ref, wo1_ref, wo2_ref, g2_ref, wq_ref, k_ref, v_ref, wo_ref,
                     o_ref):
    x1 = (x_ref[...]
          + jnp.dot(yr_ref[...].astype(BF16), wo1_ref[...], preferred_element_type=F32)
          + jnp.dot(ys_ref[...].astype(BF16), wo2_ref[...], preferred_element_type=F32))
    h = _rmsnorm(x1, g2_ref[...]).astype(BF16)
    q = jnp.dot(h, wq_ref[...], preferred_element_type=F32).astype(BF16)
    outs = []
    for hd in range(XA_HEADS):
        lanes = slice(hd * XA_HEAD_DIM, (hd + 1) * XA_HEAD_DIM)
        s = lax.dot_general(q[:, lanes], k_ref[:, lanes], _NT, preferred_element_type=F32)
        s = s * (XA_HEAD_DIM ** -0.5)
        m = jnp.max(s, axis=-1, keepdims=True)
        e = jnp.exp(s - m)
        p = e / jnp.sum(e, axis=-1, keepdims=True)
        outs.append(jnp.dot(p.astype(BF16), v_ref[:, lanes], preferred_element_type=F32))
    o = jnp.concatenate(outs, axis=1).astype(BF16)
    o_ref[...] = x1 + jnp.dot(o, wo_ref[...], preferred_element_type=F32)


def _mix_attn(x, yr, ys, wo1, wo2, g2, wq, k, v, wo, tm):
    t = x.shape[0]
    sq = _full((D_MODEL, D_MODEL))
    half = _full((RWKV_WIDTH, D_MODEL))
    return pl.pallas_call(
        _mix_attn_kernel,
        out_shape=jax.ShapeDtypeStruct((t, D_MODEL), F32),
        grid=(t // tm,),
        in_specs=[pl.BlockSpec((tm, D_MODEL), lambda i: (i, 0)),
                  pl.BlockSpec((tm, RWKV_WIDTH), lambda i: (i, 0)),
                  pl.BlockSpec((tm, SGU_WIDTH), lambda i: (i, 0)),
                  half, half, _full((1, D_MODEL)), sq,
                  _full((MEM_LEN, D_MODEL)), _full((MEM_LEN, D_MODEL)), sq],
        out_specs=pl.BlockSpec((tm, D_MODEL), lambda i: (i, 0)),
        compiler_params=_params("parallel"),
        name="mix_attn",
    )(x, yr, ys, wo1, wo2, g2, wq, k, v, wo)


def _ffn_kernel(x_ref, g3_ref, wg_ref, wu_ref, wd_ref, gf_ref, o_ref):
    x2 = x_ref[...]
    h = _rmsnorm(x2, g3_ref[...]).astype(BF16)
    gate = jnp.dot(h, wg_ref[...], preferred_element_type=F32)
    up = jnp.dot(h, wu_ref[...], preferred_element_type=F32)
    act = (jax.nn.silu(gate) * up).astype(BF16)
    x3 = x2 + jnp.dot(act, wd_ref[...], preferred_element_type=F32)
    o_ref[...] = _rmsnorm(x3, gf_ref[...])


def _ffn(x, g3, wg, wu, wd, gf, tm):
    t = x.shape[0]
    return pl.pallas_call(
        _ffn_kernel,
        out_shape=jax.ShapeDtypeStruct((t, D_MODEL), F32),
        grid=(t // tm,),
        in_specs=[pl.BlockSpec((tm, D_MODEL), lambda i: (i, 0)), _full((1, D_MODEL)),
                  _full((D_MODEL, D_FF)), _full((D_MODEL, D_FF)), _full((D_FF, D_MODEL)),
                  _full((1, D_MODEL))],
        out_specs=pl.BlockSpec((tm, D_MODEL), lambda i: (i, 0)),
        compiler_params=_params("parallel"),
        name="ffn",
    )(x, g3, wg, wu, wd, gf)


def kernel(x, mem, norm1_g, w_in, shift_mu, w0, w_lora_up, a0, a_lora_up, g_lora_up, k_k, k_a, r_k,
           lnx_w, lnx_b, sgu_ln_w, sgu_ln_b, w_spatial, b_spatial, w_out, norm2_g, mem_norm_g,
           wq_x, wk_x, wv_x, wo_x, norm3_g, w_gate, w_up, w_down, norm_f_g):
    b, t, _ = x.shape
    depth = w_in.shape[0]
    assert depth == 1, "the final RMSNorm is fused into the (single) layer's ffn call"
    assert t % TM_DENSE == 0 and t % TB_RWKV == 0
    row = lambda p: p.reshape(1, -1)
    bf = lambda p: p.astype(BF16)
    outs = []
    for bi in range(b):
        xb = x[bi]
        for l in range(depth):
            lora = w_lora_up.shape[1]
            zeros = jnp.zeros((lora, RWKV_WIDTH), F32)
            waup = jnp.concatenate(
                [jnp.concatenate([w_lora_up[l], zeros], axis=1),
                 jnp.concatenate([zeros, a_lora_up[l]], axis=1)], axis=0)
            bias = jnp.repeat(b_spatial[l].T, SGU_WIDTH // SGU_GROUPS, axis=1)

            z_rwkv, z_sgu = _in_proj(xb, row(norm1_g[l]), bf(w_in[l]), TM_DENSE)
            y_rwkv = _rwkv(z_rwkv, row(shift_mu[l]), row(w0[l]), bf(waup), row(a0[l]),
                           bf(g_lora_up[l]), row(k_k[l]), row(k_a[l]), row(r_k[l]), row(lnx_w[l]),
                           row(lnx_b[l]), TB_RWKV)
            y_sgu = _sgu(z_sgu, row(sgu_ln_w[l]), row(sgu_ln_b[l]), w_spatial[l], bias, TM_DENSE)
            k_mem, v_mem = _mem_kv(mem[bi], row(mem_norm_g[l]), bf(wk_x[l]), bf(wv_x[l]))
            x2 = _mix_attn(xb, y_rwkv, y_sgu, bf(w_out[l][:RWKV_WIDTH]), bf(w_out[l][RWKV_WIDTH:]),
                           row(norm2_g[l]), bf(wq_x[l]), k_mem, v_mem, bf(wo_x[l]), TM_DENSE)
            xb = _ffn(x2, row(norm3_g[l]), bf(w_gate[l]), bf(w_up[l]), bf(w_down[l]),
                      row(norm_f_g), TM_DENSE)
        outs.append(xb)
    return jnp.stack(outs, axis=0)
```

```python
import functools

import jax
import jax.numpy as jnp
from jax import lax
from jax.experimental import pallas as pl
from jax.experimental.pallas import tpu as pltpu

F32 = jnp.float32
BF16 = jnp.bfloat16

D_MODEL = 1024
RWKV_WIDTH = 512
RWKV_HEAD = 64
LORA_WA = 128
GATE_LORA = 128
RWKV_IN = 3 * RWKV_WIDTH + LORA_WA + GATE_LORA
SGU_WIDTH = 512
SGU_GROUPS = 8
SGU_BLOCK = 128
IN_WIDTH = RWKV_IN + 2 * SGU_WIDTH
MEM_LEN = 256
XA_HEADS = 4
XA_HEAD_DIM = D_MODEL // XA_HEADS
D_FF = 2816
RMS_EPS = 1e-6
LN_EPS = 1e-5
LNX_EPS = 64e-5
EXP_M05 = 0.6065306597126334

CHUNK = 64
PAIR = 2 * RWKV_HEAD
N_PAIRS = RWKV_WIDTH // PAIR
TM_DENSE = 512
TB_SCAN = 256
WL_ROWS = 8
IN_PROJ_COLS = 256
VMEM_LIMIT = 56 * 1024 * 1024

_NN = (((1,), (0,)), ((), ()))
_NT = (((1,), (1,)), ((), ()))
_TN = (((0,), (0,)), ((), ()))


def _mm(a, b, dims=_NN):
    return lax.dot_general(a.astype(BF16), b.astype(BF16), dims, preferred_element_type=F32)


def _split_bf16(x, terms):
    parts = []
    rem = x
    for _ in range(terms):
        part = rem.astype(BF16)
        rem = rem - part.astype(F32)
        parts.append(part)
    return parts


def _cumsum_rows(ltri01, parts):
    acc = None
    for part in parts:
        d = lax.dot_general(ltri01, part, _NN, preferred_element_type=F32)
        acc = d if acc is None else acc + d
    return acc


def _head_sum_parts(parts, seg01):
    cols = []
    for q in range(RWKV_WIDTH // 256):
        acc = None
        for part in parts:
            d = lax.dot_general(part[:, 256 * q:256 * (q + 1)], seg01, _NN,
                                preferred_element_type=F32)
            acc = d if acc is None else acc + d
        cols.append(acc)
    return jnp.concatenate(cols, axis=1)


def _head_sum(x, seg01):
    return _head_sum_parts(_split_bf16(x, 2), seg01)


def _seg01():
    li = lax.broadcasted_iota(jnp.int32, (256, 256), 0) >> 6
    lj = lax.broadcasted_iota(jnp.int32, (256, 256), 1) >> 6
    return (li == lj).astype(BF16)


def _rmsnorm(x, g):
    return x * lax.rsqrt(jnp.mean(x * x, axis=-1, keepdims=True) + RMS_EPS) * g


def _full(shape):
    n = len(shape)
    return pl.BlockSpec(shape, lambda i: (0,) * n)


def _params(sem="arbitrary"):
    return pltpu.CompilerParams(dimension_semantics=(sem,), vmem_limit_bytes=VMEM_LIMIT)


def _run_interleaved(stages):
    stages = list(stages)
    while stages:
        for entry in list(stages):
            gen, per_round = entry
            for _ in range(per_round):
                if next(gen, StopIteration) is StopIteration:
                    stages.remove(entry)
                    break


def _mem_kv_kernel(mem_ref, g_ref, wk_ref, wv_ref, k_ref, v_ref):
    m = _rmsnorm(mem_ref[...], g_ref[...]).astype(BF16)
    k_ref[...] = jnp.dot(m, wk_ref[...], preferred_element_type=F32).astype(BF16)
    v_ref[...] = jnp.dot(m, wv_ref[...], preferred_element_type=F32).astype(BF16)


def _mem_kv(mem, g, wk, wv):
    return pl.pallas_call(
        _mem_kv_kernel,
        out_shape=(jax.ShapeDtypeStruct((MEM_LEN, D_MODEL), BF16),) * 2,
        grid=(1,),
        in_specs=[_full((MEM_LEN, D_MODEL)), _full((1, D_MODEL)),
                  _full((D_MODEL, D_MODEL)), _full((D_MODEL, D_MODEL))],
        out_specs=(_full((MEM_LEN, D_MODEL)),) * 2,
        compiler_params=_params(),
        name="mem_kv",
    )(mem, g, wk, wv)


_PREP_F32 = ("rt", "at", "gate", "bonus")
_PREP_BF16 = ("bh", "kh", "bc", "kc", "v")
_PREP_NAMES = _PREP_F32 + _PREP_BF16 + ("wl",)


def _in_proj_pieces(x_ref, g_ref, w_ref, z_ref):
    h = _rmsnorm(x_ref[...], g_ref[...]).astype(BF16)
    yield
    for j in range(IN_WIDTH // IN_PROJ_COLS):
        cols = slice(j * IN_PROJ_COLS, (j + 1) * IN_PROJ_COLS)
        z_ref[:, cols] = jnp.dot(h, w_ref[:, cols], preferred_element_type=F32)
        yield


def _rwkv_prep_pieces(z_ref, mu_ref, w0_ref, waup_ref, a0_ref, gup_ref, kk_ref, ka_ref, rk_ref,
                      carry_ref, prep, *, tm):
    seg01 = _seg01()
    lane = lax.broadcasted_iota(jnp.int32, (1, LORA_WA), 1)
    row = lax.broadcasted_iota(jnp.int32, (CHUNK, 1), 0)
    ti = lax.broadcasted_iota(jnp.int32, (CHUNK, CHUNK), 0)
    tj = lax.broadcasted_iota(jnp.int32, (CHUNK, CHUNK), 1)
    ltri01 = (tj <= ti).astype(BF16)
    chunks_per_scan_tile = TB_SCAN // CHUNK
    prep["wl"][:, chunks_per_scan_tile:, :] = jnp.zeros(
        (tm // TB_SCAN, WL_ROWS - chunks_per_scan_tile, RWKV_WIDTH), F32)
    for c in range(tm // CHUNK):
        rows = slice(c * CHUNK, (c + 1) * CHUNK)
        z = z_ref[rows, :RWKV_IN]
        before = carry_ref[...] if c == 0 else z_ref[c * CHUNK - 1:c * CHUNK, :RWKV_IN]
        zprev = jnp.where(row == 0, before, pltpu.roll(z, 1, axis=0))
        h = z + (zprev - z) * mu_ref[...]
        r = h[:, 0:RWKV_WIDTH]
        k = h[:, RWKV_WIDTH:2 * RWKV_WIDTH]
        v = h[:, 2 * RWKV_WIDTH:3 * RWKV_WIDTH]
        wa_in = h[:, 3 * RWKV_WIDTH:3 * RWKV_WIDTH + LORA_WA]
        gd = h[:, 3 * RWKV_WIDTH + LORA_WA:RWKV_IN]
        wa_in = jnp.where(lane < LORA_WA // 2, jnp.tanh(wa_in), wa_in).astype(BF16)
        gd = jax.nn.sigmoid(gd).astype(BF16)
        prep["v"][rows] = v.astype(BF16)
        kk = k * kk_ref[...]
        kk_sq = _split_bf16(kk * kk, 2)
        yield
        wa = jnp.dot(wa_in, waup_ref[...], preferred_element_type=F32)
        prep["gate"][rows] = jnp.dot(gd, gup_ref[...], preferred_element_type=F32)
        kk_ss = _head_sum_parts(kk_sq, seg01)
        yield
        w_pre = w0_ref[...] + wa[:, :RWKV_WIDTH]
        a = jax.nn.sigmoid(a0_ref[...] + wa[:, RWKV_WIDTH:])
        lw = jax.nn.sigmoid(w_pre) * (-EXP_M05)
        lw_parts = _split_bf16(lw, 3)
        kk = kk * lax.rsqrt(jnp.maximum(kk_ss, 1e-24))
        kmod = k * (1.0 + (a - 1.0) * ka_ref[...])
        kka = kk * a
        rkk = _split_bf16(r * kmod * rk_ref[...], 2)
        yield
        cs = _cumsum_rows(ltri01, lw_parts)
        prep["bonus"][rows] = _head_sum_parts(rkk, seg01) * v
        yield
        cs_last = cs[CHUNK - 1:CHUNK, :]
        w_inv = jnp.exp(-cs)
        w_tail = jnp.exp(cs_last - cs)
        prep["rt"][rows] = r * jnp.exp(cs)
        prep["at"][rows] = -kk * jnp.exp(cs - lw)
        prep["bh"][rows] = (kka * w_inv).astype(BF16)
        prep["kh"][rows] = (kmod * w_inv).astype(BF16)
        prep["bc"][rows] = (kka * w_tail).astype(BF16)
        prep["kc"][rows] = (kmod * w_tail).astype(BF16)
        cq, cr = divmod(c, chunks_per_scan_tile)
        prep["wl"][cq, cr:cr + 1, :] = jnp.exp(cs_last)
        yield
    carry_ref[...] = z_ref[tm - 1:tm, :RWKV_IN]


def _sgu_pieces(z_ref, lnw_ref, lnb_ref, wcat_ref, bias_ref, o_ref, *, tm):
    bi = lax.broadcasted_iota(jnp.int32, (2 * SGU_BLOCK, PAIR), 0) >> 7
    bj = lax.broadcasted_iota(jnp.int32, (2 * SGU_BLOCK, PAIR), 1) >> 6
    sel = bi == bj
    for b in range(tm // SGU_BLOCK):
        rows = slice(b * SGU_BLOCK, (b + 1) * SGU_BLOCK)
        hz = jax.nn.gelu(z_ref[rows, RWKV_IN:])
        u = hz[:, :SGU_WIDTH]
        vf = hz[:, SGU_WIDTH:]
        mu = jnp.mean(vf, axis=-1, keepdims=True)
        d = vf - mu
        var = jnp.mean(d * d, axis=-1, keepdims=True)
        vn = d * lax.rsqrt(var + LN_EPS) * lnw_ref[...] + lnb_ref[...]
        stacks = []
        for p in range(SGU_WIDTH // PAIR):
            vb = vn[:, p * PAIR:(p + 1) * PAIR]
            stacks.append(jnp.where(sel, jnp.concatenate([vb, vb], axis=0), 0.0).astype(BF16))
        yield
        for p in range(SGU_WIDTH // PAIR):
            lanes = slice(p * PAIR, (p + 1) * PAIR)
            mixed = jnp.dot(wcat_ref[p], stacks[p], preferred_element_type=F32) + bias_ref[:, lanes]
            o_ref[rows, lanes] = u[:, lanes] * mixed
        yield


def _front_kernel(x_ref, g1_ref, win_ref, mu_ref, w0_ref, waup_ref, a0_ref, gup_ref, kk_ref, ka_ref,
                  rk_ref, slnw_ref, slnb_ref, ws_ref, sbias_ref, *rest, tm):
    n = len(_PREP_NAMES)
    prep = dict(zip(_PREP_NAMES, rest[:n]))
    ysgu_ref = rest[n]
    carry_ref, wcat_ref, z0_ref, z1_ref = rest[n + 1:]
    i = pl.program_id(0)

    @pl.when(i == 0)
    def _():
        carry_ref[...] = jnp.zeros_like(carry_ref)
        z1_ref[...] = jnp.zeros_like(z1_ref)
        ti = lax.broadcasted_iota(jnp.int32, (SGU_BLOCK, SGU_BLOCK), 0)
        tj = lax.broadcasted_iota(jnp.int32, (SGU_BLOCK, SGU_BLOCK), 1)
        tril = tj <= ti
        for p in range(SGU_WIDTH // PAIR):
            wcat_ref[p] = jnp.concatenate(
                [jnp.where(tril, ws_ref[2 * p], 0.0), jnp.where(tril, ws_ref[2 * p + 1], 0.0)],
                axis=1).astype(BF16)

    def step(z_write, z_read):
        _run_interleaved([
            (_in_proj_pieces(x_ref, g1_ref, win_ref, z_write), 1),
            (_rwkv_prep_pieces(z_read, mu_ref, w0_ref, waup_ref, a0_ref, gup_ref, kk_ref, ka_ref,
                               rk_ref, carry_ref, prep, tm=tm), 4),
            (_sgu_pieces(z_read, slnw_ref, slnb_ref, wcat_ref, sbias_ref, ysgu_ref, tm=tm), 1),
        ])

    @pl.when((i & 1) == 0)
    def _():
        step(z0_ref, z1_ref)

    @pl.when((i & 1) == 1)
    def _():
        step(z1_ref, z0_ref)


def _front(x, g1, w_in, mu, w0, waup, a0, gup, k_k, k_a, r_k, slnw, slnb, ws, sbias, tm):
    t = x.shape[0]
    n_tiles = t // tm
    vec = _full((1, RWKV_WIDTH))
    out_tile = lambda i: (jnp.maximum(i - 1, 0), 0)
    out_shapes = ([jax.ShapeDtypeStruct((t, RWKV_WIDTH), F32)] * len(_PREP_F32)
                  + [jax.ShapeDtypeStruct((t, RWKV_WIDTH), BF16)] * len(_PREP_BF16)
                  + [jax.ShapeDtypeStruct((t // TB_SCAN, WL_ROWS, RWKV_WIDTH), F32),
                     jax.ShapeDtypeStruct((t, SGU_WIDTH), F32)])
    out_specs = ([pl.BlockSpec((tm, RWKV_WIDTH), out_tile)] * (len(_PREP_F32) + len(_PREP_BF16))
                 + [pl.BlockSpec((tm // TB_SCAN, WL_ROWS, RWKV_WIDTH),
                                 lambda i: (jnp.maximum(i - 1, 0), 0, 0)),
                    pl.BlockSpec((tm, SGU_WIDTH), out_tile)])
    return pl.pallas_call(
        functools.partial(_front_kernel, tm=tm),
        out_shape=tuple(out_shapes),
        grid=(n_tiles + 1,),
        in_specs=[pl.BlockSpec((tm, D_MODEL), lambda i: (jnp.minimum(i, n_tiles - 1), 0)),
                  _full((1, D_MODEL)), _full((D_MODEL, IN_WIDTH)), _full((1, RWKV_IN)), vec,
                  _full((LORA_WA, 2 * RWKV_WIDTH)), vec, _full((GATE_LORA, RWKV_WIDTH)),
                  vec, vec, vec, _full((1, SGU_WIDTH)), _full((1, SGU_WIDTH)),
                  _full((SGU_GROUPS, SGU_BLOCK, SGU_BLOCK)), _full((SGU_BLOCK, SGU_WIDTH))],
        out_specs=tuple(out_specs),
        scratch_shapes=[pltpu.VMEM((1, RWKV_IN), F32),
                        pltpu.VMEM((SGU_WIDTH // PAIR, SGU_BLOCK, 2 * SGU_BLOCK), BF16),
                        pltpu.VMEM((tm, IN_WIDTH), F32), pltpu.VMEM((tm, IN_WIDTH), F32)],
        compiler_params=_params(),
        name="front",
    )(x, g1, w_in, mu, w0, waup, a0, gup, k_k, k_a, r_k, slnw, slnb, ws, sbias)


def _pair_masks():
    t = lax.broadcasted_iota(jnp.int32, (CHUNK, PAIR), 0)
    j = lax.broadcasted_iota(jnp.int32, (CHUNK, PAIR), 1) & (CHUNK - 1)
    strict = j < t
    incl = j <= t
    blk16 = (t >> 4) == (j >> 4)
    blk32 = (t >> 5) == (j >> 5)
    return strict, incl, blk16, blk32


def _bd(x, bd_mask):
    x = x.astype(BF16)
    return jnp.where(bd_mask, jnp.concatenate([x, x], axis=0), 0.0).astype(BF16)


def _unit_lower_inverse_minus_identity(a_list, masks, bd_mask):
    _, _, blk16, blk32 = masks
    ad = [jnp.where(blk16, a, 0.0) for a in a_list]
    ap = [_mm(x, _bd(x, bd_mask)) for x in ad]
    tp = ad
    for _ in range(2):
        both = [_mm(p, jnp.concatenate([_bd(p, bd_mask), _bd(t, bd_mask)], axis=1))
                for p, t in zip(ap, tp)]
        tp = [t + p + b[:, PAIR:] for t, p, b in zip(tp, ap, both)]
        ap = [b[:, :PAIR] for b in both]
    last = [_mm(p, _bd(t, bd_mask)) for p, t in zip(ap, tp)]
    tp = [t + p + x for t, p, x in zip(tp, ap, last)]
    for off_mask in (blk32 & ~blk16, ~blk32):
        off = [jnp.where(off_mask, a, 0.0) for a in a_list]
        x = [o + _mm(t, _bd(o, bd_mask)) for o, t in zip(off, tp)]
        tp = [t + xx + _mm(xx, _bd(t, bd_mask)) for t, xx in zip(tp, x)]
    return tp


def _scan_kernel(rt_ref, at_ref, gate_ref, bonus_ref, bh_ref, kh_ref, bc_ref, kc_ref, v_ref, wl_ref,
                 lnw_ref, lnb_ref, o_ref, s_ref, y_ref, *, tb):
    @pl.when(pl.program_id(0) == 0)
    def _():
        s_ref[...] = jnp.zeros_like(s_ref)

    masks = _pair_masks()
    strict, incl = masks[0], masks[1]
    bi = lax.broadcasted_iota(jnp.int32, (PAIR, PAIR), 0) >> 6
    bj = lax.broadcasted_iota(jnp.int32, (PAIR, PAIR), 1) >> 6
    bd1 = bi == bj
    bd2 = jnp.concatenate([bd1, bd1], axis=1)

    n_chunks = tb // CHUNK
    probs = [(c, p) for c in range(n_chunks) for p in range(N_PAIRS)]
    cut = lambda ref: [ref[c * CHUNK:(c + 1) * CHUNK, p * PAIR:(p + 1) * PAIR] for c, p in probs]
    rt_p, at_p, bh_p, kh_p, bc_p, kc_p, v_p = map(
        cut, (rt_ref, at_ref, bh_ref, kh_ref, bc_ref, kc_ref, v_ref))

    gram = [_mm(jnp.concatenate([a_, r_], axis=0),
                jnp.concatenate([_bd(b_, bd1), _bd(k_, bd1)], axis=0), _NT)
            for a_, r_, b_, k_ in zip(at_p, rt_p, bh_p, kh_p)]
    a_ab = [jnp.where(strict, g_[:CHUNK, :PAIR], 0.0) for g_ in gram]
    a_ak = [jnp.where(strict, g_[:CHUNK, PAIR:], 0.0) for g_ in gram]
    b_rb = [jnp.where(incl, g_[CHUNK:, :PAIR], 0.0) for g_ in gram]
    b_rk = [jnp.where(incl, g_[CHUNK:, PAIR:], 0.0) for g_ in gram]
    v_bd = [_bd(x_, bd1) for x_ in v_p]
    rhs = [jnp.concatenate([_mm(m_, x_), a_], axis=1) for m_, x_, a_ in zip(a_ak, v_bd, at_p)]
    tp = _unit_lower_inverse_minus_identity(a_ab, masks, bd1)
    sol = [x_ + _mm(t_, _bd(x_, bd2)) for x_, t_ in zip(rhs, tp)]
    u_v = [x_[:, :PAIR] for x_ in sol]
    a_chk = [x_[:, PAIR:] for x_ in sol]
    p_mat = [jnp.where(bd1, _mm(a_, b_, _TN), 0.0) for a_, b_ in zip(a_chk, bc_p)]
    q_mat = [jnp.where(bd1, _mm(jnp.concatenate([u_.astype(BF16), x_], axis=0),
                                jnp.concatenate([b_, k_], axis=0), _TN), 0.0)
             for u_, x_, b_, k_ in zip(u_v, v_p, bc_p, kc_p)]
    r_chk = [r_ + _mm(m_, _bd(a_, bd1)) for r_, m_, a_ in zip(rt_p, b_rb, a_chk)]
    y_v = [_mm(jnp.concatenate([m1, m2], axis=1), jnp.concatenate([_bd(u_, bd1), x_], axis=0))
           for m1, m2, u_, x_ in zip(b_rb, b_rk, u_v, v_bd)]

    s = [s_ref[p] for p in range(N_PAIRS)]
    for c in range(n_chunks):
        idx = [c * N_PAIRS + p for p in range(N_PAIRS)]
        w_last = wl_ref[0, c:c + 1, :]
        s_next = [s[p] * w_last[:, p * PAIR:(p + 1) * PAIR] + _mm(s[p], p_mat[i]) + q_mat[i]
                  for p, i in enumerate(idx)]
        for p, i in enumerate(idx):
            y_ref[c * CHUNK:(c + 1) * CHUNK, p * PAIR:(p + 1) * PAIR] = (
                _mm(r_chk[i], s[p], _NT) + y_v[i])
        s = s_next
    for p in range(N_PAIRS):
        s_ref[p] = s[p]

    seg01 = _seg01()
    y = y_ref[...]
    mean = _head_sum(y, seg01) * (1.0 / RWKV_HEAD)
    d = y - mean
    var = _head_sum(d * d, seg01) * (1.0 / RWKV_HEAD)
    yn = d * lax.rsqrt(var + LNX_EPS) * lnw_ref[...] + lnb_ref[...]
    o_ref[...] = (yn + bonus_ref[...]) * gate_ref[...]


def _scan(prep, lnw, lnb, tb):
    t = prep["rt"].shape[0]
    assert tb // CHUNK <= WL_ROWS
    tile = pl.BlockSpec((tb, RWKV_WIDTH), lambda i: (i, 0))
    vec = _full((1, RWKV_WIDTH))
    return pl.pallas_call(
        functools.partial(_scan_kernel, tb=tb),
        out_shape=jax.ShapeDtypeStruct((t, RWKV_WIDTH), F32),
        grid=(t // tb,),
        in_specs=[tile] * (len(_PREP_F32) + len(_PREP_BF16))
        + [pl.BlockSpec((1, WL_ROWS, RWKV_WIDTH), lambda i: (i, 0, 0)), vec, vec],
        out_specs=tile,
        scratch_shapes=[pltpu.VMEM((N_PAIRS, PAIR, PAIR), F32), pltpu.VMEM((tb, RWKV_WIDTH), F32)],
        compiler_params=_params(),
        name="scan",
    )(*[prep[name] for name in _PREP_NAMES], lnw, lnb)


def _mix_attn_kernel(x_ref, yr_ref, ys_ref, wo1_ref, wo2_ref, g2_ref, wq_ref, k_ref, v_ref, wo_ref,
                     o_ref):
    x1 = (x_ref[...]
          + jnp.dot(yr_ref[...].astype(BF16), wo1_ref[...], preferred_element_type=F32)
          + jnp.dot(ys_ref[...].astype(BF16), wo2_ref[...], preferred_element_type=F32))
    h = _rmsnorm(x1, g2_ref[...]).astype(BF16)
    q = jnp.dot(h, wq_ref[...], preferred_element_type=F32).astype(BF16)
    outs = []
    for hd in range(XA_HEADS):
        lanes = slice(hd * XA_HEAD_DIM, (hd + 1) * XA_HEAD_DIM)
        s = lax.dot_general(q[:, lanes], k_ref[:, lanes], _NT, preferred_element_type=F32)
        s = s * (XA_HEAD_DIM ** -0.5)
        m = jnp.max(s, axis=-1, keepdims=True)
        e = jnp.exp(s - m)
        p = e / jnp.sum(e, axis=-1, keepdims=True)
        outs.append(jnp.dot(p.astype(BF16), v_ref[:, lanes], preferred_element_type=F32))
    o = jnp.concatenate(outs, axis=1).astype(BF16)
    o_ref[...] = x1 + jnp.dot(o, wo_ref[...], preferred_element_type=F32)


def _mix_attn(x, yr, ys, wo1, wo2, g2, wq, k, v, wo, tm):
    t = x.shape[0]
    sq = _full((D_MODEL, D_MODEL))
    half = _full((RWKV_WIDTH, D_MODEL))
    return pl.pallas_call(
        _mix_attn_kernel,
        out_shape=jax.ShapeDtypeStruct((t, D_MODEL), F32),
        grid=(t // tm,),
        in_specs=[pl.BlockSpec((tm, D_MODEL), lambda i: (i, 0)),
                  pl.BlockSpec((tm, RWKV_WIDTH), lambda i: (i, 0)),
                  pl.BlockSpec((tm, SGU_WIDTH), lambda i: (i, 0)),
                  half, half, _full((1, D_MODEL)), sq,
                  _full((MEM_LEN, D_MODEL)), _full((MEM_LEN, D_MODEL)), sq],
        out_specs=pl.BlockSpec((tm, D_MODEL), lambda i: (i, 0)),
        compiler_params=_params("parallel"),
        name="mix_attn",
    )(x, yr, ys, wo1, wo2, g2, wq, k, v, wo)


def _ffn_kernel(x_ref, g3_ref, wg_ref, wu_ref, wd_ref, gf_ref, o_ref):
    x2 = x_ref[...]
    h = _rmsnorm(x2, g3_ref[...]).astype(BF16)
    gate = jnp.dot(h, wg_ref[...], preferred_element_type=F32)
    up = jnp.dot(h, wu_ref[...], preferred_element_type=F32)
    act = (jax.nn.silu(gate) * up).astype(BF16)
    x3 = x2 + jnp.dot(act, wd_ref[...], preferred_element_type=F32)
    o_ref[...] = _rmsnorm(x3, gf_ref[...])


def _ffn(x, g3, wg, wu, wd, gf, tm):
    t = x.shape[0]
    return pl.pallas_call(
        _ffn_kernel,
        out_shape=jax.ShapeDtypeStruct((t, D_MODEL), F32),
        grid=(t // tm,),
        in_specs=[pl.BlockSpec((tm, D_MODEL), lambda i: (i, 0)), _full((1, D_MODEL)),
                  _full((D_MODEL, D_FF)), _full((D_MODEL, D_FF)), _full((D_FF, D_MODEL)),
                  _full((1, D_MODEL))],
        out_specs=pl.BlockSpec((tm, D_MODEL), lambda i: (i, 0)),
        compiler_params=_params("parallel"),
        name="ffn",
    )(x, g3, wg, wu, wd, gf)


def kernel(x, mem, norm1_g, w_in, shift_mu, w0, w_lora_up, a0, a_lora_up, g_lora_up, k_k, k_a, r_k,
           lnx_w, lnx_b, sgu_ln_w, sgu_ln_b, w_spatial, b_spatial, w_out, norm2_g, mem_norm_g,
           wq_x, wk_x, wv_x, wo_x, norm3_g, w_gate, w_up, w_down, norm_f_g):
    b, t, _ = x.shape
    depth = w_in.shape[0]
    assert depth == 1, "the final RMSNorm is fused into the (single) layer's ffn call"
    assert t % TM_DENSE == 0 and TM_DENSE % TB_SCAN == 0 and TB_SCAN % CHUNK == 0
    row = lambda p: p.reshape(1, -1)
    bf = lambda p: p.astype(BF16)
    outs = []
    for bi in range(b):
        xb = x[bi]
        for l in range(depth):
            lora = w_lora_up.shape[1]
            zeros = jnp.zeros((lora, RWKV_WIDTH), F32)
            waup = jnp.concatenate(
                [jnp.concatenate([w_lora_up[l], zeros], axis=1),
                 jnp.concatenate([zeros, a_lora_up[l]], axis=1)], axis=0)
            bias = jnp.repeat(b_spatial[l].T, SGU_WIDTH // SGU_GROUPS, axis=1)

            front = _front(xb, row(norm1_g[l]), bf(w_in[l]), row(shift_mu[l]), row(w0[l]), bf(waup),
                           row(a0[l]), bf(g_lora_up[l]), row(k_k[l]), row(k_a[l]), row(r_k[l]),
                           row(sgu_ln_w[l]), row(sgu_ln_b[l]), w_spatial[l], bias, TM_DENSE)
            prep = dict(zip(_PREP_NAMES, front[:len(_PREP_NAMES)]))
            y_sgu = front[len(_PREP_NAMES)]
            y_rwkv = _scan(prep, row(lnx_w[l]), row(lnx_b[l]), TB_SCAN)
            k_mem, v_mem = _mem_kv(mem[bi], row(mem_norm_g[l]), bf(wk_x[l]), bf(wv_x[l]))
            x2 = _mix_attn(xb, y_rwkv, y_sgu, bf(w_out[l][:RWKV_WIDTH]), bf(w_out[l][RWKV_WIDTH:]),
                           row(norm2_g[l]), bf(wq_x[l]), k_mem, v_mem, bf(wo_x[l]), TM_DENSE)
            xb = _ffn(x2, row(norm3_g[l]), bf(w_gate[l]), bf(w_up[l]), bf(w_down[l]),
                      row(norm_f_g), TM_DENSE)
        outs.append(xb)
    return jnp.stack(outs, axis=0)
```

```python
import functools

import jax
import jax.numpy as jnp
from jax import lax
from jax.experimental import pallas as pl
from jax.experimental.pallas import tpu as pltpu

F32 = jnp.float32
BF16 = jnp.bfloat16

D_MODEL = 1024
RWKV_WIDTH = 512
RWKV_HEAD = 64
LORA_WA = 128
GATE_LORA = 128
RWKV_IN = 3 * RWKV_WIDTH + LORA_WA + GATE_LORA
SGU_WIDTH = 512
SGU_GROUPS = 8
SGU_BLOCK = 128
IN_WIDTH = RWKV_IN + 2 * SGU_WIDTH
MEM_LEN = 256
XA_HEADS = 4
XA_HEAD_DIM = D_MODEL // XA_HEADS
D_FF = 2816
RMS_EPS = 1e-6
LN_EPS = 1e-5
LNX_EPS = 64e-5
EXP_M05 = 0.6065306597126334

CHUNK = 64
PAIR = 2 * RWKV_HEAD
N_PAIRS = RWKV_WIDTH // PAIR
TM_DENSE = 512
TB_SCAN = 512
SCAN_GROUP_CHUNKS = 8
WL_ROWS = 8
IN_PROJ_COLS = 256
PREP_STREAMS = 2
PREP_PIECES_PER_CHUNK = 10
TERMS_DECAY_CUMSUM = 2
TERMS_HEAD_SUM = 1
VMEM_LIMIT = 56 * 1024 * 1024

_NN = (((1,), (0,)), ((), ()))
_NT = (((1,), (1,)), ((), ()))
_TN = (((0,), (0,)), ((), ()))


def _mm(a, b, dims=_NN):
    return lax.dot_general(a.astype(BF16), b.astype(BF16), dims, preferred_element_type=F32)


def _split_bf16(x, terms):
    parts = []
    rem = x
    for _ in range(terms):
        part = rem.astype(BF16)
        rem = rem - part.astype(F32)
        parts.append(part)
    return parts


def _cumsum_rows(ltri01, parts):
    acc = None
    for part in parts:
        d = lax.dot_general(ltri01, part, _NN, preferred_element_type=F32)
        acc = d if acc is None else acc + d
    return acc


def _head_sum_parts(parts, seg01):
    cols = []
    for q in range(parts[0].shape[1] // 256):
        acc = None
        for part in parts:
            d = lax.dot_general(part[:, 256 * q:256 * (q + 1)], seg01, _NN,
                                preferred_element_type=F32)
            acc = d if acc is None else acc + d
        cols.append(acc)
    return jnp.concatenate(cols, axis=1)


def _head_sum(x, seg01):
    return _head_sum_parts(_split_bf16(x, TERMS_HEAD_SUM), seg01)


def _seg01():
    li = lax.broadcasted_iota(jnp.int32, (256, 256), 0) >> 6
    lj = lax.broadcasted_iota(jnp.int32, (256, 256), 1) >> 6
    return (li == lj).astype(BF16)


def _rmsnorm(x, g):
    return x * lax.rsqrt(jnp.mean(x * x, axis=-1, keepdims=True) + RMS_EPS) * g


def _full(shape):
    n = len(shape)
    return pl.BlockSpec(shape, lambda i: (0,) * n)


def _params(sem="arbitrary"):
    return pltpu.CompilerParams(dimension_semantics=(sem,), vmem_limit_bytes=VMEM_LIMIT)


def _run_interleaved(stages, rounds):
    for r in range(rounds):
        for gen, n in stages:
            for _ in range((r + 1) * n // rounds - r * n // rounds):
                next(gen, None)
    for gen, _ in stages:
        assert next(gen, StopIteration) is StopIteration, "piece count too small"


def _mem_kv_kernel(mem_ref, g_ref, wk_ref, wv_ref, k_ref, v_ref):
    m = _rmsnorm(mem_ref[...], g_ref[...]).astype(BF16)
    k_ref[...] = jnp.dot(m, wk_ref[...], preferred_element_type=F32).astype(BF16)
    v_ref[...] = jnp.dot(m, wv_ref[...], preferred_element_type=F32).astype(BF16)


def _mem_kv(mem, g, wk, wv):
    return pl.pallas_call(
        _mem_kv_kernel,
        out_shape=(jax.ShapeDtypeStruct((MEM_LEN, D_MODEL), BF16),) * 2,
        grid=(1,),
        in_specs=[_full((MEM_LEN, D_MODEL)), _full((1, D_MODEL)),
                  _full((D_MODEL, D_MODEL)), _full((D_MODEL, D_MODEL))],
        out_specs=(_full((MEM_LEN, D_MODEL)),) * 2,
        compiler_params=_params(),
        name="mem_kv",
    )(mem, g, wk, wv)


_PREP_F32 = ("rt", "at", "gate", "bonus")
_PREP_BF16 = ("bh", "kh", "bc", "kc", "v")
_PREP_NAMES = _PREP_F32 + _PREP_BF16 + ("wl",)


def _in_proj_pieces(x_ref, g_ref, w_ref, z_ref):
    h = _rmsnorm(x_ref[...], g_ref[...]).astype(BF16)
    yield
    for j in range(IN_WIDTH // IN_PROJ_COLS):
        cols = slice(j * IN_PROJ_COLS, (j + 1) * IN_PROJ_COLS)
        z_ref[:, cols] = jnp.dot(h, w_ref[:, cols], preferred_element_type=F32)
        yield


def _rwkv_prep_pieces(z_ref, mu_ref, w0_ref, waup_ref, a0_ref, gup_ref, kk_ref, ka_ref, rk_ref,
                      carry_ref, prep, chunks):
    seg01 = _seg01()
    lane = lax.broadcasted_iota(jnp.int32, (1, LORA_WA), 1)
    row = lax.broadcasted_iota(jnp.int32, (CHUNK, 1), 0)
    ti = lax.broadcasted_iota(jnp.int32, (CHUNK, CHUNK), 0)
    tj = lax.broadcasted_iota(jnp.int32, (CHUNK, CHUNK), 1)
    ltri01 = (tj <= ti).astype(BF16)
    chunks_per_scan_tile = TB_SCAN // CHUNK
    half_w = RWKV_WIDTH // 2

    def shifted(c, cols):
        z = z_ref[c * CHUNK:(c + 1) * CHUNK, cols]
        before = carry_ref[:, cols] if c == 0 else z_ref[c * CHUNK - 1:c * CHUNK, cols]
        zprev = jnp.where(row == 0, before, pltpu.roll(z, 1, axis=0))
        return z + (zprev - z) * mu_ref[:, cols]

    for c in chunks:
        rows = slice(c * CHUNK, (c + 1) * CHUNK)
        wa_in = shifted(c, slice(3 * RWKV_WIDTH, 3 * RWKV_WIDTH + LORA_WA))
        wa_in = jnp.where(lane < LORA_WA // 2, jnp.tanh(wa_in), wa_in).astype(BF16)
        gd = jax.nn.sigmoid(shifted(c, slice(3 * RWKV_WIDTH + LORA_WA, RWKV_IN))).astype(BF16)
        for q in range(2):
            hc = slice(q * half_w, (q + 1) * half_w)
            r = shifted(c, hc)
            k = shifted(c, slice(RWKV_WIDTH + q * half_w, RWKV_WIDTH + (q + 1) * half_w))
            v = shifted(c, slice(2 * RWKV_WIDTH + q * half_w, 2 * RWKV_WIDTH + (q + 1) * half_w))
            prep["v"][rows, hc] = v.astype(BF16)
            kk = k * kk_ref[:, hc]
            kk_sq = _split_bf16(kk * kk, TERMS_HEAD_SUM)
            yield
            w_pre = w0_ref[:, hc] + jnp.dot(wa_in, waup_ref[:, hc], preferred_element_type=F32)
            a_pre = a0_ref[:, hc] + jnp.dot(
                wa_in, waup_ref[:, RWKV_WIDTH + q * half_w:RWKV_WIDTH + (q + 1) * half_w],
                preferred_element_type=F32)
            prep["gate"][rows, hc] = jnp.dot(gd, gup_ref[:, hc], preferred_element_type=F32)
            kk_ss = _head_sum_parts(kk_sq, seg01)
            yield
            a = jax.nn.sigmoid(a_pre)
            lw = jax.nn.sigmoid(w_pre) * (-EXP_M05)
            lw_parts = _split_bf16(lw, TERMS_DECAY_CUMSUM)
            kk = kk * lax.rsqrt(jnp.maximum(kk_ss, 1e-24))
            kmod = k * (1.0 + (a - 1.0) * ka_ref[:, hc])
            kka = kk * a
            rkk = _split_bf16(r * kmod * rk_ref[:, hc], TERMS_HEAD_SUM)
            yield
            cs = _cumsum_rows(ltri01, lw_parts)
            prep["bonus"][rows, hc] = _head_sum_parts(rkk, seg01) * v
            yield
            w_last = jnp.exp(cs[CHUNK - 1:CHUNK, :])
            w_inv = jnp.exp(-cs)
            w_tail = w_last * w_inv
            prep["rt"][rows, hc] = r * jnp.exp(cs)
            prep["at"][rows, hc] = -kk * jnp.exp(cs - lw)
            prep["bh"][rows, hc] = (kka * w_inv).astype(BF16)
            prep["kh"][rows, hc] = (kmod * w_inv).astype(BF16)
            prep["bc"][rows, hc] = (kka * w_tail).astype(BF16)
            prep["kc"][rows, hc] = (kmod * w_tail).astype(BF16)
            cq, cr = divmod(c, chunks_per_scan_tile)
            prep["wl"][cq, cr:cr + 1, hc] = w_last
            yield


def _sgu_pieces(z_ref, lnw_ref, lnb_ref, wcat_ref, bias_ref, o_ref, *, tm):
    bi = lax.broadcasted_iota(jnp.int32, (2 * SGU_BLOCK, PAIR), 0) >> 7
    bj = lax.broadcasted_iota(jnp.int32, (2 * SGU_BLOCK, PAIR), 1) >> 6
    sel = bi == bj
    for b in range(tm // SGU_BLOCK):
        rows = slice(b * SGU_BLOCK, (b + 1) * SGU_BLOCK)
        hz = jax.nn.gelu(z_ref[rows, RWKV_IN:])
        u = hz[:, :SGU_WIDTH]
        vf = hz[:, SGU_WIDTH:]
        mu = jnp.mean(vf, axis=-1, keepdims=True)
        d = vf - mu
        var = jnp.mean(d * d, axis=-1, keepdims=True)
        vn = d * lax.rsqrt(var + LN_EPS) * lnw_ref[...] + lnb_ref[...]
        stacks = []
        for p in range(SGU_WIDTH // PAIR):
            vb = vn[:, p * PAIR:(p + 1) * PAIR]
            stacks.append(jnp.where(sel, jnp.concatenate([vb, vb], axis=0), 0.0).astype(BF16))
        yield
        for p in range(SGU_WIDTH // PAIR):
            lanes = slice(p * PAIR, (p + 1) * PAIR)
            mixed = jnp.dot(wcat_ref[p], stacks[p], preferred_element_type=F32) + bias_ref[:, lanes]
            o_ref[rows, lanes] = u[:, lanes] * mixed
        yield


def _front_kernel(x_ref, g1_ref, win_ref, mu_ref, w0_ref, waup_ref, a0_ref, gup_ref, kk_ref, ka_ref,
                  rk_ref, slnw_ref, slnb_ref, ws_ref, sbias_ref, *rest, tm):
    n = len(_PREP_NAMES)
    prep = dict(zip(_PREP_NAMES, rest[:n]))
    ysgu_ref = rest[n]
    carry_ref, wcat_ref, z0_ref, z1_ref = rest[n + 1:]
    i = pl.program_id(0)

    @pl.when(i == 0)
    def _():
        carry_ref[...] = jnp.zeros_like(carry_ref)
        z1_ref[...] = jnp.zeros_like(z1_ref)
        ti = lax.broadcasted_iota(jnp.int32, (SGU_BLOCK, SGU_BLOCK), 0)
        tj = lax.broadcasted_iota(jnp.int32, (SGU_BLOCK, SGU_BLOCK), 1)
        tril = tj <= ti
        for p in range(SGU_WIDTH // PAIR):
            wcat_ref[p] = jnp.concatenate(
                [jnp.where(tril, ws_ref[2 * p], 0.0), jnp.where(tril, ws_ref[2 * p + 1], 0.0)],
                axis=1).astype(BF16)

    def step(z_write, z_read):
        n_chunks = tm // CHUNK
        prep_pieces = [
            _rwkv_prep_pieces(z_read, mu_ref, w0_ref, waup_ref, a0_ref, gup_ref, kk_ref, ka_ref,
                              rk_ref, carry_ref, prep, range(k, n_chunks, PREP_STREAMS))
            for k in range(PREP_STREAMS)]
        n_dot = 1 + IN_WIDTH // IN_PROJ_COLS + 1
        _run_interleaved(
            [(_in_proj_pieces(x_ref, g1_ref, win_ref, z_write), n_dot)]
            + [(gen, PREP_PIECES_PER_CHUNK * n_chunks // PREP_STREAMS + 1) for gen in prep_pieces]
            + [(_sgu_pieces(z_read, slnw_ref, slnb_ref, wcat_ref, sbias_ref, ysgu_ref, tm=tm),
                2 * tm // SGU_BLOCK + 1)],
            rounds=n_dot)
        if TB_SCAN // CHUNK < WL_ROWS:
            prep["wl"][:, TB_SCAN // CHUNK:, :] = jnp.zeros(
                (tm // TB_SCAN, WL_ROWS - TB_SCAN // CHUNK, RWKV_WIDTH), F32)
        carry_ref[...] = z_read[tm - 1:tm, :RWKV_IN]

    @pl.when((i & 1) == 0)
    def _():
        step(z0_ref, z1_ref)

    @pl.when((i & 1) == 1)
    def _():
        step(z1_ref, z0_ref)


def _front(x, g1, w_in, mu, w0, waup, a0, gup, k_k, k_a, r_k, slnw, slnb, ws, sbias, tm):
    t = x.shape[0]
    n_tiles = t // tm
    vec = _full((1, RWKV_WIDTH))
    out_tile = lambda i: (jnp.maximum(i - 1, 0), 0)
    out_shapes = ([jax.ShapeDtypeStruct((t, RWKV_WIDTH), F32)] * len(_PREP_F32)
                  + [jax.ShapeDtypeStruct((t, RWKV_WIDTH), BF16)] * len(_PREP_BF16)
                  + [jax.ShapeDtypeStruct((t // TB_SCAN, WL_ROWS, RWKV_WIDTH), F32),
                     jax.ShapeDtypeStruct((t, SGU_WIDTH), F32)])
    out_specs = ([pl.BlockSpec((tm, RWKV_WIDTH), out_tile)] * (len(_PREP_F32) + len(_PREP_BF16))
                 + [pl.BlockSpec((tm // TB_SCAN, WL_ROWS, RWKV_WIDTH),
                                 lambda i: (jnp.maximum(i - 1, 0), 0, 0)),
                    pl.BlockSpec((tm, SGU_WIDTH), out_tile)])
    return pl.pallas_call(
        functools.partial(_front_kernel, tm=tm),
        out_shape=tuple(out_shapes),
        grid=(n_tiles + 1,),
        in_specs=[pl.BlockSpec((tm, D_MODEL), lambda i: (jnp.minimum(i, n_tiles - 1), 0)),
                  _full((1, D_MODEL)), _full((D_MODEL, IN_WIDTH)), _full((1, RWKV_IN)), vec,
                  _full((LORA_WA, 2 * RWKV_WIDTH)), vec, _full((GATE_LORA, RWKV_WIDTH)),
                  vec, vec, vec, _full((1, SGU_WIDTH)), _full((1, SGU_WIDTH)),
                  _full((SGU_GROUPS, SGU_BLOCK, SGU_BLOCK)), _full((SGU_BLOCK, SGU_WIDTH))],
        out_specs=tuple(out_specs),
        scratch_shapes=[pltpu.VMEM((1, RWKV_IN), F32),
                        pltpu.VMEM((SGU_WIDTH // PAIR, SGU_BLOCK, 2 * SGU_BLOCK), BF16),
                        pltpu.VMEM((tm, IN_WIDTH), F32), pltpu.VMEM((tm, IN_WIDTH), F32)],
        compiler_params=_params(),
        name="front",
    )(x, g1, w_in, mu, w0, waup, a0, gup, k_k, k_a, r_k, slnw, slnb, ws, sbias)


def _pair_masks():
    t = lax.broadcasted_iota(jnp.int32, (CHUNK, PAIR), 0)
    j = lax.broadcasted_iota(jnp.int32, (CHUNK, PAIR), 1) & (CHUNK - 1)
    strict = j < t
    incl = j <= t
    blk16 = (t >> 4) == (j >> 4)
    blk32 = (t >> 5) == (j >> 5)
    return strict, incl, blk16, blk32


def _bd(x, bd_mask):
    x = x.astype(BF16)
    return jnp.where(bd_mask, jnp.concatenate([x, x], axis=0), 0.0).astype(BF16)


def _unit_lower_inverse_minus_identity(a_list, masks, bd_mask):
    _, _, blk16, blk32 = masks
    ad = [jnp.where(blk16, a, 0.0) for a in a_list]
    ap = [_mm(x, _bd(x, bd_mask)) for x in ad]
    tp = ad
    for _ in range(2):
        both = [_mm(p, jnp.concatenate([_bd(p, bd_mask), _bd(t, bd_mask)], axis=1))
                for p, t in zip(ap, tp)]
        tp = [t + p + b[:, PAIR:] for t, p, b in zip(tp, ap, both)]
        ap = [b[:, :PAIR] for b in both]
    last = [_mm(p, _bd(t, bd_mask)) for p, t in zip(ap, tp)]
    tp = [t + p + x for t, p, x in zip(tp, ap, last)]
    for off_mask in (blk32 & ~blk16, ~blk32):
        off = [jnp.where(off_mask, a, 0.0) for a in a_list]
        x = [o + _mm(t, _bd(o, bd_mask)) for o, t in zip(off, tp)]
        tp = [t + xx + _mm(xx, _bd(t, bd_mask)) for t, xx in zip(tp, x)]
    return tp


def _scan_kernel(rt_ref, at_ref, gate_ref, bonus_ref, bh_ref, kh_ref, bc_ref, kc_ref, v_ref, wl_ref,
                 lnw_ref, lnb_ref, o_ref, s_ref, y_ref, *, tb):
    @pl.when(pl.program_id(0) == 0)
    def _():
        s_ref[...] = jnp.zeros_like(s_ref)

    masks = _pair_masks()
    strict, incl = masks[0], masks[1]
    bi = lax.broadcasted_iota(jnp.int32, (PAIR, PAIR), 0) >> 6
    bj = lax.broadcasted_iota(jnp.int32, (PAIR, PAIR), 1) >> 6
    bd1 = bi == bj
    bd2 = jnp.concatenate([bd1, bd1], axis=1)

    n_chunks = tb // CHUNK

    def chunk_terms(probs):
        cut = lambda ref: [ref[c * CHUNK:(c + 1) * CHUNK, p * PAIR:(p + 1) * PAIR]
                           for c, p in probs]
        rt_p, at_p, bh_p, kh_p, bc_p, kc_p, v_p = map(
            cut, (rt_ref, at_ref, bh_ref, kh_ref, bc_ref, kc_ref, v_ref))
        gram = [_mm(jnp.concatenate([a_, r_], axis=0),
                    jnp.concatenate([_bd(b_, bd1), _bd(k_, bd1)], axis=0), _NT)
                for a_, r_, b_, k_ in zip(at_p, rt_p, bh_p, kh_p)]
        a_ab = [jnp.where(strict, g_[:CHUNK, :PAIR], 0.0) for g_ in gram]
        a_ak = [jnp.where(strict, g_[:CHUNK, PAIR:], 0.0) for g_ in gram]
        b_rb = [jnp.where(incl, g_[CHUNK:, :PAIR], 0.0) for g_ in gram]
        b_rk = [jnp.where(incl, g_[CHUNK:, PAIR:], 0.0) for g_ in gram]
        v_bd = [_bd(x_, bd1) for x_ in v_p]
        rhs = [jnp.concatenate([_mm(m_, x_), a_], axis=1) for m_, x_, a_ in zip(a_ak, v_bd, at_p)]
        tp = _unit_lower_inverse_minus_identity(a_ab, masks, bd1)
        sol = [x_ + _mm(t_, _bd(x_, bd2)) for x_, t_ in zip(rhs, tp)]
        u_v = [x_[:, :PAIR] for x_ in sol]
        a_chk = [x_[:, PAIR:] for x_ in sol]
        p_mat = [jnp.where(bd1, _mm(a_, b_, _TN), 0.0) for a_, b_ in zip(a_chk, bc_p)]
        q_mat = [jnp.where(bd1, _mm(jnp.concatenate([u_.astype(BF16), x_], axis=0),
                                    jnp.concatenate([b_, k_], axis=0), _TN), 0.0)
                 for u_, x_, b_, k_ in zip(u_v, v_p, bc_p, kc_p)]
        r_chk = [r_ + _mm(m_, _bd(a_, bd1)) for r_, m_, a_ in zip(rt_p, b_rb, a_chk)]
        y_v = [_mm(jnp.concatenate([m1, m2], axis=1), jnp.concatenate([_bd(u_, bd1), x_], axis=0))
               for m1, m2, u_, x_ in zip(b_rb, b_rk, u_v, v_bd)]
        return p_mat, q_mat, r_chk, y_v

    p_mat, q_mat, r_chk, y_v = [], [], [], []
    for c0 in range(0, n_chunks, SCAN_GROUP_CHUNKS):
        group = [(c, p) for c in range(c0, c0 + SCAN_GROUP_CHUNKS) for p in range(N_PAIRS)]
        for acc, part in zip((p_mat, q_mat, r_chk, y_v), chunk_terms(group)):
            acc.extend(part)

    s = [s_ref[p] for p in range(N_PAIRS)]
    for c in range(n_chunks):
        idx = [c * N_PAIRS + p for p in range(N_PAIRS)]
        w_last = wl_ref[0, c:c + 1, :]
        s_next = [s[p] * w_last[:, p * PAIR:(p + 1) * PAIR] + _mm(s[p], p_mat[i]) + q_mat[i]
                  for p, i in enumerate(idx)]
        for p, i in enumerate(idx):
            y_ref[c * CHUNK:(c + 1) * CHUNK, p * PAIR:(p + 1) * PAIR] = (
                _mm(r_chk[i], s[p], _NT) + y_v[i])
        s = s_next
    for p in range(N_PAIRS):
        s_ref[p] = s[p]

    seg01 = _seg01()
    y = y_ref[...]
    mean = _head_sum(y, seg01) * (1.0 / RWKV_HEAD)
    d = y - mean
    var = _head_sum(d * d, seg01) * (1.0 / RWKV_HEAD)
    yn = d * lax.rsqrt(var + LNX_EPS) * lnw_ref[...] + lnb_ref[...]
    o_ref[...] = (yn + bonus_ref[...]) * gate_ref[...]


def _scan(prep, lnw, lnb, tb):
    t = prep["rt"].shape[0]
    assert tb // CHUNK <= WL_ROWS
    tile = pl.BlockSpec((tb, RWKV_WIDTH), lambda i: (i, 0))
    vec = _full((1, RWKV_WIDTH))
    return pl.pallas_call(
        functools.partial(_scan_kernel, tb=tb),
        out_shape=jax.ShapeDtypeStruct((t, RWKV_WIDTH), F32),
        grid=(t // tb,),
        in_specs=[tile] * (len(_PREP_F32) + len(_PREP_BF16))
        + [pl.BlockSpec((1, WL_ROWS, RWKV_WIDTH), lambda i: (i, 0, 0)), vec, vec],
        out_specs=tile,
        scratch_shapes=[pltpu.VMEM((N_PAIRS, PAIR, PAIR), F32), pltpu.VMEM((tb, RWKV_WIDTH), F32)],
        compiler_params=_params(),
        name="scan",
    )(*[prep[name] for name in _PREP_NAMES], lnw, lnb)


def _mix_attn_kernel(x_ref, yr_ref, ys_ref, wo1_ref, wo2_ref, g2_ref, wq_ref, k_ref, v_ref, wo_ref,
                     o_ref):
    x1 = (x_ref[...]
          + jnp.dot(yr_ref[...].astype(BF16), wo1_ref[...], preferred_element_type=F32)
          + jnp.dot(ys_ref[...].astype(BF16), wo2_ref[...], preferred_element_type=F32))
    h = _rmsnorm(x1, g2_ref[...]).astype(BF16)
    q = jnp.dot(h, wq_ref[...], preferred_element_type=F32).astype(BF16)
    outs = []
    for hd in range(XA_HEADS):
        lanes = slice(hd * XA_HEAD_DIM, (hd + 1) * XA_HEAD_DIM)
        s = lax.dot_general(q[:, lanes], k_ref[:, lanes], _NT, preferred_element_type=F32)
        s = s * (XA_HEAD_DIM ** -0.5)
        m = jnp.max(s, axis=-1, keepdims=True)
        e = jnp.exp(s - m)
        p = e / jnp.sum(e, axis=-1, keepdims=True)
        outs.append(jnp.dot(p.astype(BF16), v_ref[:, lanes], preferred_element_type=F32))
    o = jnp.concatenate(outs, axis=1).astype(BF16)
    o_ref[...] = x1 + jnp.dot(o, wo_ref[...], preferred_element_type=F32)


def _mix_attn(x, yr, ys, wo1, wo2, g2, wq, k, v, wo, tm):
    t = x.shape[0]
    sq = _full((D_MODEL, D_MODEL))
    half = _full((RWKV_WIDTH, D_MODEL))
    return pl.pallas_call(
        _mix_attn_kernel,
        out_shape=jax.ShapeDtypeStruct((t, D_MODEL), F32),
        grid=(t // tm,),
        in_specs=[pl.BlockSpec((tm, D_MODEL), lambda i: (i, 0)),
                  pl.BlockSpec((tm, RWKV_WIDTH), lambda i: (i, 0)),
                  pl.BlockSpec((tm, SGU_WIDTH), lambda i: (i, 0)),
                  half, half, _full((1, D_MODEL)), sq,
                  _full((MEM_LEN, D_MODEL)), _full((MEM_LEN, D_MODEL)), sq],
        out_specs=pl.BlockSpec((tm, D_MODEL), lambda i: (i, 0)),
        compiler_params=_params("parallel"),
        name="mix_attn",
    )(x, yr, ys, wo1, wo2, g2, wq, k, v, wo)


def _ffn_kernel(x_ref, g3_ref, wg_ref, wu_ref, wd_ref, gf_ref, o_ref):
    x2 = x_ref[...]
    h = _rmsnorm(x2, g3_ref[...]).astype(BF16)
    gate = jnp.dot(h, wg_ref[...], preferred_element_type=F32)
    up = jnp.dot(h, wu_ref[...], preferred_element_type=F32)
    act = (jax.nn.silu(gate) * up).astype(BF16)
    x3 = x2 + jnp.dot(act, wd_ref[...], preferred_element_type=F32)
    o_ref[...] = _rmsnorm(x3, gf_ref[...])


def _ffn(x, g3, wg, wu, wd, gf, tm):
    t = x.shape[0]
    return pl.pallas_call(
        _ffn_kernel,
        out_shape=jax.ShapeDtypeStruct((t, D_MODEL), F32),
        grid=(t // tm,),
        in_specs=[pl.BlockSpec((tm, D_MODEL), lambda i: (i, 0)), _full((1, D_MODEL)),
                  _full((D_MODEL, D_FF)), _full((D_MODEL, D_FF)), _full((D_FF, D_MODEL)),
                  _full((1, D_MODEL))],
        out_specs=pl.BlockSpec((tm, D_MODEL), lambda i: (i, 0)),
        compiler_params=_params("parallel"),
        name="ffn",
    )(x, g3, wg, wu, wd, gf)


def kernel(x, mem, norm1_g, w_in, shift_mu, w0, w_lora_up, a0, a_lora_up, g_lora_up, k_k, k_a, r_k,
           lnx_w, lnx_b, sgu_ln_w, sgu_ln_b, w_spatial, b_spatial, w_out, norm2_g, mem_norm_g,
           wq_x, wk_x, wv_x, wo_x, norm3_g, w_gate, w_up, w_down, norm_f_g):
    b, t, _ = x.shape
    depth = w_in.shape[0]
    assert depth == 1, "the final RMSNorm is fused into the (single) layer's ffn call"
    assert t % TM_DENSE == 0 and TM_DENSE % TB_SCAN == 0 and TB_SCAN % CHUNK == 0
    row = lambda p: p.reshape(1, -1)
    bf = lambda p: p.astype(BF16)
    outs = []
    for bi in range(b):
        xb = x[bi]
        for l in range(depth):
            lora = w_lora_up.shape[1]
            zeros = jnp.zeros((lora, RWKV_WIDTH), F32)
            waup = jnp.concatenate(
                [jnp.concatenate([w_lora_up[l], zeros], axis=1),
                 jnp.concatenate([zeros, a_lora_up[l]], axis=1)], axis=0)
            bias = jnp.repeat(b_spatial[l].T, SGU_WIDTH // SGU_GROUPS, axis=1)

            front = _front(xb, row(norm1_g[l]), bf(w_in[l]), row(shift_mu[l]), row(w0[l]), bf(waup),
                           row(a0[l]), bf(g_lora_up[l]), row(k_k[l]), row(k_a[l]), row(r_k[l]),
                           row(sgu_ln_w[l]), row(sgu_ln_b[l]), w_spatial[l], bias, TM_DENSE)
            prep = dict(zip(_PREP_NAMES, front[:len(_PREP_NAMES)]))
            y_sgu = front[len(_PREP_NAMES)]
            y_rwkv = _scan(prep, row(lnx_w[l]), row(lnx_b[l]), TB_SCAN)
            k_mem, v_mem = _mem_kv(mem[bi], row(mem_norm_g[l]), bf(wk_x[l]), bf(wv_x[l]))
            x2 = _mix_attn(xb, y_rwkv, y_sgu, bf(w_out[l][:RWKV_WIDTH]), bf(w_out[l][RWKV_WIDTH:]),
                           row(norm2_g[l]), bf(wq_x[l]), k_mem, v_mem, bf(wo_x[l]), TM_DENSE)
            xb = _ffn(x2, row(norm3_g[l]), bf(w_gate[l]), bf(w_up[l]), bf(w_down[l]),
                      row(norm_f_g), TM_DENSE)
        outs.append(xb)
    return jnp.stack(outs, axis=0)
```

```python
import functools

import jax
import jax.numpy as jnp
from jax import lax
from jax.experimental import pallas as pl
from jax.experimental.pallas import tpu as pltpu

F32 = jnp.float32
BF16 = jnp.bfloat16

D_MODEL = 1024
RWKV_WIDTH = 512
RWKV_HEAD = 64
LORA_WA = 128
GATE_LORA = 128
RWKV_IN = 3 * RWKV_WIDTH + LORA_WA + GATE_LORA
SGU_WIDTH = 512
SGU_GROUPS = 8
SGU_BLOCK = 128
IN_WIDTH = RWKV_IN + 2 * SGU_WIDTH
MEM_LEN = 256
XA_HEADS = 4
XA_HEAD_DIM = D_MODEL // XA_HEADS
D_FF = 2816
RMS_EPS = 1e-6
LN_EPS = 1e-5
LNX_EPS = 64e-5
EXP_M05 = 0.6065306597126334

CHUNK = 64
PAIR = 2 * RWKV_HEAD
N_PAIRS = RWKV_WIDTH // PAIR
TM_DENSE = 512
TM_ATTN = 1024
TM_FFN = 1024
TB_SCAN = 512
SCAN_GROUP_CHUNKS = 8
WL_ROWS = 8
ATTN_ROW_GROUP = 256
FFN_ROW_GROUP = 256
IN_PROJ_COLS = 256
PREP_STREAMS = 8
PREP_PIECES_PER_CHUNK = 10
TERMS_DECAY_CUMSUM = 2
TERMS_HEAD_SUM = 1
VMEM_LIMIT = 56 * 1024 * 1024

_NN = (((1,), (0,)), ((), ()))
_NT = (((1,), (1,)), ((), ()))
_TN = (((0,), (0,)), ((), ()))


def _mm(a, b, dims=_NN):
    return lax.dot_general(a.astype(BF16), b.astype(BF16), dims, preferred_element_type=F32)


def _split_bf16(x, terms):
    parts = []
    rem = x
    for _ in range(terms):
        part = rem.astype(BF16)
        rem = rem - part.astype(F32)
        parts.append(part)
    return parts


def _cumsum_rows(ltri01, parts):
    acc = None
    for part in parts:
        d = lax.dot_general(ltri01, part, _NN, preferred_element_type=F32)
        acc = d if acc is None else acc + d
    return acc


def _head_sum_parts(parts, seg01):
    cols = []
    for q in range(parts[0].shape[1] // 256):
        acc = None
        for part in parts:
            d = lax.dot_general(part[:, 256 * q:256 * (q + 1)], seg01, _NN,
                                preferred_element_type=F32)
            acc = d if acc is None else acc + d
        cols.append(acc)
    return jnp.concatenate(cols, axis=1)


def _head_sum(x, seg01):
    return _head_sum_parts(_split_bf16(x, TERMS_HEAD_SUM), seg01)


def _seg01():
    li = lax.broadcasted_iota(jnp.int32, (256, 256), 0) >> 6
    lj = lax.broadcasted_iota(jnp.int32, (256, 256), 1) >> 6
    return (li == lj).astype(BF16)


def _rmsnorm(x, g):
    return x * lax.rsqrt(jnp.mean(x * x, axis=-1, keepdims=True) + RMS_EPS) * g


def _full(shape):
    n = len(shape)
    return pl.BlockSpec(shape, lambda i: (0,) * n, pipeline_mode=pl.Buffered(1))


def _params(sem="arbitrary"):
    return pltpu.CompilerParams(dimension_semantics=(sem,), vmem_limit_bytes=VMEM_LIMIT)


def _run_interleaved(stages, rounds):
    for r in range(rounds):
        for gen, n in stages:
            for _ in range((r + 1) * n // rounds - r * n // rounds):
                next(gen, None)
    for gen, _ in stages:
        assert next(gen, StopIteration) is StopIteration, "piece count too small"


def _mem_kv_kernel(mem_ref, g_ref, wk_ref, wv_ref, k_ref, v_ref):
    m = _rmsnorm(mem_ref[...], g_ref[...]).astype(BF16)
    k_ref[...] = jnp.dot(m, wk_ref[...], preferred_element_type=F32).astype(BF16)
    v_ref[...] = jnp.dot(m, wv_ref[...], preferred_element_type=F32).astype(BF16)


def _mem_kv(mem, g, wk, wv):
    return pl.pallas_call(
        _mem_kv_kernel,
        out_shape=(jax.ShapeDtypeStruct((MEM_LEN, D_MODEL), BF16),) * 2,
        grid=(1,),
        in_specs=[_full((MEM_LEN, D_MODEL)), _full((1, D_MODEL)),
                  _full((D_MODEL, D_MODEL)), _full((D_MODEL, D_MODEL))],
        out_specs=(_full((MEM_LEN, D_MODEL)),) * 2,
        compiler_params=_params(),
        name="mem_kv",
    )(mem, g, wk, wv)


_PREP_F32 = ("rt", "at", "gate", "bonus")
_PREP_BF16 = ("bh", "kh", "bc", "kc", "v")
_PREP_NAMES = _PREP_F32 + _PREP_BF16 + ("wl",)


def _in_proj_pieces(x_ref, g_ref, w_ref, z_ref):
    h = _rmsnorm(x_ref[...], g_ref[...]).astype(BF16)
    yield
    for j in range(IN_WIDTH // IN_PROJ_COLS):
        cols = slice(j * IN_PROJ_COLS, (j + 1) * IN_PROJ_COLS)
        z_ref[:, cols] = jnp.dot(h, w_ref[:, cols], preferred_element_type=F32)
        yield


def _rwkv_prep_pieces(z_ref, mu_ref, w0_ref, waup_ref, a0_ref, gup_ref, kk_ref, ka_ref, rk_ref,
                      carry_ref, prep, chunks):
    seg01 = _seg01()
    lane = lax.broadcasted_iota(jnp.int32, (1, LORA_WA), 1)
    row = lax.broadcasted_iota(jnp.int32, (CHUNK, 1), 0)
    ti = lax.broadcasted_iota(jnp.int32, (CHUNK, CHUNK), 0)
    tj = lax.broadcasted_iota(jnp.int32, (CHUNK, CHUNK), 1)
    ltri01 = (tj <= ti).astype(BF16)
    chunks_per_scan_tile = TB_SCAN // CHUNK
    half_w = RWKV_WIDTH // 2

    def shifted(c, cols):
        z = z_ref[c * CHUNK:(c + 1) * CHUNK, cols]
        before = carry_ref[:, cols] if c == 0 else z_ref[c * CHUNK - 1:c * CHUNK, cols]
        zprev = jnp.where(row == 0, before, pltpu.roll(z, 1, axis=0))
        return z + (zprev - z) * mu_ref[:, cols]

    for c in chunks:
        rows = slice(c * CHUNK, (c + 1) * CHUNK)
        wa_in = shifted(c, slice(3 * RWKV_WIDTH, 3 * RWKV_WIDTH + LORA_WA))
        wa_in = jnp.where(lane < LORA_WA // 2, jnp.tanh(wa_in), wa_in).astype(BF16)
        gd = jax.nn.sigmoid(shifted(c, slice(3 * RWKV_WIDTH + LORA_WA, RWKV_IN))).astype(BF16)
        for q in range(2):
            hc = slice(q * half_w, (q + 1) * half_w)
            r = shifted(c, hc)
            k = shifted(c, slice(RWKV_WIDTH + q * half_w, RWKV_WIDTH + (q + 1) * half_w))
            v = shifted(c, slice(2 * RWKV_WIDTH + q * half_w, 2 * RWKV_WIDTH + (q + 1) * half_w))
            prep["v"][rows, hc] = v.astype(BF16)
            kk = k * kk_ref[:, hc]
            kk_sq = _split_bf16(kk * kk, TERMS_HEAD_SUM)
            yield
            w_pre = w0_ref[:, hc] + jnp.dot(wa_in, waup_ref[:, hc], preferred_element_type=F32)
            a_pre = a0_ref[:, hc] + jnp.dot(
                wa_in, waup_ref[:, RWKV_WIDTH + q * half_w:RWKV_WIDTH + (q + 1) * half_w],
                preferred_element_type=F32)
            prep["gate"][rows, hc] = jnp.dot(gd, gup_ref[:, hc], preferred_element_type=F32)
            kk_ss = _head_sum_parts(kk_sq, seg01)
            yield
            a = jax.nn.sigmoid(a_pre)
            lw = jax.nn.sigmoid(w_pre) * (-EXP_M05)
            lw_parts = _split_bf16(lw, TERMS_DECAY_CUMSUM)
            kk = kk * lax.rsqrt(jnp.maximum(kk_ss, 1e-24))
            kmod = k * (1.0 + (a - 1.0) * ka_ref[:, hc])
            kka = kk * a
            rkk = _split_bf16(r * kmod * rk_ref[:, hc], TERMS_HEAD_SUM)
            yield
            cs = _cumsum_rows(ltri01, lw_parts)
            prep["bonus"][rows, hc] = _head_sum_parts(rkk, seg01) * v
            yield
            w_last = jnp.exp(cs[CHUNK - 1:CHUNK, :])
            w_inv = jnp.exp(-cs)
            w_tail = w_last * w_inv
            prep["rt"][rows, hc] = r * jnp.exp(cs)
            prep["at"][rows, hc] = -kk * jnp.exp(cs - lw)
            prep["bh"][rows, hc] = (kka * w_inv).astype(BF16)
            prep["kh"][rows, hc] = (kmod * w_inv).astype(BF16)
            prep["bc"][rows, hc] = (kka * w_tail).astype(BF16)
            prep["kc"][rows, hc] = (kmod * w_tail).astype(BF16)
            cq, cr = divmod(c, chunks_per_scan_tile)
            prep["wl"][cq, cr:cr + 1, hc] = w_last
            yield


def _sgu_pieces(z_ref, lnw_ref, lnb_ref, wcat_ref, bias_ref, o_ref, *, tm):
    bi = lax.broadcasted_iota(jnp.int32, (2 * SGU_BLOCK, PAIR), 0) >> 7
    bj = lax.broadcasted_iota(jnp.int32, (2 * SGU_BLOCK, PAIR), 1) >> 6
    sel = bi == bj
    for b in range(tm // SGU_BLOCK):
        rows = slice(b * SGU_BLOCK, (b + 1) * SGU_BLOCK)
        hz = jax.nn.gelu(z_ref[rows, RWKV_IN:])
        u = hz[:, :SGU_WIDTH]
        vf = hz[:, SGU_WIDTH:]
        mu = jnp.mean(vf, axis=-1, keepdims=True)
        d = vf - mu
        var = jnp.mean(d * d, axis=-1, keepdims=True)
        vn = d * lax.rsqrt(var + LN_EPS) * lnw_ref[...] + lnb_ref[...]
        stacks = []
        for p in range(SGU_WIDTH // PAIR):
            vb = vn[:, p * PAIR:(p + 1) * PAIR]
            stacks.append(jnp.where(sel, jnp.concatenate([vb, vb], axis=0), 0.0).astype(BF16))
        yield
        for p in range(SGU_WIDTH // PAIR):
            lanes = slice(p * PAIR, (p + 1) * PAIR)
            mixed = jnp.dot(wcat_ref[p], stacks[p], preferred_element_type=F32) + bias_ref[:, lanes]
            o_ref[rows, lanes] = u[:, lanes] * mixed
        yield


def _front_kernel(x_ref, g1_ref, win_ref, mu_ref, w0_ref, waup_ref, a0_ref, gup_ref, kk_ref, ka_ref,
                  rk_ref, slnw_ref, slnb_ref, ws_ref, sbias_ref, *rest, tm):
    n = len(_PREP_NAMES)
    prep = dict(zip(_PREP_NAMES, rest[:n]))
    ysgu_ref = rest[n]
    carry_ref, wcat_ref, z0_ref, z1_ref = rest[n + 1:]
    i = pl.program_id(0)

    @pl.when(i == 0)
    def _():
        carry_ref[...] = jnp.zeros_like(carry_ref)
        z1_ref[...] = jnp.zeros_like(z1_ref)
        ti = lax.broadcasted_iota(jnp.int32, (SGU_BLOCK, SGU_BLOCK), 0)
        tj = lax.broadcasted_iota(jnp.int32, (SGU_BLOCK, SGU_BLOCK), 1)
        tril = tj <= ti
        for p in range(SGU_WIDTH // PAIR):
            wcat_ref[p] = jnp.concatenate(
                [jnp.where(tril, ws_ref[2 * p], 0.0), jnp.where(tril, ws_ref[2 * p + 1], 0.0)],
                axis=1).astype(BF16)

    def step(z_write, z_read):
        n_chunks = tm // CHUNK
        prep_pieces = [
            _rwkv_prep_pieces(z_read, mu_ref, w0_ref, waup_ref, a0_ref, gup_ref, kk_ref, ka_ref,
                              rk_ref, carry_ref, prep, range(k, n_chunks, PREP_STREAMS))
            for k in range(PREP_STREAMS)]
        n_dot = 1 + IN_WIDTH // IN_PROJ_COLS + 1
        _run_interleaved(
            [(_in_proj_pieces(x_ref, g1_ref, win_ref, z_write), n_dot)]
            + [(gen, PREP_PIECES_PER_CHUNK * n_chunks // PREP_STREAMS + 1) for gen in prep_pieces]
            + [(_sgu_pieces(z_read, slnw_ref, slnb_ref, wcat_ref, sbias_ref, ysgu_ref, tm=tm),
                2 * tm // SGU_BLOCK + 1)],
            rounds=n_dot)
        if TB_SCAN // CHUNK < WL_ROWS:
            prep["wl"][:, TB_SCAN // CHUNK:, :] = jnp.zeros(
                (tm // TB_SCAN, WL_ROWS - TB_SCAN // CHUNK, RWKV_WIDTH), F32)
        carry_ref[...] = z_read[tm - 1:tm, :RWKV_IN]

    @pl.when((i & 1) == 0)
    def _():
        step(z0_ref, z1_ref)

    @pl.when((i & 1) == 1)
    def _():
        step(z1_ref, z0_ref)


def _front(x, g1, w_in, mu, w0, waup, a0, gup, k_k, k_a, r_k, slnw, slnb, ws, sbias, tm):
    t = x.shape[0]
    n_tiles = t // tm
    vec = _full((1, RWKV_WIDTH))
    out_tile = lambda i: (jnp.maximum(i - 1, 0), 0)
    out_shapes = ([jax.ShapeDtypeStruct((t, RWKV_WIDTH), F32)] * len(_PREP_F32)
                  + [jax.ShapeDtypeStruct((t, RWKV_WIDTH), BF16)] * len(_PREP_BF16)
                  + [jax.ShapeDtypeStruct((t // TB_SCAN, WL_ROWS, RWKV_WIDTH), F32),
                     jax.ShapeDtypeStruct((t, SGU_WIDTH), F32)])
    out_specs = ([pl.BlockSpec((tm, RWKV_WIDTH), out_tile)] * (len(_PREP_F32) + len(_PREP_BF16))
                 + [pl.BlockSpec((tm // TB_SCAN, WL_ROWS, RWKV_WIDTH),
                                 lambda i: (jnp.maximum(i - 1, 0), 0, 0)),
                    pl.BlockSpec((tm, SGU_WIDTH), out_tile)])
    return pl.pallas_call(
        functools.partial(_front_kernel, tm=tm),
        out_shape=tuple(out_shapes),
        grid=(n_tiles + 1,),
        in_specs=[pl.BlockSpec((tm, D_MODEL), lambda i: (jnp.minimum(i, n_tiles - 1), 0)),
                  _full((1, D_MODEL)), _full((D_MODEL, IN_WIDTH)), _full((1, RWKV_IN)), vec,
                  _full((LORA_WA, 2 * RWKV_WIDTH)), vec, _full((GATE_LORA, RWKV_WIDTH)),
                  vec, vec, vec, _full((1, SGU_WIDTH)), _full((1, SGU_WIDTH)),
                  _full((SGU_GROUPS, SGU_BLOCK, SGU_BLOCK)), _full((SGU_BLOCK, SGU_WIDTH))],
        out_specs=tuple(out_specs),
        scratch_shapes=[pltpu.VMEM((1, RWKV_IN), F32),
                        pltpu.VMEM((SGU_WIDTH // PAIR, SGU_BLOCK, 2 * SGU_BLOCK), BF16),
                        pltpu.VMEM((tm, IN_WIDTH), F32), pltpu.VMEM((tm, IN_WIDTH), F32)],
        compiler_params=_params(),
        name="front",
    )(x, g1, w_in, mu, w0, waup, a0, gup, k_k, k_a, r_k, slnw, slnb, ws, sbias)


def _pair_masks():
    t = lax.broadcasted_iota(jnp.int32, (CHUNK, PAIR), 0)
    j = lax.broadcasted_iota(jnp.int32, (CHUNK, PAIR), 1) & (CHUNK - 1)
    strict = j < t
    incl = j <= t
    blk16 = (t >> 4) == (j >> 4)
    blk32 = (t >> 5) == (j >> 5)
    return strict, incl, blk16, blk32


def _bd(x, bd_mask):
    x = x.astype(BF16)
    return jnp.where(bd_mask, jnp.concatenate([x, x], axis=0), 0.0).astype(BF16)


def _unit_lower_inverse_minus_identity(a_list, masks, bd_mask):
    _, _, blk16, blk32 = masks
    ad = [jnp.where(blk16, a, 0.0) for a in a_list]
    ap = [_mm(x, _bd(x, bd_mask)) for x in ad]
    tp = ad
    for _ in range(2):
        both = [_mm(p, jnp.concatenate([_bd(p, bd_mask), _bd(t, bd_mask)], axis=1))
                for p, t in zip(ap, tp)]
        tp = [t + p + b[:, PAIR:] for t, p, b in zip(tp, ap, both)]
        ap = [b[:, :PAIR] for b in both]
    last = [_mm(p, _bd(t, bd_mask)) for p, t in zip(ap, tp)]
    tp = [t + p + x for t, p, x in zip(tp, ap, last)]
    for off_mask in (blk32 & ~blk16, ~blk32):
        off = [jnp.where(off_mask, a, 0.0) for a in a_list]
        x = [o + _mm(t, _bd(o, bd_mask)) for o, t in zip(off, tp)]
        tp = [t + xx + _mm(xx, _bd(t, bd_mask)) for t, xx in zip(tp, x)]
    return tp


def _scan_kernel(rt_ref, at_ref, gate_ref, bonus_ref, bh_ref, kh_ref, bc_ref, kc_ref, v_ref, wl_ref,
                 lnw_ref, lnb_ref, o_ref, s_ref, y_ref, *, tb):
    @pl.when(pl.program_id(0) == 0)
    def _():
        s_ref[...] = jnp.zeros_like(s_ref)

    masks = _pair_masks()
    strict, incl = masks[0], masks[1]
    bi = lax.broadcasted_iota(jnp.int32, (PAIR, PAIR), 0) >> 6
    bj = lax.broadcasted_iota(jnp.int32, (PAIR, PAIR), 1) >> 6
    bd1 = bi == bj
    bd2 = jnp.concatenate([bd1, bd1], axis=1)

    n_chunks = tb // CHUNK

    def chunk_terms(probs):
        cut = lambda ref: [ref[c * CHUNK:(c + 1) * CHUNK, p * PAIR:(p + 1) * PAIR]
                           for c, p in probs]
        rt_p, at_p, bh_p, kh_p, bc_p, kc_p, v_p = map(
            cut, (rt_ref, at_ref, bh_ref, kh_ref, bc_ref, kc_ref, v_ref))
        gram = [_mm(jnp.concatenate([a_, r_], axis=0),
                    jnp.concatenate([_bd(b_, bd1), _bd(k_, bd1)], axis=0), _NT)
                for a_, r_, b_, k_ in zip(at_p, rt_p, bh_p, kh_p)]
        a_ab = [jnp.where(strict, g_[:CHUNK, :PAIR], 0.0) for g_ in gram]
        a_ak = [jnp.where(strict, g_[:CHUNK, PAIR:], 0.0) for g_ in gram]
        b_rb = [jnp.where(incl, g_[CHUNK:, :PAIR], 0.0) for g_ in gram]
        b_rk = [jnp.where(incl, g_[CHUNK:, PAIR:], 0.0) for g_ in gram]
        v_bd = [_bd(x_, bd1) for x_ in v_p]
        rhs = [jnp.concatenate([_mm(m_, x_), a_], axis=1) for m_, x_, a_ in zip(a_ak, v_bd, at_p)]
        tp = _unit_lower_inverse_minus_identity(a_ab, masks, bd1)
        sol = [x_ + _mm(t_, _bd(x_, bd2)) for x_, t_ in zip(rhs, tp)]
        u_v = [x_[:, :PAIR] for x_ in sol]
        a_chk = [x_[:, PAIR:] for x_ in sol]
        p_mat = [jnp.where(bd1, _mm(a_, b_, _TN), 0.0) for a_, b_ in zip(a_chk, bc_p)]
        q_mat = [jnp.where(bd1, _mm(jnp.concatenate([u_.astype(BF16), x_], axis=0),
                                    jnp.concatenate([b_, k_], axis=0), _TN), 0.0)
                 for u_, x_, b_, k_ in zip(u_v, v_p, bc_p, kc_p)]
        r_chk = [r_ + _mm(m_, _bd(a_, bd1)) for r_, m_, a_ in zip(rt_p, b_rb, a_chk)]
        y_v = [_mm(jnp.concatenate([m1, m2], axis=1), jnp.concatenate([_bd(u_, bd1), x_], axis=0))
               for m1, m2, u_, x_ in zip(b_rb, b_rk, u_v, v_bd)]
        return p_mat, q_mat, r_chk, y_v

    p_mat, q_mat, r_chk, y_v = [], [], [], []
    for c0 in range(0, n_chunks, SCAN_GROUP_CHUNKS):
        group = [(c, p) for c in range(c0, c0 + SCAN_GROUP_CHUNKS) for p in range(N_PAIRS)]
        for acc, part in zip((p_mat, q_mat, r_chk, y_v), chunk_terms(group)):
            acc.extend(part)

    s = [s_ref[p] for p in range(N_PAIRS)]
    for c in range(n_chunks):
        idx = [c * N_PAIRS + p for p in range(N_PAIRS)]
        w_last = wl_ref[0, c:c + 1, :]
        s_next = [s[p] * w_last[:, p * PAIR:(p + 1) * PAIR] + _mm(s[p], p_mat[i]) + q_mat[i]
                  for p, i in enumerate(idx)]
        for p, i in enumerate(idx):
            y_ref[c * CHUNK:(c + 1) * CHUNK, p * PAIR:(p + 1) * PAIR] = (
                _mm(r_chk[i], s[p], _NT) + y_v[i])
        s = s_next
    for p in range(N_PAIRS):
        s_ref[p] = s[p]

    seg01 = _seg01()
    y = y_ref[...]
    mean = _head_sum(y, seg01) * (1.0 / RWKV_HEAD)
    d = y - mean
    var = _head_sum(d * d, seg01) * (1.0 / RWKV_HEAD)
    yn = d * lax.rsqrt(var + LNX_EPS) * lnw_ref[...] + lnb_ref[...]
    o_ref[...] = (yn + bonus_ref[...]) * gate_ref[...]


def _scan(prep, lnw, lnb, tb):
    t = prep["rt"].shape[0]
    assert tb // CHUNK <= WL_ROWS
    tile = pl.BlockSpec((tb, RWKV_WIDTH), lambda i: (i, 0))
    vec = _full((1, RWKV_WIDTH))
    return pl.pallas_call(
        functools.partial(_scan_kernel, tb=tb),
        out_shape=jax.ShapeDtypeStruct((t, RWKV_WIDTH), F32),
        grid=(t // tb,),
        in_specs=[tile] * (len(_PREP_F32) + len(_PREP_BF16))
        + [pl.BlockSpec((1, WL_ROWS, RWKV_WIDTH), lambda i: (i, 0, 0)), vec, vec],
        out_specs=tile,
        scratch_shapes=[pltpu.VMEM((N_PAIRS, PAIR, PAIR), F32), pltpu.VMEM((tb, RWKV_WIDTH), F32)],
        compiler_params=_params(),
        name="scan",
    )(*[prep[name] for name in _PREP_NAMES], lnw, lnb)


def _mix_attn_kernel(x_ref, yr_ref, ys_ref, wo1_ref, wo2_ref, g2_ref, wq_ref, k_ref, v_ref, wo_ref,
                     o_ref):
    tm = x_ref.shape[0]
    groups = [slice(r, r + ATTN_ROW_GROUP) for r in range(0, tm, ATTN_ROW_GROUP)]
    heads = [slice(hd * XA_HEAD_DIM, (hd + 1) * XA_HEAD_DIM) for hd in range(XA_HEADS)]
    x1 = [x_ref[r, :]
          + jnp.dot(yr_ref[r, :].astype(BF16), wo1_ref[...], preferred_element_type=F32)
          + jnp.dot(ys_ref[r, :].astype(BF16), wo2_ref[...], preferred_element_type=F32)
          for r in groups]
    h = [_rmsnorm(x, g2_ref[...]).astype(BF16) for x in x1]
    q = [jnp.dot(h_, wq_ref[...], preferred_element_type=F32).astype(BF16) for h_ in h]
    s = [[lax.dot_general(q_[:, hl], k_ref[:, hl], _NT, preferred_element_type=F32)
          * (XA_HEAD_DIM ** -0.5) for hl in heads] for q_ in q]
    p = []
    for s_g in s:
        p_g = []
        for s_h in s_g:
            e = jnp.exp(s_h - jnp.max(s_h, axis=-1, keepdims=True))
            p_g.append((e / jnp.sum(e, axis=-1, keepdims=True)).astype(BF16))
        p.append(p_g)
    o = [jnp.concatenate([jnp.dot(p_h, v_ref[:, hl], preferred_element_type=F32)
                          for p_h, hl in zip(p_g, heads)], axis=1).astype(BF16) for p_g in p]
    for r, x, o_ in zip(groups, x1, o):
        o_ref[r, :] = x + jnp.dot(o_, wo_ref[...], preferred_element_type=F32)


def _mix_attn(x, yr, ys, wo1, wo2, g2, wq, k, v, wo, tm):
    t = x.shape[0]
    sq = _full((D_MODEL, D_MODEL))
    half = _full((RWKV_WIDTH, D_MODEL))
    return pl.pallas_call(
        _mix_attn_kernel,
        out_shape=jax.ShapeDtypeStruct((t, D_MODEL), F32),
        grid=(t // tm,),
        in_specs=[pl.BlockSpec((tm, D_MODEL), lambda i: (i, 0)),
                  pl.BlockSpec((tm, RWKV_WIDTH), lambda i: (i, 0)),
                  pl.BlockSpec((tm, SGU_WIDTH), lambda i: (i, 0)),
                  half, half, _full((1, D_MODEL)), sq,
                  _full((MEM_LEN, D_MODEL)), _full((MEM_LEN, D_MODEL)), sq],
        out_specs=pl.BlockSpec((tm, D_MODEL), lambda i: (i, 0)),
        compiler_params=_params("parallel"),
        name="mix_attn",
    )(x, yr, ys, wo1, wo2, g2, wq, k, v, wo)


def _ffn_kernel(x_ref, g3_ref, wg_ref, wu_ref, wd_ref, gf_ref, o_ref):
    tm = x_ref.shape[0]
    groups = [slice(r, r + FFN_ROW_GROUP) for r in range(0, tm, FFN_ROW_GROUP)]
    x2 = [x_ref[r, :] for r in groups]
    h = [_rmsnorm(x, g3_ref[...]).astype(BF16) for x in x2]
    gate = [jnp.dot(h_, wg_ref[...], preferred_element_type=F32) for h_ in h]
    up = [jnp.dot(h_, wu_ref[...], preferred_element_type=F32) for h_ in h]
    act = [(jax.nn.silu(g_) * u_).astype(BF16) for g_, u_ in zip(gate, up)]
    x3 = [x + jnp.dot(a_, wd_ref[...], preferred_element_type=F32) for x, a_ in zip(x2, act)]
    for r, x in zip(groups, x3):
        o_ref[r, :] = _rmsnorm(x, gf_ref[...])


def _ffn(x, g3, wg, wu, wd, gf, tm):
    t = x.shape[0]
    return pl.pallas_call(
        _ffn_kernel,
        out_shape=jax.ShapeDtypeStruct((t, D_MODEL), F32),
        grid=(t // tm,),
        in_specs=[pl.BlockSpec((tm, D_MODEL), lambda i: (i, 0)), _full((1, D_MODEL)),
                  _full((D_MODEL, D_FF)), _full((D_MODEL, D_FF)), _full((D_FF, D_MODEL)),
                  _full((1, D_MODEL))],
        out_specs=pl.BlockSpec((tm, D_MODEL), lambda i: (i, 0)),
        compiler_params=_params("parallel"),
        name="ffn",
    )(x, g3, wg, wu, wd, gf)


def kernel(x, mem, norm1_g, w_in, shift_mu, w0, w_lora_up, a0, a_lora_up, g_lora_up, k_k, k_a, r_k,
           lnx_w, lnx_b, sgu_ln_w, sgu_ln_b, w_spatial, b_spatial, w_out, norm2_g, mem_norm_g,
           wq_x, wk_x, wv_x, wo_x, norm3_g, w_gate, w_up, w_down, norm_f_g):
    b, t, _ = x.shape
    depth = w_in.shape[0]
    assert depth == 1, "the final RMSNorm is fused into the (single) layer's ffn call"
    assert t % TM_DENSE == 0 and TM_DENSE % TB_SCAN == 0 and TB_SCAN % CHUNK == 0
    row = lambda p: p.reshape(1, -1)
    bf = lambda p: p.astype(BF16)
    outs = []
    for bi in range(b):
        xb = x[bi]
        for l in range(depth):
            lora = w_lora_up.shape[1]
            zeros = jnp.zeros((lora, RWKV_WIDTH), F32)
            waup = jnp.concatenate(
                [jnp.concatenate([w_lora_up[l], zeros], axis=1),
                 jnp.concatenate([zeros, a_lora_up[l]], axis=1)], axis=0)
            bias = jnp.repeat(b_spatial[l].T, SGU_WIDTH // SGU_GROUPS, axis=1)

            front = _front(xb, row(norm1_g[l]), bf(w_in[l]), row(shift_mu[l]), row(w0[l]), bf(waup),
                           row(a0[l]), bf(g_lora_up[l]), row(k_k[l]), row(k_a[l]), row(r_k[l]),
                           row(sgu_ln_w[l]), row(sgu_ln_b[l]), w_spatial[l], bias, TM_DENSE)
            prep = dict(zip(_PREP_NAMES, front[:len(_PREP_NAMES)]))
            y_sgu = front[len(_PREP_NAMES)]
            y_rwkv = _scan(prep, row(lnx_w[l]), row(lnx_b[l]), TB_SCAN)
            k_mem, v_mem = _mem_kv(mem[bi], row(mem_norm_g[l]), bf(wk_x[l]), bf(wv_x[l]))
            x2 = _mix_attn(xb, y_rwkv, y_sgu, bf(w_out[l][:RWKV_WIDTH]), bf(w_out[l][RWKV_WIDTH:]),
                           row(norm2_g[l]), bf(wq_x[l]), k_mem, v_mem, bf(wo_x[l]), TM_ATTN)
            xb = _ffn(x2, row(norm3_g[l]), bf(w_gate[l]), bf(w_up[l]), bf(w_down[l]),
                      row(norm_f_g), TM_FFN)
        outs.append(xb)
    return jnp.stack(outs, axis=0)
```

```python
import functools
import math

import jax
import jax.numpy as jnp
from jax import lax
from jax.experimental import pallas as pl
from jax.experimental.pallas import tpu as pltpu

F32 = jnp.float32
BF16 = jnp.bfloat16

D_MODEL = 1024
RWKV_WIDTH = 512
RWKV_HEAD = 64
LORA_WA = 128
GATE_LORA = 128
RWKV_IN = 3 * RWKV_WIDTH + LORA_WA + GATE_LORA
SGU_WIDTH = 512
SGU_GROUPS = 8
SGU_BLOCK = 128
IN_WIDTH = RWKV_IN + 2 * SGU_WIDTH
MEM_LEN = 256
XA_HEADS = 4
XA_HEAD_DIM = D_MODEL // XA_HEADS
D_FF = 2816
RMS_EPS = 1e-6
LN_EPS = 1e-5
LNX_EPS = 64e-5
EXP_M05 = 0.6065306597126334

CHUNK = 64
PAIR = 2 * RWKV_HEAD
N_PAIRS = RWKV_WIDTH // PAIR
TM_DENSE = 512
TM_ATTN = 1024
TM_FFN = 1024
TB_SCAN = 512
SCAN_GROUP_CHUNKS = 8
Z_PAD = 8
WL_ROWS = 8
ATTN_ROW_GROUP = 256
ATTN_STAGGER = 2
FFN_ROW_GROUP = 256
IN_PROJ_COLS = 256
PREP_STREAMS = 8
PREP_PIECES_PER_CHUNK = 10
TERMS_DECAY_CUMSUM = 2
TERMS_HEAD_SUM = 1
VMEM_LIMIT = 56 * 1024 * 1024

_NN = (((1,), (0,)), ((), ()))
_NT = (((1,), (1,)), ((), ()))
_TN = (((0,), (0,)), ((), ()))


def _mm(a, b, dims=_NN):
    return lax.dot_general(a.astype(BF16), b.astype(BF16), dims, preferred_element_type=F32)


def _split_bf16(x, terms):
    parts = []
    rem = x
    for _ in range(terms):
        part = rem.astype(BF16)
        rem = rem - part.astype(F32)
        parts.append(part)
    return parts


def _cumsum_rows(ltri01, parts):
    acc = None
    for part in parts:
        d = lax.dot_general(ltri01, part, _NN, preferred_element_type=F32)
        acc = d if acc is None else acc + d
    return acc


def _head_sum_parts(parts, seg01):
    cols = []
    for q in range(parts[0].shape[1] // 256):
        acc = None
        for part in parts:
            d = lax.dot_general(part[:, 256 * q:256 * (q + 1)], seg01, _NN,
                                preferred_element_type=F32)
            acc = d if acc is None else acc + d
        cols.append(acc)
    return jnp.concatenate(cols, axis=1)


def _head_sum(x, seg01):
    return _head_sum_parts(_split_bf16(x, TERMS_HEAD_SUM), seg01)


def _seg01():
    li = lax.broadcasted_iota(jnp.int32, (256, 256), 0) >> 6
    lj = lax.broadcasted_iota(jnp.int32, (256, 256), 1) >> 6
    return (li == lj).astype(BF16)


def _gelu_tanh(x):
    k1 = -2.0 * math.sqrt(2.0 / math.pi) * math.log2(math.e)
    return x / (1.0 + jnp.exp2(x * (k1 + (k1 * 0.044715) * (x * x))))


def _rmsnorm(x, g):
    return x * lax.rsqrt(jnp.mean(x * x, axis=-1, keepdims=True) + RMS_EPS) * g


def _full(shape):
    n = len(shape)
    return pl.BlockSpec(shape, lambda i: (0,) * n, pipeline_mode=pl.Buffered(1))


def _params(sem="arbitrary"):
    return pltpu.CompilerParams(dimension_semantics=(sem,), vmem_limit_bytes=VMEM_LIMIT)


def _run_interleaved(stages, rounds):
    for r in range(rounds):
        for gen, n in stages:
            for _ in range((r + 1) * n // rounds - r * n // rounds):
                next(gen, None)
    for gen, _ in stages:
        assert next(gen, StopIteration) is StopIteration, "piece count too small"


def _run_wavefront(gens, stagger):
    live = list(enumerate(gens))
    r = 0
    while live:
        for entry in list(live):
            g, gen = entry
            if r >= g * stagger and next(gen, StopIteration) is StopIteration:
                live.remove(entry)
        r += 1


def _mem_kv_kernel(mem_ref, g_ref, wk_ref, wv_ref, k_ref, v_ref):
    m = _rmsnorm(mem_ref[...], g_ref[...]).astype(BF16)
    k_ref[...] = jnp.dot(m, wk_ref[...], preferred_element_type=F32).astype(BF16)
    v_ref[...] = jnp.dot(m, wv_ref[...], preferred_element_type=F32).astype(BF16)


def _mem_kv(mem, g, wk, wv):
    return pl.pallas_call(
        _mem_kv_kernel,
        out_shape=(jax.ShapeDtypeStruct((MEM_LEN, D_MODEL), BF16),) * 2,
        grid=(1,),
        in_specs=[_full((MEM_LEN, D_MODEL)), _full((1, D_MODEL)),
                  _full((D_MODEL, D_MODEL)), _full((D_MODEL, D_MODEL))],
        out_specs=(_full((MEM_LEN, D_MODEL)),) * 2,
        compiler_params=_params(),
        name="mem_kv",
    )(mem, g, wk, wv)


_PREP_F32 = ("rt", "at", "gate", "bonus")
_PREP_BF16 = ("bh", "kh", "bc", "kc", "v")
_PREP_NAMES = _PREP_F32 + _PREP_BF16 + ("wl",)


def _in_proj_pieces(x_ref, g_ref, w_ref, z_ref):
    h = _rmsnorm(x_ref[...], g_ref[...]).astype(BF16)
    yield
    for j in range(IN_WIDTH // IN_PROJ_COLS):
        cols = slice(j * IN_PROJ_COLS, (j + 1) * IN_PROJ_COLS)
        z_ref[Z_PAD:, cols] = jnp.dot(h, w_ref[:, cols], preferred_element_type=F32)
        yield


def _rwkv_prep_pieces(z_ref, mu_ref, w0_ref, waup_ref, a0_ref, gup_ref, kk_ref, ka_ref, rk_ref,
                      prep, chunks):
    seg01 = _seg01()
    lane = lax.broadcasted_iota(jnp.int32, (1, LORA_WA), 1)
    row = lax.broadcasted_iota(jnp.int32, (CHUNK, 1), 0)
    ti = lax.broadcasted_iota(jnp.int32, (CHUNK, CHUNK), 0)
    tj = lax.broadcasted_iota(jnp.int32, (CHUNK, CHUNK), 1)
    ltri01 = (tj <= ti).astype(BF16)
    chunks_per_scan_tile = TB_SCAN // CHUNK
    half_w = RWKV_WIDTH // 2

    def shifted(c, cols):
        z = z_ref[Z_PAD + c * CHUNK:Z_PAD + (c + 1) * CHUNK, cols]
        before = z_ref[Z_PAD + c * CHUNK - 1:Z_PAD + c * CHUNK, cols]
        zprev = jnp.where(row == 0, before, pltpu.roll(z, 1, axis=0))
        return z + (zprev - z) * mu_ref[:, cols]

    for c in chunks:
        rows = slice(c * CHUNK, (c + 1) * CHUNK)
        wa_in = shifted(c, slice(3 * RWKV_WIDTH, 3 * RWKV_WIDTH + LORA_WA))
        wa_in = jnp.where(lane < LORA_WA // 2, jnp.tanh(wa_in), wa_in).astype(BF16)
        gd = jax.nn.sigmoid(shifted(c, slice(3 * RWKV_WIDTH + LORA_WA, RWKV_IN))).astype(BF16)
        for q in range(2):
            hc = slice(q * half_w, (q + 1) * half_w)
            r = shifted(c, hc)
            k = shifted(c, slice(RWKV_WIDTH + q * half_w, RWKV_WIDTH + (q + 1) * half_w))
            v = shifted(c, slice(2 * RWKV_WIDTH + q * half_w, 2 * RWKV_WIDTH + (q + 1) * half_w))
            prep["v"][rows, hc] = v.astype(BF16)
            kk = k * kk_ref[:, hc]
            kk_sq = _split_bf16(kk * kk, TERMS_HEAD_SUM)
            yield
            w_pre = w0_ref[:, hc] + jnp.dot(wa_in, waup_ref[:, hc], preferred_element_type=F32)
            a_pre = a0_ref[:, hc] + jnp.dot(
                wa_in, waup_ref[:, RWKV_WIDTH + q * half_w:RWKV_WIDTH + (q + 1) * half_w],
                preferred_element_type=F32)
            prep["gate"][rows, hc] = jnp.dot(gd, gup_ref[:, hc], preferred_element_type=F32)
            kk_ss = _head_sum_parts(kk_sq, seg01)
            yield
            a = jax.nn.sigmoid(a_pre)
            lw = jax.nn.sigmoid(w_pre) * (-EXP_M05)
            lw_parts = _split_bf16(lw, TERMS_DECAY_CUMSUM)
            kk = kk * lax.rsqrt(jnp.maximum(kk_ss, 1e-24))
            kmod = k * (1.0 + (a - 1.0) * ka_ref[:, hc])
            kka = kk * a
            rkk = _split_bf16(r * kmod * rk_ref[:, hc], TERMS_HEAD_SUM)
            yield
            cs = _cumsum_rows(ltri01, lw_parts)
            prep["bonus"][rows, hc] = _head_sum_parts(rkk, seg01) * v
            yield
            w_last = jnp.exp(cs[CHUNK - 1:CHUNK, :])
            w_inv = jnp.exp(-cs)
            w_tail = w_last * w_inv
            prep["rt"][rows, hc] = r * jnp.exp(cs)
            prep["at"][rows, hc] = -kk * jnp.exp(cs - lw)
            prep["bh"][rows, hc] = (kka * w_inv).astype(BF16)
            prep["kh"][rows, hc] = (kmod * w_inv).astype(BF16)
            prep["bc"][rows, hc] = (kka * w_tail).astype(BF16)
            prep["kc"][rows, hc] = (kmod * w_tail).astype(BF16)
            cq, cr = divmod(c, chunks_per_scan_tile)
            prep["wl"][cq, cr:cr + 1, hc] = w_last
            yield


def _copy_pieces(z_ref, o_ref, *, tm):
    for b in range(tm // SGU_BLOCK):
        o_ref[b * SGU_BLOCK:(b + 1) * SGU_BLOCK, :] = (
            z_ref[Z_PAD + b * SGU_BLOCK:Z_PAD + (b + 1) * SGU_BLOCK, RWKV_IN:])
        yield


def _front_kernel(x_ref, g1_ref, win_ref, mu_ref, w0_ref, waup_ref, a0_ref, gup_ref, kk_ref, ka_ref,
                  rk_ref, *rest, tm):
    n = len(_PREP_NAMES)
    prep = dict(zip(_PREP_NAMES, rest[:n]))
    zs_ref = rest[n]
    z0_ref, z1_ref = rest[n + 1:]
    i = pl.program_id(0)

    @pl.when(i == 0)
    def _():
        z1_ref[...] = jnp.zeros_like(z1_ref)

    def step(z_write, z_read):
        n_chunks = tm // CHUNK
        prep_pieces = [
            _rwkv_prep_pieces(z_read, mu_ref, w0_ref, waup_ref, a0_ref, gup_ref, kk_ref, ka_ref,
                              rk_ref, prep, range(k, n_chunks, PREP_STREAMS))
            for k in range(PREP_STREAMS)]
        n_dot = 1 + IN_WIDTH // IN_PROJ_COLS + 1
        _run_interleaved(
            [(_in_proj_pieces(x_ref, g1_ref, win_ref, z_write), n_dot)]
            + [(gen, PREP_PIECES_PER_CHUNK * n_chunks // PREP_STREAMS + 1) for gen in prep_pieces]
            + [(_copy_pieces(z_read, zs_ref, tm=tm), tm // SGU_BLOCK + 1)],
            rounds=n_dot)
        if TB_SCAN // CHUNK < WL_ROWS:
            prep["wl"][:, TB_SCAN // CHUNK:, :] = jnp.zeros(
                (tm // TB_SCAN, WL_ROWS - TB_SCAN // CHUNK, RWKV_WIDTH), F32)
        z_write[Z_PAD - 1:Z_PAD, :] = z_read[Z_PAD + tm - 1:Z_PAD + tm, :]

    @pl.when((i & 1) == 0)
    def _():
        step(z0_ref, z1_ref)

    @pl.when((i & 1) == 1)
    def _():
        step(z1_ref, z0_ref)


def _front(x, g1, w_in, mu, w0, waup, a0, gup, k_k, k_a, r_k, tm):
    t = x.shape[0]
    n_tiles = t // tm
    vec = _full((1, RWKV_WIDTH))
    out_tile = lambda i: (jnp.maximum(i - 1, 0), 0)
    out_shapes = ([jax.ShapeDtypeStruct((t, RWKV_WIDTH), F32)] * len(_PREP_F32)
                  + [jax.ShapeDtypeStruct((t, RWKV_WIDTH), BF16)] * len(_PREP_BF16)
                  + [jax.ShapeDtypeStruct((t // TB_SCAN, WL_ROWS, RWKV_WIDTH), F32),
                     jax.ShapeDtypeStruct((t, 2 * SGU_WIDTH), F32)])
    out_specs = ([pl.BlockSpec((tm, RWKV_WIDTH), out_tile)] * (len(_PREP_F32) + len(_PREP_BF16))
                 + [pl.BlockSpec((tm // TB_SCAN, WL_ROWS, RWKV_WIDTH),
                                 lambda i: (jnp.maximum(i - 1, 0), 0, 0)),
                    pl.BlockSpec((tm, 2 * SGU_WIDTH), out_tile)])
    return pl.pallas_call(
        functools.partial(_front_kernel, tm=tm),
        out_shape=tuple(out_shapes),
        grid=(n_tiles + 1,),
        in_specs=[pl.BlockSpec((tm, D_MODEL), lambda i: (jnp.minimum(i, n_tiles - 1), 0)),
                  _full((1, D_MODEL)), _full((D_MODEL, IN_WIDTH)), _full((1, RWKV_IN)), vec,
                  _full((LORA_WA, 2 * RWKV_WIDTH)), vec, _full((GATE_LORA, RWKV_WIDTH)),
                  vec, vec, vec],
        out_specs=tuple(out_specs),
        scratch_shapes=[pltpu.VMEM((Z_PAD + tm, IN_WIDTH), F32),
                        pltpu.VMEM((Z_PAD + tm, IN_WIDTH), F32)],
        compiler_params=_params(),
        name="front",
    )(x, g1, w_in, mu, w0, waup, a0, gup, k_k, k_a, r_k)


def _pair_masks():
    t = lax.broadcasted_iota(jnp.int32, (CHUNK, PAIR), 0)
    j = lax.broadcasted_iota(jnp.int32, (CHUNK, PAIR), 1) & (CHUNK - 1)
    strict = j < t
    incl = j <= t
    blk16 = (t >> 4) == (j >> 4)
    blk32 = (t >> 5) == (j >> 5)
    return strict, incl, blk16, blk32


def _bd(x, bd_mask):
    x = x.astype(BF16)
    return jnp.where(bd_mask, jnp.concatenate([x, x], axis=0), 0.0).astype(BF16)


def _unit_lower_inverse_minus_identity(a_list, masks, bd_mask):
    _, _, blk16, blk32 = masks
    ad = [jnp.where(blk16, a, 0.0) for a in a_list]
    ap = [_mm(x, _bd(x, bd_mask)) for x in ad]
    tp = ad
    for _ in range(2):
        both = [_mm(p, jnp.concatenate([_bd(p, bd_mask), _bd(t, bd_mask)], axis=1))
                for p, t in zip(ap, tp)]
        tp = [t + p + b[:, PAIR:] for t, p, b in zip(tp, ap, both)]
        ap = [b[:, :PAIR] for b in both]
    last = [_mm(p, _bd(t, bd_mask)) for p, t in zip(ap, tp)]
    tp = [t + p + x for t, p, x in zip(tp, ap, last)]
    for off_mask in (blk32 & ~blk16, ~blk32):
        off = [jnp.where(off_mask, a, 0.0) for a in a_list]
        x = [o + _mm(t, _bd(o, bd_mask)) for o, t in zip(off, tp)]
        tp = [t + xx + _mm(xx, _bd(t, bd_mask)) for t, xx in zip(tp, x)]
    return tp


def _scan_kernel(rt_ref, at_ref, gate_ref, bonus_ref, bh_ref, kh_ref, bc_ref, kc_ref, v_ref, wl_ref,
                 lnw_ref, lnb_ref, o_ref, s_ref, y_ref, *, tb):
    @pl.when(pl.program_id(0) == 0)
    def _():
        s_ref[...] = jnp.zeros_like(s_ref)

    masks = _pair_masks()
    strict, incl = masks[0], masks[1]
    bi = lax.broadcasted_iota(jnp.int32, (PAIR, PAIR), 0) >> 6
    bj = lax.broadcasted_iota(jnp.int32, (PAIR, PAIR), 1) >> 6
    bd1 = bi == bj
    bd2 = jnp.concatenate([bd1, bd1], axis=1)

    n_chunks = tb // CHUNK

    def chunk_terms(probs):
        cut = lambda ref: [ref[c * CHUNK:(c + 1) * CHUNK, p * PAIR:(p + 1) * PAIR]
                           for c, p in probs]
        rt_p, at_p, bh_p, kh_p, bc_p, kc_p, v_p = map(
            cut, (rt_ref, at_ref, bh_ref, kh_ref, bc_ref, kc_ref, v_ref))
        gram = [_mm(jnp.concatenate([a_, r_], axis=0),
                    jnp.concatenate([_bd(b_, bd1), _bd(k_, bd1)], axis=0), _NT)
                for a_, r_, b_, k_ in zip(at_p, rt_p, bh_p, kh_p)]
        a_ab = [jnp.where(strict, g_[:CHUNK, :PAIR], 0.0) for g_ in gram]
        a_ak = [jnp.where(strict, g_[:CHUNK, PAIR:], 0.0) for g_ in gram]
        b_rb = [jnp.where(incl, g_[CHUNK:, :PAIR], 0.0) for g_ in gram]
        b_rk = [jnp.where(incl, g_[CHUNK:, PAIR:], 0.0) for g_ in gram]
        v_bd = [_bd(x_, bd1) for x_ in v_p]
        rhs = [jnp.concatenate([_mm(m_, x_), a_], axis=1) for m_, x_, a_ in zip(a_ak, v_bd, at_p)]
        tp = _unit_lower_inverse_minus_identity(a_ab, masks, bd1)
        sol = [x_ + _mm(t_, _bd(x_, bd2)) for x_, t_ in zip(rhs, tp)]
        u_v = [x_[:, :PAIR] for x_ in sol]
        a_chk = [x_[:, PAIR:] for x_ in sol]
        p_mat = [jnp.where(bd1, _mm(a_, b_, _TN), 0.0) for a_, b_ in zip(a_chk, bc_p)]
        q_mat = [jnp.where(bd1, _mm(jnp.concatenate([u_.astype(BF16), x_], axis=0),
                                    jnp.concatenate([b_, k_], axis=0), _TN), 0.0)
                 for u_, x_, b_, k_ in zip(u_v, v_p, bc_p, kc_p)]
        r_chk = [r_ + _mm(m_, _bd(a_, bd1)) for r_, m_, a_ in zip(rt_p, b_rb, a_chk)]
        y_v = [_mm(jnp.concatenate([m1, m2], axis=1), jnp.concatenate([_bd(u_, bd1), x_], axis=0))
               for m1, m2, u_, x_ in zip(b_rb, b_rk, u_v, v_bd)]
        return p_mat, q_mat, r_chk, y_v

    p_mat, q_mat, r_chk, y_v = [], [], [], []
    for c0 in range(0, n_chunks, SCAN_GROUP_CHUNKS):
        group = [(c, p) for c in range(c0, c0 + SCAN_GROUP_CHUNKS) for p in range(N_PAIRS)]
        for acc, part in zip((p_mat, q_mat, r_chk, y_v), chunk_terms(group)):
            acc.extend(part)

    s = [s_ref[p] for p in range(N_PAIRS)]
    for c in range(n_chunks):
        idx = [c * N_PAIRS + p for p in range(N_PAIRS)]
        w_last = wl_ref[0, c:c + 1, :]
        s_next = [s[p] * w_last[:, p * PAIR:(p + 1) * PAIR] + _mm(s[p], p_mat[i]) + q_mat[i]
                  for p, i in enumerate(idx)]
        for p, i in enumerate(idx):
            y_ref[c * CHUNK:(c + 1) * CHUNK, p * PAIR:(p + 1) * PAIR] = (
                _mm(r_chk[i], s[p], _NT) + y_v[i])
        s = s_next
    for p in range(N_PAIRS):
        s_ref[p] = s[p]

    seg01 = _seg01()
    y = y_ref[...]
    mean = _head_sum(y, seg01) * (1.0 / RWKV_HEAD)
    d = y - mean
    var = _head_sum(d * d, seg01) * (1.0 / RWKV_HEAD)
    yn = d * lax.rsqrt(var + LNX_EPS) * lnw_ref[...] + lnb_ref[...]
    o_ref[...] = (yn + bonus_ref[...]) * gate_ref[...]


def _scan(prep, lnw, lnb, tb):
    t = prep["rt"].shape[0]
    assert tb // CHUNK <= WL_ROWS
    tile = pl.BlockSpec((tb, RWKV_WIDTH), lambda i: (i, 0))
    vec = _full((1, RWKV_WIDTH))
    return pl.pallas_call(
        functools.partial(_scan_kernel, tb=tb),
        out_shape=jax.ShapeDtypeStruct((t, RWKV_WIDTH), F32),
        grid=(t // tb,),
        in_specs=[tile] * (len(_PREP_F32) + len(_PREP_BF16))
        + [pl.BlockSpec((1, WL_ROWS, RWKV_WIDTH), lambda i: (i, 0, 0)), vec, vec],
        out_specs=tile,
        scratch_shapes=[pltpu.VMEM((N_PAIRS, PAIR, PAIR), F32), pltpu.VMEM((tb, RWKV_WIDTH), F32)],
        compiler_params=_params(),
        name="scan",
    )(*[prep[name] for name in _PREP_NAMES], lnw, lnb)


def _sgu_block_prepare(z, lnw, lnb, sel):
    hz = _gelu_tanh(z)
    u = hz[:, :SGU_WIDTH]
    vf = hz[:, SGU_WIDTH:]
    mu = jnp.mean(vf, axis=-1, keepdims=True)
    d = vf - mu
    var = jnp.mean(d * d, axis=-1, keepdims=True)
    vn = d * lax.rsqrt(var + LN_EPS) * lnw + lnb
    stacks = []
    for p in range(SGU_WIDTH // PAIR):
        vb = vn[:, p * PAIR:(p + 1) * PAIR]
        stacks.append(jnp.where(sel, jnp.concatenate([vb, vb], axis=0), 0.0).astype(BF16))
    return u, stacks


def _sgu_block_mix(u, stacks, wcat, bias):
    return jnp.concatenate(
        [u[:, p * PAIR:(p + 1) * PAIR]
         * (jnp.dot(wcat[p], stacks[p], preferred_element_type=F32) + bias[:, p * PAIR:(p + 1) * PAIR])
         for p in range(SGU_WIDTH // PAIR)], axis=1)


def _mix_attn_group(r, x_ref, yr_ref, zs_ref, slnw_ref, slnb_ref, sbias_ref, wo1_ref, wo2_ref,
                    g2_ref, wq_ref, k_ref, v_ref, wo_ref, o_ref, wcat, sel):
    heads = [slice(hd * XA_HEAD_DIM, (hd + 1) * XA_HEAD_DIM) for hd in range(XA_HEADS)]
    prepared = [_sgu_block_prepare(zs_ref[b:b + SGU_BLOCK, :], slnw_ref[...], slnb_ref[...], sel)
                for b in range(r.start, r.stop, SGU_BLOCK)]
    yield
    x1 = x_ref[r, :] + jnp.dot(yr_ref[r, :].astype(BF16), wo1_ref[...],
                               preferred_element_type=F32)
    y_sgu = jnp.concatenate([_sgu_block_mix(u, st, wcat, sbias_ref[...]) for u, st in prepared],
                            axis=0)
    yield
    x1 = x1 + jnp.dot(y_sgu.astype(BF16), wo2_ref[...], preferred_element_type=F32)
    yield
    h = _rmsnorm(x1, g2_ref[...]).astype(BF16)
    yield
    q = jnp.dot(h, wq_ref[...], preferred_element_type=F32).astype(BF16)
    s = [lax.dot_general(q[:, hl], k_ref[:, hl], _NT, preferred_element_type=F32)
         * (XA_HEAD_DIM ** -0.5) for hl in heads]
    yield
    p = []
    for s_h in s:
        e = jnp.exp(s_h - jnp.max(s_h, axis=-1, keepdims=True))
        p.append((e / jnp.sum(e, axis=-1, keepdims=True)).astype(BF16))
    yield
    o = jnp.concatenate([jnp.dot(p_h, v_ref[:, hl], preferred_element_type=F32)
                         for p_h, hl in zip(p, heads)], axis=1).astype(BF16)
    o_ref[r, :] = x1 + jnp.dot(o, wo_ref[...], preferred_element_type=F32)


def _mix_attn_kernel(x_ref, yr_ref, zs_ref, slnw_ref, slnb_ref, ws_ref, sbias_ref, wo1_ref, wo2_ref,
                     g2_ref, wq_ref, k_ref, v_ref, wo_ref, o_ref):
    tm = x_ref.shape[0]
    ti = lax.broadcasted_iota(jnp.int32, (SGU_BLOCK, SGU_BLOCK), 0)
    tj = lax.broadcasted_iota(jnp.int32, (SGU_BLOCK, SGU_BLOCK), 1)
    tril = tj <= ti
    wcat = [jnp.concatenate([jnp.where(tril, ws_ref[2 * p], 0.0),
                             jnp.where(tril, ws_ref[2 * p + 1], 0.0)], axis=1).astype(BF16)
            for p in range(SGU_WIDTH // PAIR)]
    bi = lax.broadcasted_iota(jnp.int32, (2 * SGU_BLOCK, PAIR), 0) >> 7
    bj = lax.broadcasted_iota(jnp.int32, (2 * SGU_BLOCK, PAIR), 1) >> 6
    sel = bi == bj
    gens = [_mix_attn_group(slice(r, r + ATTN_ROW_GROUP), x_ref, yr_ref, zs_ref, slnw_ref, slnb_ref,
                            sbias_ref, wo1_ref, wo2_ref, g2_ref, wq_ref, k_ref, v_ref, wo_ref,
                            o_ref, wcat, sel)
            for r in range(0, tm, ATTN_ROW_GROUP)]
    _run_wavefront(gens, ATTN_STAGGER)


def _mix_attn(x, yr, zs, slnw, slnb, ws, sbias, wo1, wo2, g2, wq, k, v, wo, tm):
    t = x.shape[0]
    sq = _full((D_MODEL, D_MODEL))
    half = _full((RWKV_WIDTH, D_MODEL))
    return pl.pallas_call(
        _mix_attn_kernel,
        out_shape=jax.ShapeDtypeStruct((t, D_MODEL), F32),
        grid=(t // tm,),
        in_specs=[pl.BlockSpec((tm, D_MODEL), lambda i: (i, 0)),
                  pl.BlockSpec((tm, RWKV_WIDTH), lambda i: (i, 0)),
                  pl.BlockSpec((tm, 2 * SGU_WIDTH), lambda i: (i, 0)),
                  _full((1, SGU_WIDTH)), _full((1, SGU_WIDTH)),
                  _full((SGU_GROUPS, SGU_BLOCK, SGU_BLOCK)), _full((SGU_BLOCK, SGU_WIDTH)),
                  half, half, _full((1, D_MODEL)), sq,
                  _full((MEM_LEN, D_MODEL)), _full((MEM_LEN, D_MODEL)), sq],
        out_specs=pl.BlockSpec((tm, D_MODEL), lambda i: (i, 0)),
        compiler_params=_params("parallel"),
        name="mix_attn",
    )(x, yr, zs, slnw, slnb, ws, sbias, wo1, wo2, g2, wq, k, v, wo)


def _ffn_kernel(x_ref, g3_ref, wg_ref, wu_ref, wd_ref, gf_ref, o_ref):
    tm = x_ref.shape[0]
    groups = [slice(r, r + FFN_ROW_GROUP) for r in range(0, tm, FFN_ROW_GROUP)]
    x2 = [x_ref[r, :] for r in groups]
    h = [_rmsnorm(x, g3_ref[...]).astype(BF16) for x in x2]
    gate = [jnp.dot(h_, wg_ref[...], preferred_element_type=F32) for h_ in h]
    up = [jnp.dot(h_, wu_ref[...], preferred_element_type=F32) for h_ in h]
    act = [(jax.nn.silu(g_) * u_).astype(BF16) for g_, u_ in zip(gate, up)]
    x3 = [x + jnp.dot(a_, wd_ref[...], preferred_element_type=F32) for x, a_ in zip(x2, act)]
    for r, x in zip(groups, x3):
        o_ref[r, :] = _rmsnorm(x, gf_ref[...])


def _ffn(x, g3, wg, wu, wd, gf, tm):
    t = x.shape[0]
    return pl.pallas_call(
        _ffn_kernel,
        out_shape=jax.ShapeDtypeStruct((t, D_MODEL), F32),
        grid=(t // tm,),
        in_specs=[pl.BlockSpec((tm, D_MODEL), lambda i: (i, 0)), _full((1, D_MODEL)),
                  _full((D_MODEL, D_FF)), _full((D_MODEL, D_FF)), _full((D_FF, D_MODEL)),
                  _full((1, D_MODEL))],
        out_specs=pl.BlockSpec((tm, D_MODEL), lambda i: (i, 0)),
        compiler_params=_params("parallel"),
        name="ffn",
    )(x, g3, wg, wu, wd, gf)


def kernel(x, mem, norm1_g, w_in, shift_mu, w0, w_lora_up, a0, a_lora_up, g_lora_up, k_k, k_a, r_k,
           lnx_w, lnx_b, sgu_ln_w, sgu_ln_b, w_spatial, b_spatial, w_out, norm2_g, mem_norm_g,
           wq_x, wk_x, wv_x, wo_x, norm3_g, w_gate, w_up, w_down, norm_f_g):
    b, t, _ = x.shape
    depth = w_in.shape[0]
    assert depth == 1, "the final RMSNorm is fused into the (single) layer's ffn call"
    assert t % TM_ATTN == 0 and t % TM_FFN == 0
    assert t % TM_DENSE == 0 and TM_DENSE % TB_SCAN == 0 and TB_SCAN % CHUNK == 0
    row = lambda p: p.reshape(1, -1)
    bf = lambda p: p.astype(BF16)
    outs = []
    for bi in range(b):
        xb = x[bi]
        for l in range(depth):
            lora = w_lora_up.shape[1]
            zeros = jnp.zeros((lora, RWKV_WIDTH), F32)
            waup = jnp.concatenate(
                [jnp.concatenate([w_lora_up[l], zeros], axis=1),
                 jnp.concatenate([zeros, a_lora_up[l]], axis=1)], axis=0)
            bias = jnp.repeat(b_spatial[l].T, SGU_WIDTH // SGU_GROUPS, axis=1)

            front = _front(xb, row(norm1_g[l]), bf(w_in[l]), row(shift_mu[l]), row(w0[l]), bf(waup),
                           row(a0[l]), bf(g_lora_up[l]), row(k_k[l]), row(k_a[l]), row(r_k[l]),
                           TM_DENSE)
            prep = dict(zip(_PREP_NAMES, front[:len(_PREP_NAMES)]))
            z_sgu = front[len(_PREP_NAMES)]
            y_rwkv = _scan(prep, row(lnx_w[l]), row(lnx_b[l]), TB_SCAN)
            k_mem, v_mem = _mem_kv(mem[bi], row(mem_norm_g[l]), bf(wk_x[l]), bf(wv_x[l]))
            x2 = _mix_attn(xb, y_rwkv, z_sgu, row(sgu_ln_w[l]), row(sgu_ln_b[l]), w_spatial[l], bias,
                           bf(w_out[l][:RWKV_WIDTH]), bf(w_out[l][RWKV_WIDTH:]),
                           row(norm2_g[l]), bf(wq_x[l]), k_mem, v_mem, bf(wo_x[l]), TM_ATTN)
            xb = _ffn(x2, row(norm3_g[l]), bf(w_gate[l]), bf(w_up[l]), bf(w_down[l]),
                      row(norm_f_g), TM_FFN)
        outs.append(xb)
    return jnp.stack(outs, axis=0)
```

```python
import functools
import math

import jax
import jax.numpy as jnp
from jax import lax
from jax.experimental import pallas as pl
from jax.experimental.pallas import tpu as pltpu

F32 = jnp.float32
BF16 = jnp.bfloat16

D_MODEL = 1024
RWKV_WIDTH = 512
RWKV_HEAD = 64
LORA_WA = 128
GATE_LORA = 128
RWKV_IN = 3 * RWKV_WIDTH + LORA_WA + GATE_LORA
SGU_WIDTH = 512
SGU_GROUPS = 8
SGU_BLOCK = 128
IN_WIDTH = RWKV_IN + 2 * SGU_WIDTH
MEM_LEN = 256
XA_HEADS = 4
XA_HEAD_DIM = D_MODEL // XA_HEADS
D_FF = 2816
RMS_EPS = 1e-6
LN_EPS = 1e-5
LNX_EPS = 64e-5
EXP_M05 = 0.6065306597126334
LOG2_E = 1.4426950408889634

CHUNK = 64
PAIR = 2 * RWKV_HEAD
N_PAIRS = RWKV_WIDTH // PAIR
TM_DENSE = 512
TM_ATTN = 1024
TM_FFN = 1024
TB_SCAN = 512
SCAN_GROUP_CHUNKS = 8
Z_PAD = 8
WL_ROWS = 8
ATTN_ROW_GROUP = 256
ATTN_STAGGER = 2
FFN_ROW_GROUP = 256
IN_PROJ_COLS = 256
PREP_ROWS = 128
PREP_STREAMS = 4
PREP_PIECES_PER_CHUNK = 10
TERMS_DECAY_CUMSUM = 2
TERMS_HEAD_SUM = 1
VMEM_LIMIT = 56 * 1024 * 1024

_NN = (((1,), (0,)), ((), ()))
_NT = (((1,), (1,)), ((), ()))
_TN = (((0,), (0,)), ((), ()))


def _mm(a, b, dims=_NN):
    return lax.dot_general(a.astype(BF16), b.astype(BF16), dims, preferred_element_type=F32)


def _split_bf16(x, terms):
    parts = []
    rem = x
    for _ in range(terms):
        part = rem.astype(BF16)
        rem = rem - part.astype(F32)
        parts.append(part)
    return parts


def _cumsum_rows(ltri01, parts):
    return lax.dot_general(jnp.concatenate([ltri01] * len(parts), axis=1),
                           jnp.concatenate(parts, axis=0), _NN, preferred_element_type=F32)


def _head_sum_parts(parts, seg01):
    cols = []
    for q in range(parts[0].shape[1] // 256):
        acc = None
        for part in parts:
            d = lax.dot_general(part[:, 256 * q:256 * (q + 1)], seg01, _NN,
                                preferred_element_type=F32)
            acc = d if acc is None else acc + d
        cols.append(acc)
    return jnp.concatenate(cols, axis=1)


def _head_sum(x, seg01):
    return _head_sum_parts(_split_bf16(x, TERMS_HEAD_SUM), seg01)


def _seg01():
    li = lax.broadcasted_iota(jnp.int32, (256, 256), 0) >> 6
    lj = lax.broadcasted_iota(jnp.int32, (256, 256), 1) >> 6
    return (li == lj).astype(BF16)


def _gelu_tanh(x):
    k1 = -2.0 * math.sqrt(2.0 / math.pi) * math.log2(math.e)
    return x / (1.0 + jnp.exp2(x * (k1 + (k1 * 0.044715) * (x * x))))


def _rmsnorm(x, g):
    return x * lax.rsqrt(jnp.mean(x * x, axis=-1, keepdims=True) + RMS_EPS) * g


def _full(shape):
    n = len(shape)
    return pl.BlockSpec(shape, lambda i: (0,) * n, pipeline_mode=pl.Buffered(1))


def _params(sem="arbitrary"):
    return pltpu.CompilerParams(dimension_semantics=(sem,), vmem_limit_bytes=VMEM_LIMIT)


def _run_interleaved(stages, rounds):
    for r in range(rounds):
        for gen, n in stages:
            for _ in range((r + 1) * n // rounds - r * n // rounds):
                next(gen, None)
    for gen, _ in stages:
        assert next(gen, StopIteration) is StopIteration, "piece count too small"


def _run_wavefront(gens, stagger):
    live = list(enumerate(gens))
    r = 0
    while live:
        for entry in list(live):
            g, gen = entry
            if r >= g * stagger and next(gen, StopIteration) is StopIteration:
                live.remove(entry)
        r += 1


def _mem_kv_kernel(mem_ref, g_ref, wk_ref, wv_ref, k_ref, v_ref):
    m = _rmsnorm(mem_ref[...], g_ref[...]).astype(BF16)
    k_ref[...] = jnp.dot(m, wk_ref[...], preferred_element_type=F32).astype(BF16)
    v_ref[...] = jnp.dot(m, wv_ref[...], preferred_element_type=F32).astype(BF16)


def _mem_kv(mem, g, wk, wv):
    return pl.pallas_call(
        _mem_kv_kernel,
        out_shape=(jax.ShapeDtypeStruct((MEM_LEN, D_MODEL), BF16),) * 2,
        grid=(1,),
        in_specs=[_full((MEM_LEN, D_MODEL)), _full((1, D_MODEL)),
                  _full((D_MODEL, D_MODEL)), _full((D_MODEL, D_MODEL))],
        out_specs=(_full((MEM_LEN, D_MODEL)),) * 2,
        compiler_params=_params(),
        name="mem_kv",
    )(mem, g, wk, wv)


_PREP_F32 = ("rt", "at", "gate", "bonus")
_PREP_BF16 = ("bh", "kh", "bc", "kc", "v")
_PREP_NAMES = _PREP_F32 + _PREP_BF16 + ("wl",)


def _in_proj_pieces(x_ref, g_ref, w_ref, z_ref):
    h = _rmsnorm(x_ref[...], g_ref[...]).astype(BF16)
    yield
    for j in range(IN_WIDTH // IN_PROJ_COLS):
        cols = slice(j * IN_PROJ_COLS, (j + 1) * IN_PROJ_COLS)
        z_ref[Z_PAD:, cols] = jnp.dot(h, w_ref[:, cols], preferred_element_type=F32)
        yield


def _rwkv_prep_pieces(z_ref, mu_ref, w0_ref, waup_ref, a0_ref, gup_ref, kk_ref, ka_ref, rk_ref,
                      prep, chunks):
    seg01 = _seg01()
    lane = lax.broadcasted_iota(jnp.int32, (1, LORA_WA), 1)
    row = lax.broadcasted_iota(jnp.int32, (PREP_ROWS, 1), 0)
    ti = lax.broadcasted_iota(jnp.int32, (PREP_ROWS, PREP_ROWS), 0)
    tj = lax.broadcasted_iota(jnp.int32, (PREP_ROWS, PREP_ROWS), 1)
    ltri01 = ((tj <= ti) & ((ti >> 6) == (tj >> 6))).astype(BF16)
    chunks_per_scan_tile = TB_SCAN // CHUNK
    chunks_per_unit = PREP_ROWS // CHUNK
    half_w = RWKV_WIDTH // 2

    def shifted(c, cols):
        z = z_ref[Z_PAD + c * PREP_ROWS:Z_PAD + (c + 1) * PREP_ROWS, cols]
        before = z_ref[Z_PAD + c * PREP_ROWS - 1:Z_PAD + c * PREP_ROWS, cols]
        zprev = jnp.where(row == 0, before, pltpu.roll(z, 1, axis=0))
        return z + (zprev - z) * mu_ref[:, cols]

    for c in chunks:
        rows = slice(c * PREP_ROWS, (c + 1) * PREP_ROWS)
        wa_in = shifted(c, slice(3 * RWKV_WIDTH, 3 * RWKV_WIDTH + LORA_WA))
        wa_in = jnp.where(lane < LORA_WA // 2, jnp.tanh(wa_in), wa_in).astype(BF16)
        gd = jax.nn.sigmoid(shifted(c, slice(3 * RWKV_WIDTH + LORA_WA, RWKV_IN))).astype(BF16)
        for q in range(2):
            hc = slice(q * half_w, (q + 1) * half_w)
            r = shifted(c, hc)
            k = shifted(c, slice(RWKV_WIDTH + q * half_w, RWKV_WIDTH + (q + 1) * half_w))
            v = shifted(c, slice(2 * RWKV_WIDTH + q * half_w, 2 * RWKV_WIDTH + (q + 1) * half_w))
            prep["v"][rows, hc] = v.astype(BF16)
            kk = k * kk_ref[:, hc]
            kk_sq = _split_bf16(kk * kk, TERMS_HEAD_SUM)
            yield
            w_pre = w0_ref[:, hc] + jnp.dot(wa_in, waup_ref[:, hc], preferred_element_type=F32)
            a_pre = a0_ref[:, hc] + jnp.dot(
                wa_in, waup_ref[:, RWKV_WIDTH + q * half_w:RWKV_WIDTH + (q + 1) * half_w],
                preferred_element_type=F32)
            prep["gate"][rows, hc] = jnp.dot(gd, gup_ref[:, hc], preferred_element_type=F32)
            kk_ss = _head_sum_parts(kk_sq, seg01)
            yield
            a = jax.nn.sigmoid(a_pre)
            lw = jax.nn.sigmoid(w_pre) * (-EXP_M05 * LOG2_E)
            lw_parts = _split_bf16(lw, TERMS_DECAY_CUMSUM)
            kk = kk * lax.rsqrt(jnp.maximum(kk_ss, 1e-24))
            kmod = k * ((1.0 - ka_ref[:, hc]) + a * ka_ref[:, hc])
            kka = kk * a
            rkk = _split_bf16(r * kmod * rk_ref[:, hc], TERMS_HEAD_SUM)
            yield
            cs = _cumsum_rows(ltri01, lw_parts)
            prep["bonus"][rows, hc] = _head_sum_parts(rkk, seg01) * v
            yield
            w_inv = jnp.exp2(-cs)
            w_last = [jnp.exp2(cs[(j + 1) * CHUNK - 1:(j + 1) * CHUNK, :])
                      for j in range(chunks_per_unit)]
            w_tail = jnp.concatenate(
                [w_last[j] * w_inv[j * CHUNK:(j + 1) * CHUNK] for j in range(chunks_per_unit)],
                axis=0)
            prep["rt"][rows, hc] = r * jnp.exp2(cs)
            prep["at"][rows, hc] = -kk * jnp.exp2(cs - lw)
            prep["bh"][rows, hc] = (kka * w_inv).astype(BF16)
            prep["kh"][rows, hc] = (kmod * w_inv).astype(BF16)
            prep["bc"][rows, hc] = (kka * w_tail).astype(BF16)
            prep["kc"][rows, hc] = (kmod * w_tail).astype(BF16)
            for j in range(chunks_per_unit):
                cq, cr = divmod(c * chunks_per_unit + j, chunks_per_scan_tile)
                prep["wl"][cq, cr:cr + 1, hc] = w_last[j]
            yield


def _copy_pieces(z_ref, o_ref, *, tm):
    for b in range(tm // SGU_BLOCK):
        o_ref[b * SGU_BLOCK:(b + 1) * SGU_BLOCK, :] = (
            z_ref[Z_PAD + b * SGU_BLOCK:Z_PAD + (b + 1) * SGU_BLOCK, RWKV_IN:])
        yield


def _front_kernel(x_ref, g1_ref, win_ref, mu_ref, w0_ref, waup_ref, a0_ref, gup_ref, kk_ref, ka_ref,
                  rk_ref, *rest, tm):
    n = len(_PREP_NAMES)
    prep = dict(zip(_PREP_NAMES, rest[:n]))
    zs_ref = rest[n]
    z0_ref, z1_ref = rest[n + 1:]
    i = pl.program_id(0)

    @pl.when(i == 0)
    def _():
        z1_ref[...] = jnp.zeros_like(z1_ref)

    def step(z_write, z_read):
        n_chunks = tm // PREP_ROWS
        prep_pieces = [
            _rwkv_prep_pieces(z_read, mu_ref, w0_ref, waup_ref, a0_ref, gup_ref, kk_ref, ka_ref,
                              rk_ref, prep, range(k, n_chunks, PREP_STREAMS))
            for k in range(PREP_STREAMS)]
        n_dot = 1 + IN_WIDTH // IN_PROJ_COLS + 1
        _run_interleaved(
            [(_in_proj_pieces(x_ref, g1_ref, win_ref, z_write), n_dot)]
            + [(gen, PREP_PIECES_PER_CHUNK * n_chunks // PREP_STREAMS + 1) for gen in prep_pieces]
            + [(_copy_pieces(z_read, zs_ref, tm=tm), tm // SGU_BLOCK + 1)],
            rounds=n_dot)
        if TB_SCAN // CHUNK < WL_ROWS:
            prep["wl"][:, TB_SCAN // CHUNK:, :] = jnp.zeros(
                (tm // TB_SCAN, WL_ROWS - TB_SCAN // CHUNK, RWKV_WIDTH), F32)
        z_write[Z_PAD - 1:Z_PAD, :] = z_read[Z_PAD + tm - 1:Z_PAD + tm, :]

    @pl.when((i & 1) == 0)
    def _():
        step(z0_ref, z1_ref)

    @pl.when((i & 1) == 1)
    def _():
        step(z1_ref, z0_ref)


def _front(x, g1, w_in, mu, w0, waup, a0, gup, k_k, k_a, r_k, tm):
    t = x.shape[0]
    n_tiles = t // tm
    vec = _full((1, RWKV_WIDTH))
    out_tile = lambda i: (jnp.maximum(i - 1, 0), 0)
    out_shapes = ([jax.ShapeDtypeStruct((t, RWKV_WIDTH), F32)] * len(_PREP_F32)
                  + [jax.ShapeDtypeStruct((t, RWKV_WIDTH), BF16)] * len(_PREP_BF16)
                  + [jax.ShapeDtypeStruct((t // TB_SCAN, WL_ROWS, RWKV_WIDTH), F32),
                     jax.ShapeDtypeStruct((t, 2 * SGU_WIDTH), F32)])
    out_specs = ([pl.BlockSpec((tm, RWKV_WIDTH), out_tile)] * (len(_PREP_F32) + len(_PREP_BF16))
                 + [pl.BlockSpec((tm // TB_SCAN, WL_ROWS, RWKV_WIDTH),
                                 lambda i: (jnp.maximum(i - 1, 0), 0, 0)),
                    pl.BlockSpec((tm, 2 * SGU_WIDTH), out_tile)])
    return pl.pallas_call(
        functools.partial(_front_kernel, tm=tm),
        out_shape=tuple(out_shapes),
        grid=(n_tiles + 1,),
        in_specs=[pl.BlockSpec((tm, D_MODEL), lambda i: (jnp.minimum(i, n_tiles - 1), 0)),
                  _full((1, D_MODEL)), _full((D_MODEL, IN_WIDTH)), _full((1, RWKV_IN)), vec,
                  _full((LORA_WA, 2 * RWKV_WIDTH)), vec, _full((GATE_LORA, RWKV_WIDTH)),
                  vec, vec, vec],
        out_specs=tuple(out_specs),
        scratch_shapes=[pltpu.VMEM((Z_PAD + tm, IN_WIDTH), F32),
                        pltpu.VMEM((Z_PAD + tm, IN_WIDTH), F32)],
        compiler_params=_params(),
        name="front",
    )(x, g1, w_in, mu, w0, waup, a0, gup, k_k, k_a, r_k)


def _pair_masks():
    t = lax.broadcasted_iota(jnp.int32, (CHUNK, PAIR), 0)
    j = lax.broadcasted_iota(jnp.int32, (CHUNK, PAIR), 1) & (CHUNK - 1)
    strict = j < t
    incl = j <= t
    blk16 = (t >> 4) == (j >> 4)
    blk32 = (t >> 5) == (j >> 5)
    return strict, incl, blk16, blk32


def _bd(x, bd_mask):
    x = x.astype(BF16)
    return jnp.where(bd_mask, jnp.concatenate([x, x], axis=0), 0.0).astype(BF16)


def _unit_lower_inverse_minus_identity(a_list, masks, bd_mask):
    _, _, blk16, blk32 = masks
    ad = [jnp.where(blk16, a, 0.0) for a in a_list]
    ap = [_mm(x, _bd(x, bd_mask)) for x in ad]
    tp = ad
    for _ in range(2):
        both = [_mm(p, jnp.concatenate([_bd(p, bd_mask), _bd(t, bd_mask)], axis=1))
                for p, t in zip(ap, tp)]
        tp = [t + p + b[:, PAIR:] for t, p, b in zip(tp, ap, both)]
        ap = [b[:, :PAIR] for b in both]
    last = [_mm(p, _bd(t, bd_mask)) for p, t in zip(ap, tp)]
    tp = [t + p + x for t, p, x in zip(tp, ap, last)]
    for off_mask in (blk32 & ~blk16, ~blk32):
        off = [jnp.where(off_mask, a, 0.0) for a in a_list]
        x = [o + _mm(t, _bd(o, bd_mask)) for o, t in zip(off, tp)]
        tp = [t + xx + _mm(xx, _bd(t, bd_mask)) for t, xx in zip(tp, x)]
    return tp


def _scan_kernel(rt_ref, at_ref, gate_ref, bonus_ref, bh_ref, kh_ref, bc_ref, kc_ref, v_ref, wl_ref,
                 lnw_ref, lnb_ref, o_ref, s_ref, y_ref, *, tb):
    @pl.when(pl.program_id(0) == 0)
    def _():
        s_ref[...] = jnp.zeros_like(s_ref)

    masks = _pair_masks()
    strict, incl = masks[0], masks[1]
    bi = lax.broadcasted_iota(jnp.int32, (PAIR, PAIR), 0) >> 6
    bj = lax.broadcasted_iota(jnp.int32, (PAIR, PAIR), 1) >> 6
    bd1 = bi == bj
    bd2 = jnp.concatenate([bd1, bd1], axis=1)

    n_chunks = tb // CHUNK

    def chunk_terms(probs):
        cut = lambda ref: [ref[c * CHUNK:(c + 1) * CHUNK, p * PAIR:(p + 1) * PAIR]
                           for c, p in probs]
        rt_p, at_p, bh_p, kh_p, bc_p, kc_p, v_p = map(
            cut, (rt_ref, at_ref, bh_ref, kh_ref, bc_ref, kc_ref, v_ref))
        gram = [_mm(jnp.concatenate([a_, r_], axis=0),
                    jnp.concatenate([_bd(b_, bd1), _bd(k_, bd1)], axis=0), _NT)
                for a_, r_, b_, k_ in zip(at_p, rt_p, bh_p, kh_p)]
        a_ab = [jnp.where(strict, g_[:CHUNK, :PAIR], 0.0) for g_ in gram]
        a_ak = [jnp.where(strict, g_[:CHUNK, PAIR:], 0.0) for g_ in gram]
        b_rb = [jnp.where(incl, g_[CHUNK:, :PAIR], 0.0) for g_ in gram]
        b_rk = [jnp.where(incl, g_[CHUNK:, PAIR:], 0.0) for g_ in gram]
        v_bd = [_bd(x_, bd1) for x_ in v_p]
        rhs = [jnp.concatenate([_mm(m_, x_), a_], axis=1) for m_, x_, a_ in zip(a_ak, v_bd, at_p)]
        tp = _unit_lower_inverse_minus_identity(a_ab, masks, bd1)
        sol = [x_ + _mm(t_, _bd(x_, bd2)) for x_, t_ in zip(rhs, tp)]
        u_v = [x_[:, :PAIR] for x_ in sol]
        a_chk = [x_[:, PAIR:] for x_ in sol]
        p_mat = [jnp.where(bd1, _mm(a_, b_, _TN), 0.0) for a_, b_ in zip(a_chk, bc_p)]
        q_mat = [jnp.where(bd1, _mm(jnp.concatenate([u_.astype(BF16), x_], axis=0),
                                    jnp.concatenate([b_, k_], axis=0), _TN), 0.0)
                 for u_, x_, b_, k_ in zip(u_v, v_p, bc_p, kc_p)]
        r_chk = [r_ + _mm(m_, _bd(a_, bd1)) for r_, m_, a_ in zip(rt_p, b_rb, a_chk)]
        y_v = [_mm(jnp.concatenate([m1, m2], axis=1), jnp.concatenate([_bd(u_, bd1), x_], axis=0))
               for m1, m2, u_, x_ in zip(b_rb, b_rk, u_v, v_bd)]
        return p_mat, q_mat, r_chk, y_v

    p_mat, q_mat, r_chk, y_v = [], [], [], []
    for c0 in range(0, n_chunks, SCAN_GROUP_CHUNKS):
        group = [(c, p) for c in range(c0, c0 + SCAN_GROUP_CHUNKS) for p in range(N_PAIRS)]
        for acc, part in zip((p_mat, q_mat, r_chk, y_v), chunk_terms(group)):
            acc.extend(part)

    s = [s_ref[p] for p in range(N_PAIRS)]
    for c in range(n_chunks):
        idx = [c * N_PAIRS + p for p in range(N_PAIRS)]
        w_last = wl_ref[0, c:c + 1, :]
        s_next = [s[p] * w_last[:, p * PAIR:(p + 1) * PAIR] + _mm(s[p], p_mat[i]) + q_mat[i]
                  for p, i in enumerate(idx)]
        for p, i in enumerate(idx):
            y_ref[c * CHUNK:(c + 1) * CHUNK, p * PAIR:(p + 1) * PAIR] = (
                _mm(r_chk[i], s[p], _NT) + y_v[i])
        s = s_next
    for p in range(N_PAIRS):
        s_ref[p] = s[p]

    seg01 = _seg01()
    y = y_ref[...]
    mean = _head_sum(y, seg01) * (1.0 / RWKV_HEAD)
    d = y - mean
    var = _head_sum(d * d, seg01) * (1.0 / RWKV_HEAD)
    yn = d * lax.rsqrt(var + LNX_EPS) * lnw_ref[...] + lnb_ref[...]
    o_ref[...] = (yn + bonus_ref[...]) * gate_ref[...]


def _scan(prep, lnw, lnb, tb):
    t = prep["rt"].shape[0]
    assert tb // CHUNK <= WL_ROWS
    tile = pl.BlockSpec((tb, RWKV_WIDTH), lambda i: (i, 0))
    vec = _full((1, RWKV_WIDTH))
    return pl.pallas_call(
        functools.partial(_scan_kernel, tb=tb),
        out_shape=jax.ShapeDtypeStruct((t, RWKV_WIDTH), F32),
        grid=(t // tb,),
        in_specs=[tile] * (len(_PREP_F32) + len(_PREP_BF16))
        + [pl.BlockSpec((1, WL_ROWS, RWKV_WIDTH), lambda i: (i, 0, 0)), vec, vec],
        out_specs=tile,
        scratch_shapes=[pltpu.VMEM((N_PAIRS, PAIR, PAIR), F32), pltpu.VMEM((tb, RWKV_WIDTH), F32)],
        compiler_params=_params(),
        name="scan",
    )(*[prep[name] for name in _PREP_NAMES], lnw, lnb)


def _sgu_block_prepare(z, lnw, lnb, sel):
    hz = _gelu_tanh(z)
    u = hz[:, :SGU_WIDTH]
    vf = hz[:, SGU_WIDTH:]
    mu = jnp.mean(vf, axis=-1, keepdims=True)
    d = vf - mu
    var = jnp.mean(d * d, axis=-1, keepdims=True)
    vn = d * lax.rsqrt(var + LN_EPS) * lnw + lnb
    stacks = []
    for p in range(SGU_WIDTH // PAIR):
        vb = vn[:, p * PAIR:(p + 1) * PAIR]
        stacks.append(jnp.where(sel, jnp.concatenate([vb, vb], axis=0), 0.0).astype(BF16))
    return u, stacks


def _sgu_block_mix(u, stacks, wcat, bias):
    return jnp.concatenate(
        [u[:, p * PAIR:(p + 1) * PAIR]
         * (jnp.dot(wcat[p], stacks[p], preferred_element_type=F32) + bias[:, p * PAIR:(p + 1) * PAIR])
         for p in range(SGU_WIDTH // PAIR)], axis=1)


def _mix_attn_group(r, x_ref, yr_ref, zs_ref, slnw_ref, slnb_ref, sbias_ref, wo1_ref, wo2_ref,
                    g2_ref, wq_ref, k_ref, v_ref, wo_ref, o_ref, wcat, sel):
    heads = [slice(hd * XA_HEAD_DIM, (hd + 1) * XA_HEAD_DIM) for hd in range(XA_HEADS)]
    prepared = [_sgu_block_prepare(zs_ref[b:b + SGU_BLOCK, :], slnw_ref[...], slnb_ref[...], sel)
                for b in range(r.start, r.stop, SGU_BLOCK)]
    yield
    x1 = x_ref[r, :] + jnp.dot(yr_ref[r, :].astype(BF16), wo1_ref[...],
                               preferred_element_type=F32)
    y_sgu = jnp.concatenate([_sgu_block_mix(u, st, wcat, sbias_ref[...]) for u, st in prepared],
                            axis=0)
    yield
    x1 = x1 + jnp.dot(y_sgu.astype(BF16), wo2_ref[...], preferred_element_type=F32)
    yield
    h = _rmsnorm(x1, g2_ref[...]).astype(BF16)
    yield
    q = jnp.dot(h, wq_ref[...], preferred_element_type=F32).astype(BF16)
    s = [lax.dot_general(q[:, hl], k_ref[:, hl], _NT, preferred_element_type=F32)
         * (XA_HEAD_DIM ** -0.5) for hl in heads]
    yield
    p = []
    for s_h in s:
        e = jnp.exp(s_h - jnp.max(s_h, axis=-1, keepdims=True))
        p.append((e / jnp.sum(e, axis=-1, keepdims=True)).astype(BF16))
    yield
    o = jnp.concatenate([jnp.dot(p_h, v_ref[:, hl], preferred_element_type=F32)
                         for p_h, hl in zip(p, heads)], axis=1).astype(BF16)
    o_ref[r, :] = x1 + jnp.dot(o, wo_ref[...], preferred_element_type=F32)


def _mix_attn_kernel(x_ref, yr_ref, zs_ref, slnw_ref, slnb_ref, ws_ref, sbias_ref, wo1_ref, wo2_ref,
                     g2_ref, wq_ref, k_ref, v_ref, wo_ref, o_ref):
    tm = x_ref.shape[0]
    ti = lax.broadcasted_iota(jnp.int32, (SGU_BLOCK, SGU_BLOCK), 0)
    tj = lax.broadcasted_iota(jnp.int32, (SGU_BLOCK, SGU_BLOCK), 1)
    tril = tj <= ti
    wcat = [jnp.concatenate([jnp.where(tril, ws_ref[2 * p], 0.0),
                             jnp.where(tril, ws_ref[2 * p + 1], 0.0)], axis=1).astype(BF16)
            for p in range(SGU_WIDTH // PAIR)]
    bi = lax.broadcasted_iota(jnp.int32, (2 * SGU_BLOCK, PAIR), 0) >> 7
    bj = lax.broadcasted_iota(jnp.int32, (2 * SGU_BLOCK, PAIR), 1) >> 6
    sel = bi == bj
    gens = [_mix_attn_group(slice(r, r + ATTN_ROW_GROUP), x_ref, yr_ref, zs_ref, slnw_ref, slnb_ref,
                            sbias_ref, wo1_ref, wo2_ref, g2_ref, wq_ref, k_ref, v_ref, wo_ref,
                            o_ref, wcat, sel)
            for r in range(0, tm, ATTN_ROW_GROUP)]
    _run_wavefront(gens, ATTN_STAGGER)


def _mix_attn(x, yr, zs, slnw, slnb, ws, sbias, wo1, wo2, g2, wq, k, v, wo, tm):
    t = x.shape[0]
    sq = _full((D_MODEL, D_MODEL))
    half = _full((RWKV_WIDTH, D_MODEL))
    return pl.pallas_call(
        _mix_attn_kernel,
        out_shape=jax.ShapeDtypeStruct((t, D_MODEL), F32),
        grid=(t // tm,),
        in_specs=[pl.BlockSpec((tm, D_MODEL), lambda i: (i, 0)),
                  pl.BlockSpec((tm, RWKV_WIDTH), lambda i: (i, 0)),
                  pl.BlockSpec((tm, 2 * SGU_WIDTH), lambda i: (i, 0)),
                  _full((1, SGU_WIDTH)), _full((1, SGU_WIDTH)),
                  _full((SGU_GROUPS, SGU_BLOCK, SGU_BLOCK)), _full((SGU_BLOCK, SGU_WIDTH)),
                  half, half, _full((1, D_MODEL)), sq,
                  _full((MEM_LEN, D_MODEL)), _full((MEM_LEN, D_MODEL)), sq],
        out_specs=pl.BlockSpec((tm, D_MODEL), lambda i: (i, 0)),
        compiler_params=_params("parallel"),
        name="mix_attn",
    )(x, yr, zs, slnw, slnb, ws, sbias, wo1, wo2, g2, wq, k, v, wo)


def _ffn_kernel(x_ref, g3_ref, wg_ref, wu_ref, wd_ref, gf_ref, o_ref):
    tm = x_ref.shape[0]
    groups = [slice(r, r + FFN_ROW_GROUP) for r in range(0, tm, FFN_ROW_GROUP)]
    x2 = [x_ref[r, :] for r in groups]
    h = [_rmsnorm(x, g3_ref[...]).astype(BF16) for x in x2]
    gate = [jnp.dot(h_, wg_ref[...], preferred_element_type=F32) for h_ in h]
    up = [jnp.dot(h_, wu_ref[...], preferred_element_type=F32) for h_ in h]
    act = [(jax.nn.silu(g_) * u_).astype(BF16) for g_, u_ in zip(gate, up)]
    x3 = [x + jnp.dot(a_, wd_ref[...], preferred_element_type=F32) for x, a_ in zip(x2, act)]
    for r, x in zip(groups, x3):
        o_ref[r, :] = _rmsnorm(x, gf_ref[...])


def _ffn(x, g3, wg, wu, wd, gf, tm):
    t = x.shape[0]
    return pl.pallas_call(
        _ffn_kernel,
        out_shape=jax.ShapeDtypeStruct((t, D_MODEL), F32),
        grid=(t // tm,),
        in_specs=[pl.BlockSpec((tm, D_MODEL), lambda i: (i, 0)), _full((1, D_MODEL)),
                  _full((D_MODEL, D_FF)), _full((D_MODEL, D_FF)), _full((D_FF, D_MODEL)),
                  _full((1, D_MODEL))],
        out_specs=pl.BlockSpec((tm, D_MODEL), lambda i: (i, 0)),
        compiler_params=_params("parallel"),
        name="ffn",
    )(x, g3, wg, wu, wd, gf)


def kernel(x, mem, norm1_g, w_in, shift_mu, w0, w_lora_up, a0, a_lora_up, g_lora_up, k_k, k_a, r_k,
           lnx_w, lnx_b, sgu_ln_w, sgu_ln_b, w_spatial, b_spatial, w_out, norm2_g, mem_norm_g,
           wq_x, wk_x, wv_x, wo_x, norm3_g, w_gate, w_up, w_down, norm_f_g):
    b, t, _ = x.shape
    depth = w_in.shape[0]
    assert depth == 1, "the final RMSNorm is fused into the (single) layer's ffn call"
    assert t % TM_ATTN == 0 and t % TM_FFN == 0
    assert t % TM_DENSE == 0 and TM_DENSE % TB_SCAN == 0 and TB_SCAN % CHUNK == 0
    row = lambda p: p.reshape(1, -1)
    bf = lambda p: p.astype(BF16)
    outs = []
    for bi in range(b):
        xb = x[bi]
        for l in range(depth):
            lora = w_lora_up.shape[1]
            zeros = jnp.zeros((lora, RWKV_WIDTH), F32)
            waup = jnp.concatenate(
                [jnp.concatenate([w_lora_up[l], zeros], axis=1),
                 jnp.concatenate([zeros, a_lora_up[l]], axis=1)], axis=0)
            bias = jnp.repeat(b_spatial[l].T, SGU_WIDTH // SGU_GROUPS, axis=1)

            front = _front(xb, row(norm1_g[l]), bf(w_in[l]), row(shift_mu[l]), row(w0[l]), bf(waup),
                           row(a0[l]), bf(g_lora_up[l]), row(k_k[l]), row(k_a[l]), row(r_k[l]),
                           TM_DENSE)
            prep = dict(zip(_PREP_NAMES, front[:len(_PREP_NAMES)]))
            z_sgu = front[len(_PREP_NAMES)]
            y_rwkv = _scan(prep, row(lnx_w[l]), row(lnx_b[l]), TB_SCAN)
            k_mem, v_mem = _mem_kv(mem[bi], row(mem_norm_g[l]), bf(wk_x[l]), bf(wv_x[l]))
            x2 = _mix_attn(xb, y_rwkv, z_sgu, row(sgu_ln_w[l]), row(sgu_ln_b[l]), w_spatial[l], bias,
                           bf(w_out[l][:RWKV_WIDTH]), bf(w_out[l][RWKV_WIDTH:]),
                           row(norm2_g[l]), bf(wq_x[l]), k_mem, v_mem, bf(wo_x[l]), TM_ATTN)
            xb = _ffn(x2, row(norm3_g[l]), bf(w_gate[l]), bf(w_up[l]), bf(w_down[l]),
                      row(norm_f_g), TM_FFN)
        outs.append(xb)
    return jnp.stack(outs, axis=0)
```

```python
import functools
import math

import jax
import jax.numpy as jnp
from jax import lax
from jax.experimental import pallas as pl
from jax.experimental.pallas import tpu as pltpu

F32 = jnp.float32
BF16 = jnp.bfloat16

D_MODEL = 1024
RWKV_WIDTH = 512
RWKV_HEAD = 64
LORA_WA = 128
GATE_LORA = 128
RWKV_IN = 3 * RWKV_WIDTH + LORA_WA + GATE_LORA
SGU_WIDTH = 512
SGU_GROUPS = 8
SGU_BLOCK = 128
IN_WIDTH = RWKV_IN + 2 * SGU_WIDTH
MEM_LEN = 256
XA_HEADS = 4
XA_HEAD_DIM = D_MODEL // XA_HEADS
D_FF = 2816
RMS_EPS = 1e-6
LN_EPS = 1e-5
LNX_EPS = 64e-5
EXP_M05 = 0.6065306597126334
LOG2_E = 1.4426950408889634

CHUNK = 64
PAIR = 2 * RWKV_HEAD
N_PAIRS = RWKV_WIDTH // PAIR
TM_DENSE = 512
TM_ATTN = 1024
TM_FFN = 1024
TB_SCAN = 512
SCAN_GROUP_CHUNKS = 8
Z_PAD = 8
WL_ROWS = 8
ATTN_ROW_GROUP = 256
ATTN_STAGGER = 2
FFN_ROW_GROUP = 256
IN_PROJ_COLS = 256
PREP_ROWS = 128
PREP_STREAMS = 4
PREP_PIECES_PER_CHUNK = 10
TERMS_DECAY_CUMSUM = 2
TERMS_HEAD_SUM = 1
VMEM_LIMIT = 56 * 1024 * 1024

_NN = (((1,), (0,)), ((), ()))
_NT = (((1,), (1,)), ((), ()))
_TN = (((0,), (0,)), ((), ()))


def _mm(a, b, dims=_NN):
    return lax.dot_general(a.astype(BF16), b.astype(BF16), dims, preferred_element_type=F32)


def _split_bf16(x, terms):
    parts = []
    rem = x
    for _ in range(terms):
        part = rem.astype(BF16)
        rem = rem - part.astype(F32)
        parts.append(part)
    return parts


def _cumsum_rows(ltri01, parts):
    return lax.dot_general(jnp.concatenate([ltri01] * len(parts), axis=1),
                           jnp.concatenate(parts, axis=0), _NN, preferred_element_type=F32)


def _head_sum_parts(parts, seg01):
    cols = []
    for q in range(parts[0].shape[1] // 256):
        acc = None
        for part in parts:
            d = lax.dot_general(part[:, 256 * q:256 * (q + 1)], seg01, _NN,
                                preferred_element_type=F32)
            acc = d if acc is None else acc + d
        cols.append(acc)
    return jnp.concatenate(cols, axis=1)


def _head_sum(x, seg01):
    return _head_sum_parts(_split_bf16(x, TERMS_HEAD_SUM), seg01)


def _seg01():
    li = lax.broadcasted_iota(jnp.int32, (256, 256), 0) >> 6
    lj = lax.broadcasted_iota(jnp.int32, (256, 256), 1) >> 6
    return (li == lj).astype(BF16)


def _gelu_tanh(x):
    k1 = -2.0 * math.sqrt(2.0 / math.pi) * math.log2(math.e)
    return x / (1.0 + jnp.exp2(x * (k1 + (k1 * 0.044715) * (x * x))))


def _rmsnorm(x, g):
    return x * lax.rsqrt(jnp.mean(x * x, axis=-1, keepdims=True) + RMS_EPS) * g


def _full(shape):
    n = len(shape)
    return pl.BlockSpec(shape, lambda i: (0,) * n, pipeline_mode=pl.Buffered(1))


def _params(sem="arbitrary"):
    return pltpu.CompilerParams(dimension_semantics=(sem,), vmem_limit_bytes=VMEM_LIMIT)


def _run_interleaved(stages, rounds):
    for r in range(rounds):
        for gen, n in stages:
            for _ in range((r + 1) * n // rounds - r * n // rounds):
                next(gen, None)
    for gen, _ in stages:
        assert next(gen, StopIteration) is StopIteration, "piece count too small"


def _run_wavefront(gens, stagger):
    live = list(enumerate(gens))
    r = 0
    while live:
        for entry in list(live):
            g, gen = entry
            if r >= g * stagger and next(gen, StopIteration) is StopIteration:
                live.remove(entry)
        r += 1


def _mem_kv_kernel(mem_ref, g_ref, wk_ref, wv_ref, k_ref, v_ref):
    m = _rmsnorm(mem_ref[...], g_ref[...]).astype(BF16)
    k_ref[...] = jnp.dot(m, wk_ref[...].astype(BF16), preferred_element_type=F32).astype(BF16)
    v_ref[...] = jnp.dot(m, wv_ref[...].astype(BF16), preferred_element_type=F32).astype(BF16)


def _mem_kv(mem, g, wk, wv):
    return pl.pallas_call(
        _mem_kv_kernel,
        out_shape=(jax.ShapeDtypeStruct((MEM_LEN, D_MODEL), BF16),) * 2,
        grid=(1,),
        in_specs=[_full((MEM_LEN, D_MODEL)), _full((1, D_MODEL)),
                  _full((D_MODEL, D_MODEL)), _full((D_MODEL, D_MODEL))],
        out_specs=(_full((MEM_LEN, D_MODEL)),) * 2,
        compiler_params=_params(),
        name="mem_kv",
    )(mem, g, wk, wv)


_PREP_F32 = ("rt", "at", "gate", "bonus")
_PREP_BF16 = ("bh", "kh", "bc", "kc", "v")
_PREP_NAMES = _PREP_F32 + _PREP_BF16 + ("wl",)


def _in_proj_pieces(x_ref, g_ref, w_ref, z_ref):
    h = _rmsnorm(x_ref[...], g_ref[...]).astype(BF16)
    yield
    for j in range(IN_WIDTH // IN_PROJ_COLS):
        cols = slice(j * IN_PROJ_COLS, (j + 1) * IN_PROJ_COLS)
        z_ref[Z_PAD:, cols] = jnp.dot(h, w_ref[:, cols], preferred_element_type=F32)
        yield


def _rwkv_prep_pieces(z_ref, mu_ref, w0_ref, waup_ref, a0_ref, gup_ref, kk_ref, ka_ref, rk_ref,
                      prep, chunks):
    seg01 = _seg01()
    lane = lax.broadcasted_iota(jnp.int32, (1, LORA_WA), 1)
    row = lax.broadcasted_iota(jnp.int32, (PREP_ROWS, 1), 0)
    ti = lax.broadcasted_iota(jnp.int32, (PREP_ROWS, PREP_ROWS), 0)
    tj = lax.broadcasted_iota(jnp.int32, (PREP_ROWS, PREP_ROWS), 1)
    ltri01 = ((tj <= ti) & ((ti >> 6) == (tj >> 6))).astype(BF16)
    chunks_per_scan_tile = TB_SCAN // CHUNK
    chunks_per_unit = PREP_ROWS // CHUNK
    half_w = RWKV_WIDTH // 2

    def shifted(c, cols):
        z = z_ref[Z_PAD + c * PREP_ROWS:Z_PAD + (c + 1) * PREP_ROWS, cols]
        before = z_ref[Z_PAD + c * PREP_ROWS - 1:Z_PAD + c * PREP_ROWS, cols]
        zprev = jnp.where(row == 0, before, pltpu.roll(z, 1, axis=0))
        return z + (zprev - z) * mu_ref[:, cols]

    for c in chunks:
        rows = slice(c * PREP_ROWS, (c + 1) * PREP_ROWS)
        wa_in = shifted(c, slice(3 * RWKV_WIDTH, 3 * RWKV_WIDTH + LORA_WA))
        wa_in = jnp.where(lane < LORA_WA // 2, jnp.tanh(wa_in), wa_in).astype(BF16)
        gd = jax.nn.sigmoid(shifted(c, slice(3 * RWKV_WIDTH + LORA_WA, RWKV_IN))).astype(BF16)
        for q in range(2):
            hc = slice(q * half_w, (q + 1) * half_w)
            r = shifted(c, hc)
            k = shifted(c, slice(RWKV_WIDTH + q * half_w, RWKV_WIDTH + (q + 1) * half_w))
            v = shifted(c, slice(2 * RWKV_WIDTH + q * half_w, 2 * RWKV_WIDTH + (q + 1) * half_w))
            prep["v"][rows, hc] = v.astype(BF16)
            kk = k * kk_ref[:, hc]
            kk_sq = _split_bf16(kk * kk, TERMS_HEAD_SUM)
            yield
            w_pre = w0_ref[:, hc] + jnp.dot(wa_in, waup_ref[:, hc], preferred_element_type=F32)
            a_pre = a0_ref[:, hc] + jnp.dot(
                wa_in, waup_ref[:, RWKV_WIDTH + q * half_w:RWKV_WIDTH + (q + 1) * half_w],
                preferred_element_type=F32)
            prep["gate"][rows, hc] = jnp.dot(gd, gup_ref[:, hc], preferred_element_type=F32)
            kk_ss = _head_sum_parts(kk_sq, seg01)
            yield
            a = jax.nn.sigmoid(a_pre)
            lw = jax.nn.sigmoid(w_pre) * (-EXP_M05 * LOG2_E)
            lw_parts = _split_bf16(lw, TERMS_DECAY_CUMSUM)
            kk = kk * lax.rsqrt(jnp.maximum(kk_ss, 1e-24))
            kmod = k * ((1.0 - ka_ref[:, hc]) + a * ka_ref[:, hc])
            kka = kk * a
            rkk = _split_bf16(r * kmod * rk_ref[:, hc], TERMS_HEAD_SUM)
            yield
            cs = _cumsum_rows(ltri01, lw_parts)
            prep["bonus"][rows, hc] = _head_sum_parts(rkk, seg01) * v
            yield
            w_inv = jnp.exp2(-cs)
            w_last = [jnp.exp2(cs[(j + 1) * CHUNK - 1:(j + 1) * CHUNK, :])
                      for j in range(chunks_per_unit)]
            w_tail = jnp.concatenate(
                [w_last[j] * w_inv[j * CHUNK:(j + 1) * CHUNK] for j in range(chunks_per_unit)],
                axis=0)
            prep["rt"][rows, hc] = r * jnp.exp2(cs)
            prep["at"][rows, hc] = -kk * jnp.exp2(cs - lw)
            prep["bh"][rows, hc] = (kka * w_inv).astype(BF16)
            prep["kh"][rows, hc] = (kmod * w_inv).astype(BF16)
            prep["bc"][rows, hc] = (kka * w_tail).astype(BF16)
            prep["kc"][rows, hc] = (kmod * w_tail).astype(BF16)
            for j in range(chunks_per_unit):
                cq, cr = divmod(c * chunks_per_unit + j, chunks_per_scan_tile)
                prep["wl"][cq, cr:cr + 1, hc] = w_last[j]
            yield


def _copy_pieces(z_ref, o_ref, *, tm):
    for b in range(tm // SGU_BLOCK):
        o_ref[b * SGU_BLOCK:(b + 1) * SGU_BLOCK, :] = (
            z_ref[Z_PAD + b * SGU_BLOCK:Z_PAD + (b + 1) * SGU_BLOCK, RWKV_IN:])
        yield


def _front_kernel(x_ref, g1_ref, win_ref, mu_ref, w0_ref, waup_ref, a0_ref, gup_ref, kk_ref, ka_ref,
                  rk_ref, *rest, tm):
    n = len(_PREP_NAMES)
    prep = dict(zip(_PREP_NAMES, rest[:n]))
    zs_ref = rest[n]
    wbf_ref, z0_ref, z1_ref = rest[n + 1:]
    i = pl.program_id(0)

    @pl.when(i == 0)
    def _():
        z1_ref[...] = jnp.zeros_like(z1_ref)
        wbf_ref[...] = win_ref[...].astype(BF16)

    def step(z_write, z_read):
        n_chunks = tm // PREP_ROWS
        prep_pieces = [
            _rwkv_prep_pieces(z_read, mu_ref, w0_ref, waup_ref, a0_ref, gup_ref, kk_ref, ka_ref,
                              rk_ref, prep, range(k, n_chunks, PREP_STREAMS))
            for k in range(PREP_STREAMS)]
        n_dot = 1 + IN_WIDTH // IN_PROJ_COLS + 1
        _run_interleaved(
            [(_in_proj_pieces(x_ref, g1_ref, wbf_ref, z_write), n_dot)]
            + [(gen, PREP_PIECES_PER_CHUNK * n_chunks // PREP_STREAMS + 1) for gen in prep_pieces]
            + [(_copy_pieces(z_read, zs_ref, tm=tm), tm // SGU_BLOCK + 1)],
            rounds=n_dot)
        if TB_SCAN // CHUNK < WL_ROWS:
            prep["wl"][:, TB_SCAN // CHUNK:, :] = jnp.zeros(
                (tm // TB_SCAN, WL_ROWS - TB_SCAN // CHUNK, RWKV_WIDTH), F32)
        z_write[Z_PAD - 1:Z_PAD, :] = z_read[Z_PAD + tm - 1:Z_PAD + tm, :]

    @pl.when((i & 1) == 0)
    def _():
        step(z0_ref, z1_ref)

    @pl.when((i & 1) == 1)
    def _():
        step(z1_ref, z0_ref)


def _front(x, g1, w_in, mu, w0, waup, a0, gup, k_k, k_a, r_k, tm):
    t = x.shape[0]
    n_tiles = t // tm
    vec = _full((1, RWKV_WIDTH))
    out_tile = lambda i: (jnp.maximum(i - 1, 0), 0)
    out_shapes = ([jax.ShapeDtypeStruct((t, RWKV_WIDTH), F32)] * len(_PREP_F32)
                  + [jax.ShapeDtypeStruct((t, RWKV_WIDTH), BF16)] * len(_PREP_BF16)
                  + [jax.ShapeDtypeStruct((t // TB_SCAN, WL_ROWS, RWKV_WIDTH), F32),
                     jax.ShapeDtypeStruct((t, 2 * SGU_WIDTH), F32)])
    out_specs = ([pl.BlockSpec((tm, RWKV_WIDTH), out_tile)] * (len(_PREP_F32) + len(_PREP_BF16))
                 + [pl.BlockSpec((tm // TB_SCAN, WL_ROWS, RWKV_WIDTH),
                                 lambda i: (jnp.maximum(i - 1, 0), 0, 0)),
                    pl.BlockSpec((tm, 2 * SGU_WIDTH), out_tile)])
    return pl.pallas_call(
        functools.partial(_front_kernel, tm=tm),
        out_shape=tuple(out_shapes),
        grid=(n_tiles + 1,),
        in_specs=[pl.BlockSpec((tm, D_MODEL), lambda i: (jnp.minimum(i, n_tiles - 1), 0)),
                  _full((1, D_MODEL)), _full((D_MODEL, IN_WIDTH)), _full((1, RWKV_IN)), vec,
                  _full((LORA_WA, 2 * RWKV_WIDTH)), vec, _full((GATE_LORA, RWKV_WIDTH)),
                  vec, vec, vec],
        out_specs=tuple(out_specs),
        scratch_shapes=[pltpu.VMEM((D_MODEL, IN_WIDTH), BF16),
                        pltpu.VMEM((Z_PAD + tm, IN_WIDTH), F32),
                        pltpu.VMEM((Z_PAD + tm, IN_WIDTH), F32)],
        compiler_params=_params(),
        name="front",
    )(x, g1, w_in, mu, w0, waup, a0, gup, k_k, k_a, r_k)


def _pair_masks():
    t = lax.broadcasted_iota(jnp.int32, (CHUNK, PAIR), 0)
    j = lax.broadcasted_iota(jnp.int32, (CHUNK, PAIR), 1) & (CHUNK - 1)
    strict = j < t
    incl = j <= t
    blk16 = (t >> 4) == (j >> 4)
    blk32 = (t >> 5) == (j >> 5)
    return strict, incl, blk16, blk32


def _bd(x, bd_mask):
    x = x.astype(BF16)
    return jnp.where(bd_mask, jnp.concatenate([x, x], axis=0), 0.0).astype(BF16)


def _unit_lower_inverse_minus_identity(a_list, masks, bd_mask):
    _, _, blk16, blk32 = masks
    ad = [jnp.where(blk16, a, 0.0) for a in a_list]
    ap = [_mm(x, _bd(x, bd_mask)) for x in ad]
    tp = ad
    for _ in range(2):
        both = [_mm(p, jnp.concatenate([_bd(p, bd_mask), _bd(t, bd_mask)], axis=1))
                for p, t in zip(ap, tp)]
        tp = [t + p + b[:, PAIR:] for t, p, b in zip(tp, ap, both)]
        ap = [b[:, :PAIR] for b in both]
    last = [_mm(p, _bd(t, bd_mask)) for p, t in zip(ap, tp)]
    tp = [t + p + x for t, p, x in zip(tp, ap, last)]
    for off_mask in (blk32 & ~blk16, ~blk32):
        off = [jnp.where(off_mask, a, 0.0) for a in a_list]
        x = [o + _mm(t, _bd(o, bd_mask)) for o, t in zip(off, tp)]
        tp = [t + xx + _mm(xx, _bd(t, bd_mask)) for t, xx in zip(tp, x)]
    return tp


def _scan_kernel(rt_ref, at_ref, gate_ref, bonus_ref, bh_ref, kh_ref, bc_ref, kc_ref, v_ref, wl_ref,
                 lnw_ref, lnb_ref, o_ref, s_ref, y_ref, *, tb):
    @pl.when(pl.program_id(0) == 0)
    def _():
        s_ref[...] = jnp.zeros_like(s_ref)

    masks = _pair_masks()
    strict, incl = masks[0], masks[1]
    bi = lax.broadcasted_iota(jnp.int32, (PAIR, PAIR), 0) >> 6
    bj = lax.broadcasted_iota(jnp.int32, (PAIR, PAIR), 1) >> 6
    bd1 = bi == bj
    bd2 = jnp.concatenate([bd1, bd1], axis=1)

    n_chunks = tb // CHUNK

    def chunk_terms(probs):
        cut = lambda ref: [ref[c * CHUNK:(c + 1) * CHUNK, p * PAIR:(p + 1) * PAIR]
                           for c, p in probs]
        rt_p, at_p, bh_p, kh_p, bc_p, kc_p, v_p = map(
            cut, (rt_ref, at_ref, bh_ref, kh_ref, bc_ref, kc_ref, v_ref))
        gram = [_mm(jnp.concatenate([a_, r_], axis=0),
                    jnp.concatenate([_bd(b_, bd1), _bd(k_, bd1)], axis=0), _NT)
                for a_, r_, b_, k_ in zip(at_p, rt_p, bh_p, kh_p)]
        a_ab = [jnp.where(strict, g_[:CHUNK, :PAIR], 0.0) for g_ in gram]
        a_ak = [jnp.where(strict, g_[:CHUNK, PAIR:], 0.0) for g_ in gram]
        b_rb = [jnp.where(incl, g_[CHUNK:, :PAIR], 0.0) for g_ in gram]
        b_rk = [jnp.where(incl, g_[CHUNK:, PAIR:], 0.0) for g_ in gram]
        v_bd = [_bd(x_, bd1) for x_ in v_p]
        rhs = [jnp.concatenate([_mm(m_, x_), a_], axis=1) for m_, x_, a_ in zip(a_ak, v_bd, at_p)]
        tp = _unit_lower_inverse_minus_identity(a_ab, masks, bd1)
        sol = [x_ + _mm(t_, _bd(x_, bd2)) for x_, t_ in zip(rhs, tp)]
        u_v = [x_[:, :PAIR] for x_ in sol]
        a_chk = [x_[:, PAIR:] for x_ in sol]
        p_mat = [jnp.where(bd1, _mm(a_, b_, _TN), 0.0) for a_, b_ in zip(a_chk, bc_p)]
        q_mat = [jnp.where(bd1, _mm(jnp.concatenate([u_.astype(BF16), x_], axis=0),
                                    jnp.concatenate([b_, k_], axis=0), _TN), 0.0)
                 for u_, x_, b_, k_ in zip(u_v, v_p, bc_p, kc_p)]
        r_chk = [r_ + _mm(m_, _bd(a_, bd1)) for r_, m_, a_ in zip(rt_p, b_rb, a_chk)]
        y_v = [_mm(jnp.concatenate([m1, m2], axis=1), jnp.concatenate([_bd(u_, bd1), x_], axis=0))
               for m1, m2, u_, x_ in zip(b_rb, b_rk, u_v, v_bd)]
        return p_mat, q_mat, r_chk, y_v

    p_mat, q_mat, r_chk, y_v = [], [], [], []
    for c0 in range(0, n_chunks, SCAN_GROUP_CHUNKS):
        group = [(c, p) for c in range(c0, c0 + SCAN_GROUP_CHUNKS) for p in range(N_PAIRS)]
        for acc, part in zip((p_mat, q_mat, r_chk, y_v), chunk_terms(group)):
            acc.extend(part)

    s = [s_ref[p] for p in range(N_PAIRS)]
    for c in range(n_chunks):
        idx = [c * N_PAIRS + p for p in range(N_PAIRS)]
        w_last = wl_ref[0, c:c + 1, :]
        s_next = [s[p] * w_last[:, p * PAIR:(p + 1) * PAIR] + _mm(s[p], p_mat[i]) + q_mat[i]
                  for p, i in enumerate(idx)]
        for p, i in enumerate(idx):
            y_ref[c * CHUNK:(c + 1) * CHUNK, p * PAIR:(p + 1) * PAIR] = (
                _mm(r_chk[i], s[p], _NT) + y_v[i])
        s = s_next
    for p in range(N_PAIRS):
        s_ref[p] = s[p]

    seg01 = _seg01()
    y = y_ref[...]
    mean = _head_sum(y, seg01) * (1.0 / RWKV_HEAD)
    d = y - mean
    var = _head_sum(d * d, seg01) * (1.0 / RWKV_HEAD)
    yn = d * lax.rsqrt(var + LNX_EPS) * lnw_ref[...] + lnb_ref[...]
    o_ref[...] = (yn + bonus_ref[...]) * gate_ref[...]


def _scan(prep, lnw, lnb, tb):
    t = prep["rt"].shape[0]
    assert tb // CHUNK <= WL_ROWS
    tile = pl.BlockSpec((tb, RWKV_WIDTH), lambda i: (i, 0))
    vec = _full((1, RWKV_WIDTH))
    return pl.pallas_call(
        functools.partial(_scan_kernel, tb=tb),
        out_shape=jax.ShapeDtypeStruct((t, RWKV_WIDTH), F32),
        grid=(t // tb,),
        in_specs=[tile] * (len(_PREP_F32) + len(_PREP_BF16))
        + [pl.BlockSpec((1, WL_ROWS, RWKV_WIDTH), lambda i: (i, 0, 0)), vec, vec],
        out_specs=tile,
        scratch_shapes=[pltpu.VMEM((N_PAIRS, PAIR, PAIR), F32), pltpu.VMEM((tb, RWKV_WIDTH), F32)],
        compiler_params=_params(),
        name="scan",
    )(*[prep[name] for name in _PREP_NAMES], lnw, lnb)


def _sgu_block_prepare(z, lnw, lnb, sel):
    hz = _gelu_tanh(z)
    u = hz[:, :SGU_WIDTH]
    vf = hz[:, SGU_WIDTH:]
    mu = jnp.mean(vf, axis=-1, keepdims=True)
    d = vf - mu
    var = jnp.mean(d * d, axis=-1, keepdims=True)
    vn = d * lax.rsqrt(var + LN_EPS) * lnw + lnb
    stacks = []
    for p in range(SGU_WIDTH // PAIR):
        vb = vn[:, p * PAIR:(p + 1) * PAIR]
        stacks.append(jnp.where(sel, jnp.concatenate([vb, vb], axis=0), 0.0).astype(BF16))
    return u, stacks


def _sgu_block_mix(u, stacks, wcat, bias):
    return jnp.concatenate(
        [u[:, p * PAIR:(p + 1) * PAIR]
         * (jnp.dot(wcat[p], stacks[p], preferred_element_type=F32) + bias[:, p * PAIR:(p + 1) * PAIR])
         for p in range(SGU_WIDTH // PAIR)], axis=1)


def _mix_attn_group(r, x_ref, yr_ref, zs_ref, slnw_ref, slnb_ref, sbias_ref, wo1_ref, wo2_ref,
                    g2_ref, wq_ref, k_ref, v_ref, wo_ref, o_ref, wcat, sel):
    heads = [slice(hd * XA_HEAD_DIM, (hd + 1) * XA_HEAD_DIM) for hd in range(XA_HEADS)]
    prepared = [_sgu_block_prepare(zs_ref[b:b + SGU_BLOCK, :], slnw_ref[...], slnb_ref[...], sel)
                for b in range(r.start, r.stop, SGU_BLOCK)]
    yield
    x1 = x_ref[r, :] + jnp.dot(yr_ref[r, :].astype(BF16), wo1_ref[...],
                               preferred_element_type=F32)
    y_sgu = jnp.concatenate([_sgu_block_mix(u, st, wcat, sbias_ref[...]) for u, st in prepared],
                            axis=0)
    yield
    x1 = x1 + jnp.dot(y_sgu.astype(BF16), wo2_ref[...], preferred_element_type=F32)
    yield
    h = _rmsnorm(x1, g2_ref[...]).astype(BF16)
    yield
    q = jnp.dot(h, wq_ref[...], preferred_element_type=F32).astype(BF16)
    s = [lax.dot_general(q[:, hl], k_ref[:, hl], _NT, preferred_element_type=F32)
         * (XA_HEAD_DIM ** -0.5) for hl in heads]
    yield
    p = []
    for s_h in s:
        e = jnp.exp(s_h - jnp.max(s_h, axis=-1, keepdims=True))
        p.append((e / jnp.sum(e, axis=-1, keepdims=True)).astype(BF16))
    yield
    o = jnp.concatenate([jnp.dot(p_h, v_ref[:, hl], preferred_element_type=F32)
                         for p_h, hl in zip(p, heads)], axis=1).astype(BF16)
    o_ref[r, :] = x1 + jnp.dot(o, wo_ref[...], preferred_element_type=F32)


def _mix_attn_kernel(x_ref, yr_ref, zs_ref, slnw_ref, slnb_ref, ws_ref, sbias_ref, wout_f32_ref,
                     g2_ref, wq_f32_ref, k_ref, v_ref, wo_f32_ref, o_ref, wout_ref, wq_ref, wo_ref):
    @pl.when(pl.program_id(0) == 0)
    def _():
        wout_ref[...] = wout_f32_ref[...].astype(BF16)
        wq_ref[...] = wq_f32_ref[...].astype(BF16)
        wo_ref[...] = wo_f32_ref[...].astype(BF16)

    wo1_ref = wout_ref.at[:RWKV_WIDTH]
    wo2_ref = wout_ref.at[RWKV_WIDTH:]
    tm = x_ref.shape[0]
    ti = lax.broadcasted_iota(jnp.int32, (SGU_BLOCK, SGU_BLOCK), 0)
    tj = lax.broadcasted_iota(jnp.int32, (SGU_BLOCK, SGU_BLOCK), 1)
    tril = tj <= ti
    wcat = [jnp.concatenate([jnp.where(tril, ws_ref[2 * p], 0.0),
                             jnp.where(tril, ws_ref[2 * p + 1], 0.0)], axis=1).astype(BF16)
            for p in range(SGU_WIDTH // PAIR)]
    bi = lax.broadcasted_iota(jnp.int32, (2 * SGU_BLOCK, PAIR), 0) >> 7
    bj = lax.broadcasted_iota(jnp.int32, (2 * SGU_BLOCK, PAIR), 1) >> 6
    sel = bi == bj
    gens = [_mix_attn_group(slice(r, r + ATTN_ROW_GROUP), x_ref, yr_ref, zs_ref, slnw_ref, slnb_ref,
                            sbias_ref, wo1_ref, wo2_ref, g2_ref, wq_ref, k_ref, v_ref, wo_ref,
                            o_ref, wcat, sel)
            for r in range(0, tm, ATTN_ROW_GROUP)]
    _run_wavefront(gens, ATTN_STAGGER)


def _mix_attn(x, yr, zs, slnw, slnb, ws, sbias, w_out, g2, wq, k, v, wo, tm):
    t = x.shape[0]
    sq = _full((D_MODEL, D_MODEL))
    return pl.pallas_call(
        _mix_attn_kernel,
        out_shape=jax.ShapeDtypeStruct((t, D_MODEL), F32),
        grid=(t // tm,),
        in_specs=[pl.BlockSpec((tm, D_MODEL), lambda i: (i, 0)),
                  pl.BlockSpec((tm, RWKV_WIDTH), lambda i: (i, 0)),
                  pl.BlockSpec((tm, 2 * SGU_WIDTH), lambda i: (i, 0)),
                  _full((1, SGU_WIDTH)), _full((1, SGU_WIDTH)),
                  _full((SGU_GROUPS, SGU_BLOCK, SGU_BLOCK)), _full((SGU_BLOCK, SGU_WIDTH)),
                  sq, _full((1, D_MODEL)), sq,
                  _full((MEM_LEN, D_MODEL)), _full((MEM_LEN, D_MODEL)), sq],
        out_specs=pl.BlockSpec((tm, D_MODEL), lambda i: (i, 0)),
        scratch_shapes=[pltpu.VMEM((D_MODEL, D_MODEL), BF16)] * 3,
        compiler_params=_params(),
        name="mix_attn",
    )(x, yr, zs, slnw, slnb, ws, sbias, w_out, g2, wq, k, v, wo)


def _ffn_kernel(x_ref, g3_ref, wg_ref, wu_ref, wd_ref, gf_ref, o_ref):
    tm = x_ref.shape[0]
    groups = [slice(r, r + FFN_ROW_GROUP) for r in range(0, tm, FFN_ROW_GROUP)]
    x2 = [x_ref[r, :] for r in groups]
    h = [_rmsnorm(x, g3_ref[...]).astype(BF16) for x in x2]
    gate = [jnp.dot(h_, wg_ref[...], preferred_element_type=F32) for h_ in h]
    up = [jnp.dot(h_, wu_ref[...], preferred_element_type=F32) for h_ in h]
    act = [(jax.nn.silu(g_) * u_).astype(BF16) for g_, u_ in zip(gate, up)]
    x3 = [x + jnp.dot(a_, wd_ref[...], preferred_element_type=F32) for x, a_ in zip(x2, act)]
    for r, x in zip(groups, x3):
        o_ref[r, :] = _rmsnorm(x, gf_ref[...])


def _ffn(x, g3, wg, wu, wd, gf, tm):
    t = x.shape[0]
    return pl.pallas_call(
        _ffn_kernel,
        out_shape=jax.ShapeDtypeStruct((t, D_MODEL), F32),
        grid=(t // tm,),
        in_specs=[pl.BlockSpec((tm, D_MODEL), lambda i: (i, 0)), _full((1, D_MODEL)),
                  _full((D_MODEL, D_FF)), _full((D_MODEL, D_FF)), _full((D_FF, D_MODEL)),
                  _full((1, D_MODEL))],
        out_specs=pl.BlockSpec((tm, D_MODEL), lambda i: (i, 0)),
        compiler_params=_params("parallel"),
        name="ffn",
    )(x, g3, wg, wu, wd, gf)


def kernel(x, mem, norm1_g, w_in, shift_mu, w0, w_lora_up, a0, a_lora_up, g_lora_up, k_k, k_a, r_k,
           lnx_w, lnx_b, sgu_ln_w, sgu_ln_b, w_spatial, b_spatial, w_out, norm2_g, mem_norm_g,
           wq_x, wk_x, wv_x, wo_x, norm3_g, w_gate, w_up, w_down, norm_f_g):
    b, t, _ = x.shape
    depth = w_in.shape[0]
    assert depth == 1, "the final RMSNorm is fused into the (single) layer's ffn call"
    assert t % TM_ATTN == 0 and t % TM_FFN == 0
    assert t % TM_DENSE == 0 and TM_DENSE % TB_SCAN == 0 and TB_SCAN % CHUNK == 0
    row = lambda p: p.reshape(1, -1)
    bf = lambda p: p.astype(BF16)
    outs = []
    for bi in range(b):
        xb = x[bi]
        for l in range(depth):
            lora = w_lora_up.shape[1]
            zeros = jnp.zeros((lora, RWKV_WIDTH), F32)
            waup = jnp.concatenate(
                [jnp.concatenate([w_lora_up[l], zeros], axis=1),
                 jnp.concatenate([zeros, a_lora_up[l]], axis=1)], axis=0)
            bias = jnp.repeat(b_spatial[l].T, SGU_WIDTH // SGU_GROUPS, axis=1)

            front = _front(xb, row(norm1_g[l]), w_in[l], row(shift_mu[l]), row(w0[l]), bf(waup),
                           row(a0[l]), bf(g_lora_up[l]), row(k_k[l]), row(k_a[l]), row(r_k[l]),
                           TM_DENSE)
            prep = dict(zip(_PREP_NAMES, front[:len(_PREP_NAMES)]))
            z_sgu = front[len(_PREP_NAMES)]
            y_rwkv = _scan(prep, row(lnx_w[l]), row(lnx_b[l]), TB_SCAN)
            k_mem, v_mem = _mem_kv(mem[bi], row(mem_norm_g[l]), wk_x[l], wv_x[l])
            x2 = _mix_attn(xb, y_rwkv, z_sgu, row(sgu_ln_w[l]), row(sgu_ln_b[l]), w_spatial[l], bias,
                           w_out[l], row(norm2_g[l]), wq_x[l], k_mem, v_mem, wo_x[l], TM_ATTN)
            xb = _ffn(x2, row(norm3_g[l]), bf(w_gate[l]), bf(w_up[l]), bf(w_down[l]),
                      row(norm_f_g), TM_FFN)
        outs.append(xb)
    return jnp.stack(outs, axis=0)
```

```python
import functools
import math

import jax
import jax.numpy as jnp
from jax import lax
from jax.experimental import pallas as pl
from jax.experimental.pallas import tpu as pltpu

F32 = jnp.float32
BF16 = jnp.bfloat16

D_MODEL = 1024
RWKV_WIDTH = 512
RWKV_HEAD = 64
LORA_WA = 128
GATE_LORA = 128
RWKV_IN = 3 * RWKV_WIDTH + LORA_WA + GATE_LORA
SGU_WIDTH = 512
SGU_GROUPS = 8
SGU_BLOCK = 128
IN_WIDTH = RWKV_IN + 2 * SGU_WIDTH
MEM_LEN = 256
XA_HEADS = 4
XA_HEAD_DIM = D_MODEL // XA_HEADS
D_FF = 2816
RMS_EPS = 1e-6
LN_EPS = 1e-5
LNX_EPS = 64e-5
EXP_M05 = 0.6065306597126334
LOG2_E = 1.4426950408889634

CHUNK = 64
PAIR = 2 * RWKV_HEAD
N_PAIRS = RWKV_WIDTH // PAIR
TM_DENSE = 512
TM_ATTN = 1024
TM_FFN = 1024
TB_SCAN = 512
SCAN_GROUP_CHUNKS = 8
Z_PAD = 8
WL_ROWS = 8
ATTN_ROW_GROUP = 256
ATTN_STAGGER = 2
FFN_ROW_GROUP = 256
IN_PROJ_COLS = 256
PREP_ROWS = 128
PREP_STREAMS = 4
PREP_PIECES_PER_CHUNK = 10
TERMS_DECAY_CUMSUM = 2
TERMS_HEAD_SUM = 1
VMEM_LIMIT = 56 * 1024 * 1024

_NN = (((1,), (0,)), ((), ()))
_NT = (((1,), (1,)), ((), ()))
_TN = (((0,), (0,)), ((), ()))


def _mm(a, b, dims=_NN):
    return lax.dot_general(a.astype(BF16), b.astype(BF16), dims, preferred_element_type=F32)


def _split_bf16(x, terms):
    parts = []
    rem = x
    for _ in range(terms):
        part = rem.astype(BF16)
        rem = rem - part.astype(F32)
        parts.append(part)
    return parts


def _cumsum_rows(ltri01, parts):
    return lax.dot_general(jnp.concatenate([ltri01] * len(parts), axis=1),
                           jnp.concatenate(parts, axis=0), _NN, preferred_element_type=F32)


def _head_sum_parts(parts, seg01):
    cols = []
    for q in range(parts[0].shape[1] // 256):
        acc = None
        for part in parts:
            d = lax.dot_general(part[:, 256 * q:256 * (q + 1)], seg01, _NN,
                                preferred_element_type=F32)
            acc = d if acc is None else acc + d
        cols.append(acc)
    return jnp.concatenate(cols, axis=1)


def _head_sum(x, seg01):
    return _head_sum_parts(_split_bf16(x, TERMS_HEAD_SUM), seg01)


def _seg01():
    li = lax.broadcasted_iota(jnp.int32, (256, 256), 0) >> 6
    lj = lax.broadcasted_iota(jnp.int32, (256, 256), 1) >> 6
    return (li == lj).astype(BF16)


def _gelu_tanh(x):
    k1 = -2.0 * math.sqrt(2.0 / math.pi) * math.log2(math.e)
    return x / (1.0 + jnp.exp2(x * (k1 + (k1 * 0.044715) * (x * x))))


def _rmsnorm(x, g):
    return x * lax.rsqrt(jnp.mean(x * x, axis=-1, keepdims=True) + RMS_EPS) * g


def _full(shape):
    n = len(shape)
    return pl.BlockSpec(shape, lambda i: (0,) * n, pipeline_mode=pl.Buffered(1))


def _params(sem="arbitrary"):
    return pltpu.CompilerParams(dimension_semantics=(sem,), vmem_limit_bytes=VMEM_LIMIT)


def _run_interleaved(stages, rounds):
    for r in range(rounds):
        for gen, n in stages:
            for _ in range((r + 1) * n // rounds - r * n // rounds):
                next(gen, None)
    for gen, _ in stages:
        assert next(gen, StopIteration) is StopIteration, "piece count too small"


def _run_wavefront(gens, stagger):
    live = list(enumerate(gens))
    r = 0
    while live:
        for entry in list(live):
            g, gen = entry
            if r >= g * stagger and next(gen, StopIteration) is StopIteration:
                live.remove(entry)
        r += 1


def _mem_kv_kernel(mem_ref, g_ref, wk_ref, wv_ref, k_ref, v_ref):
    m = _rmsnorm(mem_ref[...], g_ref[...]).astype(BF16)
    k_ref[...] = jnp.dot(m, wk_ref[...].astype(BF16), preferred_element_type=F32).astype(BF16)
    v_ref[...] = jnp.dot(m, wv_ref[...].astype(BF16), preferred_element_type=F32).astype(BF16)


def _mem_kv(mem, g, wk, wv):
    return pl.pallas_call(
        _mem_kv_kernel,
        out_shape=(jax.ShapeDtypeStruct((MEM_LEN, D_MODEL), BF16),) * 2,
        grid=(1,),
        in_specs=[_full((MEM_LEN, D_MODEL)), _full((1, D_MODEL)),
                  _full((D_MODEL, D_MODEL)), _full((D_MODEL, D_MODEL))],
        out_specs=(_full((MEM_LEN, D_MODEL)),) * 2,
        compiler_params=_params(),
        name="mem_kv",
    )(mem, g, wk, wv)


_PREP_F32 = ("rt", "at", "gate", "bonus")
_PREP_BF16 = ("bh", "kh", "bc", "kc", "v")
_PREP_NAMES = _PREP_F32 + _PREP_BF16 + ("wl",)


def _in_proj_pieces(x_ref, g_ref, w_ref, z_ref):
    h = _rmsnorm(x_ref[...], g_ref[...]).astype(BF16)
    yield
    for j in range(IN_WIDTH // IN_PROJ_COLS):
        cols = slice(j * IN_PROJ_COLS, (j + 1) * IN_PROJ_COLS)
        z_ref[Z_PAD:, cols] = jnp.dot(h, w_ref[:, cols], preferred_element_type=F32)
        yield


def _rwkv_prep_pieces(z_ref, mu_ref, w0_ref, waup_ref, a0_ref, gup_ref, kk_ref, ka_ref, rk_ref,
                      prep, chunks):
    seg01 = _seg01()
    lane = lax.broadcasted_iota(jnp.int32, (1, LORA_WA), 1)
    row = lax.broadcasted_iota(jnp.int32, (PREP_ROWS, 1), 0)
    ti = lax.broadcasted_iota(jnp.int32, (PREP_ROWS, PREP_ROWS), 0)
    tj = lax.broadcasted_iota(jnp.int32, (PREP_ROWS, PREP_ROWS), 1)
    ltri01 = ((tj <= ti) & ((ti >> 6) == (tj >> 6))).astype(BF16)
    chunks_per_scan_tile = TB_SCAN // CHUNK
    chunks_per_unit = PREP_ROWS // CHUNK
    half_w = RWKV_WIDTH // 2

    def shifted(c, cols):
        z = z_ref[Z_PAD + c * PREP_ROWS:Z_PAD + (c + 1) * PREP_ROWS, cols]
        before = z_ref[Z_PAD + c * PREP_ROWS - 1:Z_PAD + c * PREP_ROWS, cols]
        zprev = jnp.where(row == 0, before, pltpu.roll(z, 1, axis=0))
        return z + (zprev - z) * mu_ref[:, cols]

    for c in chunks:
        rows = slice(c * PREP_ROWS, (c + 1) * PREP_ROWS)
        wa_in = shifted(c, slice(3 * RWKV_WIDTH, 3 * RWKV_WIDTH + LORA_WA))
        wa_in = jnp.where(lane < LORA_WA // 2, jnp.tanh(wa_in), wa_in).astype(BF16)
        gd = jax.nn.sigmoid(shifted(c, slice(3 * RWKV_WIDTH + LORA_WA, RWKV_IN))).astype(BF16)
        for q in range(2):
            hc = slice(q * half_w, (q + 1) * half_w)
            r = shifted(c, hc)
            k = shifted(c, slice(RWKV_WIDTH + q * half_w, RWKV_WIDTH + (q + 1) * half_w))
            v = shifted(c, slice(2 * RWKV_WIDTH + q * half_w, 2 * RWKV_WIDTH + (q + 1) * half_w))
            prep["v"][rows, hc] = v.astype(BF16)
            kk = k * kk_ref[:, hc]
            kk_sq = _split_bf16(kk * kk, TERMS_HEAD_SUM)
            yield
            w_pre = w0_ref[:, hc] + jnp.dot(wa_in, waup_ref[:, hc], preferred_element_type=F32)
            a_pre = a0_ref[:, hc] + jnp.dot(
                wa_in, waup_ref[:, RWKV_WIDTH + q * half_w:RWKV_WIDTH + (q + 1) * half_w],
                preferred_element_type=F32)
            prep["gate"][rows, hc] = jnp.dot(gd, gup_ref[:, hc], preferred_element_type=F32)
            kk_ss = _head_sum_parts(kk_sq, seg01)
            yield
            a = jax.nn.sigmoid(a_pre)
            lw = jax.nn.sigmoid(w_pre) * (-EXP_M05 * LOG2_E)
            lw_parts = _split_bf16(lw, TERMS_DECAY_CUMSUM)
            kk = kk * lax.rsqrt(jnp.maximum(kk_ss, 1e-24))
            kmod = k * ((1.0 - ka_ref[:, hc]) + a * ka_ref[:, hc])
            kka = kk * a
            rkk = _split_bf16(r * kmod * rk_ref[:, hc], TERMS_HEAD_SUM)
            yield
            cs = _cumsum_rows(ltri01, lw_parts)
            prep["bonus"][rows, hc] = _head_sum_parts(rkk, seg01) * v
            yield
            w_inv = jnp.exp2(-cs)
            w_last = [jnp.exp2(cs[(j + 1) * CHUNK - 1:(j + 1) * CHUNK, :])
                      for j in range(chunks_per_unit)]
            w_tail = jnp.concatenate(
                [w_last[j] * w_inv[j * CHUNK:(j + 1) * CHUNK] for j in range(chunks_per_unit)],
                axis=0)
            prep["rt"][rows, hc] = r * jnp.exp2(cs)
            prep["at"][rows, hc] = -kk * jnp.exp2(cs - lw)
            prep["bh"][rows, hc] = (kka * w_inv).astype(BF16)
            prep["kh"][rows, hc] = (kmod * w_inv).astype(BF16)
            prep["bc"][rows, hc] = (kka * w_tail).astype(BF16)
            prep["kc"][rows, hc] = (kmod * w_tail).astype(BF16)
            for j in range(chunks_per_unit):
                cq, cr = divmod(c * chunks_per_unit + j, chunks_per_scan_tile)
                prep["wl"][cq, cr:cr + 1, hc] = w_last[j]
            yield


def _copy_pieces(z_ref, o_ref, *, tm):
    for b in range(tm // SGU_BLOCK):
        o_ref[b * SGU_BLOCK:(b + 1) * SGU_BLOCK, :] = (
            z_ref[Z_PAD + b * SGU_BLOCK:Z_PAD + (b + 1) * SGU_BLOCK, RWKV_IN:])
        yield


def _front_kernel(x_ref, g1_ref, win_ref, mu_ref, w0_ref, waup_ref, a0_ref, gup_ref, kk_ref, ka_ref,
                  rk_ref, *rest, tm):
    n = len(_PREP_NAMES)
    prep = dict(zip(_PREP_NAMES, rest[:n]))
    zs_ref = rest[n]
    wbf_ref, z0_ref, z1_ref = rest[n + 1:]
    i = pl.program_id(0)

    @pl.when(i == 0)
    def _():
        z1_ref[...] = jnp.zeros_like(z1_ref)
        wbf_ref[...] = win_ref[...].astype(BF16)

    def step(z_write, z_read):
        n_chunks = tm // PREP_ROWS
        prep_pieces = [
            _rwkv_prep_pieces(z_read, mu_ref, w0_ref, waup_ref, a0_ref, gup_ref, kk_ref, ka_ref,
                              rk_ref, prep, range(k, n_chunks, PREP_STREAMS))
            for k in range(PREP_STREAMS)]
        n_dot = 1 + IN_WIDTH // IN_PROJ_COLS + 1
        _run_interleaved(
            [(_in_proj_pieces(x_ref, g1_ref, wbf_ref, z_write), n_dot)]
            + [(gen, PREP_PIECES_PER_CHUNK * n_chunks // PREP_STREAMS + 1) for gen in prep_pieces]
            + [(_copy_pieces(z_read, zs_ref, tm=tm), tm // SGU_BLOCK + 1)],
            rounds=n_dot)
        if TB_SCAN // CHUNK < WL_ROWS:
            prep["wl"][:, TB_SCAN // CHUNK:, :] = jnp.zeros(
                (tm // TB_SCAN, WL_ROWS - TB_SCAN // CHUNK, RWKV_WIDTH), F32)
        z_write[Z_PAD - 1:Z_PAD, :] = z_read[Z_PAD + tm - 1:Z_PAD + tm, :]

    @pl.when((i & 1) == 0)
    def _():
        step(z0_ref, z1_ref)

    @pl.when((i & 1) == 1)
    def _():
        step(z1_ref, z0_ref)


def _front(x, g1, w_in, mu, w0, waup, a0, gup, k_k, k_a, r_k, tm):
    t = x.shape[0]
    n_tiles = t // tm
    vec = _full((1, RWKV_WIDTH))
    out_tile = lambda i: (jnp.maximum(i - 1, 0), 0)
    out_shapes = ([jax.ShapeDtypeStruct((t, RWKV_WIDTH), F32)] * len(_PREP_F32)
                  + [jax.ShapeDtypeStruct((t, RWKV_WIDTH), BF16)] * len(_PREP_BF16)
                  + [jax.ShapeDtypeStruct((t // TB_SCAN, WL_ROWS, RWKV_WIDTH), F32),
                     jax.ShapeDtypeStruct((t, 2 * SGU_WIDTH), F32)])
    out_specs = ([pl.BlockSpec((tm, RWKV_WIDTH), out_tile)] * (len(_PREP_F32) + len(_PREP_BF16))
                 + [pl.BlockSpec((tm // TB_SCAN, WL_ROWS, RWKV_WIDTH),
                                 lambda i: (jnp.maximum(i - 1, 0), 0, 0)),
                    pl.BlockSpec((tm, 2 * SGU_WIDTH), out_tile)])
    return pl.pallas_call(
        functools.partial(_front_kernel, tm=tm),
        out_shape=tuple(out_shapes),
        grid=(n_tiles + 1,),
        in_specs=[pl.BlockSpec((tm, D_MODEL), lambda i: (jnp.minimum(i, n_tiles - 1), 0)),
                  _full((1, D_MODEL)), _full((D_MODEL, IN_WIDTH)), _full((1, RWKV_IN)), vec,
                  _full((LORA_WA, 2 * RWKV_WIDTH)), vec, _full((GATE_LORA, RWKV_WIDTH)),
                  vec, vec, vec],
        out_specs=tuple(out_specs),
        scratch_shapes=[pltpu.VMEM((D_MODEL, IN_WIDTH), BF16),
                        pltpu.VMEM((Z_PAD + tm, IN_WIDTH), F32),
                        pltpu.VMEM((Z_PAD + tm, IN_WIDTH), F32)],
        compiler_params=_params(),
        name="front",
    )(x, g1, w_in, mu, w0, waup, a0, gup, k_k, k_a, r_k)


def _pair_masks():
    t = lax.broadcasted_iota(jnp.int32, (CHUNK, PAIR), 0)
    j = lax.broadcasted_iota(jnp.int32, (CHUNK, PAIR), 1) & (CHUNK - 1)
    strict = j < t
    incl = j <= t
    blk16 = (t >> 4) == (j >> 4)
    blk32 = (t >> 5) == (j >> 5)
    return strict, incl, blk16, blk32


def _bd(x, bd_mask):
    x = x.astype(BF16)
    return jnp.where(bd_mask, jnp.concatenate([x, x], axis=0), 0.0).astype(BF16)


def _unit_lower_inverse_minus_identity(a_list, masks, bd_mask):
    _, _, blk16, blk32 = masks
    ad = [jnp.where(blk16, a, 0.0) for a in a_list]
    ap = [_mm(x, _bd(x, bd_mask)) for x in ad]
    tp = ad
    for _ in range(2):
        both = [_mm(p, jnp.concatenate([_bd(p, bd_mask), _bd(t, bd_mask)], axis=1))
                for p, t in zip(ap, tp)]
        tp = [t + p + b[:, PAIR:] for t, p, b in zip(tp, ap, both)]
        ap = [b[:, :PAIR] for b in both]
    last = [_mm(p, _bd(t, bd_mask)) for p, t in zip(ap, tp)]
    tp = [t + p + x for t, p, x in zip(tp, ap, last)]
    for off_mask in (blk32 & ~blk16, ~blk32):
        off = [jnp.where(off_mask, a, 0.0) for a in a_list]
        x = [o + _mm(t, _bd(o, bd_mask)) for o, t in zip(off, tp)]
        tp = [t + xx + _mm(xx, _bd(t, bd_mask)) for t, xx in zip(tp, x)]
    return tp


def _scan_kernel(rt_ref, at_ref, gate_ref, bonus_ref, bh_ref, kh_ref, bc_ref, kc_ref, v_ref, wl_ref,
                 lnw_ref, lnb_ref, wg_ref, wu_ref, wd_ref, o_ref, wg_bf_ref, wu_bf_ref, wd_bf_ref,
                 s_ref, y_ref, *, tb):
    @pl.when(pl.program_id(0) == 0)
    def _():
        s_ref[...] = jnp.zeros_like(s_ref)

    wg_bf_ref[...] = wg_ref[...].astype(BF16)
    wu_bf_ref[...] = wu_ref[...].astype(BF16)
    wd_bf_ref[...] = wd_ref[...].astype(BF16)

    masks = _pair_masks()
    strict, incl = masks[0], masks[1]
    bi = lax.broadcasted_iota(jnp.int32, (PAIR, PAIR), 0) >> 6
    bj = lax.broadcasted_iota(jnp.int32, (PAIR, PAIR), 1) >> 6
    bd1 = bi == bj
    bd2 = jnp.concatenate([bd1, bd1], axis=1)

    n_chunks = tb // CHUNK

    def chunk_terms(probs):
        cut = lambda ref: [ref[c * CHUNK:(c + 1) * CHUNK, p * PAIR:(p + 1) * PAIR]
                           for c, p in probs]
        rt_p, at_p, bh_p, kh_p, bc_p, kc_p, v_p = map(
            cut, (rt_ref, at_ref, bh_ref, kh_ref, bc_ref, kc_ref, v_ref))
        gram = [_mm(jnp.concatenate([a_, r_], axis=0),
                    jnp.concatenate([_bd(b_, bd1), _bd(k_, bd1)], axis=0), _NT)
                for a_, r_, b_, k_ in zip(at_p, rt_p, bh_p, kh_p)]
        a_ab = [jnp.where(strict, g_[:CHUNK, :PAIR], 0.0) for g_ in gram]
        a_ak = [jnp.where(strict, g_[:CHUNK, PAIR:], 0.0) for g_ in gram]
        b_rb = [jnp.where(incl, g_[CHUNK:, :PAIR], 0.0) for g_ in gram]
        b_rk = [jnp.where(incl, g_[CHUNK:, PAIR:], 0.0) for g_ in gram]
        v_bd = [_bd(x_, bd1) for x_ in v_p]
        rhs = [jnp.concatenate([_mm(m_, x_), a_], axis=1) for m_, x_, a_ in zip(a_ak, v_bd, at_p)]
        tp = _unit_lower_inverse_minus_identity(a_ab, masks, bd1)
        sol = [x_ + _mm(t_, _bd(x_, bd2)) for x_, t_ in zip(rhs, tp)]
        u_v = [x_[:, :PAIR] for x_ in sol]
        a_chk = [x_[:, PAIR:] for x_ in sol]
        p_mat = [jnp.where(bd1, _mm(a_, b_, _TN), 0.0) for a_, b_ in zip(a_chk, bc_p)]
        q_mat = [jnp.where(bd1, _mm(jnp.concatenate([u_.astype(BF16), x_], axis=0),
                                    jnp.concatenate([b_, k_], axis=0), _TN), 0.0)
                 for u_, x_, b_, k_ in zip(u_v, v_p, bc_p, kc_p)]
        r_chk = [r_ + _mm(m_, _bd(a_, bd1)) for r_, m_, a_ in zip(rt_p, b_rb, a_chk)]
        y_v = [_mm(jnp.concatenate([m1, m2], axis=1), jnp.concatenate([_bd(u_, bd1), x_], axis=0))
               for m1, m2, u_, x_ in zip(b_rb, b_rk, u_v, v_bd)]
        return p_mat, q_mat, r_chk, y_v

    p_mat, q_mat, r_chk, y_v = [], [], [], []
    for c0 in range(0, n_chunks, SCAN_GROUP_CHUNKS):
        group = [(c, p) for c in range(c0, c0 + SCAN_GROUP_CHUNKS) for p in range(N_PAIRS)]
        for acc, part in zip((p_mat, q_mat, r_chk, y_v), chunk_terms(group)):
            acc.extend(part)

    s = [s_ref[p] for p in range(N_PAIRS)]
    for c in range(n_chunks):
        idx = [c * N_PAIRS + p for p in range(N_PAIRS)]
        w_last = wl_ref[0, c:c + 1, :]
        s_next = [s[p] * w_last[:, p * PAIR:(p + 1) * PAIR] + _mm(s[p], p_mat[i]) + q_mat[i]
                  for p, i in enumerate(idx)]
        for p, i in enumerate(idx):
            y_ref[c * CHUNK:(c + 1) * CHUNK, p * PAIR:(p + 1) * PAIR] = (
                _mm(r_chk[i], s[p], _NT) + y_v[i])
        s = s_next
    for p in range(N_PAIRS):
        s_ref[p] = s[p]

    seg01 = _seg01()
    y = y_ref[...]
    mean = _head_sum(y, seg01) * (1.0 / RWKV_HEAD)
    d = y - mean
    var = _head_sum(d * d, seg01) * (1.0 / RWKV_HEAD)
    yn = d * lax.rsqrt(var + LNX_EPS) * lnw_ref[...] + lnb_ref[...]
    o_ref[...] = (yn + bonus_ref[...]) * gate_ref[...]


def _scan(prep, lnw, lnb, w_gate, w_up, w_down, tb):
    t = prep["rt"].shape[0]
    n_steps = t // tb
    assert tb // CHUNK <= WL_ROWS
    tile = pl.BlockSpec((tb, RWKV_WIDTH), lambda i: (i, 0))
    vec = _full((1, RWKV_WIDTH))
    assert D_MODEL % n_steps == 0 and (D_MODEL // n_steps) % 16 == 0
    assert n_steps % 2 == 0 and D_FF % (n_steps // 2) == 0 and (D_FF // (n_steps // 2)) % 16 == 0
    up_rows = pl.BlockSpec((D_MODEL // n_steps, D_FF), lambda i: (i, 0))
    down_blk = pl.BlockSpec((D_FF // (n_steps // 2), D_MODEL // 2), lambda i: (i // 2, i % 2))
    return pl.pallas_call(
        functools.partial(_scan_kernel, tb=tb),
        out_shape=(jax.ShapeDtypeStruct((t, RWKV_WIDTH), F32),
                   jax.ShapeDtypeStruct((D_MODEL, D_FF), BF16),
                   jax.ShapeDtypeStruct((D_MODEL, D_FF), BF16),
                   jax.ShapeDtypeStruct((D_FF, D_MODEL), BF16)),
        grid=(n_steps,),
        in_specs=[tile] * (len(_PREP_F32) + len(_PREP_BF16))
        + [pl.BlockSpec((1, WL_ROWS, RWKV_WIDTH), lambda i: (i, 0, 0)), vec, vec,
           up_rows, up_rows, down_blk],
        out_specs=(tile, up_rows, up_rows, down_blk),
        scratch_shapes=[pltpu.VMEM((N_PAIRS, PAIR, PAIR), F32), pltpu.VMEM((tb, RWKV_WIDTH), F32)],
        compiler_params=_params(),
        name="scan",
    )(*[prep[name] for name in _PREP_NAMES], lnw, lnb, w_gate, w_up, w_down)


def _sgu_block_prepare(z, lnw, lnb, sel):
    hz = _gelu_tanh(z)
    u = hz[:, :SGU_WIDTH]
    vf = hz[:, SGU_WIDTH:]
    mu = jnp.mean(vf, axis=-1, keepdims=True)
    d = vf - mu
    var = jnp.mean(d * d, axis=-1, keepdims=True)
    vn = d * lax.rsqrt(var + LN_EPS) * lnw + lnb
    stacks = []
    for p in range(SGU_WIDTH // PAIR):
        vb = vn[:, p * PAIR:(p + 1) * PAIR]
        stacks.append(jnp.where(sel, jnp.concatenate([vb, vb], axis=0), 0.0).astype(BF16))
    return u, stacks


def _sgu_block_mix(u, stacks, wcat, bias):
    return jnp.concatenate(
        [u[:, p * PAIR:(p + 1) * PAIR]
         * (jnp.dot(wcat[p], stacks[p], preferred_element_type=F32) + bias[:, p * PAIR:(p + 1) * PAIR])
         for p in range(SGU_WIDTH // PAIR)], axis=1)


def _mix_attn_group(r, x_ref, yr_ref, zs_ref, slnw_ref, slnb_ref, sbias_ref, wo1_ref, wo2_ref,
                    g2_ref, wq_ref, k_ref, v_ref, wo_ref, o_ref, wcat, sel):
    heads = [slice(hd * XA_HEAD_DIM, (hd + 1) * XA_HEAD_DIM) for hd in range(XA_HEADS)]
    prepared = [_sgu_block_prepare(zs_ref[b:b + SGU_BLOCK, :], slnw_ref[...], slnb_ref[...], sel)
                for b in range(r.start, r.stop, SGU_BLOCK)]
    yield
    x1 = x_ref[r, :] + jnp.dot(yr_ref[r, :].astype(BF16), wo1_ref[...],
                               preferred_element_type=F32)
    y_sgu = jnp.concatenate([_sgu_block_mix(u, st, wcat, sbias_ref[...]) for u, st in prepared],
                            axis=0)
    yield
    x1 = x1 + jnp.dot(y_sgu.astype(BF16), wo2_ref[...], preferred_element_type=F32)
    yield
    h = _rmsnorm(x1, g2_ref[...]).astype(BF16)
    yield
    q = jnp.dot(h, wq_ref[...], preferred_element_type=F32).astype(BF16)
    s = [lax.dot_general(q[:, hl], k_ref[:, hl], _NT, preferred_element_type=F32)
         * (XA_HEAD_DIM ** -0.5) for hl in heads]
    yield
    p = []
    for s_h in s:
        e = jnp.exp(s_h - jnp.max(s_h, axis=-1, keepdims=True))
        p.append((e / jnp.sum(e, axis=-1, keepdims=True)).astype(BF16))
    yield
    o = jnp.concatenate([jnp.dot(p_h, v_ref[:, hl], preferred_element_type=F32)
                         for p_h, hl in zip(p, heads)], axis=1).astype(BF16)
    o_ref[r, :] = x1 + jnp.dot(o, wo_ref[...], preferred_element_type=F32)


def _mix_attn_kernel(x_ref, yr_ref, zs_ref, slnw_ref, slnb_ref, ws_ref, sbias_ref, wout_f32_ref,
                     g2_ref, wq_f32_ref, k_ref, v_ref, wo_f32_ref, o_ref, wout_ref, wq_ref, wo_ref):
    @pl.when(pl.program_id(0) == 0)
    def _():
        wout_ref[...] = wout_f32_ref[...].astype(BF16)
        wq_ref[...] = wq_f32_ref[...].astype(BF16)
        wo_ref[...] = wo_f32_ref[...].astype(BF16)

    wo1_ref = wout_ref.at[:RWKV_WIDTH]
    wo2_ref = wout_ref.at[RWKV_WIDTH:]
    tm = x_ref.shape[0]
    ti = lax.broadcasted_iota(jnp.int32, (SGU_BLOCK, SGU_BLOCK), 0)
    tj = lax.broadcasted_iota(jnp.int32, (SGU_BLOCK, SGU_BLOCK), 1)
    tril = tj <= ti
    wcat = [jnp.concatenate([jnp.where(tril, ws_ref[2 * p], 0.0),
                             jnp.where(tril, ws_ref[2 * p + 1], 0.0)], axis=1).astype(BF16)
            for p in range(SGU_WIDTH // PAIR)]
    bi = lax.broadcasted_iota(jnp.int32, (2 * SGU_BLOCK, PAIR), 0) >> 7
    bj = lax.broadcasted_iota(jnp.int32, (2 * SGU_BLOCK, PAIR), 1) >> 6
    sel = bi == bj
    gens = [_mix_attn_group(slice(r, r + ATTN_ROW_GROUP), x_ref, yr_ref, zs_ref, slnw_ref, slnb_ref,
                            sbias_ref, wo1_ref, wo2_ref, g2_ref, wq_ref, k_ref, v_ref, wo_ref,
                            o_ref, wcat, sel)
            for r in range(0, tm, ATTN_ROW_GROUP)]
    _run_wavefront(gens, ATTN_STAGGER)


def _mix_attn(x, yr, zs, slnw, slnb, ws, sbias, w_out, g2, wq, k, v, wo, tm):
    t = x.shape[0]
    sq = _full((D_MODEL, D_MODEL))
    return pl.pallas_call(
        _mix_attn_kernel,
        out_shape=jax.ShapeDtypeStruct((t, D_MODEL), F32),
        grid=(t // tm,),
        in_specs=[pl.BlockSpec((tm, D_MODEL), lambda i: (i, 0)),
                  pl.BlockSpec((tm, RWKV_WIDTH), lambda i: (i, 0)),
                  pl.BlockSpec((tm, 2 * SGU_WIDTH), lambda i: (i, 0)),
                  _full((1, SGU_WIDTH)), _full((1, SGU_WIDTH)),
                  _full((SGU_GROUPS, SGU_BLOCK, SGU_BLOCK)), _full((SGU_BLOCK, SGU_WIDTH)),
                  sq, _full((1, D_MODEL)), sq,
                  _full((MEM_LEN, D_MODEL)), _full((MEM_LEN, D_MODEL)), sq],
        out_specs=pl.BlockSpec((tm, D_MODEL), lambda i: (i, 0)),
        scratch_shapes=[pltpu.VMEM((D_MODEL, D_MODEL), BF16)] * 3,
        compiler_params=_params(),
        name="mix_attn",
    )(x, yr, zs, slnw, slnb, ws, sbias, w_out, g2, wq, k, v, wo)


def _ffn_kernel(x_ref, g3_ref, wg_ref, wu_ref, wd_ref, gf_ref, o_ref):
    tm = x_ref.shape[0]
    groups = [slice(r, r + FFN_ROW_GROUP) for r in range(0, tm, FFN_ROW_GROUP)]
    x2 = [x_ref[r, :] for r in groups]
    h = [_rmsnorm(x, g3_ref[...]).astype(BF16) for x in x2]
    gate = [jnp.dot(h_, wg_ref[...], preferred_element_type=F32) for h_ in h]
    up = [jnp.dot(h_, wu_ref[...], preferred_element_type=F32) for h_ in h]
    act = [(jax.nn.silu(g_) * u_).astype(BF16) for g_, u_ in zip(gate, up)]
    x3 = [x + jnp.dot(a_, wd_ref[...], preferred_element_type=F32) for x, a_ in zip(x2, act)]
    for r, x in zip(groups, x3):
        o_ref[r, :] = _rmsnorm(x, gf_ref[...])


def _ffn(x, g3, wg, wu, wd, gf, tm):
    t = x.shape[0]
    return pl.pallas_call(
        _ffn_kernel,
        out_shape=jax.ShapeDtypeStruct((t, D_MODEL), F32),
        grid=(t // tm,),
        in_specs=[pl.BlockSpec((tm, D_MODEL), lambda i: (i, 0)), _full((1, D_MODEL)),
                  _full((D_MODEL, D_FF)), _full((D_MODEL, D_FF)), _full((D_FF, D_MODEL)),
                  _full((1, D_MODEL))],
        out_specs=pl.BlockSpec((tm, D_MODEL), lambda i: (i, 0)),
        compiler_params=_params("parallel"),
        name="ffn",
    )(x, g3, wg, wu, wd, gf)


def kernel(x, mem, norm1_g, w_in, shift_mu, w0, w_lora_up, a0, a_lora_up, g_lora_up, k_k, k_a, r_k,
           lnx_w, lnx_b, sgu_ln_w, sgu_ln_b, w_spatial, b_spatial, w_out, norm2_g, mem_norm_g,
           wq_x, wk_x, wv_x, wo_x, norm3_g, w_gate, w_up, w_down, norm_f_g):
    b, t, _ = x.shape
    depth = w_in.shape[0]
    assert depth == 1, "the final RMSNorm is fused into the (single) layer's ffn call"
    assert t % TM_ATTN == 0 and t % TM_FFN == 0
    assert t % TM_DENSE == 0 and TM_DENSE % TB_SCAN == 0 and TB_SCAN % CHUNK == 0
    row = lambda p: p.reshape(1, -1)
    bf = lambda p: p.astype(BF16)
    outs = []
    for bi in range(b):
        xb = x[bi]
        for l in range(depth):
            lora = w_lora_up.shape[1]
            zeros = jnp.zeros((lora, RWKV_WIDTH), F32)
            waup = jnp.concatenate(
                [jnp.concatenate([w_lora_up[l], zeros], axis=1),
                 jnp.concatenate([zeros, a_lora_up[l]], axis=1)], axis=0)
            bias = jnp.repeat(b_spatial[l].T, SGU_WIDTH // SGU_GROUPS, axis=1)

            front = _front(xb, row(norm1_g[l]), w_in[l], row(shift_mu[l]), row(w0[l]), bf(waup),
                           row(a0[l]), bf(g_lora_up[l]), row(k_k[l]), row(k_a[l]), row(r_k[l]),
                           TM_DENSE)
            prep = dict(zip(_PREP_NAMES, front[:len(_PREP_NAMES)]))
            z_sgu = front[len(_PREP_NAMES)]
            y_rwkv, wg_bf, wu_bf, wd_bf = _scan(prep, row(lnx_w[l]), row(lnx_b[l]),
                                                w_gate[l], w_up[l], w_down[l], TB_SCAN)
            k_mem, v_mem = _mem_kv(mem[bi], row(mem_norm_g[l]), wk_x[l], wv_x[l])
            x2 = _mix_attn(xb, y_rwkv, z_sgu, row(sgu_ln_w[l]), row(sgu_ln_b[l]), w_spatial[l], bias,
                           w_out[l], row(norm2_g[l]), wq_x[l], k_mem, v_mem, wo_x[l], TM_ATTN)
            xb = _ffn(x2, row(norm3_g[l]), wg_bf, wu_bf, wd_bf, row(norm_f_g), TM_FFN)
        outs.append(xb)
    return jnp.stack(outs, axis=0)
```

```python
import functools
import math

import jax
import jax.numpy as jnp
from jax import lax
from jax.experimental import pallas as pl
from jax.experimental.pallas import tpu as pltpu

F32 = jnp.float32
BF16 = jnp.bfloat16

D_MODEL = 1024
RWKV_WIDTH = 512
RWKV_HEAD = 64
LORA_WA = 128
GATE_LORA = 128
RWKV_IN = 3 * RWKV_WIDTH + LORA_WA + GATE_LORA
SGU_WIDTH = 512
SGU_GROUPS = 8
SGU_BLOCK = 128
IN_WIDTH = RWKV_IN + 2 * SGU_WIDTH
MEM_LEN = 256
XA_HEADS = 4
XA_HEAD_DIM = D_MODEL // XA_HEADS
D_FF = 2816
RMS_EPS = 1e-6
LN_EPS = 1e-5
LNX_EPS = 64e-5
EXP_M05 = 0.6065306597126334
LOG2_E = 1.4426950408889634

CHUNK = 64
PAIR = 2 * RWKV_HEAD
N_PAIRS = RWKV_WIDTH // PAIR
TM_DENSE = 512
TM_ATTN = 1024
TM_FFN = 1024
TB_SCAN = 512
Z_PAD = 8
WL_ROWS = 8
ATTN_ROW_GROUP = 256
ATTN_STAGGER = 2
FFN_ROW_GROUP = 256
IN_PROJ_COLS = 256
PREP_ROWS = 128
PREP_STREAMS = 4
PREP_PIECES_PER_CHUNK = 10
TERMS_DECAY_CUMSUM = 2
TERMS_HEAD_SUM = 1
VMEM_LIMIT = 56 * 1024 * 1024

_NN = (((1,), (0,)), ((), ()))
_NT = (((1,), (1,)), ((), ()))
_TN = (((0,), (0,)), ((), ()))


def _mm(a, b, dims=_NN):
    return lax.dot_general(a.astype(BF16), b.astype(BF16), dims, preferred_element_type=F32)


def _split_bf16(x, terms):
    parts = []
    rem = x
    for _ in range(terms):
        part = rem.astype(BF16)
        rem = rem - part.astype(F32)
        parts.append(part)
    return parts


def _cumsum_rows(ltri01, parts):
    return lax.dot_general(jnp.concatenate([ltri01] * len(parts), axis=1),
                           jnp.concatenate(parts, axis=0), _NN, preferred_element_type=F32)


def _head_sum_parts(parts, seg01):
    cols = []
    for q in range(parts[0].shape[1] // 256):
        acc = None
        for part in parts:
            d = lax.dot_general(part[:, 256 * q:256 * (q + 1)], seg01, _NN,
                                preferred_element_type=F32)
            acc = d if acc is None else acc + d
        cols.append(acc)
    return jnp.concatenate(cols, axis=1)


def _head_sum(x, seg01):
    return _head_sum_parts(_split_bf16(x, TERMS_HEAD_SUM), seg01)


def _seg01():
    li = lax.broadcasted_iota(jnp.int32, (256, 256), 0) >> 6
    lj = lax.broadcasted_iota(jnp.int32, (256, 256), 1) >> 6
    return (li == lj).astype(BF16)


def _gelu_tanh(x):
    k1 = -2.0 * math.sqrt(2.0 / math.pi) * math.log2(math.e)
    return x / (1.0 + jnp.exp2(x * (k1 + (k1 * 0.044715) * (x * x))))


def _rmsnorm(x, g):
    return x * lax.rsqrt(jnp.mean(x * x, axis=-1, keepdims=True) + RMS_EPS) * g


def _full(shape):
    n = len(shape)
    return pl.BlockSpec(shape, lambda i: (0,) * n, pipeline_mode=pl.Buffered(1))


def _params(sem="arbitrary"):
    return pltpu.CompilerParams(dimension_semantics=(sem,), vmem_limit_bytes=VMEM_LIMIT)


def _run_interleaved(stages, rounds):
    for r in range(rounds):
        for gen, n in stages:
            for _ in range((r + 1) * n // rounds - r * n // rounds):
                next(gen, None)
    for gen, _ in stages:
        assert next(gen, StopIteration) is StopIteration, "piece count too small"


def _run_wavefront(gens, stagger):
    live = list(enumerate(gens))
    r = 0
    while live:
        for entry in list(live):
            g, gen = entry
            if r >= g * stagger and next(gen, StopIteration) is StopIteration:
                live.remove(entry)
        r += 1


def _mem_kv_kernel(mem_ref, g_ref, wk_ref, wv_ref, k_ref, v_ref):
    m = _rmsnorm(mem_ref[...], g_ref[...]).astype(BF16)
    k_ref[...] = jnp.dot(m, wk_ref[...].astype(BF16), preferred_element_type=F32).astype(BF16)
    v_ref[...] = jnp.dot(m, wv_ref[...].astype(BF16), preferred_element_type=F32).astype(BF16)


def _mem_kv(mem, g, wk, wv):
    return pl.pallas_call(
        _mem_kv_kernel,
        out_shape=(jax.ShapeDtypeStruct((MEM_LEN, D_MODEL), BF16),) * 2,
        grid=(1,),
        in_specs=[_full((MEM_LEN, D_MODEL)), _full((1, D_MODEL)),
                  _full((D_MODEL, D_MODEL)), _full((D_MODEL, D_MODEL))],
        out_specs=(_full((MEM_LEN, D_MODEL)),) * 2,
        compiler_params=_params(),
        name="mem_kv",
    )(mem, g, wk, wv)


_PREP_F32 = ("rt", "at", "gate", "bonus")
_PREP_BF16 = ("bh", "kh", "bc", "kc", "v")
_PREP_NAMES = _PREP_F32 + _PREP_BF16 + ("wl",)


def _in_proj_pieces(x_ref, g_ref, w_ref, z_ref):
    h = _rmsnorm(x_ref[...], g_ref[...]).astype(BF16)
    yield
    for j in range(IN_WIDTH // IN_PROJ_COLS):
        cols = slice(j * IN_PROJ_COLS, (j + 1) * IN_PROJ_COLS)
        z_ref[Z_PAD:, cols] = jnp.dot(h, w_ref[:, cols], preferred_element_type=F32)
        yield


def _rwkv_prep_pieces(z_ref, mu_ref, w0_ref, waup_ref, a0_ref, gup_ref, kk_ref, ka_ref, rk_ref,
                      prep, chunks):
    seg01 = _seg01()
    lane = lax.broadcasted_iota(jnp.int32, (1, LORA_WA), 1)
    row = lax.broadcasted_iota(jnp.int32, (PREP_ROWS, 1), 0)
    ti = lax.broadcasted_iota(jnp.int32, (PREP_ROWS, PREP_ROWS), 0)
    tj = lax.broadcasted_iota(jnp.int32, (PREP_ROWS, PREP_ROWS), 1)
    ltri01 = ((tj <= ti) & ((ti >> 6) == (tj >> 6))).astype(BF16)
    chunks_per_scan_tile = TB_SCAN // CHUNK
    chunks_per_unit = PREP_ROWS // CHUNK
    half_w = RWKV_WIDTH // 2

    def shifted(c, cols):
        z = z_ref[Z_PAD + c * PREP_ROWS:Z_PAD + (c + 1) * PREP_ROWS, cols]
        before = z_ref[Z_PAD + c * PREP_ROWS - 1:Z_PAD + c * PREP_ROWS, cols]
        zprev = jnp.where(row == 0, before, pltpu.roll(z, 1, axis=0))
        return z + (zprev - z) * mu_ref[:, cols]

    for c in chunks:
        rows = slice(c * PREP_ROWS, (c + 1) * PREP_ROWS)
        wa_in = shifted(c, slice(3 * RWKV_WIDTH, 3 * RWKV_WIDTH + LORA_WA))
        wa_in = jnp.where(lane < LORA_WA // 2, jnp.tanh(wa_in), wa_in).astype(BF16)
        gd = jax.nn.sigmoid(shifted(c, slice(3 * RWKV_WIDTH + LORA_WA, RWKV_IN))).astype(BF16)
        for q in range(2):
            hc = slice(q * half_w, (q + 1) * half_w)
            r = shifted(c, hc)
            k = shifted(c, slice(RWKV_WIDTH + q * half_w, RWKV_WIDTH + (q + 1) * half_w))
            v = shifted(c, slice(2 * RWKV_WIDTH + q * half_w, 2 * RWKV_WIDTH + (q + 1) * half_w))
            prep["v"][rows, hc] = v.astype(BF16)
            kk = k * kk_ref[:, hc]
            kk_sq = _split_bf16(kk * kk, TERMS_HEAD_SUM)
            yield
            w_pre = w0_ref[:, hc] + jnp.dot(wa_in, waup_ref[:, hc], preferred_element_type=F32)
            a_pre = a0_ref[:, hc] + jnp.dot(
                wa_in, waup_ref[:, RWKV_WIDTH + q * half_w:RWKV_WIDTH + (q + 1) * half_w],
                preferred_element_type=F32)
            prep["gate"][rows, hc] = jnp.dot(gd, gup_ref[:, hc], preferred_element_type=F32)
            kk_ss = _head_sum_parts(kk_sq, seg01)
            yield
            a = jax.nn.sigmoid(a_pre)
            lw = jax.nn.sigmoid(w_pre) * (-EXP_M05 * LOG2_E)
            lw_parts = _split_bf16(lw, TERMS_DECAY_CUMSUM)
            kk = kk * lax.rsqrt(jnp.maximum(kk_ss, 1e-24))
            kmod = k * ((1.0 - ka_ref[:, hc]) + a * ka_ref[:, hc])
            kka = kk * a
            rkk = _split_bf16(r * kmod * rk_ref[:, hc], TERMS_HEAD_SUM)
            yield
            cs = _cumsum_rows(ltri01, lw_parts)
            prep["bonus"][rows, hc] = _head_sum_parts(rkk, seg01) * v
            yield
            w_inv = jnp.exp2(-cs)
            w_last = [jnp.exp2(cs[(j + 1) * CHUNK - 1:(j + 1) * CHUNK, :])
                      for j in range(chunks_per_unit)]
            w_tail = jnp.concatenate(
                [w_last[j] * w_inv[j * CHUNK:(j + 1) * CHUNK] for j in range(chunks_per_unit)],
                axis=0)
            prep["rt"][rows, hc] = r * jnp.exp2(cs)
            prep["at"][rows, hc] = -kk * jnp.exp2(cs - lw)
            prep["bh"][rows, hc] = (kka * w_inv).astype(BF16)
            prep["kh"][rows, hc] = (kmod * w_inv).astype(BF16)
            prep["bc"][rows, hc] = (kka * w_tail).astype(BF16)
            prep["kc"][rows, hc] = (kmod * w_tail).astype(BF16)
            for j in range(chunks_per_unit):
                cq, cr = divmod(c * chunks_per_unit + j, chunks_per_scan_tile)
                prep["wl"][cq, cr:cr + 1, hc] = w_last[j]
            yield


def _copy_pieces(z_ref, o_ref, *, tm):
    for b in range(tm // SGU_BLOCK):
        o_ref[b * SGU_BLOCK:(b + 1) * SGU_BLOCK, :] = (
            z_ref[Z_PAD + b * SGU_BLOCK:Z_PAD + (b + 1) * SGU_BLOCK, RWKV_IN:])
        yield


def _front_kernel(x_ref, g1_ref, win_ref, mu_ref, w0_ref, waup_ref, a0_ref, gup_ref, kk_ref, ka_ref,
                  rk_ref, *rest, tm):
    n = len(_PREP_NAMES)
    prep = dict(zip(_PREP_NAMES, rest[:n]))
    zs_ref = rest[n]
    wbf_ref, z0_ref, z1_ref = rest[n + 1:]
    i = pl.program_id(0)

    @pl.when(i == 0)
    def _():
        z1_ref[...] = jnp.zeros_like(z1_ref)
        wbf_ref[...] = win_ref[...].astype(BF16)

    def step(z_write, z_read):
        n_chunks = tm // PREP_ROWS
        prep_pieces = [
            _rwkv_prep_pieces(z_read, mu_ref, w0_ref, waup_ref, a0_ref, gup_ref, kk_ref, ka_ref,
                              rk_ref, prep, range(k, n_chunks, PREP_STREAMS))
            for k in range(PREP_STREAMS)]
        n_dot = 1 + IN_WIDTH // IN_PROJ_COLS + 1
        _run_interleaved(
            [(_in_proj_pieces(x_ref, g1_ref, wbf_ref, z_write), n_dot)]
            + [(gen, PREP_PIECES_PER_CHUNK * n_chunks // PREP_STREAMS + 1) for gen in prep_pieces]
            + [(_copy_pieces(z_read, zs_ref, tm=tm), tm // SGU_BLOCK + 1)],
            rounds=n_dot)
        if TB_SCAN // CHUNK < WL_ROWS:
            prep["wl"][:, TB_SCAN // CHUNK:, :] = jnp.zeros(
                (tm // TB_SCAN, WL_ROWS - TB_SCAN // CHUNK, RWKV_WIDTH), F32)
        z_write[Z_PAD - 1:Z_PAD, :] = z_read[Z_PAD + tm - 1:Z_PAD + tm, :]

    @pl.when((i & 1) == 0)
    def _():
        step(z0_ref, z1_ref)

    @pl.when((i & 1) == 1)
    def _():
        step(z1_ref, z0_ref)


def _front(x, g1, w_in, mu, w0, waup, a0, gup, k_k, k_a, r_k, tm):
    t = x.shape[0]
    n_tiles = t // tm
    vec = _full((1, RWKV_WIDTH))
    out_tile = lambda i: (jnp.maximum(i - 1, 0), 0)
    out_shapes = ([jax.ShapeDtypeStruct((t, RWKV_WIDTH), F32)] * len(_PREP_F32)
                  + [jax.ShapeDtypeStruct((t, RWKV_WIDTH), BF16)] * len(_PREP_BF16)
                  + [jax.ShapeDtypeStruct((t // TB_SCAN, WL_ROWS, RWKV_WIDTH), F32),
                     jax.ShapeDtypeStruct((t, 2 * SGU_WIDTH), F32)])
    out_specs = ([pl.BlockSpec((tm, RWKV_WIDTH), out_tile)] * (len(_PREP_F32) + len(_PREP_BF16))
                 + [pl.BlockSpec((tm // TB_SCAN, WL_ROWS, RWKV_WIDTH),
                                 lambda i: (jnp.maximum(i - 1, 0), 0, 0)),
                    pl.BlockSpec((tm, 2 * SGU_WIDTH), out_tile)])
    return pl.pallas_call(
        functools.partial(_front_kernel, tm=tm),
        out_shape=tuple(out_shapes),
        grid=(n_tiles + 1,),
        in_specs=[pl.BlockSpec((tm, D_MODEL), lambda i: (jnp.minimum(i, n_tiles - 1), 0)),
                  _full((1, D_MODEL)), _full((D_MODEL, IN_WIDTH)), _full((1, RWKV_IN)), vec,
                  _full((LORA_WA, 2 * RWKV_WIDTH)), vec, _full((GATE_LORA, RWKV_WIDTH)),
                  vec, vec, vec],
        out_specs=tuple(out_specs),
        scratch_shapes=[pltpu.VMEM((D_MODEL, IN_WIDTH), BF16),
                        pltpu.VMEM((Z_PAD + tm, IN_WIDTH), F32),
                        pltpu.VMEM((Z_PAD + tm, IN_WIDTH), F32)],
        compiler_params=_params(),
        name="front",
    )(x, g1, w_in, mu, w0, waup, a0, gup, k_k, k_a, r_k)


def _pair_masks():
    t = lax.broadcasted_iota(jnp.int32, (CHUNK, PAIR), 0)
    j = lax.broadcasted_iota(jnp.int32, (CHUNK, PAIR), 1) & (CHUNK - 1)
    strict = j < t
    incl = j <= t
    blk16 = (t >> 4) == (j >> 4)
    blk32 = (t >> 5) == (j >> 5)
    return strict, incl, blk16, blk32


def _bd(x, bd_mask):
    x = x.astype(BF16)
    return jnp.where(bd_mask, jnp.concatenate([x, x], axis=0), 0.0).astype(BF16)


def _staged(fn, items, parts=2):
    out = []
    n = len(items) // parts
    for k in range(parts):
        out += [fn(*item) for item in items[k * n:(k + 1) * n]]
        yield
    return out


def _unit_lower_inverse_minus_identity(a_list, masks, bd_mask):
    _, _, blk16, blk32 = masks
    ad = [jnp.where(blk16, a, 0.0) for a in a_list]
    ap = yield from _staged(lambda x: _mm(x, _bd(x, bd_mask)), [(x,) for x in ad])
    tp = ad
    for _ in range(2):
        both = yield from _staged(
            lambda p, t: _mm(p, jnp.concatenate([_bd(p, bd_mask), _bd(t, bd_mask)], axis=1)),
            list(zip(ap, tp)))
        tp = [t + p + b[:, PAIR:] for t, p, b in zip(tp, ap, both)]
        ap = [b[:, :PAIR] for b in both]
    last = yield from _staged(lambda p, t: _mm(p, _bd(t, bd_mask)), list(zip(ap, tp)))
    tp = [t + p + x for t, p, x in zip(tp, ap, last)]
    for off_mask in (blk32 & ~blk16, ~blk32):
        off = [jnp.where(off_mask, a, 0.0) for a in a_list]
        x = yield from _staged(lambda o, t: o + _mm(t, _bd(o, bd_mask)), list(zip(off, tp)))
        tp = yield from _staged(lambda t, xx: t + xx + _mm(xx, _bd(t, bd_mask)), list(zip(tp, x)))
    return tp


_TERM_STAGES = 11
_TERM_NAMES = ("achk", "uv", "bb")


def _bd_masks():
    bi = lax.broadcasted_iota(jnp.int32, (PAIR, PAIR), 0) >> 6
    bj = lax.broadcasted_iota(jnp.int32, (PAIR, PAIR), 1) >> 6
    bd1 = bi == bj
    return bd1, jnp.concatenate([bd1, bd1], axis=1)


def _scan_terms_pieces(rt_ref, at_ref, bh_ref, kh_ref, v_ref, terms, *, tb):
    masks = _pair_masks()
    strict, incl = masks[0], masks[1]
    bd1, bd2 = _bd_masks()
    probs = [(c, p) for c in range(tb // CHUNK) for p in range(N_PAIRS)]
    cut = lambda ref: [ref[c * CHUNK:(c + 1) * CHUNK, p * PAIR:(p + 1) * PAIR] for c, p in probs]
    rt_p, at_p, bh_p, kh_p, v_p = map(cut, (rt_ref, at_ref, bh_ref, kh_ref, v_ref))
    gram = yield from _staged(
        lambda a_, r_, b_, k_: _mm(jnp.concatenate([a_, r_], axis=0),
                                   jnp.concatenate([_bd(b_, bd1), _bd(k_, bd1)], axis=0), _NT),
        list(zip(at_p, rt_p, bh_p, kh_p)))
    a_ab = [jnp.where(strict, g_[:CHUNK, :PAIR], 0.0) for g_ in gram]
    a_ak = [jnp.where(strict, g_[:CHUNK, PAIR:], 0.0) for g_ in gram]
    incl2 = jnp.concatenate([incl, incl], axis=1)
    for i, g_ in enumerate(gram):
        terms["bb"][i] = jnp.where(incl2, g_[CHUNK:], 0.0).astype(BF16)
    rhs = yield from _staged(
        lambda m_, x_, a_: jnp.concatenate([_mm(m_, _bd(x_, bd1)), a_], axis=1),
        list(zip(a_ak, v_p, at_p)))
    tp = yield from _unit_lower_inverse_minus_identity(a_ab, masks, bd1)
    sol = yield from _staged(lambda x_, t_: x_ + _mm(t_, _bd(x_, bd2)), list(zip(rhs, tp)))
    for i, x_ in enumerate(sol):
        terms["uv"][i] = x_[:, :PAIR]
        terms["achk"][i] = x_[:, PAIR:].astype(BF16)


def _scan_state_pieces(terms, rt_ref, v_ref, bc_ref, kc_ref, wl_ref, gate_ref, bonus_ref, lnw_ref,
                       lnb_ref, o_ref, s_ref, y_ref, *, tb):
    bd1, _ = _bd_masks()
    bd_upd = jnp.logical_and(bd1, pl.program_id(0) > 0)
    s = [s_ref[p] for p in range(N_PAIRS)]
    for c in range(tb // CHUNK):
        rows = slice(c * CHUNK, (c + 1) * CHUNK)
        lanes = [slice(p * PAIR, (p + 1) * PAIR) for p in range(N_PAIRS)]
        idx = [c * N_PAIRS + p for p in range(N_PAIRS)]
        u = [_mm(terms["achk"][i], s[p], _NT) + terms["uv"][i] for p, i in enumerate(idx)]
        yield
        w_last = wl_ref[0, c:c + 1, :]
        upd = []
        for p, i in enumerate(idx):
            v_i = v_ref[rows, lanes[p]]
            y_ref[rows, lanes[p]] = (
                _mm(rt_ref[rows, lanes[p]], s[p], _NT)
                + _mm(terms["bb"][i], jnp.concatenate([_bd(u[p], bd1), _bd(v_i, bd1)], axis=0)))
            upd.append(_mm(jnp.concatenate([u[p].astype(BF16), v_i], axis=0),
                           jnp.concatenate([bc_ref[rows, lanes[p]], kc_ref[rows, lanes[p]]], axis=0),
                           _TN))
        s = [s[p] * w_last[:, lanes[p]] + jnp.where(bd_upd, upd[p], 0.0) for p in range(N_PAIRS)]
        yield
    for p in range(N_PAIRS):
        s_ref[p] = s[p]
    seg01 = _seg01()
    y = y_ref[...]
    mean = _head_sum(y, seg01) * (1.0 / RWKV_HEAD)
    yield
    d = y - mean
    var = _head_sum(d * d, seg01) * (1.0 / RWKV_HEAD)
    yield
    yn = d * lax.rsqrt(var + LNX_EPS) * lnw_ref[...] + lnb_ref[...]
    o_ref[...] = (yn + bonus_ref[...]) * gate_ref[...]


def _scan_kernel(rt_ref, at_ref, bh_ref, kh_ref, v_ref, rtp_ref, vp_ref, bcp_ref, kcp_ref, gate_ref,
                 bonus_ref, wl_ref, lnw_ref, lnb_ref, wg_ref, wu_ref, wd_ref, o_ref, wg_bf_ref,
                 wu_bf_ref, wd_bf_ref, s_ref, y_ref, *term_refs, tb):
    n = len(_TERM_NAMES)
    slots = [dict(zip(_TERM_NAMES, term_refs[k * n:(k + 1) * n])) for k in range(2)]
    i = pl.program_id(0)

    @pl.when(i == 0)
    def _():
        s_ref[...] = jnp.zeros_like(s_ref)
        for ref in slots[1].values():
            ref[...] = jnp.zeros_like(ref)

    wg_bf_ref[...] = wg_ref[...].astype(BF16)
    wu_bf_ref[...] = wu_ref[...].astype(BF16)
    wd_bf_ref[...] = wd_ref[...].astype(BF16)

    def step(write, read):
        n_terms = 2 * _TERM_STAGES + 1
        _run_interleaved(
            [(_scan_terms_pieces(rt_ref, at_ref, bh_ref, kh_ref, v_ref, write, tb=tb), n_terms),
             (_scan_state_pieces(read, rtp_ref, vp_ref, bcp_ref, kcp_ref, wl_ref, gate_ref,
                                 bonus_ref, lnw_ref, lnb_ref, o_ref, s_ref, y_ref, tb=tb),
              2 * (tb // CHUNK) + 3)],
            rounds=n_terms)

    @pl.when((i & 1) == 0)
    def _():
        step(slots[0], slots[1])

    @pl.when((i & 1) == 1)
    def _():
        step(slots[1], slots[0])


def _scan(prep, lnw, lnb, w_gate, w_up, w_down, tb):
    t = prep["rt"].shape[0]
    n_tiles = t // tb
    assert tb // CHUNK <= WL_ROWS
    last = n_tiles - 1
    cur = pl.BlockSpec((tb, RWKV_WIDTH), lambda i: (jnp.minimum(i, last), 0))
    prev = pl.BlockSpec((tb, RWKV_WIDTH), lambda i: (jnp.maximum(i - 1, 0), 0))
    vec = _full((1, RWKV_WIDTH))
    assert D_MODEL % n_tiles == 0 and (D_MODEL // n_tiles) % 16 == 0
    assert n_tiles % 2 == 0 and D_FF % (n_tiles // 2) == 0 and (D_FF // (n_tiles // 2)) % 16 == 0
    up_rows = pl.BlockSpec((D_MODEL // n_tiles, D_FF), lambda i: (jnp.minimum(i, last), 0))
    down_blk = pl.BlockSpec((D_FF // (n_tiles // 2), D_MODEL // 2),
                            lambda i: (jnp.minimum(i, last) // 2, jnp.minimum(i, last) % 2))
    n_prob = (tb // CHUNK) * N_PAIRS
    term_shapes = [pltpu.VMEM((n_prob, CHUNK, PAIR), BF16), pltpu.VMEM((n_prob, CHUNK, PAIR), F32),
                   pltpu.VMEM((n_prob, CHUNK, 2 * PAIR), BF16)]
    return pl.pallas_call(
        functools.partial(_scan_kernel, tb=tb),
        out_shape=(jax.ShapeDtypeStruct((t, RWKV_WIDTH), F32),
                   jax.ShapeDtypeStruct((D_MODEL, D_FF), BF16),
                   jax.ShapeDtypeStruct((D_MODEL, D_FF), BF16),
                   jax.ShapeDtypeStruct((D_FF, D_MODEL), BF16)),
        grid=(n_tiles + 1,),
        in_specs=[cur] * 5 + [prev] * 6
        + [pl.BlockSpec((1, WL_ROWS, RWKV_WIDTH), lambda i: (jnp.maximum(i - 1, 0), 0, 0)),
           vec, vec, up_rows, up_rows, down_blk],
        out_specs=(prev, up_rows, up_rows, down_blk),
        scratch_shapes=[pltpu.VMEM((N_PAIRS, PAIR, PAIR), F32), pltpu.VMEM((tb, RWKV_WIDTH), F32)]
        + term_shapes * 2,
        compiler_params=_params(),
        name="scan",
    )(prep["rt"], prep["at"], prep["bh"], prep["kh"], prep["v"],
      prep["rt"], prep["v"], prep["bc"], prep["kc"], prep["gate"], prep["bonus"], prep["wl"],
      lnw, lnb, w_gate, w_up, w_down)


def _sgu_block_prepare(z, lnw, lnb, sel):
    hz = _gelu_tanh(z)
    u = hz[:, :SGU_WIDTH]
    vf = hz[:, SGU_WIDTH:]
    mu = jnp.mean(vf, axis=-1, keepdims=True)
    d = vf - mu
    var = jnp.mean(d * d, axis=-1, keepdims=True)
    vn = d * lax.rsqrt(var + LN_EPS) * lnw + lnb
    stacks = []
    for p in range(SGU_WIDTH // PAIR):
        vb = vn[:, p * PAIR:(p + 1) * PAIR]
        stacks.append(jnp.where(sel, jnp.concatenate([vb, vb], axis=0), 0.0).astype(BF16))
    return u, stacks


def _sgu_block_mix(u, stacks, wcat, bias):
    return jnp.concatenate(
        [u[:, p * PAIR:(p + 1) * PAIR]
         * (jnp.dot(wcat[p], stacks[p], preferred_element_type=F32) + bias[:, p * PAIR:(p + 1) * PAIR])
         for p in range(SGU_WIDTH // PAIR)], axis=1)


def _mix_attn_group(r, x_ref, yr_ref, zs_ref, slnw_ref, slnb_ref, sbias_ref, wo1_ref, wo2_ref,
                    g2_ref, wq_ref, k_ref, v_ref, wo_ref, o_ref, wcat, sel):
    heads = [slice(hd * XA_HEAD_DIM, (hd + 1) * XA_HEAD_DIM) for hd in range(XA_HEADS)]
    prepared = [_sgu_block_prepare(zs_ref[b:b + SGU_BLOCK, :], slnw_ref[...], slnb_ref[...], sel)
                for b in range(r.start, r.stop, SGU_BLOCK)]
    yield
    x1 = x_ref[r, :] + jnp.dot(yr_ref[r, :].astype(BF16), wo1_ref[...],
                               preferred_element_type=F32)
    y_sgu = jnp.concatenate([_sgu_block_mix(u, st, wcat, sbias_ref[...]) for u, st in prepared],
                            axis=0)
    yield
    x1 = x1 + jnp.dot(y_sgu.astype(BF16), wo2_ref[...], preferred_element_type=F32)
    yield
    h = _rmsnorm(x1, g2_ref[...]).astype(BF16)
    yield
    q = jnp.dot(h, wq_ref[...], preferred_element_type=F32).astype(BF16)
    s = [lax.dot_general(q[:, hl], k_ref[:, hl], _NT, preferred_element_type=F32)
         * (XA_HEAD_DIM ** -0.5) for hl in heads]
    yield
    p = []
    for s_h in s:
        e = jnp.exp(s_h - jnp.max(s_h, axis=-1, keepdims=True))
        p.append((e / jnp.sum(e, axis=-1, keepdims=True)).astype(BF16))
    yield
    o = jnp.concatenate([jnp.dot(p_h, v_ref[:, hl], preferred_element_type=F32)
                         for p_h, hl in zip(p, heads)], axis=1).astype(BF16)
    o_ref[r, :] = x1 + jnp.dot(o, wo_ref[...], preferred_element_type=F32)


def _mix_attn_kernel(x_ref, yr_ref, zs_ref, slnw_ref, slnb_ref, ws_ref, sbias_ref, wout_f32_ref,
                     g2_ref, wq_f32_ref, k_ref, v_ref, wo_f32_ref, o_ref, wout_ref, wq_ref, wo_ref):
    @pl.when(pl.program_id(0) == 0)
    def _():
        wout_ref[...] = wout_f32_ref[...].astype(BF16)
        wq_ref[...] = wq_f32_ref[...].astype(BF16)
        wo_ref[...] = wo_f32_ref[...].astype(BF16)

    wo1_ref = wout_ref.at[:RWKV_WIDTH]
    wo2_ref = wout_ref.at[RWKV_WIDTH:]
    tm = x_ref.shape[0]
    ti = lax.broadcasted_iota(jnp.int32, (SGU_BLOCK, SGU_BLOCK), 0)
    tj = lax.broadcasted_iota(jnp.int32, (SGU_BLOCK, SGU_BLOCK), 1)
    tril = tj <= ti
    wcat = [jnp.concatenate([jnp.where(tril, ws_ref[2 * p], 0.0),
                             jnp.where(tril, ws_ref[2 * p + 1], 0.0)], axis=1).astype(BF16)
            for p in range(SGU_WIDTH // PAIR)]
    bi = lax.broadcasted_iota(jnp.int32, (2 * SGU_BLOCK, PAIR), 0) >> 7
    bj = lax.broadcasted_iota(jnp.int32, (2 * SGU_BLOCK, PAIR), 1) >> 6
    sel = bi == bj
    gens = [_mix_attn_group(slice(r, r + ATTN_ROW_GROUP), x_ref, yr_ref, zs_ref, slnw_ref, slnb_ref,
                            sbias_ref, wo1_ref, wo2_ref, g2_ref, wq_ref, k_ref, v_ref, wo_ref,
                            o_ref, wcat, sel)
            for r in range(0, tm, ATTN_ROW_GROUP)]
    _run_wavefront(gens, ATTN_STAGGER)


def _mix_attn(x, yr, zs, slnw, slnb, ws, sbias, w_out, g2, wq, k, v, wo, tm):
    t = x.shape[0]
    sq = _full((D_MODEL, D_MODEL))
    return pl.pallas_call(
        _mix_attn_kernel,
        out_shape=jax.ShapeDtypeStruct((t, D_MODEL), F32),
        grid=(t // tm,),
        in_specs=[pl.BlockSpec((tm, D_MODEL), lambda i: (i, 0)),
                  pl.BlockSpec((tm, RWKV_WIDTH), lambda i: (i, 0)),
                  pl.BlockSpec((tm, 2 * SGU_WIDTH), lambda i: (i, 0)),
                  _full((1, SGU_WIDTH)), _full((1, SGU_WIDTH)),
                  _full((SGU_GROUPS, SGU_BLOCK, SGU_BLOCK)), _full((SGU_BLOCK, SGU_WIDTH)),
                  sq, _full((1, D_MODEL)), sq,
                  _full((MEM_LEN, D_MODEL)), _full((MEM_LEN, D_MODEL)), sq],
        out_specs=pl.BlockSpec((tm, D_MODEL), lambda i: (i, 0)),
        scratch_shapes=[pltpu.VMEM((D_MODEL, D_MODEL), BF16)] * 3,
        compiler_params=_params(),
        name="mix_attn",
    )(x, yr, zs, slnw, slnb, ws, sbias, w_out, g2, wq, k, v, wo)


def _ffn_kernel(x_ref, g3_ref, wg_ref, wu_ref, wd_ref, gf_ref, o_ref):
    tm = x_ref.shape[0]
    groups = [slice(r, r + FFN_ROW_GROUP) for r in range(0, tm, FFN_ROW_GROUP)]
    x2 = [x_ref[r, :] for r in groups]
    h = [_rmsnorm(x, g3_ref[...]).astype(BF16) for x in x2]
    gate = [jnp.dot(h_, wg_ref[...], preferred_element_type=F32) for h_ in h]
    up = [jnp.dot(h_, wu_ref[...], preferred_element_type=F32) for h_ in h]
    act = [(jax.nn.silu(g_) * u_).astype(BF16) for g_, u_ in zip(gate, up)]
    x3 = [x + jnp.dot(a_, wd_ref[...], preferred_element_type=F32) for x, a_ in zip(x2, act)]
    for r, x in zip(groups, x3):
        o_ref[r, :] = _rmsnorm(x, gf_ref[...])


def _ffn(x, g3, wg, wu, wd, gf, tm):
    t = x.shape[0]
    return pl.pallas_call(
        _ffn_kernel,
        out_shape=jax.ShapeDtypeStruct((t, D_MODEL), F32),
        grid=(t // tm,),
        in_specs=[pl.BlockSpec((tm, D_MODEL), lambda i: (i, 0)), _full((1, D_MODEL)),
                  _full((D_MODEL, D_FF)), _full((D_MODEL, D_FF)), _full((D_FF, D_MODEL)),
                  _full((1, D_MODEL))],
        out_specs=pl.BlockSpec((tm, D_MODEL), lambda i: (i, 0)),
        compiler_params=_params("parallel"),
        name="ffn",
    )(x, g3, wg, wu, wd, gf)


def kernel(x, mem, norm1_g, w_in, shift_mu, w0, w_lora_up, a0, a_lora_up, g_lora_up, k_k, k_a, r_k,
           lnx_w, lnx_b, sgu_ln_w, sgu_ln_b, w_spatial, b_spatial, w_out, norm2_g, mem_norm_g,
           wq_x, wk_x, wv_x, wo_x, norm3_g, w_gate, w_up, w_down, norm_f_g):
    b, t, _ = x.shape
    depth = w_in.shape[0]
    assert depth == 1, "the final RMSNorm is fused into the (single) layer's ffn call"
    assert t % TM_ATTN == 0 and t % TM_FFN == 0
    assert t % TM_DENSE == 0 and TM_DENSE % TB_SCAN == 0 and TB_SCAN % CHUNK == 0
    row = lambda p: p.reshape(1, -1)
    bf = lambda p: p.astype(BF16)
    outs = []
    for bi in range(b):
        xb = x[bi]
        for l in range(depth):
            lora = w_lora_up.shape[1]
            zeros = jnp.zeros((lora, RWKV_WIDTH), F32)
            waup = jnp.concatenate(
                [jnp.concatenate([w_lora_up[l], zeros], axis=1),
                 jnp.concatenate([zeros, a_lora_up[l]], axis=1)], axis=0)
            bias = jnp.repeat(b_spatial[l].T, SGU_WIDTH // SGU_GROUPS, axis=1)

            front = _front(xb, row(norm1_g[l]), w_in[l], row(shift_mu[l]), row(w0[l]), bf(waup),
                           row(a0[l]), bf(g_lora_up[l]), row(k_k[l]), row(k_a[l]), row(r_k[l]),
                           TM_DENSE)
            prep = dict(zip(_PREP_NAMES, front[:len(_PREP_NAMES)]))
            z_sgu = front[len(_PREP_NAMES)]
            y_rwkv, wg_bf, wu_bf, wd_bf = _scan(prep, row(lnx_w[l]), row(lnx_b[l]),
                                                w_gate[l], w_up[l], w_down[l], TB_SCAN)
            k_mem, v_mem = _mem_kv(mem[bi], row(mem_norm_g[l]), wk_x[l], wv_x[l])
            x2 = _mix_attn(xb, y_rwkv, z_sgu, row(sgu_ln_w[l]), row(sgu_ln_b[l]), w_spatial[l], bias,
                           w_out[l], row(norm2_g[l]), wq_x[l], k_mem, v_mem, wo_x[l], TM_ATTN)
            xb = _ffn(x2, row(norm3_g[l]), wg_bf, wu_bf, wd_bf, row(norm_f_g), TM_FFN)
        outs.append(xb)
    return jnp.stack(outs, axis=0)
```

```python
import functools
import math

import jax
import jax.numpy as jnp
from jax import lax
from jax.experimental import pallas as pl
from jax.experimental.pallas import tpu as pltpu

F32 = jnp.float32
BF16 = jnp.bfloat16

D_MODEL = 1024
RWKV_WIDTH = 512
RWKV_HEAD = 64
LORA_WA = 128
GATE_LORA = 128
RWKV_IN = 3 * RWKV_WIDTH + LORA_WA + GATE_LORA
SGU_WIDTH = 512
SGU_GROUPS = 8
SGU_BLOCK = 128
IN_WIDTH = RWKV_IN + 2 * SGU_WIDTH
MEM_LEN = 256
XA_HEADS = 4
XA_HEAD_DIM = D_MODEL // XA_HEADS
D_FF = 2816
RMS_EPS = 1e-6
LN_EPS = 1e-5
LNX_EPS = 64e-5
EXP_M05 = 0.6065306597126334
LOG2_E = 1.4426950408889634

CHUNK = 64
PAIR = 2 * RWKV_HEAD
N_PAIRS = RWKV_WIDTH // PAIR
TM_DENSE = 512
TM_ATTN = 1024
TM_FFN = 1024
TB_SCAN = 512
Z_PAD = 8
WL_ROWS = 8
ATTN_ROW_GROUP = 256
ATTN_STAGGER = 2
FFN_ROW_GROUP = 256
IN_PROJ_COLS = 256
PREP_ROWS = 128
PREP_STREAMS = 4
PREP_PIECES_PER_CHUNK = 10
TERMS_DECAY_CUMSUM = 2
TERMS_HEAD_SUM = 1
TERMS_GROUP_MEAN = 2
VMEM_LIMIT = 56 * 1024 * 1024

_NN = (((1,), (0,)), ((), ()))
_NT = (((1,), (1,)), ((), ()))
_TN = (((0,), (0,)), ((), ()))


def _mm(a, b, dims=_NN):
    return lax.dot_general(a.astype(BF16), b.astype(BF16), dims, preferred_element_type=F32)


def _split_bf16(x, terms):
    parts = []
    rem = x
    for _ in range(terms):
        part = rem.astype(BF16)
        rem = rem - part.astype(F32)
        parts.append(part)
    return parts


def _cumsum_rows(ltri01, parts):
    return lax.dot_general(jnp.concatenate([ltri01] * len(parts), axis=1),
                           jnp.concatenate(parts, axis=0), _NN, preferred_element_type=F32)


def _head_sum_parts(parts, seg01):
    cols = []
    for q in range(parts[0].shape[1] // 256):
        acc = None
        for part in parts:
            d = lax.dot_general(part[:, 256 * q:256 * (q + 1)], seg01, _NN,
                                preferred_element_type=F32)
            acc = d if acc is None else acc + d
        cols.append(acc)
    return jnp.concatenate(cols, axis=1)


def _head_sum(x, seg01, terms=TERMS_HEAD_SUM):
    return _head_sum_parts(_split_bf16(x, terms), seg01)


def _seg01():
    li = lax.broadcasted_iota(jnp.int32, (256, 256), 0) >> 6
    lj = lax.broadcasted_iota(jnp.int32, (256, 256), 1) >> 6
    return (li == lj).astype(BF16)


def _gelu_tanh(x):
    k1 = -2.0 * math.sqrt(2.0 / math.pi) * math.log2(math.e)
    return x / (1.0 + jnp.exp2(x * (k1 + (k1 * 0.044715) * (x * x))))


def _rmsnorm(x, g):
    return x * lax.rsqrt(jnp.mean(x * x, axis=-1, keepdims=True) + RMS_EPS) * g


def _full(shape):
    n = len(shape)
    return pl.BlockSpec(shape, lambda i: (0,) * n, pipeline_mode=pl.Buffered(1))


def _params(sem="arbitrary"):
    return pltpu.CompilerParams(dimension_semantics=(sem,), vmem_limit_bytes=VMEM_LIMIT)


def _run_interleaved(stages, rounds):
    for r in range(rounds):
        for gen, n in stages:
            for _ in range((r + 1) * n // rounds - r * n // rounds):
                next(gen, None)
    for gen, _ in stages:
        assert next(gen, StopIteration) is StopIteration, "piece count too small"


def _run_wavefront(gens, stagger):
    live = list(enumerate(gens))
    r = 0
    while live:
        for entry in list(live):
            g, gen = entry
            if r >= g * stagger and next(gen, StopIteration) is StopIteration:
                live.remove(entry)
        r += 1


def _mem_kv_kernel(mem_ref, g_ref, wk_ref, wv_ref, k_ref, v_ref):
    m = _rmsnorm(mem_ref[...], g_ref[...]).astype(BF16)
    k_ref[...] = jnp.dot(m, wk_ref[...].astype(BF16), preferred_element_type=F32).astype(BF16)
    v_ref[...] = jnp.dot(m, wv_ref[...].astype(BF16), preferred_element_type=F32).astype(BF16)


def _mem_kv(mem, g, wk, wv):
    return pl.pallas_call(
        _mem_kv_kernel,
        out_shape=(jax.ShapeDtypeStruct((MEM_LEN, D_MODEL), BF16),) * 2,
        grid=(1,),
        in_specs=[_full((MEM_LEN, D_MODEL)), _full((1, D_MODEL)),
                  _full((D_MODEL, D_MODEL)), _full((D_MODEL, D_MODEL))],
        out_specs=(_full((MEM_LEN, D_MODEL)),) * 2,
        compiler_params=_params(),
        name="mem_kv",
    )(mem, g, wk, wv)


_PREP_F32 = ("rt", "at", "gate", "bonus")
_PREP_BF16 = ("bh", "kh", "bc", "kc", "v")
_PREP_NAMES = _PREP_F32 + _PREP_BF16 + ("wl",)


def _in_proj_pieces(x_ref, g_ref, w_ref, z_ref):
    h = _rmsnorm(x_ref[...], g_ref[...]).astype(BF16)
    yield
    for j in range(IN_WIDTH // IN_PROJ_COLS):
        cols = slice(j * IN_PROJ_COLS, (j + 1) * IN_PROJ_COLS)
        z_ref[Z_PAD:, cols] = jnp.dot(h, w_ref[:, cols], preferred_element_type=F32)
        yield


def _rwkv_prep_pieces(z_ref, mu_ref, w0_ref, waup_ref, a0_ref, gup_ref, kk_ref, ka_ref, rk_ref,
                      prep, chunks):
    seg01 = _seg01()
    lane = lax.broadcasted_iota(jnp.int32, (1, LORA_WA), 1)
    row = lax.broadcasted_iota(jnp.int32, (PREP_ROWS, 1), 0)
    ti = lax.broadcasted_iota(jnp.int32, (PREP_ROWS, PREP_ROWS), 0)
    tj = lax.broadcasted_iota(jnp.int32, (PREP_ROWS, PREP_ROWS), 1)
    ltri01 = ((tj <= ti) & ((ti >> 6) == (tj >> 6))).astype(BF16)
    chunks_per_scan_tile = TB_SCAN // CHUNK
    chunks_per_unit = PREP_ROWS // CHUNK
    half_w = RWKV_WIDTH // 2

    def shifted(c, cols):
        z = z_ref[Z_PAD + c * PREP_ROWS:Z_PAD + (c + 1) * PREP_ROWS, cols]
        before = z_ref[Z_PAD + c * PREP_ROWS - 1:Z_PAD + c * PREP_ROWS, cols]
        zprev = jnp.where(row == 0, before, pltpu.roll(z, 1, axis=0))
        return z + (zprev - z) * mu_ref[:, cols]

    for c in chunks:
        rows = slice(c * PREP_ROWS, (c + 1) * PREP_ROWS)
        wa_in = shifted(c, slice(3 * RWKV_WIDTH, 3 * RWKV_WIDTH + LORA_WA))
        wa_in = jnp.where(lane < LORA_WA // 2, jnp.tanh(wa_in), wa_in).astype(BF16)
        gd = jax.nn.sigmoid(shifted(c, slice(3 * RWKV_WIDTH + LORA_WA, RWKV_IN))).astype(BF16)
        for q in range(2):
            hc = slice(q * half_w, (q + 1) * half_w)
            r = shifted(c, hc)
            k = shifted(c, slice(RWKV_WIDTH + q * half_w, RWKV_WIDTH + (q + 1) * half_w))
            v = shifted(c, slice(2 * RWKV_WIDTH + q * half_w, 2 * RWKV_WIDTH + (q + 1) * half_w))
            prep["v"][rows, hc] = v.astype(BF16)
            kk = k * kk_ref[:, hc]
            kk_sq = _split_bf16(kk * kk, TERMS_HEAD_SUM)
            yield
            w_pre = w0_ref[:, hc] + jnp.dot(wa_in, waup_ref[:, hc], preferred_element_type=F32)
            a_pre = a0_ref[:, hc] + jnp.dot(
                wa_in, waup_ref[:, RWKV_WIDTH + q * half_w:RWKV_WIDTH + (q + 1) * half_w],
                preferred_element_type=F32)
            prep["gate"][rows, hc] = jnp.dot(gd, gup_ref[:, hc], preferred_element_type=F32)
            kk_ss = _head_sum_parts(kk_sq, seg01)
            yield
            a = jax.nn.sigmoid(a_pre)
            lw = jax.nn.sigmoid(w_pre) * (-EXP_M05 * LOG2_E)
            lw_parts = _split_bf16(lw, TERMS_DECAY_CUMSUM)
            kk = kk * lax.rsqrt(jnp.maximum(kk_ss, 1e-24))
            kmod = k * ((1.0 - ka_ref[:, hc]) + a * ka_ref[:, hc])
            kka = kk * a
            rkk = _split_bf16(r * kmod * rk_ref[:, hc], TERMS_HEAD_SUM)
            yield
            cs = _cumsum_rows(ltri01, lw_parts)
            prep["bonus"][rows, hc] = _head_sum_parts(rkk, seg01) * v
            yield
            w_inv = jnp.exp2(-cs)
            w_last = [jnp.exp2(cs[(j + 1) * CHUNK - 1:(j + 1) * CHUNK, :])
                      for j in range(chunks_per_unit)]
            w_tail = jnp.concatenate(
                [w_last[j] * w_inv[j * CHUNK:(j + 1) * CHUNK] for j in range(chunks_per_unit)],
                axis=0)
            prep["rt"][rows, hc] = r * jnp.exp2(cs)
            prep["at"][rows, hc] = -kk * jnp.exp2(cs - lw)
            prep["bh"][rows, hc] = (kka * w_inv).astype(BF16)
            prep["kh"][rows, hc] = (kmod * w_inv).astype(BF16)
            prep["bc"][rows, hc] = (kka * w_tail).astype(BF16)
            prep["kc"][rows, hc] = (kmod * w_tail).astype(BF16)
            for j in range(chunks_per_unit):
                cq, cr = divmod(c * chunks_per_unit + j, chunks_per_scan_tile)
                prep["wl"][cq, cr:cr + 1, hc] = w_last[j]
            yield


def _copy_pieces(z_ref, o_ref, *, tm):
    for b in range(tm // SGU_BLOCK):
        o_ref[b * SGU_BLOCK:(b + 1) * SGU_BLOCK, :] = (
            z_ref[Z_PAD + b * SGU_BLOCK:Z_PAD + (b + 1) * SGU_BLOCK, RWKV_IN:])
        yield


def _front_kernel(x_ref, g1_ref, win_ref, mu_ref, w0_ref, waup_ref, a0_ref, gup_ref, kk_ref, ka_ref,
                  rk_ref, *rest, tm):
    n = len(_PREP_NAMES)
    prep = dict(zip(_PREP_NAMES, rest[:n]))
    zs_ref = rest[n]
    wbf_ref, z0_ref, z1_ref = rest[n + 1:]
    i = pl.program_id(0)

    @pl.when(i == 0)
    def _():
        z1_ref[...] = jnp.zeros_like(z1_ref)
        wbf_ref[...] = win_ref[...].astype(BF16)

    def step(z_write, z_read):
        n_chunks = tm // PREP_ROWS
        prep_pieces = [
            _rwkv_prep_pieces(z_read, mu_ref, w0_ref, waup_ref, a0_ref, gup_ref, kk_ref, ka_ref,
                              rk_ref, prep, range(k, n_chunks, PREP_STREAMS))
            for k in range(PREP_STREAMS)]
        n_dot = 1 + IN_WIDTH // IN_PROJ_COLS + 1
        _run_interleaved(
            [(_in_proj_pieces(x_ref, g1_ref, wbf_ref, z_write), n_dot)]
            + [(gen, PREP_PIECES_PER_CHUNK * n_chunks // PREP_STREAMS + 1) for gen in prep_pieces]
            + [(_copy_pieces(z_read, zs_ref, tm=tm), tm // SGU_BLOCK + 1)],
            rounds=n_dot)
        if TB_SCAN // CHUNK < WL_ROWS:
            prep["wl"][:, TB_SCAN // CHUNK:, :] = jnp.zeros(
                (tm // TB_SCAN, WL_ROWS - TB_SCAN // CHUNK, RWKV_WIDTH), F32)
        z_write[Z_PAD - 1:Z_PAD, :] = z_read[Z_PAD + tm - 1:Z_PAD + tm, :]

    @pl.when((i & 1) == 0)
    def _():
        step(z0_ref, z1_ref)

    @pl.when((i & 1) == 1)
    def _():
        step(z1_ref, z0_ref)


def _front(x, g1, w_in, mu, w0, waup, a0, gup, k_k, k_a, r_k, tm):
    t = x.shape[0]
    n_tiles = t // tm
    vec = _full((1, RWKV_WIDTH))
    out_tile = lambda i: (jnp.maximum(i - 1, 0), 0)
    out_shapes = ([jax.ShapeDtypeStruct((t, RWKV_WIDTH), F32)] * len(_PREP_F32)
                  + [jax.ShapeDtypeStruct((t, RWKV_WIDTH), BF16)] * len(_PREP_BF16)
                  + [jax.ShapeDtypeStruct((t // TB_SCAN, WL_ROWS, RWKV_WIDTH), F32),
                     jax.ShapeDtypeStruct((t, 2 * SGU_WIDTH), F32)])
    out_specs = ([pl.BlockSpec((tm, RWKV_WIDTH), out_tile)] * (len(_PREP_F32) + len(_PREP_BF16))
                 + [pl.BlockSpec((tm // TB_SCAN, WL_ROWS, RWKV_WIDTH),
                                 lambda i: (jnp.maximum(i - 1, 0), 0, 0)),
                    pl.BlockSpec((tm, 2 * SGU_WIDTH), out_tile)])
    return pl.pallas_call(
        functools.partial(_front_kernel, tm=tm),
        out_shape=tuple(out_shapes),
        grid=(n_tiles + 1,),
        in_specs=[pl.BlockSpec((tm, D_MODEL), lambda i: (jnp.minimum(i, n_tiles - 1), 0)),
                  _full((1, D_MODEL)), _full((D_MODEL, IN_WIDTH)), _full((1, RWKV_IN)), vec,
                  _full((LORA_WA, 2 * RWKV_WIDTH)), vec, _full((GATE_LORA, RWKV_WIDTH)),
                  vec, vec, vec],
        out_specs=tuple(out_specs),
        scratch_shapes=[pltpu.VMEM((D_MODEL, IN_WIDTH), BF16),
                        pltpu.VMEM((Z_PAD + tm, IN_WIDTH), F32),
                        pltpu.VMEM((Z_PAD + tm, IN_WIDTH), F32)],
        compiler_params=_params(),
        name="front",
    )(x, g1, w_in, mu, w0, waup, a0, gup, k_k, k_a, r_k)


def _pair_masks():
    t = lax.broadcasted_iota(jnp.int32, (CHUNK, PAIR), 0)
    j = lax.broadcasted_iota(jnp.int32, (CHUNK, PAIR), 1) & (CHUNK - 1)
    strict = j < t
    incl = j <= t
    blk16 = (t >> 4) == (j >> 4)
    blk32 = (t >> 5) == (j >> 5)
    return strict, incl, blk16, blk32


def _bd(x, bd_mask):
    x = x.astype(BF16)
    return jnp.where(bd_mask, jnp.concatenate([x, x], axis=0), 0.0).astype(BF16)


def _staged(fn, items, parts=2):
    out = []
    n = len(items) // parts
    for k in range(parts):
        out += [fn(*item) for item in items[k * n:(k + 1) * n]]
        yield
    return out


def _unit_lower_inverse_minus_identity(a_list, masks, bd_mask):
    _, _, blk16, blk32 = masks
    ad = [jnp.where(blk16, a, 0.0) for a in a_list]
    ap = yield from _staged(lambda x: _mm(x, _bd(x, bd_mask)), [(x,) for x in ad])
    tp = ad
    for _ in range(2):
        both = yield from _staged(
            lambda p, t: _mm(p, jnp.concatenate([_bd(p, bd_mask), _bd(t, bd_mask)], axis=1)),
            list(zip(ap, tp)))
        tp = [t + p + b[:, PAIR:] for t, p, b in zip(tp, ap, both)]
        ap = [b[:, :PAIR] for b in both]
    last = yield from _staged(lambda p, t: _mm(p, _bd(t, bd_mask)), list(zip(ap, tp)))
    tp = [t + p + x for t, p, x in zip(tp, ap, last)]
    for off_mask in (blk32 & ~blk16, ~blk32):
        off = [jnp.where(off_mask, a, 0.0) for a in a_list]
        x = yield from _staged(lambda o, t: o + _mm(t, _bd(o, bd_mask)), list(zip(off, tp)))
        tp = yield from _staged(lambda t, xx: t + xx + _mm(xx, _bd(t, bd_mask)), list(zip(tp, x)))
    return tp


_TERM_STAGES = 11
_TERM_NAMES = ("achk", "uv", "bb")


def _bd_masks():
    bi = lax.broadcasted_iota(jnp.int32, (PAIR, PAIR), 0) >> 6
    bj = lax.broadcasted_iota(jnp.int32, (PAIR, PAIR), 1) >> 6
    bd1 = bi == bj
    return bd1, jnp.concatenate([bd1, bd1], axis=1)


def _scan_terms_pieces(rt_ref, at_ref, bh_ref, kh_ref, v_ref, terms, *, tb):
    masks = _pair_masks()
    strict, incl = masks[0], masks[1]
    bd1, bd2 = _bd_masks()
    probs = [(c, p) for c in range(tb // CHUNK) for p in range(N_PAIRS)]
    cut = lambda ref: [ref[c * CHUNK:(c + 1) * CHUNK, p * PAIR:(p + 1) * PAIR] for c, p in probs]
    rt_p, at_p, bh_p, kh_p, v_p = map(cut, (rt_ref, at_ref, bh_ref, kh_ref, v_ref))
    gram = yield from _staged(
        lambda a_, r_, b_, k_: _mm(jnp.concatenate([a_, r_], axis=0),
                                   jnp.concatenate([_bd(b_, bd1), _bd(k_, bd1)], axis=0), _NT),
        list(zip(at_p, rt_p, bh_p, kh_p)))
    a_ab = [jnp.where(strict, g_[:CHUNK, :PAIR], 0.0) for g_ in gram]
    a_ak = [jnp.where(strict, g_[:CHUNK, PAIR:], 0.0) for g_ in gram]
    incl2 = jnp.concatenate([incl, incl], axis=1)
    for i, g_ in enumerate(gram):
        terms["bb"][i] = jnp.where(incl2, g_[CHUNK:], 0.0).astype(BF16)
    rhs = yield from _staged(
        lambda m_, x_, a_: jnp.concatenate([_mm(m_, _bd(x_, bd1)), a_], axis=1),
        list(zip(a_ak, v_p, at_p)))
    tp = yield from _unit_lower_inverse_minus_identity(a_ab, masks, bd1)
    sol = yield from _staged(lambda x_, t_: x_ + _mm(t_, _bd(x_, bd2)), list(zip(rhs, tp)))
    for i, x_ in enumerate(sol):
        terms["uv"][i] = x_[:, :PAIR]
        terms["achk"][i] = x_[:, PAIR:].astype(BF16)


def _scan_state_pieces(terms, rt_ref, v_ref, bc_ref, kc_ref, wl_ref, gate_ref, bonus_ref, lnw_ref,
                       lnb_ref, o_ref, s_ref, y_ref, *, tb):
    bd1, _ = _bd_masks()
    bd_upd = jnp.logical_and(bd1, pl.program_id(0) > 0)
    s = [s_ref[p] for p in range(N_PAIRS)]
    for c in range(tb // CHUNK):
        rows = slice(c * CHUNK, (c + 1) * CHUNK)
        lanes = [slice(p * PAIR, (p + 1) * PAIR) for p in range(N_PAIRS)]
        idx = [c * N_PAIRS + p for p in range(N_PAIRS)]
        u = [_mm(terms["achk"][i], s[p], _NT) + terms["uv"][i] for p, i in enumerate(idx)]
        yield
        w_last = wl_ref[0, c:c + 1, :]
        upd = []
        for p, i in enumerate(idx):
            v_i = v_ref[rows, lanes[p]]
            y_ref[rows, lanes[p]] = (
                _mm(rt_ref[rows, lanes[p]], s[p], _NT)
                + _mm(terms["bb"][i], jnp.concatenate([_bd(u[p], bd1), _bd(v_i, bd1)], axis=0)))
            upd.append(_mm(jnp.concatenate([u[p].astype(BF16), v_i], axis=0),
                           jnp.concatenate([bc_ref[rows, lanes[p]], kc_ref[rows, lanes[p]]], axis=0),
                           _TN))
        s = [s[p] * w_last[:, lanes[p]] + jnp.where(bd_upd, upd[p], 0.0) for p in range(N_PAIRS)]
        yield
    for p in range(N_PAIRS):
        s_ref[p] = s[p]
    seg01 = _seg01()
    y = y_ref[...]
    mean = _head_sum(y, seg01, TERMS_GROUP_MEAN) * (1.0 / RWKV_HEAD)
    yield
    d = y - mean
    var = _head_sum(d * d, seg01) * (1.0 / RWKV_HEAD)
    yield
    yn = d * lax.rsqrt(var + LNX_EPS) * lnw_ref[...] + lnb_ref[...]
    o_ref[...] = (yn + bonus_ref[...]) * gate_ref[...]


def _scan_kernel(rt_ref, at_ref, bh_ref, kh_ref, v_ref, rtp_ref, vp_ref, bcp_ref, kcp_ref, gate_ref,
                 bonus_ref, wl_ref, lnw_ref, lnb_ref, wg_ref, wu_ref, wd_ref, o_ref, wg_bf_ref,
                 wu_bf_ref, wd_bf_ref, s_ref, y_ref, *term_refs, tb):
    n = len(_TERM_NAMES)
    slots = [dict(zip(_TERM_NAMES, term_refs[k * n:(k + 1) * n])) for k in range(2)]
    i = pl.program_id(0)

    @pl.when(i == 0)
    def _():
        s_ref[...] = jnp.zeros_like(s_ref)
        for ref in slots[1].values():
            ref[...] = jnp.zeros_like(ref)

    wg_bf_ref[...] = wg_ref[...].astype(BF16)
    wu_bf_ref[...] = wu_ref[...].astype(BF16)
    wd_bf_ref[...] = wd_ref[...].astype(BF16)

    def step(write, read):
        n_terms = 2 * _TERM_STAGES + 1
        _run_interleaved(
            [(_scan_terms_pieces(rt_ref, at_ref, bh_ref, kh_ref, v_ref, write, tb=tb), n_terms),
             (_scan_state_pieces(read, rtp_ref, vp_ref, bcp_ref, kcp_ref, wl_ref, gate_ref,
                                 bonus_ref, lnw_ref, lnb_ref, o_ref, s_ref, y_ref, tb=tb),
              2 * (tb // CHUNK) + 3)],
            rounds=n_terms)

    @pl.when((i & 1) == 0)
    def _():
        step(slots[0], slots[1])

    @pl.when((i & 1) == 1)
    def _():
        step(slots[1], slots[0])


def _scan(prep, lnw, lnb, w_gate, w_up, w_down, tb):
    t = prep["rt"].shape[0]
    n_tiles = t // tb
    assert tb // CHUNK <= WL_ROWS
    last = n_tiles - 1
    cur = pl.BlockSpec((tb, RWKV_WIDTH), lambda i: (jnp.minimum(i, last), 0))
    prev = pl.BlockSpec((tb, RWKV_WIDTH), lambda i: (jnp.maximum(i - 1, 0), 0))
    vec = _full((1, RWKV_WIDTH))
    assert D_MODEL % n_tiles == 0 and (D_MODEL // n_tiles) % 16 == 0
    assert n_tiles % 2 == 0 and D_FF % (n_tiles // 2) == 0 and (D_FF // (n_tiles // 2)) % 16 == 0
    up_rows = pl.BlockSpec((D_MODEL // n_tiles, D_FF), lambda i: (jnp.minimum(i, last), 0))
    down_blk = pl.BlockSpec((D_FF // (n_tiles // 2), D_MODEL // 2),
                            lambda i: (jnp.minimum(i, last) // 2, jnp.minimum(i, last) % 2))
    n_prob = (tb // CHUNK) * N_PAIRS
    term_shapes = [pltpu.VMEM((n_prob, CHUNK, PAIR), BF16), pltpu.VMEM((n_prob, CHUNK, PAIR), F32),
                   pltpu.VMEM((n_prob, CHUNK, 2 * PAIR), BF16)]
    return pl.pallas_call(
        functools.partial(_scan_kernel, tb=tb),
        out_shape=(jax.ShapeDtypeStruct((t, RWKV_WIDTH), F32),
                   jax.ShapeDtypeStruct((D_MODEL, D_FF), BF16),
                   jax.ShapeDtypeStruct((D_MODEL, D_FF), BF16),
                   jax.ShapeDtypeStruct((D_FF, D_MODEL), BF16)),
        grid=(n_tiles + 1,),
        in_specs=[cur] * 5 + [prev] * 6
        + [pl.BlockSpec((1, WL_ROWS, RWKV_WIDTH), lambda i: (jnp.maximum(i - 1, 0), 0, 0)),
           vec, vec, up_rows, up_rows, down_blk],
        out_specs=(prev, up_rows, up_rows, down_blk),
        scratch_shapes=[pltpu.VMEM((N_PAIRS, PAIR, PAIR), F32), pltpu.VMEM((tb, RWKV_WIDTH), F32)]
        + term_shapes * 2,
        compiler_params=_params(),
        name="scan",
    )(prep["rt"], prep["at"], prep["bh"], prep["kh"], prep["v"],
      prep["rt"], prep["v"], prep["bc"], prep["kc"], prep["gate"], prep["bonus"], prep["wl"],
      lnw, lnb, w_gate, w_up, w_down)


def _sgu_block_prepare(z, lnw, lnb, sel):
    hz = _gelu_tanh(z)
    u = hz[:, :SGU_WIDTH]
    vf = hz[:, SGU_WIDTH:]
    mu = jnp.mean(vf, axis=-1, keepdims=True)
    d = vf - mu
    var = jnp.mean(d * d, axis=-1, keepdims=True)
    vn = d * lax.rsqrt(var + LN_EPS) * lnw + lnb
    stacks = []
    for p in range(SGU_WIDTH // PAIR):
        vb = vn[:, p * PAIR:(p + 1) * PAIR]
        stacks.append(jnp.where(sel, jnp.concatenate([vb, vb], axis=0), 0.0).astype(BF16))
    return u, stacks


def _sgu_block_mix(u, stacks, wcat, bias):
    return jnp.concatenate(
        [u[:, p * PAIR:(p + 1) * PAIR]
         * (jnp.dot(wcat[p], stacks[p], preferred_element_type=F32) + bias[:, p * PAIR:(p + 1) * PAIR])
         for p in range(SGU_WIDTH // PAIR)], axis=1)


def _mix_attn_group(r, x_ref, yr_ref, zs_ref, slnw_ref, slnb_ref, sbias_ref, wo1_ref, wo2_ref,
                    g2_ref, wq_ref, k_ref, v_ref, wo_ref, o_ref, wcat, sel):
    heads = [slice(hd * XA_HEAD_DIM, (hd + 1) * XA_HEAD_DIM) for hd in range(XA_HEADS)]
    prepared = [_sgu_block_prepare(zs_ref[b:b + SGU_BLOCK, :], slnw_ref[...], slnb_ref[...], sel)
                for b in range(r.start, r.stop, SGU_BLOCK)]
    yield
    x1 = x_ref[r, :] + jnp.dot(yr_ref[r, :].astype(BF16), wo1_ref[...],
                               preferred_element_type=F32)
    y_sgu = jnp.concatenate([_sgu_block_mix(u, st, wcat, sbias_ref[...]) for u, st in prepared],
                            axis=0)
    yield
    x1 = x1 + jnp.dot(y_sgu.astype(BF16), wo2_ref[...], preferred_element_type=F32)
    yield
    h = _rmsnorm(x1, g2_ref[...]).astype(BF16)
    yield
    q = jnp.dot(h, wq_ref[...], preferred_element_type=F32).astype(BF16)
    s = [lax.dot_general(q[:, hl], k_ref[:, hl], _NT, preferred_element_type=F32)
         * (XA_HEAD_DIM ** -0.5) for hl in heads]
    yield
    p = []
    for s_h in s:
        e = jnp.exp(s_h - jnp.max(s_h, axis=-1, keepdims=True))
        p.append((e / jnp.sum(e, axis=-1, keepdims=True)).astype(BF16))
    yield
    o = jnp.concatenate([jnp.dot(p_h, v_ref[:, hl], preferred_element_type=F32)
                         for p_h, hl in zip(p, heads)], axis=1).astype(BF16)
    o_ref[r, :] = x1 + jnp.dot(o, wo_ref[...], preferred_element_type=F32)


def _mix_attn_kernel(x_ref, yr_ref, zs_ref, slnw_ref, slnb_ref, ws_ref, sbias_ref, wout_f32_ref,
                     g2_ref, wq_f32_ref, k_ref, v_ref, wo_f32_ref, o_ref, wout_ref, wq_ref, wo_ref):
    @pl.when(pl.program_id(0) == 0)
    def _():
        wout_ref[...] = wout_f32_ref[...].astype(BF16)
        wq_ref[...] = wq_f32_ref[...].astype(BF16)
        wo_ref[...] = wo_f32_ref[...].astype(BF16)

    wo1_ref = wout_ref.at[:RWKV_WIDTH]
    wo2_ref = wout_ref.at[RWKV_WIDTH:]
    tm = x_ref.shape[0]
    ti = lax.broadcasted_iota(jnp.int32, (SGU_BLOCK, SGU_BLOCK), 0)
    tj = lax.broadcasted_iota(jnp.int32, (SGU_BLOCK, SGU_BLOCK), 1)
    tril = tj <= ti
    wcat = [jnp.concatenate([jnp.where(tril, ws_ref[2 * p], 0.0),
                             jnp.where(tril, ws_ref[2 * p + 1], 0.0)], axis=1).astype(BF16)
            for p in range(SGU_WIDTH // PAIR)]
    bi = lax.broadcasted_iota(jnp.int32, (2 * SGU_BLOCK, PAIR), 0) >> 7
    bj = lax.broadcasted_iota(jnp.int32, (2 * SGU_BLOCK, PAIR), 1) >> 6
    sel = bi == bj
    gens = [_mix_attn_group(slice(r, r + ATTN_ROW_GROUP), x_ref, yr_ref, zs_ref, slnw_ref, slnb_ref,
                            sbias_ref, wo1_ref, wo2_ref, g2_ref, wq_ref, k_ref, v_ref, wo_ref,
                            o_ref, wcat, sel)
            for r in range(0, tm, ATTN_ROW_GROUP)]
    _run_wavefront(gens, ATTN_STAGGER)


def _mix_attn(x, yr, zs, slnw, slnb, ws, sbias, w_out, g2, wq, k, v, wo, tm):
    t = x.shape[0]
    sq = _full((D_MODEL, D_MODEL))
    return pl.pallas_call(
        _mix_attn_kernel,
        out_shape=jax.ShapeDtypeStruct((t, D_MODEL), F32),
        grid=(t // tm,),
        in_specs=[pl.BlockSpec((tm, D_MODEL), lambda i: (i, 0)),
                  pl.BlockSpec((tm, RWKV_WIDTH), lambda i: (i, 0)),
                  pl.BlockSpec((tm, 2 * SGU_WIDTH), lambda i: (i, 0)),
                  _full((1, SGU_WIDTH)), _full((1, SGU_WIDTH)),
                  _full((SGU_GROUPS, SGU_BLOCK, SGU_BLOCK)), _full((SGU_BLOCK, SGU_WIDTH)),
                  sq, _full((1, D_MODEL)), sq,
                  _full((MEM_LEN, D_MODEL)), _full((MEM_LEN, D_MODEL)), sq],
        out_specs=pl.BlockSpec((tm, D_MODEL), lambda i: (i, 0)),
        scratch_shapes=[pltpu.VMEM((D_MODEL, D_MODEL), BF16)] * 3,
        compiler_params=_params(),
        name="mix_attn",
    )(x, yr, zs, slnw, slnb, ws, sbias, w_out, g2, wq, k, v, wo)


def _ffn_kernel(x_ref, g3_ref, wg_ref, wu_ref, wd_ref, gf_ref, o_ref):
    tm = x_ref.shape[0]
    groups = [slice(r, r + FFN_ROW_GROUP) for r in range(0, tm, FFN_ROW_GROUP)]
    x2 = [x_ref[r, :] for r in groups]
    h = [_rmsnorm(x, g3_ref[...]).astype(BF16) for x in x2]
    gate = [jnp.dot(h_, wg_ref[...], preferred_element_type=F32) for h_ in h]
    up = [jnp.dot(h_, wu_ref[...], preferred_element_type=F32) for h_ in h]
    act = [(jax.nn.silu(g_) * u_).astype(BF16) for g_, u_ in zip(gate, up)]
    x3 = [x + jnp.dot(a_, wd_ref[...], preferred_element_type=F32) for x, a_ in zip(x2, act)]
    for r, x in zip(groups, x3):
        o_ref[r, :] = _rmsnorm(x, gf_ref[...])


def _ffn(x, g3, wg, wu, wd, gf, tm):
    t = x.shape[0]
    return pl.pallas_call(
        _ffn_kernel,
        out_shape=jax.ShapeDtypeStruct((t, D_MODEL), F32),
        grid=(t // tm,),
        in_specs=[pl.BlockSpec((tm, D_MODEL), lambda i: (i, 0)), _full((1, D_MODEL)),
                  _full((D_MODEL, D_FF)), _full((D_MODEL, D_FF)), _full((D_FF, D_MODEL)),
                  _full((1, D_MODEL))],
        out_specs=pl.BlockSpec((tm, D_MODEL), lambda i: (i, 0)),
        compiler_params=_params("parallel"),
        name="ffn",
    )(x, g3, wg, wu, wd, gf)


def kernel(x, mem, norm1_g, w_in, shift_mu, w0, w_lora_up, a0, a_lora_up, g_lora_up, k_k, k_a, r_k,
           lnx_w, lnx_b, sgu_ln_w, sgu_ln_b, w_spatial, b_spatial, w_out, norm2_g, mem_norm_g,
           wq_x, wk_x, wv_x, wo_x, norm3_g, w_gate, w_up, w_down, norm_f_g):
    b, t, _ = x.shape
    depth = w_in.shape[0]
    assert depth == 1, "the final RMSNorm is fused into the (single) layer's ffn call"
    assert t % TM_ATTN == 0 and t % TM_FFN == 0
    assert t % TM_DENSE == 0 and TM_DENSE % TB_SCAN == 0 and TB_SCAN % CHUNK == 0
    row = lambda p: p.reshape(1, -1)
    bf = lambda p: p.astype(BF16)
    outs = []
    for bi in range(b):
        xb = x[bi]
        for l in range(depth):
            lora = w_lora_up.shape[1]
            zeros = jnp.zeros((lora, RWKV_WIDTH), F32)
            waup = jnp.concatenate(
                [jnp.concatenate([w_lora_up[l], zeros], axis=1),
                 jnp.concatenate([zeros, a_lora_up[l]], axis=1)], axis=0)
            bias = jnp.repeat(b_spatial[l].T, SGU_WIDTH // SGU_GROUPS, axis=1)

            front = _front(xb, row(norm1_g[l]), w_in[l], row(shift_mu[l]), row(w0[l]), bf(waup),
                           row(a0[l]), bf(g_lora_up[l]), row(k_k[l]), row(k_a[l]), row(r_k[l]),
                           TM_DENSE)
            prep = dict(zip(_PREP_NAMES, front[:len(_PREP_NAMES)]))
            z_sgu = front[len(_PREP_NAMES)]
            y_rwkv, wg_bf, wu_bf, wd_bf = _scan(prep, row(lnx_w[l]), row(lnx_b[l]),
                                                w_gate[l], w_up[l], w_down[l], TB_SCAN)
            k_mem, v_mem = _mem_kv(mem[bi], row(mem_norm_g[l]), wk_x[l], wv_x[l])
            x2 = _mix_attn(xb, y_rwkv, z_sgu, row(sgu_ln_w[l]), row(sgu_ln_b[l]), w_spatial[l], bias,
                           w_out[l], row(norm2_g[l]), wq_x[l], k_mem, v_mem, wo_x[l], TM_ATTN)
            xb = _ffn(x2, row(norm3_g[l]), wg_bf, wu_bf, wd_bf, row(norm_f_g), TM_FFN)
        outs.append(xb)
    return jnp.stack(outs, axis=0)
```

```python
import functools
import math

import jax
import jax.numpy as jnp
from jax import lax
from jax.experimental import pallas as pl
from jax.experimental.pallas import tpu as pltpu

F32 = jnp.float32
BF16 = jnp.bfloat16

D_MODEL = 1024
RWKV_WIDTH = 512
RWKV_HEAD = 64
LORA_WA = 128
GATE_LORA = 128
RWKV_IN = 3 * RWKV_WIDTH + LORA_WA + GATE_LORA
SGU_WIDTH = 512
SGU_GROUPS = 8
SGU_BLOCK = 128
IN_WIDTH = RWKV_IN + 2 * SGU_WIDTH
MEM_LEN = 256
XA_HEADS = 4
XA_HEAD_DIM = D_MODEL // XA_HEADS
D_FF = 2816
RMS_EPS = 1e-6
LN_EPS = 1e-5
LNX_EPS = 64e-5
EXP_M05 = 0.6065306597126334
LOG2_E = 1.4426950408889634

CHUNK = 64
PAIR = 2 * RWKV_HEAD
N_PAIRS = RWKV_WIDTH // PAIR
TM_DENSE = 512
TM_ATTN = 1024
TM_FFN = 1024
TB_SCAN = 512
Z_PAD = 8
WL_ROWS = 8
ATTN_ROW_GROUP = 256
ATTN_STAGGER = 2
FFN_ROW_GROUP = 256
IN_PROJ_COLS = 256
PREP_ROWS = 128
PREP_STREAMS = 4
PREP_PIECES_PER_CHUNK = 10
TERMS_DECAY_CUMSUM = 2
TERMS_HEAD_SUM = 1
TERMS_GROUP_MEAN = 2
VMEM_LIMIT = 56 * 1024 * 1024

_NN = (((1,), (0,)), ((), ()))
_NT = (((1,), (1,)), ((), ()))
_TN = (((0,), (0,)), ((), ()))


def _mm(a, b, dims=_NN):
    return lax.dot_general(a.astype(BF16), b.astype(BF16), dims, preferred_element_type=F32)


def _split_bf16(x, terms):
    parts = []
    rem = x
    for _ in range(terms):
        part = rem.astype(BF16)
        rem = rem - part.astype(F32)
        parts.append(part)
    return parts


def _cumsum_rows(ltri01, parts):
    return lax.dot_general(jnp.concatenate([ltri01] * len(parts), axis=1),
                           jnp.concatenate(parts, axis=0), _NN, preferred_element_type=F32)


def _head_sum_parts(parts, seg01):
    cols = []
    for q in range(parts[0].shape[1] // 256):
        acc = None
        for part in parts:
            d = lax.dot_general(part[:, 256 * q:256 * (q + 1)], seg01, _NN,
                                preferred_element_type=F32)
            acc = d if acc is None else acc + d
        cols.append(acc)
    return jnp.concatenate(cols, axis=1)


def _head_sum(x, seg01, terms=TERMS_HEAD_SUM):
    return _head_sum_parts(_split_bf16(x, terms), seg01)


def _seg01():
    li = lax.broadcasted_iota(jnp.int32, (256, 256), 0) >> 6
    lj = lax.broadcasted_iota(jnp.int32, (256, 256), 1) >> 6
    return (li == lj).astype(BF16)


def _gelu_tanh(x):
    k1 = -2.0 * math.sqrt(2.0 / math.pi) * math.log2(math.e)
    return x / (1.0 + jnp.exp2(x * (k1 + (k1 * 0.044715) * (x * x))))


def _rmsnorm(x, g):
    return x * lax.rsqrt(jnp.mean(x * x, axis=-1, keepdims=True) + RMS_EPS) * g


def _full(shape):
    n = len(shape)
    return pl.BlockSpec(shape, lambda i: (0,) * n, pipeline_mode=pl.Buffered(1))


def _params(sem="arbitrary"):
    return pltpu.CompilerParams(dimension_semantics=(sem,), vmem_limit_bytes=VMEM_LIMIT)


def _run_interleaved(stages, rounds):
    for r in range(rounds):
        for gen, n in stages:
            for _ in range((r + 1) * n // rounds - r * n // rounds):
                next(gen, None)
    for gen, _ in stages:
        assert next(gen, StopIteration) is StopIteration, "piece count too small"


def _run_wavefront(gens, stagger):
    live = list(enumerate(gens))
    r = 0
    while live:
        for entry in list(live):
            g, gen = entry
            if r >= g * stagger and next(gen, StopIteration) is StopIteration:
                live.remove(entry)
        r += 1


def _mem_kv_kernel(mem_ref, g_ref, wk_ref, wv_ref, k_ref, v_ref):
    m = _rmsnorm(mem_ref[...], g_ref[...]).astype(BF16)
    k_ref[...] = jnp.dot(m, wk_ref[...].astype(BF16), preferred_element_type=F32).astype(BF16)
    v_ref[...] = jnp.dot(m, wv_ref[...].astype(BF16), preferred_element_type=F32).astype(BF16)


def _mem_kv(mem, g, wk, wv):
    return pl.pallas_call(
        _mem_kv_kernel,
        out_shape=(jax.ShapeDtypeStruct((MEM_LEN, D_MODEL), BF16),) * 2,
        grid=(1,),
        in_specs=[_full((MEM_LEN, D_MODEL)), _full((1, D_MODEL)),
                  _full((D_MODEL, D_MODEL)), _full((D_MODEL, D_MODEL))],
        out_specs=(_full((MEM_LEN, D_MODEL)),) * 2,
        compiler_params=_params(),
        name="mem_kv",
    )(mem, g, wk, wv)


_PREP_F32 = ("rt", "at", "gate", "bonus")
_PREP_BF16 = ("bh", "kh", "bc", "kc", "v")
_PREP_NAMES = _PREP_F32 + _PREP_BF16 + ("wl",)


def _in_proj_pieces(x_ref, g_ref, w_ref, z_ref):
    h = _rmsnorm(x_ref[...], g_ref[...]).astype(BF16)
    yield
    for j in range(IN_WIDTH // IN_PROJ_COLS):
        cols = slice(j * IN_PROJ_COLS, (j + 1) * IN_PROJ_COLS)
        z_ref[Z_PAD:, cols] = jnp.dot(h, w_ref[:, cols], preferred_element_type=F32)
        yield


def _rwkv_prep_pieces(z_ref, mu_ref, w0_ref, waup_ref, a0_ref, gup_ref, kk_ref, ka_ref, rk_ref,
                      prep, chunks):
    seg01 = _seg01()
    lane = lax.broadcasted_iota(jnp.int32, (1, LORA_WA), 1)
    row = lax.broadcasted_iota(jnp.int32, (PREP_ROWS, 1), 0)
    ti = lax.broadcasted_iota(jnp.int32, (PREP_ROWS, PREP_ROWS), 0)
    tj = lax.broadcasted_iota(jnp.int32, (PREP_ROWS, PREP_ROWS), 1)
    ltri01 = ((tj <= ti) & ((ti >> 6) == (tj >> 6))).astype(BF16)
    chunks_per_scan_tile = TB_SCAN // CHUNK
    chunks_per_unit = PREP_ROWS // CHUNK
    half_w = RWKV_WIDTH // 2

    def shifted(c, cols):
        z = z_ref[Z_PAD + c * PREP_ROWS:Z_PAD + (c + 1) * PREP_ROWS, cols]
        before = z_ref[Z_PAD + c * PREP_ROWS - 1:Z_PAD + c * PREP_ROWS, cols]
        zprev = jnp.where(row == 0, before, pltpu.roll(z, 1, axis=0))
        return z + (zprev - z) * mu_ref[:, cols]

    for c in chunks:
        rows = slice(c * PREP_ROWS, (c + 1) * PREP_ROWS)
        wa_in = shifted(c, slice(3 * RWKV_WIDTH, 3 * RWKV_WIDTH + LORA_WA))
        wa_in = jnp.where(lane < LORA_WA // 2, jnp.tanh(wa_in), wa_in).astype(BF16)
        gd = jax.nn.sigmoid(shifted(c, slice(3 * RWKV_WIDTH + LORA_WA, RWKV_IN))).astype(BF16)
        for q in range(2):
            hc = slice(q * half_w, (q + 1) * half_w)
            r = shifted(c, hc)
            k = shifted(c, slice(RWKV_WIDTH + q * half_w, RWKV_WIDTH + (q + 1) * half_w))
            v = shifted(c, slice(2 * RWKV_WIDTH + q * half_w, 2 * RWKV_WIDTH + (q + 1) * half_w))
            prep["v"][rows, hc] = v.astype(BF16)
            kk = k * kk_ref[:, hc]
            kk_sq = _split_bf16(kk * kk, TERMS_HEAD_SUM)
            yield
            w_pre = w0_ref[:, hc] + jnp.dot(wa_in, waup_ref[:, hc], preferred_element_type=F32)
            a_pre = a0_ref[:, hc] + jnp.dot(
                wa_in, waup_ref[:, RWKV_WIDTH + q * half_w:RWKV_WIDTH + (q + 1) * half_w],
                preferred_element_type=F32)
            prep["gate"][rows, hc] = jnp.dot(gd, gup_ref[:, hc], preferred_element_type=F32)
            kk_ss = _head_sum_parts(kk_sq, seg01)
            yield
            a = jax.nn.sigmoid(a_pre)
            lw = jax.nn.sigmoid(w_pre) * (-EXP_M05 * LOG2_E)
            lw_parts = _split_bf16(lw, TERMS_DECAY_CUMSUM)
            kk = kk * lax.rsqrt(jnp.maximum(kk_ss, 1e-24))
            kmod = k * ((1.0 - ka_ref[:, hc]) + a * ka_ref[:, hc])
            kka = kk * a
            rkk = _split_bf16(r * kmod * rk_ref[:, hc], TERMS_HEAD_SUM)
            yield
            cs = _cumsum_rows(ltri01, lw_parts)
            prep["bonus"][rows, hc] = _head_sum_parts(rkk, seg01) * v
            yield
            w_inv = jnp.exp2(-cs)
            w_last = [jnp.exp2(cs[(j + 1) * CHUNK - 1:(j + 1) * CHUNK, :])
                      for j in range(chunks_per_unit)]
            w_tail = jnp.concatenate(
                [w_last[j] * w_inv[j * CHUNK:(j + 1) * CHUNK] for j in range(chunks_per_unit)],
                axis=0)
            prep["rt"][rows, hc] = r * jnp.exp2(cs)
            prep["at"][rows, hc] = -kk * jnp.exp2(cs - lw)
            prep["bh"][rows, hc] = (kka * w_inv).astype(BF16)
            prep["kh"][rows, hc] = (kmod * w_inv).astype(BF16)
            prep["bc"][rows, hc] = (kka * w_tail).astype(BF16)
            prep["kc"][rows, hc] = (kmod * w_tail).astype(BF16)
            for j in range(chunks_per_unit):
                cq, cr = divmod(c * chunks_per_unit + j, chunks_per_scan_tile)
                prep["wl"][cq, cr:cr + 1, hc] = w_last[j]
            yield


def _copy_pieces(z_ref, o_ref, *, tm):
    for b in range(tm // SGU_BLOCK):
        o_ref[b * SGU_BLOCK:(b + 1) * SGU_BLOCK, :] = (
            z_ref[Z_PAD + b * SGU_BLOCK:Z_PAD + (b + 1) * SGU_BLOCK, RWKV_IN:])
        yield


def _front_kernel(x_ref, g1_ref, win_ref, mu_ref, w0_ref, waup_ref, a0_ref, gup_ref, kk_ref, ka_ref,
                  rk_ref, *rest, tm):
    n = len(_PREP_NAMES)
    prep = dict(zip(_PREP_NAMES, rest[:n]))
    zs_ref = rest[n]
    wbf_ref, z0_ref, z1_ref = rest[n + 1:]
    i = pl.program_id(0)

    @pl.when(i == 0)
    def _():
        z1_ref[...] = jnp.zeros_like(z1_ref)
        wbf_ref[...] = win_ref[...].astype(BF16)

    def step(z_write, z_read):
        n_chunks = tm // PREP_ROWS
        prep_pieces = [
            _rwkv_prep_pieces(z_read, mu_ref, w0_ref, waup_ref, a0_ref, gup_ref, kk_ref, ka_ref,
                              rk_ref, prep, range(k, n_chunks, PREP_STREAMS))
            for k in range(PREP_STREAMS)]
        n_dot = 1 + IN_WIDTH // IN_PROJ_COLS + 1
        _run_interleaved(
            [(_in_proj_pieces(x_ref, g1_ref, wbf_ref, z_write), n_dot)]
            + [(gen, PREP_PIECES_PER_CHUNK * n_chunks // PREP_STREAMS + 1) for gen in prep_pieces]
            + [(_copy_pieces(z_read, zs_ref, tm=tm), tm // SGU_BLOCK + 1)],
            rounds=n_dot)
        if TB_SCAN // CHUNK < WL_ROWS:
            prep["wl"][:, TB_SCAN // CHUNK:, :] = jnp.zeros(
                (tm // TB_SCAN, WL_ROWS - TB_SCAN // CHUNK, RWKV_WIDTH), F32)
        z_write[Z_PAD - 1:Z_PAD, :] = z_read[Z_PAD + tm - 1:Z_PAD + tm, :]

    @pl.when((i & 1) == 0)
    def _():
        step(z0_ref, z1_ref)

    @pl.when((i & 1) == 1)
    def _():
        step(z1_ref, z0_ref)


def _front(x, g1, w_in, mu, w0, waup, a0, gup, k_k, k_a, r_k, tm):
    t = x.shape[0]
    n_tiles = t // tm
    vec = _full((1, RWKV_WIDTH))
    out_tile = lambda i: (jnp.maximum(i - 1, 0), 0)
    out_shapes = ([jax.ShapeDtypeStruct((t, RWKV_WIDTH), F32)] * len(_PREP_F32)
                  + [jax.ShapeDtypeStruct((t, RWKV_WIDTH), BF16)] * len(_PREP_BF16)
                  + [jax.ShapeDtypeStruct((t // TB_SCAN, WL_ROWS, RWKV_WIDTH), F32),
                     jax.ShapeDtypeStruct((t, 2 * SGU_WIDTH), F32)])
    out_specs = ([pl.BlockSpec((tm, RWKV_WIDTH), out_tile)] * (len(_PREP_F32) + len(_PREP_BF16))
                 + [pl.BlockSpec((tm // TB_SCAN, WL_ROWS, RWKV_WIDTH),
                                 lambda i: (jnp.maximum(i - 1, 0), 0, 0)),
                    pl.BlockSpec((tm, 2 * SGU_WIDTH), out_tile)])
    return pl.pallas_call(
        functools.partial(_front_kernel, tm=tm),
        out_shape=tuple(out_shapes),
        grid=(n_tiles + 1,),
        in_specs=[pl.BlockSpec((tm, D_MODEL), lambda i: (jnp.minimum(i, n_tiles - 1), 0)),
                  _full((1, D_MODEL)), _full((D_MODEL, IN_WIDTH)), _full((1, RWKV_IN)), vec,
                  _full((LORA_WA, 2 * RWKV_WIDTH)), vec, _full((GATE_LORA, RWKV_WIDTH)),
                  vec, vec, vec],
        out_specs=tuple(out_specs),
        scratch_shapes=[pltpu.VMEM((D_MODEL, IN_WIDTH), BF16),
                        pltpu.VMEM((Z_PAD + tm, IN_WIDTH), F32),
                        pltpu.VMEM((Z_PAD + tm, IN_WIDTH), F32)],
        compiler_params=_params(),
        name="front",
    )(x, g1, w_in, mu, w0, waup, a0, gup, k_k, k_a, r_k)


def _pair_masks():
    t = lax.broadcasted_iota(jnp.int32, (CHUNK, PAIR), 0)
    j = lax.broadcasted_iota(jnp.int32, (CHUNK, PAIR), 1) & (CHUNK - 1)
    strict = j < t
    incl = j <= t
    blk16 = (t >> 4) == (j >> 4)
    blk32 = (t >> 5) == (j >> 5)
    return strict, incl, blk16, blk32


def _bd(x, bd_mask):
    x = x.astype(BF16)
    return jnp.where(bd_mask, jnp.concatenate([x, x], axis=0), 0.0).astype(BF16)


def _staged(fn, items, parts=2):
    out = []
    n = len(items) // parts
    for k in range(parts):
        out += [fn(*item) for item in items[k * n:(k + 1) * n]]
        yield
    return out


def _unit_lower_inverse_minus_identity(a_list, masks, bd_mask):
    _, _, blk16, blk32 = masks
    ad = [jnp.where(blk16, a, 0.0) for a in a_list]
    ap = yield from _staged(lambda x: _mm(x, _bd(x, bd_mask)), [(x,) for x in ad])
    tp = ad
    for _ in range(2):
        both = yield from _staged(
            lambda p, t: _mm(p, jnp.concatenate([_bd(p, bd_mask), _bd(t, bd_mask)], axis=1)),
            list(zip(ap, tp)))
        tp = [t + p + b[:, PAIR:] for t, p, b in zip(tp, ap, both)]
        ap = [b[:, :PAIR] for b in both]
    last = yield from _staged(lambda p, t: _mm(p, _bd(t, bd_mask)), list(zip(ap, tp)))
    tp = [t + p + x for t, p, x in zip(tp, ap, last)]
    for off_mask in (blk32 & ~blk16, ~blk32):
        off = [jnp.where(off_mask, a, 0.0) for a in a_list]
        x = yield from _staged(lambda o, t: o + _mm(t, _bd(o, bd_mask)), list(zip(off, tp)))
        tp = yield from _staged(lambda t, xx: t + xx + _mm(xx, _bd(t, bd_mask)), list(zip(tp, x)))
    return tp


_TERM_STAGES = 11
_TERM_NAMES = ("achk", "uv", "bb")


def _bd_masks():
    bi = lax.broadcasted_iota(jnp.int32, (PAIR, PAIR), 0) >> 6
    bj = lax.broadcasted_iota(jnp.int32, (PAIR, PAIR), 1) >> 6
    bd1 = bi == bj
    return bd1, jnp.concatenate([bd1, bd1], axis=1)


def _scan_terms_pieces(rt_ref, at_ref, bh_ref, kh_ref, v_ref, terms, *, tb):
    masks = _pair_masks()
    strict, incl = masks[0], masks[1]
    bd1, bd2 = _bd_masks()
    probs = [(c, p) for c in range(tb // CHUNK) for p in range(N_PAIRS)]
    cut = lambda ref: [ref[c * CHUNK:(c + 1) * CHUNK, p * PAIR:(p + 1) * PAIR] for c, p in probs]
    rt_p, at_p, bh_p, kh_p, v_p = map(cut, (rt_ref, at_ref, bh_ref, kh_ref, v_ref))
    gram = yield from _staged(
        lambda a_, r_, b_, k_: _mm(jnp.concatenate([a_, r_], axis=0),
                                   jnp.concatenate([_bd(b_, bd1), _bd(k_, bd1)], axis=0), _NT),
        list(zip(at_p, rt_p, bh_p, kh_p)))
    a_ab = [jnp.where(strict, g_[:CHUNK, :PAIR], 0.0) for g_ in gram]
    a_ak = [jnp.where(strict, g_[:CHUNK, PAIR:], 0.0) for g_ in gram]
    incl2 = jnp.concatenate([incl, incl], axis=1)
    for i, g_ in enumerate(gram):
        terms["bb"][i] = jnp.where(incl2, g_[CHUNK:], 0.0).astype(BF16)
    rhs = yield from _staged(
        lambda m_, x_, a_: jnp.concatenate([_mm(m_, _bd(x_, bd1)), a_], axis=1),
        list(zip(a_ak, v_p, at_p)))
    tp = yield from _unit_lower_inverse_minus_identity(a_ab, masks, bd1)
    sol = yield from _staged(lambda x_, t_: x_ + _mm(t_, _bd(x_, bd2)), list(zip(rhs, tp)))
    for i, x_ in enumerate(sol):
        terms["uv"][i] = x_[:, :PAIR]
        terms["achk"][i] = x_[:, PAIR:].astype(BF16)


def _scan_state_pieces(terms, rt_ref, v_ref, bc_ref, kc_ref, wl_ref, gate_ref, bonus_ref, lnw_ref,
                       lnb_ref, o_ref, s_ref, y_ref, *, tb):
    bd1, _ = _bd_masks()
    bd_upd = jnp.logical_and(bd1, pl.program_id(0) > 0)
    s = [s_ref[p] for p in range(N_PAIRS)]
    for c in range(tb // CHUNK):
        rows = slice(c * CHUNK, (c + 1) * CHUNK)
        lanes = [slice(p * PAIR, (p + 1) * PAIR) for p in range(N_PAIRS)]
        idx = [c * N_PAIRS + p for p in range(N_PAIRS)]
        on_s = [_mm(jnp.concatenate([terms["achk"][i], rt_ref[rows, lanes[p]].astype(BF16)], axis=0),
                    s[p], _NT) for p, i in enumerate(idx)]
        u = [x[:CHUNK] + terms["uv"][i] for x, i in zip(on_s, idx)]
        yield
        w_last = wl_ref[0, c:c + 1, :]
        upd = []
        for p, i in enumerate(idx):
            v_i = v_ref[rows, lanes[p]]
            y_ref[rows, lanes[p]] = (
                on_s[p][CHUNK:]
                + _mm(terms["bb"][i], jnp.concatenate([_bd(u[p], bd1), _bd(v_i, bd1)], axis=0)))
            upd.append(_mm(jnp.concatenate([u[p].astype(BF16), v_i], axis=0),
                           jnp.concatenate([bc_ref[rows, lanes[p]], kc_ref[rows, lanes[p]]], axis=0),
                           _TN))
        s = [s[p] * w_last[:, lanes[p]] + jnp.where(bd_upd, upd[p], 0.0) for p in range(N_PAIRS)]
        yield
    for p in range(N_PAIRS):
        s_ref[p] = s[p]
    seg01 = _seg01()
    y = y_ref[...]
    mean = _head_sum(y, seg01, TERMS_GROUP_MEAN) * (1.0 / RWKV_HEAD)
    yield
    d = y - mean
    var = _head_sum(d * d, seg01) * (1.0 / RWKV_HEAD)
    yield
    yn = d * lax.rsqrt(var + LNX_EPS) * lnw_ref[...] + lnb_ref[...]
    o_ref[...] = (yn + bonus_ref[...]) * gate_ref[...]


def _scan_kernel(rt_ref, at_ref, bh_ref, kh_ref, v_ref, rtp_ref, vp_ref, bcp_ref, kcp_ref, gate_ref,
                 bonus_ref, wl_ref, lnw_ref, lnb_ref, wg_ref, wu_ref, wd_ref, o_ref, wg_bf_ref,
                 wu_bf_ref, wd_bf_ref, s_ref, y_ref, *term_refs, tb):
    n = len(_TERM_NAMES)
    slots = [dict(zip(_TERM_NAMES, term_refs[k * n:(k + 1) * n])) for k in range(2)]
    i = pl.program_id(0)

    @pl.when(i == 0)
    def _():
        s_ref[...] = jnp.zeros_like(s_ref)
        for ref in slots[1].values():
            ref[...] = jnp.zeros_like(ref)

    wg_bf_ref[...] = wg_ref[...].astype(BF16)
    wu_bf_ref[...] = wu_ref[...].astype(BF16)
    wd_bf_ref[...] = wd_ref[...].astype(BF16)

    def step(write, read):
        n_terms = 2 * _TERM_STAGES + 1
        _run_interleaved(
            [(_scan_terms_pieces(rt_ref, at_ref, bh_ref, kh_ref, v_ref, write, tb=tb), n_terms),
             (_scan_state_pieces(read, rtp_ref, vp_ref, bcp_ref, kcp_ref, wl_ref, gate_ref,
                                 bonus_ref, lnw_ref, lnb_ref, o_ref, s_ref, y_ref, tb=tb),
              2 * (tb // CHUNK) + 3)],
            rounds=n_terms)

    @pl.when((i & 1) == 0)
    def _():
        step(slots[0], slots[1])

    @pl.when((i & 1) == 1)
    def _():
        step(slots[1], slots[0])


def _scan(prep, lnw, lnb, w_gate, w_up, w_down, tb):
    t = prep["rt"].shape[0]
    n_tiles = t // tb
    assert tb // CHUNK <= WL_ROWS
    last = n_tiles - 1
    cur = pl.BlockSpec((tb, RWKV_WIDTH), lambda i: (jnp.minimum(i, last), 0))
    prev = pl.BlockSpec((tb, RWKV_WIDTH), lambda i: (jnp.maximum(i - 1, 0), 0))
    vec = _full((1, RWKV_WIDTH))
    assert D_MODEL % n_tiles == 0 and (D_MODEL // n_tiles) % 16 == 0
    assert n_tiles % 2 == 0 and D_FF % (n_tiles // 2) == 0 and (D_FF // (n_tiles // 2)) % 16 == 0
    up_rows = pl.BlockSpec((D_MODEL // n_tiles, D_FF), lambda i: (jnp.minimum(i, last), 0))
    down_blk = pl.BlockSpec((D_FF // (n_tiles // 2), D_MODEL // 2),
                            lambda i: (jnp.minimum(i, last) // 2, jnp.minimum(i, last) % 2))
    n_prob = (tb // CHUNK) * N_PAIRS
    term_shapes = [pltpu.VMEM((n_prob, CHUNK, PAIR), BF16), pltpu.VMEM((n_prob, CHUNK, PAIR), F32),
                   pltpu.VMEM((n_prob, CHUNK, 2 * PAIR), BF16)]
    return pl.pallas_call(
        functools.partial(_scan_kernel, tb=tb),
        out_shape=(jax.ShapeDtypeStruct((t, RWKV_WIDTH), F32),
                   jax.ShapeDtypeStruct((D_MODEL, D_FF), BF16),
                   jax.ShapeDtypeStruct((D_MODEL, D_FF), BF16),
                   jax.ShapeDtypeStruct((D_FF, D_MODEL), BF16)),
        grid=(n_tiles + 1,),
        in_specs=[cur] * 5 + [prev] * 6
        + [pl.BlockSpec((1, WL_ROWS, RWKV_WIDTH), lambda i: (jnp.maximum(i - 1, 0), 0, 0)),
           vec, vec, up_rows, up_rows, down_blk],
        out_specs=(prev, up_rows, up_rows, down_blk),
        scratch_shapes=[pltpu.VMEM((N_PAIRS, PAIR, PAIR), F32), pltpu.VMEM((tb, RWKV_WIDTH), F32)]
        + term_shapes * 2,
        compiler_params=_params(),
        name="scan",
    )(prep["rt"], prep["at"], prep["bh"], prep["kh"], prep["v"],
      prep["rt"], prep["v"], prep["bc"], prep["kc"], prep["gate"], prep["bonus"], prep["wl"],
      lnw, lnb, w_gate, w_up, w_down)


def _sgu_block_prepare(z, lnw, lnb, sel):
    hz = _gelu_tanh(z)
    u = hz[:, :SGU_WIDTH]
    vf = hz[:, SGU_WIDTH:]
    mu = jnp.mean(vf, axis=-1, keepdims=True)
    d = vf - mu
    var = jnp.mean(d * d, axis=-1, keepdims=True)
    vn = d * lax.rsqrt(var + LN_EPS) * lnw + lnb
    stacks = []
    for p in range(SGU_WIDTH // PAIR):
        vb = vn[:, p * PAIR:(p + 1) * PAIR]
        stacks.append(jnp.where(sel, jnp.concatenate([vb, vb], axis=0), 0.0).astype(BF16))
    return u, stacks


def _sgu_block_mix(u, stacks, wcat, bias):
    return jnp.concatenate(
        [u[:, p * PAIR:(p + 1) * PAIR]
         * (jnp.dot(wcat[p], stacks[p], preferred_element_type=F32) + bias[:, p * PAIR:(p + 1) * PAIR])
         for p in range(SGU_WIDTH // PAIR)], axis=1)


def _mix_attn_group(r, x_ref, yr_ref, zs_ref, slnw_ref, slnb_ref, sbias_ref, wo1_ref, wo2_ref,
                    g2_ref, wq_ref, k_ref, v_ref, wo_ref, o_ref, wcat, sel):
    heads = [slice(hd * XA_HEAD_DIM, (hd + 1) * XA_HEAD_DIM) for hd in range(XA_HEADS)]
    prepared = [_sgu_block_prepare(zs_ref[b:b + SGU_BLOCK, :], slnw_ref[...], slnb_ref[...], sel)
                for b in range(r.start, r.stop, SGU_BLOCK)]
    yield
    x1 = x_ref[r, :] + jnp.dot(yr_ref[r, :].astype(BF16), wo1_ref[...],
                               preferred_element_type=F32)
    y_sgu = jnp.concatenate([_sgu_block_mix(u, st, wcat, sbias_ref[...]) for u, st in prepared],
                            axis=0)
    yield
    x1 = x1 + jnp.dot(y_sgu.astype(BF16), wo2_ref[...], preferred_element_type=F32)
    yield
    h = _rmsnorm(x1, g2_ref[...]).astype(BF16)
    yield
    q = jnp.dot(h, wq_ref[...], preferred_element_type=F32).astype(BF16)
    s = [lax.dot_general(q[:, hl], k_ref[:, hl], _NT, preferred_element_type=F32)
         * (XA_HEAD_DIM ** -0.5) for hl in heads]
    yield
    p = []
    for s_h in s:
        e = jnp.exp(s_h - jnp.max(s_h, axis=-1, keepdims=True))
        p.append((e / jnp.sum(e, axis=-1, keepdims=True)).astype(BF16))
    yield
    o = jnp.concatenate([jnp.dot(p_h, v_ref[:, hl], preferred_element_type=F32)
                         for p_h, hl in zip(p, heads)], axis=1).astype(BF16)
    o_ref[r, :] = x1 + jnp.dot(o, wo_ref[...], preferred_element_type=F32)


def _mix_attn_kernel(x_ref, yr_ref, zs_ref, slnw_ref, slnb_ref, ws_ref, sbias_ref, wout_f32_ref,
                     g2_ref, wq_f32_ref, k_ref, v_ref, wo_f32_ref, o_ref, wout_ref, wq_ref, wo_ref):
    @pl.when(pl.program_id(0) == 0)
    def _():
        wout_ref[...] = wout_f32_ref[...].astype(BF16)
        wq_ref[...] = wq_f32_ref[...].astype(BF16)
        wo_ref[...] = wo_f32_ref[...].astype(BF16)

    wo1_ref = wout_ref.at[:RWKV_WIDTH]
    wo2_ref = wout_ref.at[RWKV_WIDTH:]
    tm = x_ref.shape[0]
    ti = lax.broadcasted_iota(jnp.int32, (SGU_BLOCK, SGU_BLOCK), 0)
    tj = lax.broadcasted_iota(jnp.int32, (SGU_BLOCK, SGU_BLOCK), 1)
    tril = tj <= ti
    wcat = [jnp.concatenate([jnp.where(tril, ws_ref[2 * p], 0.0),
                             jnp.where(tril, ws_ref[2 * p + 1], 0.0)], axis=1).astype(BF16)
            for p in range(SGU_WIDTH // PAIR)]
    bi = lax.broadcasted_iota(jnp.int32, (2 * SGU_BLOCK, PAIR), 0) >> 7
    bj = lax.broadcasted_iota(jnp.int32, (2 * SGU_BLOCK, PAIR), 1) >> 6
    sel = bi == bj
    gens = [_mix_attn_group(slice(r, r + ATTN_ROW_GROUP), x_ref, yr_ref, zs_ref, slnw_ref, slnb_ref,
                            sbias_ref, wo1_ref, wo2_ref, g2_ref, wq_ref, k_ref, v_ref, wo_ref,
                            o_ref, wcat, sel)
            for r in range(0, tm, ATTN_ROW_GROUP)]
    _run_wavefront(gens, ATTN_STAGGER)


def _mix_attn(x, yr, zs, slnw, slnb, ws, sbias, w_out, g2, wq, k, v, wo, tm):
    t = x.shape[0]
    sq = _full((D_MODEL, D_MODEL))
    return pl.pallas_call(
        _mix_attn_kernel,
        out_shape=jax.ShapeDtypeStruct((t, D_MODEL), F32),
        grid=(t // tm,),
        in_specs=[pl.BlockSpec((tm, D_MODEL), lambda i: (i, 0)),
                  pl.BlockSpec((tm, RWKV_WIDTH), lambda i: (i, 0)),
                  pl.BlockSpec((tm, 2 * SGU_WIDTH), lambda i: (i, 0)),
                  _full((1, SGU_WIDTH)), _full((1, SGU_WIDTH)),
                  _full((SGU_GROUPS, SGU_BLOCK, SGU_BLOCK)), _full((SGU_BLOCK, SGU_WIDTH)),
                  sq, _full((1, D_MODEL)), sq,
                  _full((MEM_LEN, D_MODEL)), _full((MEM_LEN, D_MODEL)), sq],
        out_specs=pl.BlockSpec((tm, D_MODEL), lambda i: (i, 0)),
        scratch_shapes=[pltpu.VMEM((D_MODEL, D_MODEL), BF16)] * 3,
        compiler_params=_params(),
        name="mix_attn",
    )(x, yr, zs, slnw, slnb, ws, sbias, w_out, g2, wq, k, v, wo)


def _ffn_kernel(x_ref, g3_ref, wg_ref, wu_ref, wd_ref, gf_ref, o_ref):
    tm = x_ref.shape[0]
    groups = [slice(r, r + FFN_ROW_GROUP) for r in range(0, tm, FFN_ROW_GROUP)]
    x2 = [x_ref[r, :] for r in groups]
    h = [_rmsnorm(x, g3_ref[...]).astype(BF16) for x in x2]
    gate = [jnp.dot(h_, wg_ref[...], preferred_element_type=F32) for h_ in h]
    up = [jnp.dot(h_, wu_ref[...], preferred_element_type=F32) for h_ in h]
    act = [(jax.nn.silu(g_) * u_).astype(BF16) for g_, u_ in zip(gate, up)]
    x3 = [x + jnp.dot(a_, wd_ref[...], preferred_element_type=F32) for x, a_ in zip(x2, act)]
    for r, x in zip(groups, x3):
        o_ref[r, :] = _rmsnorm(x, gf_ref[...])


def _ffn(x, g3, wg, wu, wd, gf, tm):
    t = x.shape[0]
    return pl.pallas_call(
        _ffn_kernel,
        out_shape=jax.ShapeDtypeStruct((t, D_MODEL), F32),
        grid=(t // tm,),
        in_specs=[pl.BlockSpec((tm, D_MODEL), lambda i: (i, 0)), _full((1, D_MODEL)),
                  _full((D_MODEL, D_FF)), _full((D_MODEL, D_FF)), _full((D_FF, D_MODEL)),
                  _full((1, D_MODEL))],
        out_specs=pl.BlockSpec((tm, D_MODEL), lambda i: (i, 0)),
        compiler_params=_params("parallel"),
        name="ffn",
    )(x, g3, wg, wu, wd, gf)


def kernel(x, mem, norm1_g, w_in, shift_mu, w0, w_lora_up, a0, a_lora_up, g_lora_up, k_k, k_a, r_k,
           lnx_w, lnx_b, sgu_ln_w, sgu_ln_b, w_spatial, b_spatial, w_out, norm2_g, mem_norm_g,
           wq_x, wk_x, wv_x, wo_x, norm3_g, w_gate, w_up, w_down, norm_f_g):
    b, t, _ = x.shape
    depth = w_in.shape[0]
    assert depth == 1, "the final RMSNorm is fused into the (single) layer's ffn call"
    assert t % TM_ATTN == 0 and t % TM_FFN == 0
    assert t % TM_DENSE == 0 and TM_DENSE % TB_SCAN == 0 and TB_SCAN % CHUNK == 0
    row = lambda p: p.reshape(1, -1)
    bf = lambda p: p.astype(BF16)
    outs = []
    for bi in range(b):
        xb = x[bi]
        for l in range(depth):
            lora = w_lora_up.shape[1]
            zeros = jnp.zeros((lora, RWKV_WIDTH), F32)
            waup = jnp.concatenate(
                [jnp.concatenate([w_lora_up[l], zeros], axis=1),
                 jnp.concatenate([zeros, a_lora_up[l]], axis=1)], axis=0)
            bias = jnp.repeat(b_spatial[l].T, SGU_WIDTH // SGU_GROUPS, axis=1)

            front = _front(xb, row(norm1_g[l]), w_in[l], row(shift_mu[l]), row(w0[l]), bf(waup),
                           row(a0[l]), bf(g_lora_up[l]), row(k_k[l]), row(k_a[l]), row(r_k[l]),
                           TM_DENSE)
            prep = dict(zip(_PREP_NAMES, front[:len(_PREP_NAMES)]))
            z_sgu = front[len(_PREP_NAMES)]
            y_rwkv, wg_bf, wu_bf, wd_bf = _scan(prep, row(lnx_w[l]), row(lnx_b[l]),
                                                w_gate[l], w_up[l], w_down[l], TB_SCAN)
            k_mem, v_mem = _mem_kv(mem[bi], row(mem_norm_g[l]), wk_x[l], wv_x[l])
            x2 = _mix_attn(xb, y_rwkv, z_sgu, row(sgu_ln_w[l]), row(sgu_ln_b[l]), w_spatial[l], bias,
                           w_out[l], row(norm2_g[l]), wq_x[l], k_mem, v_mem, wo_x[l], TM_ATTN)
            xb = _ffn(x2, row(norm3_g[l]), wg_bf, wu_bf, wd_bf, row(norm_f_g), TM_FFN)
        outs.append(xb)
    return jnp.stack(outs, axis=0)
```

```python
import functools
import math

import jax
import jax.numpy as jnp
from jax import lax
from jax.experimental import pallas as pl
from jax.experimental.pallas import tpu as pltpu

F32 = jnp.float32
BF16 = jnp.bfloat16

D_MODEL = 1024
RWKV_WIDTH = 512
RWKV_HEAD = 64
LORA_WA = 128
GATE_LORA = 128
RWKV_IN = 3 * RWKV_WIDTH + LORA_WA + GATE_LORA
SGU_WIDTH = 512
SGU_GROUPS = 8
SGU_BLOCK = 128
IN_WIDTH = RWKV_IN + 2 * SGU_WIDTH
MEM_LEN = 256
XA_HEADS = 4
XA_HEAD_DIM = D_MODEL // XA_HEADS
D_FF = 2816
RMS_EPS = 1e-6
LN_EPS = 1e-5
LNX_EPS = 64e-5
EXP_M05 = 0.6065306597126334
LOG2_E = 1.4426950408889634

CHUNK = 64
PAIR = 2 * RWKV_HEAD
N_PAIRS = RWKV_WIDTH // PAIR
TM_DENSE = 512
TM_ATTN = 1024
TM_FFN = 1024
TB_SCAN = 512
Z_PAD = 8
WL_ROWS = 8
ATTN_ROW_GROUP = 256
ATTN_STAGGER = 2
FFN_ROW_GROUP = 256
IN_PROJ_COLS = 256
PREP_ROWS = 128
PREP_STREAMS = 4
PREP_PIECES_PER_CHUNK = 10
TERMS_DECAY_CUMSUM = 2
TERMS_HEAD_SUM = 1
TERMS_GROUP_MEAN = 2
VMEM_LIMIT = 56 * 1024 * 1024

_NN = (((1,), (0,)), ((), ()))
_NT = (((1,), (1,)), ((), ()))
_TN = (((0,), (0,)), ((), ()))


def _mm(a, b, dims=_NN):
    return lax.dot_general(a.astype(BF16), b.astype(BF16), dims, preferred_element_type=F32)


def _split_bf16(x, terms):
    parts = []
    rem = x
    for _ in range(terms):
        part = rem.astype(BF16)
        rem = rem - part.astype(F32)
        parts.append(part)
    return parts


def _cumsum_rows(ltri01, parts):
    return lax.dot_general(jnp.concatenate([ltri01] * len(parts), axis=1),
                           jnp.concatenate(parts, axis=0), _NN, preferred_element_type=F32)


def _head_sum_parts(parts, seg01):
    cols = []
    for q in range(parts[0].shape[1] // 256):
        acc = None
        for part in parts:
            d = lax.dot_general(part[:, 256 * q:256 * (q + 1)], seg01, _NN,
                                preferred_element_type=F32)
            acc = d if acc is None else acc + d
        cols.append(acc)
    return jnp.concatenate(cols, axis=1)


def _head_sum(x, seg01, terms=TERMS_HEAD_SUM):
    return _head_sum_parts(_split_bf16(x, terms), seg01)


def _seg01():
    li = lax.broadcasted_iota(jnp.int32, (256, 256), 0) >> 6
    lj = lax.broadcasted_iota(jnp.int32, (256, 256), 1) >> 6
    return (li == lj).astype(BF16)


def _gelu_tanh(x):
    k1 = -2.0 * math.sqrt(2.0 / math.pi) * math.log2(math.e)
    return x / (1.0 + jnp.exp2(x * (k1 + (k1 * 0.044715) * (x * x))))


def _rmsnorm(x, g):
    return x * lax.rsqrt(jnp.mean(x * x, axis=-1, keepdims=True) + RMS_EPS) * g


def _full(shape):
    n = len(shape)
    return pl.BlockSpec(shape, lambda i: (0,) * n, pipeline_mode=pl.Buffered(1))


def _params(sem="arbitrary"):
    return pltpu.CompilerParams(dimension_semantics=(sem,), vmem_limit_bytes=VMEM_LIMIT)


def _run_interleaved(stages, rounds):
    for r in range(rounds):
        for gen, n in stages:
            for _ in range((r + 1) * n // rounds - r * n // rounds):
                next(gen, None)
    for gen, _ in stages:
        assert next(gen, StopIteration) is StopIteration, "piece count too small"


def _run_wavefront(gens, stagger):
    live = list(enumerate(gens))
    r = 0
    while live:
        for entry in list(live):
            g, gen = entry
            if r >= g * stagger and next(gen, StopIteration) is StopIteration:
                live.remove(entry)
        r += 1


def _mem_kv_kernel(mem_ref, g_ref, wk_ref, wv_ref, k_ref, v_ref):
    m = _rmsnorm(mem_ref[...], g_ref[...]).astype(BF16)
    k_ref[...] = jnp.dot(m, wk_ref[...].astype(BF16), preferred_element_type=F32).astype(BF16)
    v_ref[...] = jnp.dot(m, wv_ref[...].astype(BF16), preferred_element_type=F32).astype(BF16)


def _mem_kv(mem, g, wk, wv):
    return pl.pallas_call(
        _mem_kv_kernel,
        out_shape=(jax.ShapeDtypeStruct((MEM_LEN, D_MODEL), BF16),) * 2,
        grid=(1,),
        in_specs=[_full((MEM_LEN, D_MODEL)), _full((1, D_MODEL)),
                  _full((D_MODEL, D_MODEL)), _full((D_MODEL, D_MODEL))],
        out_specs=(_full((MEM_LEN, D_MODEL)),) * 2,
        compiler_params=_params(),
        name="mem_kv",
    )(mem, g, wk, wv)


_PREP_F32 = ("at", "gate", "bonus")
_PREP_BF16 = ("rt", "bh", "kh", "bc", "kc", "v")
_PREP_NAMES = _PREP_F32 + _PREP_BF16 + ("wl",)


def _in_proj_pieces(x_ref, g_ref, w_ref, z_ref):
    h = _rmsnorm(x_ref[...], g_ref[...]).astype(BF16)
    yield
    for j in range(IN_WIDTH // IN_PROJ_COLS):
        cols = slice(j * IN_PROJ_COLS, (j + 1) * IN_PROJ_COLS)
        z_ref[Z_PAD:, cols] = jnp.dot(h, w_ref[:, cols], preferred_element_type=F32)
        yield


def _rwkv_prep_pieces(z_ref, mu_ref, w0_ref, waup_ref, a0_ref, gup_ref, kk_ref, ka_ref, rk_ref,
                      prep, chunks):
    seg01 = _seg01()
    lane = lax.broadcasted_iota(jnp.int32, (1, LORA_WA), 1)
    row = lax.broadcasted_iota(jnp.int32, (PREP_ROWS, 1), 0)
    ti = lax.broadcasted_iota(jnp.int32, (PREP_ROWS, PREP_ROWS), 0)
    tj = lax.broadcasted_iota(jnp.int32, (PREP_ROWS, PREP_ROWS), 1)
    ltri01 = ((tj <= ti) & ((ti >> 6) == (tj >> 6))).astype(BF16)
    chunks_per_scan_tile = TB_SCAN // CHUNK
    chunks_per_unit = PREP_ROWS // CHUNK
    half_w = RWKV_WIDTH // 2

    def shifted(c, cols):
        z = z_ref[Z_PAD + c * PREP_ROWS:Z_PAD + (c + 1) * PREP_ROWS, cols]
        before = z_ref[Z_PAD + c * PREP_ROWS - 1:Z_PAD + c * PREP_ROWS, cols]
        zprev = jnp.where(row == 0, before, pltpu.roll(z, 1, axis=0))
        return z + (zprev - z) * mu_ref[:, cols]

    for c in chunks:
        rows = slice(c * PREP_ROWS, (c + 1) * PREP_ROWS)
        wa_in = shifted(c, slice(3 * RWKV_WIDTH, 3 * RWKV_WIDTH + LORA_WA))
        wa_in = jnp.where(lane < LORA_WA // 2, jnp.tanh(wa_in), wa_in).astype(BF16)
        gd = jax.nn.sigmoid(shifted(c, slice(3 * RWKV_WIDTH + LORA_WA, RWKV_IN))).astype(BF16)
        for q in range(2):
            hc = slice(q * half_w, (q + 1) * half_w)
            r = shifted(c, hc)
            k = shifted(c, slice(RWKV_WIDTH + q * half_w, RWKV_WIDTH + (q + 1) * half_w))
            v = shifted(c, slice(2 * RWKV_WIDTH + q * half_w, 2 * RWKV_WIDTH + (q + 1) * half_w))
            prep["v"][rows, hc] = v.astype(BF16)
            kk = k * kk_ref[:, hc]
            kk_sq = _split_bf16(kk * kk, TERMS_HEAD_SUM)
            yield
            w_pre = w0_ref[:, hc] + jnp.dot(wa_in, waup_ref[:, hc], preferred_element_type=F32)
            a_pre = a0_ref[:, hc] + jnp.dot(
                wa_in, waup_ref[:, RWKV_WIDTH + q * half_w:RWKV_WIDTH + (q + 1) * half_w],
                preferred_element_type=F32)
            prep["gate"][rows, hc] = jnp.dot(gd, gup_ref[:, hc], preferred_element_type=F32)
            kk_ss = _head_sum_parts(kk_sq, seg01)
            yield
            a = jax.nn.sigmoid(a_pre)
            lw = jax.nn.sigmoid(w_pre) * (-EXP_M05 * LOG2_E)
            lw_parts = _split_bf16(lw, TERMS_DECAY_CUMSUM)
            kk = kk * lax.rsqrt(jnp.maximum(kk_ss, 1e-24))
            kmod = k * ((1.0 - ka_ref[:, hc]) + a * ka_ref[:, hc])
            kka = kk * a
            rkk = _split_bf16(r * kmod * rk_ref[:, hc], TERMS_HEAD_SUM)
            yield
            cs = _cumsum_rows(ltri01, lw_parts)
            prep["bonus"][rows, hc] = _head_sum_parts(rkk, seg01) * v
            yield
            w_inv = jnp.exp2(-cs)
            w_last = [jnp.exp2(cs[(j + 1) * CHUNK - 1:(j + 1) * CHUNK, :])
                      for j in range(chunks_per_unit)]
            w_tail = jnp.concatenate(
                [w_last[j] * w_inv[j * CHUNK:(j + 1) * CHUNK] for j in range(chunks_per_unit)],
                axis=0)
            prep["rt"][rows, hc] = (r * jnp.exp2(cs)).astype(BF16)
            prep["at"][rows, hc] = -kk * jnp.exp2(cs - lw)
            prep["bh"][rows, hc] = (kka * w_inv).astype(BF16)
            prep["kh"][rows, hc] = (kmod * w_inv).astype(BF16)
            prep["bc"][rows, hc] = (kka * w_tail).astype(BF16)
            prep["kc"][rows, hc] = (kmod * w_tail).astype(BF16)
            for j in range(chunks_per_unit):
                cq, cr = divmod(c * chunks_per_unit + j, chunks_per_scan_tile)
                prep["wl"][cq, cr:cr + 1, hc] = w_last[j]
            yield


def _copy_pieces(z_ref, o_ref, *, tm):
    for b in range(tm // SGU_BLOCK):
        o_ref[b * SGU_BLOCK:(b + 1) * SGU_BLOCK, :] = (
            z_ref[Z_PAD + b * SGU_BLOCK:Z_PAD + (b + 1) * SGU_BLOCK, RWKV_IN:])
        yield


def _front_kernel(x_ref, g1_ref, win_ref, mu_ref, w0_ref, waup_ref, a0_ref, gup_ref, kk_ref, ka_ref,
                  rk_ref, *rest, tm):
    n = len(_PREP_NAMES)
    prep = dict(zip(_PREP_NAMES, rest[:n]))
    zs_ref = rest[n]
    wbf_ref, z0_ref, z1_ref = rest[n + 1:]
    i = pl.program_id(0)

    @pl.when(i == 0)
    def _():
        z1_ref[...] = jnp.zeros_like(z1_ref)
        wbf_ref[...] = win_ref[...].astype(BF16)

    def step(z_write, z_read):
        n_chunks = tm // PREP_ROWS
        prep_pieces = [
            _rwkv_prep_pieces(z_read, mu_ref, w0_ref, waup_ref, a0_ref, gup_ref, kk_ref, ka_ref,
                              rk_ref, prep, range(k, n_chunks, PREP_STREAMS))
            for k in range(PREP_STREAMS)]
        n_dot = 1 + IN_WIDTH // IN_PROJ_COLS + 1
        _run_interleaved(
            [(_in_proj_pieces(x_ref, g1_ref, wbf_ref, z_write), n_dot)]
            + [(gen, PREP_PIECES_PER_CHUNK * n_chunks // PREP_STREAMS + 1) for gen in prep_pieces]
            + [(_copy_pieces(z_read, zs_ref, tm=tm), tm // SGU_BLOCK + 1)],
            rounds=n_dot)
        if TB_SCAN // CHUNK < WL_ROWS:
            prep["wl"][:, TB_SCAN // CHUNK:, :] = jnp.zeros(
                (tm // TB_SCAN, WL_ROWS - TB_SCAN // CHUNK, RWKV_WIDTH), F32)
        z_write[Z_PAD - 1:Z_PAD, :] = z_read[Z_PAD + tm - 1:Z_PAD + tm, :]

    @pl.when((i & 1) == 0)
    def _():
        step(z0_ref, z1_ref)

    @pl.when((i & 1) == 1)
    def _():
        step(z1_ref, z0_ref)


def _front(x, g1, w_in, mu, w0, waup, a0, gup, k_k, k_a, r_k, tm):
    t = x.shape[0]
    n_tiles = t // tm
    vec = _full((1, RWKV_WIDTH))
    out_tile = lambda i: (jnp.maximum(i - 1, 0), 0)
    out_shapes = ([jax.ShapeDtypeStruct((t, RWKV_WIDTH), F32)] * len(_PREP_F32)
                  + [jax.ShapeDtypeStruct((t, RWKV_WIDTH), BF16)] * len(_PREP_BF16)
                  + [jax.ShapeDtypeStruct((t // TB_SCAN, WL_ROWS, RWKV_WIDTH), F32),
                     jax.ShapeDtypeStruct((t, 2 * SGU_WIDTH), F32)])
    out_specs = ([pl.BlockSpec((tm, RWKV_WIDTH), out_tile)] * (len(_PREP_F32) + len(_PREP_BF16))
                 + [pl.BlockSpec((tm // TB_SCAN, WL_ROWS, RWKV_WIDTH),
                                 lambda i: (jnp.maximum(i - 1, 0), 0, 0)),
                    pl.BlockSpec((tm, 2 * SGU_WIDTH), out_tile)])
    return pl.pallas_call(
        functools.partial(_front_kernel, tm=tm),
        out_shape=tuple(out_shapes),
        grid=(n_tiles + 1,),
        in_specs=[pl.BlockSpec((tm, D_MODEL), lambda i: (jnp.minimum(i, n_tiles - 1), 0)),
                  _full((1, D_MODEL)), _full((D_MODEL, IN_WIDTH)), _full((1, RWKV_IN)), vec,
                  _full((LORA_WA, 2 * RWKV_WIDTH)), vec, _full((GATE_LORA, RWKV_WIDTH)),
                  vec, vec, vec],
        out_specs=tuple(out_specs),
        scratch_shapes=[pltpu.VMEM((D_MODEL, IN_WIDTH), BF16),
                        pltpu.VMEM((Z_PAD + tm, IN_WIDTH), F32),
                        pltpu.VMEM((Z_PAD + tm, IN_WIDTH), F32)],
        compiler_params=_params(),
        name="front",
    )(x, g1, w_in, mu, w0, waup, a0, gup, k_k, k_a, r_k)


def _pair_masks():
    t = lax.broadcasted_iota(jnp.int32, (CHUNK, PAIR), 0)
    j = lax.broadcasted_iota(jnp.int32, (CHUNK, PAIR), 1) & (CHUNK - 1)
    strict = j < t
    incl = j <= t
    blk16 = (t >> 4) == (j >> 4)
    blk32 = (t >> 5) == (j >> 5)
    return strict, incl, blk16, blk32


def _bd(x, bd_mask):
    x = x.astype(BF16)
    return jnp.where(bd_mask, jnp.concatenate([x, x], axis=0), 0.0).astype(BF16)


def _staged(fn, items, parts=2):
    out = []
    n = len(items) // parts
    for k in range(parts):
        out += [fn(*item) for item in items[k * n:(k + 1) * n]]
        yield
    return out


def _unit_lower_inverse_minus_identity(a_list, masks, bd_mask):
    _, _, blk16, blk32 = masks
    ad = [jnp.where(blk16, a, 0.0) for a in a_list]
    ap = yield from _staged(lambda x: _mm(x, _bd(x, bd_mask)), [(x,) for x in ad])
    tp = ad
    for _ in range(2):
        both = yield from _staged(
            lambda p, t: _mm(p, jnp.concatenate([_bd(p, bd_mask), _bd(t, bd_mask)], axis=1)),
            list(zip(ap, tp)))
        tp = [t + p + b[:, PAIR:] for t, p, b in zip(tp, ap, both)]
        ap = [b[:, :PAIR] for b in both]
    last = yield from _staged(lambda p, t: _mm(p, _bd(t, bd_mask)), list(zip(ap, tp)))
    tp = [t + p + x for t, p, x in zip(tp, ap, last)]
    for off_mask in (blk32 & ~blk16, ~blk32):
        off = [jnp.where(off_mask, a, 0.0) for a in a_list]
        x = yield from _staged(lambda o, t: o + _mm(t, _bd(o, bd_mask)), list(zip(off, tp)))
        tp = yield from _staged(lambda t, xx: t + xx + _mm(xx, _bd(t, bd_mask)), list(zip(tp, x)))
    return tp


_TERM_STAGES = 11
_TERM_NAMES = ("achk", "uv", "bb")


def _bd_masks():
    bi = lax.broadcasted_iota(jnp.int32, (PAIR, PAIR), 0) >> 6
    bj = lax.broadcasted_iota(jnp.int32, (PAIR, PAIR), 1) >> 6
    bd1 = bi == bj
    return bd1, jnp.concatenate([bd1, bd1], axis=1)


def _scan_terms_pieces(rt_ref, at_ref, bh_ref, kh_ref, v_ref, terms, *, tb):
    masks = _pair_masks()
    strict, incl = masks[0], masks[1]
    bd1, bd2 = _bd_masks()
    probs = [(c, p) for c in range(tb // CHUNK) for p in range(N_PAIRS)]
    cut = lambda ref: [ref[c * CHUNK:(c + 1) * CHUNK, p * PAIR:(p + 1) * PAIR] for c, p in probs]
    rt_p, at_p, bh_p, kh_p, v_p = map(cut, (rt_ref, at_ref, bh_ref, kh_ref, v_ref))
    gram = yield from _staged(
        lambda a_, r_, b_, k_: _mm(jnp.concatenate([a_.astype(BF16), r_], axis=0),
                                   jnp.concatenate([_bd(b_, bd1), _bd(k_, bd1)], axis=0), _NT),
        list(zip(at_p, rt_p, bh_p, kh_p)))
    a_ab = [jnp.where(strict, g_[:CHUNK, :PAIR], 0.0) for g_ in gram]
    a_ak = [jnp.where(strict, g_[:CHUNK, PAIR:], 0.0) for g_ in gram]
    incl2 = jnp.concatenate([incl, incl], axis=1)
    for i, g_ in enumerate(gram):
        terms["bb"][i] = jnp.where(incl2, g_[CHUNK:], 0.0).astype(BF16)
    rhs = yield from _staged(
        lambda m_, x_, a_: jnp.concatenate([_mm(m_, _bd(x_, bd1)), a_], axis=1),
        list(zip(a_ak, v_p, at_p)))
    tp = yield from _unit_lower_inverse_minus_identity(a_ab, masks, bd1)
    sol = yield from _staged(lambda x_, t_: x_ + _mm(t_, _bd(x_, bd2)), list(zip(rhs, tp)))
    for i, x_ in enumerate(sol):
        terms["uv"][i] = x_[:, :PAIR]
        terms["achk"][i] = x_[:, PAIR:].astype(BF16)


def _scan_state_pieces(terms, rt_ref, v_ref, bc_ref, kc_ref, wl_ref, gate_ref, bonus_ref, lnw_ref,
                       lnb_ref, o_ref, s_ref, y_ref, *, tb):
    bd1, _ = _bd_masks()
    bd_upd = jnp.logical_and(bd1, pl.program_id(0) > 0)
    s = [s_ref[p] for p in range(N_PAIRS)]
    for c in range(tb // CHUNK):
        rows = slice(c * CHUNK, (c + 1) * CHUNK)
        lanes = [slice(p * PAIR, (p + 1) * PAIR) for p in range(N_PAIRS)]
        idx = [c * N_PAIRS + p for p in range(N_PAIRS)]
        on_s = [_mm(jnp.concatenate([terms["achk"][i], rt_ref[rows, lanes[p]].astype(BF16)], axis=0),
                    s[p], _NT) for p, i in enumerate(idx)]
        u = [x[:CHUNK] + terms["uv"][i] for x, i in zip(on_s, idx)]
        yield
        w_last = wl_ref[0, c:c + 1, :]
        upd = []
        for p, i in enumerate(idx):
            v_i = v_ref[rows, lanes[p]]
            y_ref[rows, lanes[p]] = (
                on_s[p][CHUNK:]
                + _mm(terms["bb"][i], jnp.concatenate([_bd(u[p], bd1), _bd(v_i, bd1)], axis=0)))
            upd.append(_mm(jnp.concatenate([u[p].astype(BF16), v_i], axis=0),
                           jnp.concatenate([bc_ref[rows, lanes[p]], kc_ref[rows, lanes[p]]], axis=0),
                           _TN))
        s = [s[p] * w_last[:, lanes[p]] + jnp.where(bd_upd, upd[p], 0.0) for p in range(N_PAIRS)]
        yield
    for p in range(N_PAIRS):
        s_ref[p] = s[p]
    seg01 = _seg01()
    y = y_ref[...]
    mean = _head_sum(y, seg01, TERMS_GROUP_MEAN) * (1.0 / RWKV_HEAD)
    yield
    d = y - mean
    var = _head_sum(d * d, seg01) * (1.0 / RWKV_HEAD)
    yield
    yn = d * lax.rsqrt(var + LNX_EPS) * lnw_ref[...] + lnb_ref[...]
    o_ref[...] = (yn + bonus_ref[...]) * gate_ref[...]


def _scan_kernel(rt_ref, at_ref, bh_ref, kh_ref, v_ref, rtp_ref, vp_ref, bcp_ref, kcp_ref, gate_ref,
                 bonus_ref, wl_ref, lnw_ref, lnb_ref, wg_ref, wu_ref, wd_ref, o_ref, wg_bf_ref,
                 wu_bf_ref, wd_bf_ref, s_ref, y_ref, *term_refs, tb):
    n = len(_TERM_NAMES)
    slots = [dict(zip(_TERM_NAMES, term_refs[k * n:(k + 1) * n])) for k in range(2)]
    i = pl.program_id(0)

    @pl.when(i == 0)
    def _():
        s_ref[...] = jnp.zeros_like(s_ref)
        for ref in slots[1].values():
            ref[...] = jnp.zeros_like(ref)

    wg_bf_ref[...] = wg_ref[...].astype(BF16)
    wu_bf_ref[...] = wu_ref[...].astype(BF16)
    wd_bf_ref[...] = wd_ref[...].astype(BF16)

    def step(write, read):
        n_terms = 2 * _TERM_STAGES + 1
        _run_interleaved(
            [(_scan_terms_pieces(rt_ref, at_ref, bh_ref, kh_ref, v_ref, write, tb=tb), n_terms),
             (_scan_state_pieces(read, rtp_ref, vp_ref, bcp_ref, kcp_ref, wl_ref, gate_ref,
                                 bonus_ref, lnw_ref, lnb_ref, o_ref, s_ref, y_ref, tb=tb),
              2 * (tb // CHUNK) + 3)],
            rounds=n_terms)

    @pl.when((i & 1) == 0)
    def _():
        step(slots[0], slots[1])

    @pl.when((i & 1) == 1)
    def _():
        step(slots[1], slots[0])


def _scan(prep, lnw, lnb, w_gate, w_up, w_down, tb):
    t = prep["rt"].shape[0]
    n_tiles = t // tb
    assert tb // CHUNK <= WL_ROWS
    last = n_tiles - 1
    cur = pl.BlockSpec((tb, RWKV_WIDTH), lambda i: (jnp.minimum(i, last), 0))
    prev = pl.BlockSpec((tb, RWKV_WIDTH), lambda i: (jnp.maximum(i - 1, 0), 0))
    vec = _full((1, RWKV_WIDTH))
    assert D_MODEL % n_tiles == 0 and (D_MODEL // n_tiles) % 16 == 0
    assert n_tiles % 2 == 0 and D_FF % (n_tiles // 2) == 0 and (D_FF // (n_tiles // 2)) % 16 == 0
    up_rows = pl.BlockSpec((D_MODEL // n_tiles, D_FF), lambda i: (jnp.minimum(i, last), 0))
    down_blk = pl.BlockSpec((D_FF // (n_tiles // 2), D_MODEL // 2),
                            lambda i: (jnp.minimum(i, last) // 2, jnp.minimum(i, last) % 2))
    n_prob = (tb // CHUNK) * N_PAIRS
    term_shapes = [pltpu.VMEM((n_prob, CHUNK, PAIR), BF16), pltpu.VMEM((n_prob, CHUNK, PAIR), F32),
                   pltpu.VMEM((n_prob, CHUNK, 2 * PAIR), BF16)]
    return pl.pallas_call(
        functools.partial(_scan_kernel, tb=tb),
        out_shape=(jax.ShapeDtypeStruct((t, RWKV_WIDTH), F32),
                   jax.ShapeDtypeStruct((D_MODEL, D_FF), BF16),
                   jax.ShapeDtypeStruct((D_MODEL, D_FF), BF16),
                   jax.ShapeDtypeStruct((D_FF, D_MODEL), BF16)),
        grid=(n_tiles + 1,),
        in_specs=[cur] * 5 + [prev] * 6
        + [pl.BlockSpec((1, WL_ROWS, RWKV_WIDTH), lambda i: (jnp.maximum(i - 1, 0), 0, 0)),
           vec, vec, up_rows, up_rows, down_blk],
        out_specs=(prev, up_rows, up_rows, down_blk),
        scratch_shapes=[pltpu.VMEM((N_PAIRS, PAIR, PAIR), F32), pltpu.VMEM((tb, RWKV_WIDTH), F32)]
        + term_shapes * 2,
        compiler_params=_params(),
        name="scan",
    )(prep["rt"], prep["at"], prep["bh"], prep["kh"], prep["v"],
      prep["rt"], prep["v"], prep["bc"], prep["kc"], prep["gate"], prep["bonus"], prep["wl"],
      lnw, lnb, w_gate, w_up, w_down)


def _sgu_block_prepare(z, lnw, lnb, sel):
    hz = _gelu_tanh(z)
    u = hz[:, :SGU_WIDTH]
    vf = hz[:, SGU_WIDTH:]
    mu = jnp.mean(vf, axis=-1, keepdims=True)
    d = vf - mu
    var = jnp.mean(d * d, axis=-1, keepdims=True)
    vn = d * lax.rsqrt(var + LN_EPS) * lnw + lnb
    stacks = []
    for p in range(SGU_WIDTH // PAIR):
        vb = vn[:, p * PAIR:(p + 1) * PAIR]
        stacks.append(jnp.where(sel, jnp.concatenate([vb, vb], axis=0), 0.0).astype(BF16))
    return u, stacks


def _sgu_block_mix(u, stacks, wcat, bias):
    return jnp.concatenate(
        [u[:, p * PAIR:(p + 1) * PAIR]
         * (jnp.dot(wcat[p], stacks[p], preferred_element_type=F32) + bias[:, p * PAIR:(p + 1) * PAIR])
         for p in range(SGU_WIDTH // PAIR)], axis=1)


def _mix_attn_group(r, x_ref, yr_ref, zs_ref, slnw_ref, slnb_ref, sbias_ref, wo1_ref, wo2_ref,
                    g2_ref, wq_ref, k_ref, v_ref, wo_ref, o_ref, wcat, sel):
    heads = [slice(hd * XA_HEAD_DIM, (hd + 1) * XA_HEAD_DIM) for hd in range(XA_HEADS)]
    prepared = [_sgu_block_prepare(zs_ref[b:b + SGU_BLOCK, :], slnw_ref[...], slnb_ref[...], sel)
                for b in range(r.start, r.stop, SGU_BLOCK)]
    yield
    x1 = x_ref[r, :] + jnp.dot(yr_ref[r, :].astype(BF16), wo1_ref[...],
                               preferred_element_type=F32)
    y_sgu = jnp.concatenate([_sgu_block_mix(u, st, wcat, sbias_ref[...]) for u, st in prepared],
                            axis=0)
    yield
    x1 = x1 + jnp.dot(y_sgu.astype(BF16), wo2_ref[...], preferred_element_type=F32)
    yield
    h = _rmsnorm(x1, g2_ref[...]).astype(BF16)
    yield
    q = jnp.dot(h, wq_ref[...], preferred_element_type=F32).astype(BF16)
    s = [lax.dot_general(q[:, hl], k_ref[:, hl], _NT, preferred_element_type=F32)
         * (XA_HEAD_DIM ** -0.5) for hl in heads]
    yield
    p = []
    for s_h in s:
        e = jnp.exp(s_h - jnp.max(s_h, axis=-1, keepdims=True))
        p.append((e / jnp.sum(e, axis=-1, keepdims=True)).astype(BF16))
    yield
    o = jnp.concatenate([jnp.dot(p_h, v_ref[:, hl], preferred_element_type=F32)
                         for p_h, hl in zip(p, heads)], axis=1).astype(BF16)
    o_ref[r, :] = x1 + jnp.dot(o, wo_ref[...], preferred_element_type=F32)


def _mix_attn_kernel(x_ref, yr_ref, zs_ref, slnw_ref, slnb_ref, ws_ref, sbias_ref, wout_f32_ref,
                     g2_ref, wq_f32_ref, k_ref, v_ref, wo_f32_ref, o_ref, wout_ref, wq_ref, wo_ref):
    @pl.when(pl.program_id(0) == 0)
    def _():
        wout_ref[...] = wout_f32_ref[...].astype(BF16)
        wq_ref[...] = wq_f32_ref[...].astype(BF16)
        wo_ref[...] = wo_f32_ref[...].astype(BF16)

    wo1_ref = wout_ref.at[:RWKV_WIDTH]
    wo2_ref = wout_ref.at[RWKV_WIDTH:]
    tm = x_ref.shape[0]
    ti = lax.broadcasted_iota(jnp.int32, (SGU_BLOCK, SGU_BLOCK), 0)
    tj = lax.broadcasted_iota(jnp.int32, (SGU_BLOCK, SGU_BLOCK), 1)
    tril = tj <= ti
    wcat = [jnp.concatenate([jnp.where(tril, ws_ref[2 * p], 0.0),
                             jnp.where(tril, ws_ref[2 * p + 1], 0.0)], axis=1).astype(BF16)
            for p in range(SGU_WIDTH // PAIR)]
    bi = lax.broadcasted_iota(jnp.int32, (2 * SGU_BLOCK, PAIR), 0) >> 7
    bj = lax.broadcasted_iota(jnp.int32, (2 * SGU_BLOCK, PAIR), 1) >> 6
    sel = bi == bj
    gens = [_mix_attn_group(slice(r, r + ATTN_ROW_GROUP), x_ref, yr_ref, zs_ref, slnw_ref, slnb_ref,
                            sbias_ref, wo1_ref, wo2_ref, g2_ref, wq_ref, k_ref, v_ref, wo_ref,
                            o_ref, wcat, sel)
            for r in range(0, tm, ATTN_ROW_GROUP)]
    _run_wavefront(gens, ATTN_STAGGER)


def _mix_attn(x, yr, zs, slnw, slnb, ws, sbias, w_out, g2, wq, k, v, wo, tm):
    t = x.shape[0]
    sq = _full((D_MODEL, D_MODEL))
    return pl.pallas_call(
        _mix_attn_kernel,
        out_shape=jax.ShapeDtypeStruct((t, D_MODEL), F32),
        grid=(t // tm,),
        in_specs=[pl.BlockSpec((tm, D_MODEL), lambda i: (i, 0)),
                  pl.BlockSpec((tm, RWKV_WIDTH), lambda i: (i, 0)),
                  pl.BlockSpec((tm, 2 * SGU_WIDTH), lambda i: (i, 0)),
                  _full((1, SGU_WIDTH)), _full((1, SGU_WIDTH)),
                  _full((SGU_GROUPS, SGU_BLOCK, SGU_BLOCK)), _full((SGU_BLOCK, SGU_WIDTH)),
                  sq, _full((1, D_MODEL)), sq,
                  _full((MEM_LEN, D_MODEL)), _full((MEM_LEN, D_MODEL)), sq],
        out_specs=pl.BlockSpec((tm, D_MODEL), lambda i: (i, 0)),
        scratch_shapes=[pltpu.VMEM((D_MODEL, D_MODEL), BF16)] * 3,
        compiler_params=_params(),
        name="mix_attn",
    )(x, yr, zs, slnw, slnb, ws, sbias, w_out, g2, wq, k, v, wo)


def _ffn_kernel(x_ref, g3_ref, wg_ref, wu_ref, wd_ref, gf_ref, o_ref):
    tm = x_ref.shape[0]
    groups = [slice(r, r + FFN_ROW_GROUP) for r in range(0, tm, FFN_ROW_GROUP)]
    x2 = [x_ref[r, :] for r in groups]
    h = [_rmsnorm(x, g3_ref[...]).astype(BF16) for x in x2]
    gate = [jnp.dot(h_, wg_ref[...], preferred_element_type=F32) for h_ in h]
    up = [jnp.dot(h_, wu_ref[...], preferred_element_type=F32) for h_ in h]
    act = [(jax.nn.silu(g_) * u_).astype(BF16) for g_, u_ in zip(gate, up)]
    x3 = [x + jnp.dot(a_, wd_ref[...], preferred_element_type=F32) for x, a_ in zip(x2, act)]
    for r, x in zip(groups, x3):
        o_ref[r, :] = _rmsnorm(x, gf_ref[...])


def _ffn(x, g3, wg, wu, wd, gf, tm):
    t = x.shape[0]
    return pl.pallas_call(
        _ffn_kernel,
        out_shape=jax.ShapeDtypeStruct((t, D_MODEL), F32),
        grid=(t // tm,),
        in_specs=[pl.BlockSpec((tm, D_MODEL), lambda i: (i, 0)), _full((1, D_MODEL)),
                  _full((D_MODEL, D_FF)), _full((D_MODEL, D_FF)), _full((D_FF, D_MODEL)),
                  _full((1, D_MODEL))],
        out_specs=pl.BlockSpec((tm, D_MODEL), lambda i: (i, 0)),
        compiler_params=_params("parallel"),
        name="ffn",
    )(x, g3, wg, wu, wd, gf)


def kernel(x, mem, norm1_g, w_in, shift_mu, w0, w_lora_up, a0, a_lora_up, g_lora_up, k_k, k_a, r_k,
           lnx_w, lnx_b, sgu_ln_w, sgu_ln_b, w_spatial, b_spatial, w_out, norm2_g, mem_norm_g,
           wq_x, wk_x, wv_x, wo_x, norm3_g, w_gate, w_up, w_down, norm_f_g):
    b, t, _ = x.shape
    depth = w_in.shape[0]
    assert depth == 1, "the final RMSNorm is fused into the (single) layer's ffn call"
    assert t % TM_ATTN == 0 and t % TM_FFN == 0
    assert t % TM_DENSE == 0 and TM_DENSE % TB_SCAN == 0 and TB_SCAN % CHUNK == 0
    row = lambda p: p.reshape(1, -1)
    bf = lambda p: p.astype(BF16)
    outs = []
    for bi in range(b):
        xb = x[bi]
        for l in range(depth):
            lora = w_lora_up.shape[1]
            zeros = jnp.zeros((lora, RWKV_WIDTH), F32)
            waup = jnp.concatenate(
                [jnp.concatenate([w_lora_up[l], zeros], axis=1),
                 jnp.concatenate([zeros, a_lora_up[l]], axis=1)], axis=0)
            bias = jnp.repeat(b_spatial[l].T, SGU_WIDTH // SGU_GROUPS, axis=1)

            front = _front(xb, row(norm1_g[l]), w_in[l], row(shift_mu[l]), row(w0[l]), bf(waup),
                           row(a0[l]), bf(g_lora_up[l]), row(k_k[l]), row(k_a[l]), row(r_k[l]),
                           TM_DENSE)
            prep = dict(zip(_PREP_NAMES, front[:len(_PREP_NAMES)]))
            z_sgu = front[len(_PREP_NAMES)]
            y_rwkv, wg_bf, wu_bf, wd_bf = _scan(prep, row(lnx_w[l]), row(lnx_b[l]),
                                                w_gate[l], w_up[l], w_down[l], TB_SCAN)
            k_mem, v_mem = _mem_kv(mem[bi], row(mem_norm_g[l]), wk_x[l], wv_x[l])
            x2 = _mix_attn(xb, y_rwkv, z_sgu, row(sgu_ln_w[l]), row(sgu_ln_b[l]), w_spatial[l], bias,
                           w_out[l], row(norm2_g[l]), wq_x[l], k_mem, v_mem, wo_x[l], TM_ATTN)
            xb = _ffn(x2, row(norm3_g[l]), wg_bf, wu_bf, wd_bf, row(norm_f_g), TM_FFN)
        outs.append(xb)
    return jnp.stack(outs, axis=0)
```

```python
import functools
import math

import jax
import jax.numpy as jnp
from jax import lax
from jax.experimental import pallas as pl
from jax.experimental.pallas import tpu as pltpu

F32 = jnp.float32
BF16 = jnp.bfloat16

D_MODEL = 1024
RWKV_WIDTH = 512
RWKV_HEAD = 64
LORA_WA = 128
GATE_LORA = 128
RWKV_IN = 3 * RWKV_WIDTH + LORA_WA + GATE_LORA
SGU_WIDTH = 512
SGU_GROUPS = 8
SGU_BLOCK = 128
IN_WIDTH = RWKV_IN + 2 * SGU_WIDTH
MEM_LEN = 256
XA_HEADS = 4
XA_HEAD_DIM = D_MODEL // XA_HEADS
D_FF = 2816
RMS_EPS = 1e-6
LN_EPS = 1e-5
LNX_EPS = 64e-5
EXP_M05 = 0.6065306597126334
LOG2_E = 1.4426950408889634

CHUNK = 64
PAIR = 2 * RWKV_HEAD
N_PAIRS = RWKV_WIDTH // PAIR
HEAD_BLOCK = 2 * PAIR
TM_DENSE = 512
TM_ATTN = 1024
TM_FFN = 1024
TB_SCAN = 512
Z_PAD = 8
WL_ROWS = 8
ATTN_ROW_GROUP = 256
ATTN_STAGGER = 2
FFN_ROW_GROUP = 256
IN_PROJ_COLS = 256
PREP_ROWS = 128
PREP_STREAMS = 4
PREP_PIECES_PER_CHUNK = 10
TERMS_DECAY_CUMSUM = 2
TERMS_HEAD_SUM = 1
TERMS_GROUP_MEAN = 2
VMEM_LIMIT = 56 * 1024 * 1024

_NN = (((1,), (0,)), ((), ()))
_NT = (((1,), (1,)), ((), ()))
_TN = (((0,), (0,)), ((), ()))


def _mm(a, b, dims=_NN):
    return lax.dot_general(a.astype(BF16), b.astype(BF16), dims, preferred_element_type=F32)


def _split_bf16(x, terms):
    parts = []
    rem = x
    for _ in range(terms):
        part = rem.astype(BF16)
        rem = rem - part.astype(F32)
        parts.append(part)
    return parts


def _cumsum_rows(ltri01, parts):
    return lax.dot_general(jnp.concatenate([ltri01] * len(parts), axis=1),
                           jnp.concatenate(parts, axis=0), _NN, preferred_element_type=F32)


def _head_sum_parts(parts, seg01):
    cols = []
    for q in range(parts[0].shape[1] // HEAD_BLOCK):
        acc = None
        for part in parts:
            d = lax.dot_general(part[:, HEAD_BLOCK * q:HEAD_BLOCK * (q + 1)], seg01, _NN,
                                preferred_element_type=F32)
            acc = d if acc is None else acc + d
        cols.append(acc)
    return jnp.concatenate(cols, axis=1)


def _head_sum(x, seg01, terms=TERMS_HEAD_SUM):
    return _head_sum_parts(_split_bf16(x, terms), seg01)


def _seg01():
    li = lax.broadcasted_iota(jnp.int32, (HEAD_BLOCK, HEAD_BLOCK), 0) >> 6
    lj = lax.broadcasted_iota(jnp.int32, (HEAD_BLOCK, HEAD_BLOCK), 1) >> 6
    return (li == lj).astype(BF16)


def _gelu_tanh(x):
    k1 = -2.0 * math.sqrt(2.0 / math.pi) * math.log2(math.e)
    return x / (1.0 + jnp.exp2(x * (k1 + (k1 * 0.044715) * (x * x))))


def _rmsnorm(x, g):
    return x * lax.rsqrt(jnp.mean(x * x, axis=-1, keepdims=True) + RMS_EPS) * g


def _full(shape):
    n = len(shape)
    return pl.BlockSpec(shape, lambda i: (0,) * n, pipeline_mode=pl.Buffered(1))


def _params(sem="arbitrary"):
    return pltpu.CompilerParams(dimension_semantics=(sem,), vmem_limit_bytes=VMEM_LIMIT)


def _run_interleaved(stages, rounds):
    for r in range(rounds):
        for gen, n in stages:
            for _ in range((r + 1) * n // rounds - r * n // rounds):
                next(gen, None)
    for gen, _ in stages:
        assert next(gen, StopIteration) is StopIteration, "piece count too small"


def _run_wavefront(gens, stagger):
    live = list(enumerate(gens))
    r = 0
    while live:
        for entry in list(live):
            g, gen = entry
            if r >= g * stagger and next(gen, StopIteration) is StopIteration:
                live.remove(entry)
        r += 1


def _mem_kv_kernel(mem_ref, g_ref, wk_ref, wv_ref, k_ref, v_ref):
    m = _rmsnorm(mem_ref[...], g_ref[...]).astype(BF16)
    k_ref[...] = jnp.dot(m, wk_ref[...].astype(BF16), preferred_element_type=F32).astype(BF16)
    v_ref[...] = jnp.dot(m, wv_ref[...].astype(BF16), preferred_element_type=F32).astype(BF16)


def _mem_kv(mem, g, wk, wv):
    return pl.pallas_call(
        _mem_kv_kernel,
        out_shape=(jax.ShapeDtypeStruct((MEM_LEN, D_MODEL), BF16),) * 2,
        grid=(1,),
        in_specs=[_full((MEM_LEN, D_MODEL)), _full((1, D_MODEL)),
                  _full((D_MODEL, D_MODEL)), _full((D_MODEL, D_MODEL))],
        out_specs=(_full((MEM_LEN, D_MODEL)),) * 2,
        compiler_params=_params(),
        name="mem_kv",
    )(mem, g, wk, wv)


_PREP_F32 = ("at", "gate", "bonus")
_PREP_BF16 = ("rt", "bh", "kh", "bc", "kc", "v")
_PREP_NAMES = _PREP_F32 + _PREP_BF16 + ("wl",)


def _in_proj_pieces(x_ref, g_ref, w_ref, z_ref):
    h = _rmsnorm(x_ref[...], g_ref[...]).astype(BF16)
    yield
    for j in range(IN_WIDTH // IN_PROJ_COLS):
        cols = slice(j * IN_PROJ_COLS, (j + 1) * IN_PROJ_COLS)
        z_ref[Z_PAD:, cols] = jnp.dot(h, w_ref[:, cols], preferred_element_type=F32)
        yield


def _rwkv_prep_pieces(z_ref, mu_ref, w0_ref, waup_ref, a0_ref, gup_ref, kk_ref, ka_ref, rk_ref,
                      prep, chunks):
    seg01 = _seg01()
    lane = lax.broadcasted_iota(jnp.int32, (1, LORA_WA), 1)
    row = lax.broadcasted_iota(jnp.int32, (PREP_ROWS, 1), 0)
    ti = lax.broadcasted_iota(jnp.int32, (PREP_ROWS, PREP_ROWS), 0)
    tj = lax.broadcasted_iota(jnp.int32, (PREP_ROWS, PREP_ROWS), 1)
    ltri01 = ((tj <= ti) & ((ti >> 6) == (tj >> 6))).astype(BF16)
    chunks_per_scan_tile = TB_SCAN // CHUNK
    chunks_per_unit = PREP_ROWS // CHUNK
    half_w = RWKV_WIDTH // 2

    def shifted(c, cols):
        z = z_ref[Z_PAD + c * PREP_ROWS:Z_PAD + (c + 1) * PREP_ROWS, cols]
        before = z_ref[Z_PAD + c * PREP_ROWS - 1:Z_PAD + c * PREP_ROWS, cols]
        zprev = jnp.where(row == 0, before, pltpu.roll(z, 1, axis=0))
        return z + (zprev - z) * mu_ref[:, cols]

    for c in chunks:
        rows = slice(c * PREP_ROWS, (c + 1) * PREP_ROWS)
        wa_in = shifted(c, slice(3 * RWKV_WIDTH, 3 * RWKV_WIDTH + LORA_WA))
        wa_in = jnp.where(lane < LORA_WA // 2, jnp.tanh(wa_in), wa_in).astype(BF16)
        gd = jax.nn.sigmoid(shifted(c, slice(3 * RWKV_WIDTH + LORA_WA, RWKV_IN))).astype(BF16)
        for q in range(2):
            hc = slice(q * half_w, (q + 1) * half_w)
            r = shifted(c, hc)
            k = shifted(c, slice(RWKV_WIDTH + q * half_w, RWKV_WIDTH + (q + 1) * half_w))
            v = shifted(c, slice(2 * RWKV_WIDTH + q * half_w, 2 * RWKV_WIDTH + (q + 1) * half_w))
            prep["v"][rows, hc] = v.astype(BF16)
            kk = k * kk_ref[:, hc]
            kk_sq = _split_bf16(kk * kk, TERMS_HEAD_SUM)
            yield
            w_pre = w0_ref[:, hc] + jnp.dot(wa_in, waup_ref[:, hc], preferred_element_type=F32)
            a_pre = a0_ref[:, hc] + jnp.dot(
                wa_in, waup_ref[:, RWKV_WIDTH + q * half_w:RWKV_WIDTH + (q + 1) * half_w],
                preferred_element_type=F32)
            prep["gate"][rows, hc] = jnp.dot(gd, gup_ref[:, hc], preferred_element_type=F32)
            kk_ss = _head_sum_parts(kk_sq, seg01)
            yield
            a = jax.nn.sigmoid(a_pre)
            lw = jax.nn.sigmoid(w_pre) * (-EXP_M05 * LOG2_E)
            lw_parts = _split_bf16(lw, TERMS_DECAY_CUMSUM)
            kk = kk * lax.rsqrt(jnp.maximum(kk_ss, 1e-24))
            kmod = k * ((1.0 - ka_ref[:, hc]) + a * ka_ref[:, hc])
            kka = kk * a
            rkk = _split_bf16(r * kmod * rk_ref[:, hc], TERMS_HEAD_SUM)
            yield
            cs = _cumsum_rows(ltri01, lw_parts)
            prep["bonus"][rows, hc] = _head_sum_parts(rkk, seg01) * v
            yield
            w_inv = jnp.exp2(-cs)
            w_last = [jnp.exp2(cs[(j + 1) * CHUNK - 1:(j + 1) * CHUNK, :])
                      for j in range(chunks_per_unit)]
            w_tail = jnp.concatenate(
                [w_last[j] * w_inv[j * CHUNK:(j + 1) * CHUNK] for j in range(chunks_per_unit)],
                axis=0)
            prep["rt"][rows, hc] = (r * jnp.exp2(cs)).astype(BF16)
            prep["at"][rows, hc] = -kk * jnp.exp2(cs - lw)
            prep["bh"][rows, hc] = (kka * w_inv).astype(BF16)
            prep["kh"][rows, hc] = (kmod * w_inv).astype(BF16)
            prep["bc"][rows, hc] = (kka * w_tail).astype(BF16)
            prep["kc"][rows, hc] = (kmod * w_tail).astype(BF16)
            for j in range(chunks_per_unit):
                cq, cr = divmod(c * chunks_per_unit + j, chunks_per_scan_tile)
                prep["wl"][cq, cr:cr + 1, hc] = w_last[j]
            yield


def _copy_pieces(z_ref, o_ref, *, tm):
    for b in range(tm // SGU_BLOCK):
        o_ref[b * SGU_BLOCK:(b + 1) * SGU_BLOCK, :] = (
            z_ref[Z_PAD + b * SGU_BLOCK:Z_PAD + (b + 1) * SGU_BLOCK, RWKV_IN:])
        yield


def _front_kernel(x_ref, g1_ref, win_ref, mu_ref, w0_ref, waup_ref, a0_ref, gup_ref, kk_ref, ka_ref,
                  rk_ref, *rest, tm):
    n = len(_PREP_NAMES)
    prep = dict(zip(_PREP_NAMES, rest[:n]))
    zs_ref = rest[n]
    wbf_ref, z0_ref, z1_ref = rest[n + 1:]
    i = pl.program_id(0)

    @pl.when(i == 0)
    def _():
        z1_ref[...] = jnp.zeros_like(z1_ref)
        wbf_ref[...] = win_ref[...].astype(BF16)

    def step(z_write, z_read):
        n_chunks = tm // PREP_ROWS
        prep_pieces = [
            _rwkv_prep_pieces(z_read, mu_ref, w0_ref, waup_ref, a0_ref, gup_ref, kk_ref, ka_ref,
                              rk_ref, prep, range(k, n_chunks, PREP_STREAMS))
            for k in range(PREP_STREAMS)]
        n_dot = 1 + IN_WIDTH // IN_PROJ_COLS + 1
        _run_interleaved(
            [(_in_proj_pieces(x_ref, g1_ref, wbf_ref, z_write), n_dot)]
            + [(gen, PREP_PIECES_PER_CHUNK * n_chunks // PREP_STREAMS + 1) for gen in prep_pieces]
            + [(_copy_pieces(z_read, zs_ref, tm=tm), tm // SGU_BLOCK + 1)],
            rounds=n_dot)
        if TB_SCAN // CHUNK < WL_ROWS:
            prep["wl"][:, TB_SCAN // CHUNK:, :] = jnp.zeros(
                (tm // TB_SCAN, WL_ROWS - TB_SCAN // CHUNK, RWKV_WIDTH), F32)
        z_write[Z_PAD - 1:Z_PAD, :] = z_read[Z_PAD + tm - 1:Z_PAD + tm, :]

    @pl.when((i & 1) == 0)
    def _():
        step(z0_ref, z1_ref)

    @pl.when((i & 1) == 1)
    def _():
        step(z1_ref, z0_ref)


def _front(x, g1, w_in, mu, w0, waup, a0, gup, k_k, k_a, r_k, tm):
    t = x.shape[0]
    n_tiles = t // tm
    vec = _full((1, RWKV_WIDTH))
    out_tile = lambda i: (jnp.maximum(i - 1, 0), 0)
    out_shapes = ([jax.ShapeDtypeStruct((t, RWKV_WIDTH), F32)] * len(_PREP_F32)
                  + [jax.ShapeDtypeStruct((t, RWKV_WIDTH), BF16)] * len(_PREP_BF16)
                  + [jax.ShapeDtypeStruct((t // TB_SCAN, WL_ROWS, RWKV_WIDTH), F32),
                     jax.ShapeDtypeStruct((t, 2 * SGU_WIDTH), F32)])
    out_specs = ([pl.BlockSpec((tm, RWKV_WIDTH), out_tile)] * (len(_PREP_F32) + len(_PREP_BF16))
                 + [pl.BlockSpec((tm // TB_SCAN, WL_ROWS, RWKV_WIDTH),
                                 lambda i: (jnp.maximum(i - 1, 0), 0, 0)),
                    pl.BlockSpec((tm, 2 * SGU_WIDTH), out_tile)])
    return pl.pallas_call(
        functools.partial(_front_kernel, tm=tm),
        out_shape=tuple(out_shapes),
        grid=(n_tiles + 1,),
        in_specs=[pl.BlockSpec((tm, D_MODEL), lambda i: (jnp.minimum(i, n_tiles - 1), 0)),
                  _full((1, D_MODEL)), _full((D_MODEL, IN_WIDTH)), _full((1, RWKV_IN)), vec,
                  _full((LORA_WA, 2 * RWKV_WIDTH)), vec, _full((GATE_LORA, RWKV_WIDTH)),
                  vec, vec, vec],
        out_specs=tuple(out_specs),
        scratch_shapes=[pltpu.VMEM((D_MODEL, IN_WIDTH), BF16),
                        pltpu.VMEM((Z_PAD + tm, IN_WIDTH), F32),
                        pltpu.VMEM((Z_PAD + tm, IN_WIDTH), F32)],
        compiler_params=_params(),
        name="front",
    )(x, g1, w_in, mu, w0, waup, a0, gup, k_k, k_a, r_k)


def _pair_masks():
    t = lax.broadcasted_iota(jnp.int32, (CHUNK, PAIR), 0)
    j = lax.broadcasted_iota(jnp.int32, (CHUNK, PAIR), 1) & (CHUNK - 1)
    strict = j < t
    incl = j <= t
    blk16 = (t >> 4) == (j >> 4)
    blk32 = (t >> 5) == (j >> 5)
    return strict, incl, blk16, blk32


def _bd(x, bd_mask):
    x = x.astype(BF16)
    return jnp.where(bd_mask, jnp.concatenate([x, x], axis=0), 0.0).astype(BF16)


def _staged(fn, items, parts=2):
    out = []
    n = len(items) // parts
    for k in range(parts):
        out += [fn(*item) for item in items[k * n:(k + 1) * n]]
        yield
    return out


def _unit_lower_inverse_minus_identity(a_list, masks, bd_mask):
    _, _, blk16, blk32 = masks
    ad = [jnp.where(blk16, a, 0.0) for a in a_list]
    ap = yield from _staged(lambda x: _mm(x, _bd(x, bd_mask)), [(x,) for x in ad])
    tp = ad
    for _ in range(2):
        both = yield from _staged(
            lambda p, t: _mm(p, jnp.concatenate([_bd(p, bd_mask), _bd(t, bd_mask)], axis=1)),
            list(zip(ap, tp)))
        tp = [t + p + b[:, PAIR:] for t, p, b in zip(tp, ap, both)]
        ap = [b[:, :PAIR] for b in both]
    last = yield from _staged(lambda p, t: _mm(p, _bd(t, bd_mask)), list(zip(ap, tp)))
    tp = [t + p + x for t, p, x in zip(tp, ap, last)]
    for off_mask in (blk32 & ~blk16, ~blk32):
        off = [jnp.where(off_mask, a, 0.0) for a in a_list]
        x = yield from _staged(lambda o, t: o + _mm(t, _bd(o, bd_mask)), list(zip(off, tp)))
        tp = yield from _staged(lambda t, xx: t + xx + _mm(xx, _bd(t, bd_mask)), list(zip(tp, x)))
    return tp


_TERM_STAGES = 11
_TERM_NAMES = ("achk", "uv", "bb")


def _bd_masks():
    bi = lax.broadcasted_iota(jnp.int32, (PAIR, PAIR), 0) >> 6
    bj = lax.broadcasted_iota(jnp.int32, (PAIR, PAIR), 1) >> 6
    bd1 = bi == bj
    return bd1, jnp.concatenate([bd1, bd1], axis=1)


def _scan_terms_pieces(rt_ref, at_ref, bh_ref, kh_ref, v_ref, terms, *, tb):
    masks = _pair_masks()
    strict, incl = masks[0], masks[1]
    bd1, bd2 = _bd_masks()
    probs = [(c, p) for c in range(tb // CHUNK) for p in range(N_PAIRS)]
    cut = lambda ref: [ref[c * CHUNK:(c + 1) * CHUNK, p * PAIR:(p + 1) * PAIR] for c, p in probs]
    rt_p, at_p, bh_p, kh_p, v_p = map(cut, (rt_ref, at_ref, bh_ref, kh_ref, v_ref))
    gram = yield from _staged(
        lambda a_, r_, b_, k_: _mm(jnp.concatenate([a_.astype(BF16), r_], axis=0),
                                   jnp.concatenate([_bd(b_, bd1), _bd(k_, bd1)], axis=0), _NT),
        list(zip(at_p, rt_p, bh_p, kh_p)))
    a_ab = [jnp.where(strict, g_[:CHUNK, :PAIR], 0.0) for g_ in gram]
    a_ak = [jnp.where(strict, g_[:CHUNK, PAIR:], 0.0) for g_ in gram]
    incl2 = jnp.concatenate([incl, incl], axis=1)
    for i, g_ in enumerate(gram):
        terms["bb"][i] = jnp.where(incl2, g_[CHUNK:], 0.0).astype(BF16)
    rhs = yield from _staged(
        lambda m_, x_, a_: jnp.concatenate([_mm(m_, _bd(x_, bd1)), a_], axis=1),
        list(zip(a_ak, v_p, at_p)))
    tp = yield from _unit_lower_inverse_minus_identity(a_ab, masks, bd1)
    sol = yield from _staged(lambda x_, t_: x_ + _mm(t_, _bd(x_, bd2)), list(zip(rhs, tp)))
    for i, x_ in enumerate(sol):
        terms["uv"][i] = x_[:, :PAIR]
        terms["achk"][i] = x_[:, PAIR:].astype(BF16)


def _scan_state_pieces(terms, rt_ref, v_ref, bc_ref, kc_ref, wl_ref, gate_ref, bonus_ref, lnw_ref,
                       lnb_ref, o_ref, s_ref, y_ref, *, tb):
    bd1, _ = _bd_masks()
    bd_upd = jnp.logical_and(bd1, pl.program_id(0) > 0)
    s = [s_ref[p] for p in range(N_PAIRS)]
    for c in range(tb // CHUNK):
        rows = slice(c * CHUNK, (c + 1) * CHUNK)
        lanes = [slice(p * PAIR, (p + 1) * PAIR) for p in range(N_PAIRS)]
        idx = [c * N_PAIRS + p for p in range(N_PAIRS)]
        on_s = [_mm(jnp.concatenate([terms["achk"][i], rt_ref[rows, lanes[p]].astype(BF16)], axis=0),
                    s[p], _NT) for p, i in enumerate(idx)]
        u = [x[:CHUNK] + terms["uv"][i] for x, i in zip(on_s, idx)]
        yield
        w_last = wl_ref[0, c:c + 1, :]
        upd = []
        for p, i in enumerate(idx):
            v_i = v_ref[rows, lanes[p]]
            y_ref[rows, lanes[p]] = (
                on_s[p][CHUNK:]
                + _mm(terms["bb"][i], jnp.concatenate([_bd(u[p], bd1), _bd(v_i, bd1)], axis=0)))
            upd.append(_mm(jnp.concatenate([u[p].astype(BF16), v_i], axis=0),
                           jnp.concatenate([bc_ref[rows, lanes[p]], kc_ref[rows, lanes[p]]], axis=0),
                           _TN))
        s = [s[p] * w_last[:, lanes[p]] + jnp.where(bd_upd, upd[p], 0.0) for p in range(N_PAIRS)]
        yield
    for p in range(N_PAIRS):
        s_ref[p] = s[p]
    seg01 = _seg01()
    y = y_ref[...]
    mean = _head_sum(y, seg01, TERMS_GROUP_MEAN) * (1.0 / RWKV_HEAD)
    yield
    d = y - mean
    var = _head_sum(d * d, seg01) * (1.0 / RWKV_HEAD)
    yield
    yn = d * lax.rsqrt(var + LNX_EPS) * lnw_ref[...] + lnb_ref[...]
    o_ref[...] = ((yn + bonus_ref[...]) * gate_ref[...]).astype(BF16)


def _scan_kernel(rt_ref, at_ref, bh_ref, kh_ref, v_ref, rtp_ref, vp_ref, bcp_ref, kcp_ref, gate_ref,
                 bonus_ref, wl_ref, lnw_ref, lnb_ref, wg_ref, wu_ref, wd_ref, o_ref, wg_bf_ref,
                 wu_bf_ref, wd_bf_ref, s_ref, y_ref, *term_refs, tb):
    n = len(_TERM_NAMES)
    slots = [dict(zip(_TERM_NAMES, term_refs[k * n:(k + 1) * n])) for k in range(2)]
    i = pl.program_id(0)

    @pl.when(i == 0)
    def _():
        s_ref[...] = jnp.zeros_like(s_ref)
        for ref in slots[1].values():
            ref[...] = jnp.zeros_like(ref)

    wg_bf_ref[...] = wg_ref[...].astype(BF16)
    wu_bf_ref[...] = wu_ref[...].astype(BF16)
    wd_bf_ref[...] = wd_ref[...].astype(BF16)

    def step(write, read):
        n_terms = 2 * _TERM_STAGES + 1
        _run_interleaved(
            [(_scan_terms_pieces(rt_ref, at_ref, bh_ref, kh_ref, v_ref, write, tb=tb), n_terms),
             (_scan_state_pieces(read, rtp_ref, vp_ref, bcp_ref, kcp_ref, wl_ref, gate_ref,
                                 bonus_ref, lnw_ref, lnb_ref, o_ref, s_ref, y_ref, tb=tb),
              2 * (tb // CHUNK) + 3)],
            rounds=n_terms)

    @pl.when((i & 1) == 0)
    def _():
        step(slots[0], slots[1])

    @pl.when((i & 1) == 1)
    def _():
        step(slots[1], slots[0])


def _scan(prep, lnw, lnb, w_gate, w_up, w_down, tb):
    t = prep["rt"].shape[0]
    n_tiles = t // tb
    assert tb // CHUNK <= WL_ROWS
    last = n_tiles - 1
    cur = pl.BlockSpec((tb, RWKV_WIDTH), lambda i: (jnp.minimum(i, last), 0))
    prev = pl.BlockSpec((tb, RWKV_WIDTH), lambda i: (jnp.maximum(i - 1, 0), 0))
    vec = _full((1, RWKV_WIDTH))
    assert D_MODEL % n_tiles == 0 and (D_MODEL // n_tiles) % 16 == 0
    assert n_tiles % 2 == 0 and D_FF % (n_tiles // 2) == 0 and (D_FF // (n_tiles // 2)) % 16 == 0
    up_rows = pl.BlockSpec((D_MODEL // n_tiles, D_FF), lambda i: (jnp.minimum(i, last), 0))
    down_blk = pl.BlockSpec((D_FF // (n_tiles // 2), D_MODEL // 2),
                            lambda i: (jnp.minimum(i, last) // 2, jnp.minimum(i, last) % 2))
    n_prob = (tb // CHUNK) * N_PAIRS
    term_shapes = [pltpu.VMEM((n_prob, CHUNK, PAIR), BF16), pltpu.VMEM((n_prob, CHUNK, PAIR), F32),
                   pltpu.VMEM((n_prob, CHUNK, 2 * PAIR), BF16)]
    return pl.pallas_call(
        functools.partial(_scan_kernel, tb=tb),
        out_shape=(jax.ShapeDtypeStruct((t, RWKV_WIDTH), BF16),
                   jax.ShapeDtypeStruct((D_MODEL, D_FF), BF16),
                   jax.ShapeDtypeStruct((D_MODEL, D_FF), BF16),
                   jax.ShapeDtypeStruct((D_FF, D_MODEL), BF16)),
        grid=(n_tiles + 1,),
        in_specs=[cur] * 5 + [prev] * 6
        + [pl.BlockSpec((1, WL_ROWS, RWKV_WIDTH), lambda i: (jnp.maximum(i - 1, 0), 0, 0)),
           vec, vec, up_rows, up_rows, down_blk],
        out_specs=(prev, up_rows, up_rows, down_blk),
        scratch_shapes=[pltpu.VMEM((N_PAIRS, PAIR, PAIR), F32), pltpu.VMEM((tb, RWKV_WIDTH), F32)]
        + term_shapes * 2,
        compiler_params=_params(),
        name="scan",
    )(prep["rt"], prep["at"], prep["bh"], prep["kh"], prep["v"],
      prep["rt"], prep["v"], prep["bc"], prep["kc"], prep["gate"], prep["bonus"], prep["wl"],
      lnw, lnb, w_gate, w_up, w_down)


def _sgu_block_prepare(z, lnw, lnb, sel):
    hz = _gelu_tanh(z)
    u = hz[:, :SGU_WIDTH]
    vf = hz[:, SGU_WIDTH:]
    mu = jnp.mean(vf, axis=-1, keepdims=True)
    d = vf - mu
    var = jnp.mean(d * d, axis=-1, keepdims=True)
    vn = d * lax.rsqrt(var + LN_EPS) * lnw + lnb
    stacks = []
    for p in range(SGU_WIDTH // PAIR):
        vb = vn[:, p * PAIR:(p + 1) * PAIR]
        stacks.append(jnp.where(sel, jnp.concatenate([vb, vb], axis=0), 0.0).astype(BF16))
    return u, stacks


def _sgu_block_mix(u, stacks, wcat, bias):
    return jnp.concatenate(
        [u[:, p * PAIR:(p + 1) * PAIR]
         * (jnp.dot(wcat[p], stacks[p], preferred_element_type=F32) + bias[:, p * PAIR:(p + 1) * PAIR])
         for p in range(SGU_WIDTH // PAIR)], axis=1)


def _mix_attn_group(r, x_ref, yr_ref, zs_ref, slnw_ref, slnb_ref, sbias_ref, wo1_ref, wo2_ref,
                    g2_ref, wq_ref, k_ref, v_ref, wo_ref, o_ref, wcat, sel):
    heads = [slice(hd * XA_HEAD_DIM, (hd + 1) * XA_HEAD_DIM) for hd in range(XA_HEADS)]
    prepared = [_sgu_block_prepare(zs_ref[b:b + SGU_BLOCK, :], slnw_ref[...], slnb_ref[...], sel)
                for b in range(r.start, r.stop, SGU_BLOCK)]
    yield
    x1 = x_ref[r, :] + jnp.dot(yr_ref[r, :].astype(BF16), wo1_ref[...],
                               preferred_element_type=F32)
    y_sgu = jnp.concatenate([_sgu_block_mix(u, st, wcat, sbias_ref[...]) for u, st in prepared],
                            axis=0)
    yield
    x1 = x1 + jnp.dot(y_sgu.astype(BF16), wo2_ref[...], preferred_element_type=F32)
    yield
    h = _rmsnorm(x1, g2_ref[...]).astype(BF16)
    yield
    q = jnp.dot(h, wq_ref[...], preferred_element_type=F32).astype(BF16)
    s = [lax.dot_general(q[:, hl], k_ref[:, hl], _NT, preferred_element_type=F32)
         * (XA_HEAD_DIM ** -0.5) for hl in heads]
    yield
    p = []
    for s_h in s:
        e = jnp.exp(s_h - jnp.max(s_h, axis=-1, keepdims=True))
        p.append((e / jnp.sum(e, axis=-1, keepdims=True)).astype(BF16))
    yield
    o = jnp.concatenate([jnp.dot(p_h, v_ref[:, hl], preferred_element_type=F32)
                         for p_h, hl in zip(p, heads)], axis=1).astype(BF16)
    o_ref[r, :] = x1 + jnp.dot(o, wo_ref[...], preferred_element_type=F32)


def _mix_attn_kernel(x_ref, yr_ref, zs_ref, slnw_ref, slnb_ref, ws_ref, sbias_ref, wout_f32_ref,
                     g2_ref, wq_f32_ref, k_ref, v_ref, wo_f32_ref, o_ref, wout_ref, wq_ref, wo_ref):
    @pl.when(pl.program_id(0) == 0)
    def _():
        wout_ref[...] = wout_f32_ref[...].astype(BF16)
        wq_ref[...] = wq_f32_ref[...].astype(BF16)
        wo_ref[...] = wo_f32_ref[...].astype(BF16)

    wo1_ref = wout_ref.at[:RWKV_WIDTH]
    wo2_ref = wout_ref.at[RWKV_WIDTH:]
    tm = x_ref.shape[0]
    ti = lax.broadcasted_iota(jnp.int32, (SGU_BLOCK, SGU_BLOCK), 0)
    tj = lax.broadcasted_iota(jnp.int32, (SGU_BLOCK, SGU_BLOCK), 1)
    tril = tj <= ti
    wcat = [jnp.concatenate([jnp.where(tril, ws_ref[2 * p], 0.0),
                             jnp.where(tril, ws_ref[2 * p + 1], 0.0)], axis=1).astype(BF16)
            for p in range(SGU_WIDTH // PAIR)]
    bi = lax.broadcasted_iota(jnp.int32, (2 * SGU_BLOCK, PAIR), 0) >> 7
    bj = lax.broadcasted_iota(jnp.int32, (2 * SGU_BLOCK, PAIR), 1) >> 6
    sel = bi == bj
    gens = [_mix_attn_group(slice(r, r + ATTN_ROW_GROUP), x_ref, yr_ref, zs_ref, slnw_ref, slnb_ref,
                            sbias_ref, wo1_ref, wo2_ref, g2_ref, wq_ref, k_ref, v_ref, wo_ref,
                            o_ref, wcat, sel)
            for r in range(0, tm, ATTN_ROW_GROUP)]
    _run_wavefront(gens, ATTN_STAGGER)


def _mix_attn(x, yr, zs, slnw, slnb, ws, sbias, w_out, g2, wq, k, v, wo, tm):
    t = x.shape[0]
    sq = _full((D_MODEL, D_MODEL))
    return pl.pallas_call(
        _mix_attn_kernel,
        out_shape=jax.ShapeDtypeStruct((t, D_MODEL), F32),
        grid=(t // tm,),
        in_specs=[pl.BlockSpec((tm, D_MODEL), lambda i: (i, 0)),
                  pl.BlockSpec((tm, RWKV_WIDTH), lambda i: (i, 0)),
                  pl.BlockSpec((tm, 2 * SGU_WIDTH), lambda i: (i, 0)),
                  _full((1, SGU_WIDTH)), _full((1, SGU_WIDTH)),
                  _full((SGU_GROUPS, SGU_BLOCK, SGU_BLOCK)), _full((SGU_BLOCK, SGU_WIDTH)),
                  sq, _full((1, D_MODEL)), sq,
                  _full((MEM_LEN, D_MODEL)), _full((MEM_LEN, D_MODEL)), sq],
        out_specs=pl.BlockSpec((tm, D_MODEL), lambda i: (i, 0)),
        scratch_shapes=[pltpu.VMEM((D_MODEL, D_MODEL), BF16)] * 3,
        compiler_params=_params(),
        name="mix_attn",
    )(x, yr, zs, slnw, slnb, ws, sbias, w_out, g2, wq, k, v, wo)


def _ffn_kernel(x_ref, g3_ref, wg_ref, wu_ref, wd_ref, gf_ref, o_ref):
    tm = x_ref.shape[0]
    groups = [slice(r, r + FFN_ROW_GROUP) for r in range(0, tm, FFN_ROW_GROUP)]
    x2 = [x_ref[r, :] for r in groups]
    h = [_rmsnorm(x, g3_ref[...]).astype(BF16) for x in x2]
    gate = [jnp.dot(h_, wg_ref[...], preferred_element_type=F32) for h_ in h]
    up = [jnp.dot(h_, wu_ref[...], preferred_element_type=F32) for h_ in h]
    act = [(jax.nn.silu(g_) * u_).astype(BF16) for g_, u_ in zip(gate, up)]
    x3 = [x + jnp.dot(a_, wd_ref[...], preferred_element_type=F32) for x, a_ in zip(x2, act)]
    for r, x in zip(groups, x3):
        o_ref[r, :] = _rmsnorm(x, gf_ref[...])


def _ffn(x, g3, wg, wu, wd, gf, tm):
    t = x.shape[0]
    return pl.pallas_call(
        _ffn_kernel,
        out_shape=jax.ShapeDtypeStruct((t, D_MODEL), F32),
        grid=(t // tm,),
        in_specs=[pl.BlockSpec((tm, D_MODEL), lambda i: (i, 0)), _full((1, D_MODEL)),
                  _full((D_MODEL, D_FF)), _full((D_MODEL, D_FF)), _full((D_FF, D_MODEL)),
                  _full((1, D_MODEL))],
        out_specs=pl.BlockSpec((tm, D_MODEL), lambda i: (i, 0)),
        compiler_params=_params("parallel"),
        name="ffn",
    )(x, g3, wg, wu, wd, gf)


def kernel(x, mem, norm1_g, w_in, shift_mu, w0, w_lora_up, a0, a_lora_up, g_lora_up, k_k, k_a, r_k,
           lnx_w, lnx_b, sgu_ln_w, sgu_ln_b, w_spatial, b_spatial, w_out, norm2_g, mem_norm_g,
           wq_x, wk_x, wv_x, wo_x, norm3_g, w_gate, w_up, w_down, norm_f_g):
    b, t, _ = x.shape
    depth = w_in.shape[0]
    assert depth == 1, "the final RMSNorm is fused into the (single) layer's ffn call"
    assert t % TM_ATTN == 0 and t % TM_FFN == 0
    assert t % TM_DENSE == 0 and TM_DENSE % TB_SCAN == 0 and TB_SCAN % CHUNK == 0
    row = lambda p: p.reshape(1, -1)
    bf = lambda p: p.astype(BF16)
    outs = []
    for bi in range(b):
        xb = x[bi]
        for l in range(depth):
            lora = w_lora_up.shape[1]
            zeros = jnp.zeros((lora, RWKV_WIDTH), F32)
            waup = jnp.concatenate(
                [jnp.concatenate([w_lora_up[l], zeros], axis=1),
                 jnp.concatenate([zeros, a_lora_up[l]], axis=1)], axis=0)
            bias = jnp.repeat(b_spatial[l].T, SGU_WIDTH // SGU_GROUPS, axis=1)

            front = _front(xb, row(norm1_g[l]), w_in[l], row(shift_mu[l]), row(w0[l]), bf(waup),
                           row(a0[l]), bf(g_lora_up[l]), row(k_k[l]), row(k_a[l]), row(r_k[l]),
                           TM_DENSE)
            prep = dict(zip(_PREP_NAMES, front[:len(_PREP_NAMES)]))
            z_sgu = front[len(_PREP_NAMES)]
            y_rwkv, wg_bf, wu_bf, wd_bf = _scan(prep, row(lnx_w[l]), row(lnx_b[l]),
                                                w_gate[l], w_up[l], w_down[l], TB_SCAN)
            k_mem, v_mem = _mem_kv(mem[bi], row(mem_norm_g[l]), wk_x[l], wv_x[l])
            x2 = _mix_attn(xb, y_rwkv, z_sgu, row(sgu_ln_w[l]), row(sgu_ln_b[l]), w_spatial[l], bias,
                           w_out[l], row(norm2_g[l]), wq_x[l], k_mem, v_mem, wo_x[l], TM_ATTN)
            xb = _ffn(x2, row(norm3_g[l]), wg_bf, wu_bf, wd_bf, row(norm_f_g), TM_FFN)
        outs.append(xb)
    return jnp.stack(outs, axis=0)
```

```python
import functools
import math

import jax
import jax.numpy as jnp
from jax import lax
from jax.experimental import pallas as pl
from jax.experimental.pallas import tpu as pltpu

F32 = jnp.float32
BF16 = jnp.bfloat16

D_MODEL = 1024
RWKV_WIDTH = 512
RWKV_HEAD = 64
LORA_WA = 128
GATE_LORA = 128
RWKV_IN = 3 * RWKV_WIDTH + LORA_WA + GATE_LORA
SGU_WIDTH = 512
SGU_GROUPS = 8
SGU_BLOCK = 128
IN_WIDTH = RWKV_IN + 2 * SGU_WIDTH
MEM_LEN = 256
XA_HEADS = 4
XA_HEAD_DIM = D_MODEL // XA_HEADS
D_FF = 2816
RMS_EPS = 1e-6
LN_EPS = 1e-5
LNX_EPS = 64e-5
EXP_M05 = 0.6065306597126334
LOG2_E = 1.4426950408889634

CHUNK = 64
PAIR = 2 * RWKV_HEAD
N_PAIRS = RWKV_WIDTH // PAIR
HEAD_BLOCK = 2 * PAIR
TM_DENSE = 512
TM_ATTN = 1024
TM_FFN = 1024
TB_SCAN = 512
Z_PAD = 8
WL_ROWS = 8
ATTN_ROW_GROUP = 256
ATTN_STAGGER = 2
FFN_ROW_GROUP = 256
IN_PROJ_COLS = 256
PREP_ROWS = 128
PREP_STREAMS = 4
PREP_PIECES_PER_CHUNK = 10
TERMS_DECAY_CUMSUM = 2
TERMS_HEAD_SUM = 1
TERMS_GROUP_MEAN = 2
VMEM_LIMIT = 56 * 1024 * 1024

_NN = (((1,), (0,)), ((), ()))
_NT = (((1,), (1,)), ((), ()))
_TN = (((0,), (0,)), ((), ()))


def _mm(a, b, dims=_NN):
    return lax.dot_general(a.astype(BF16), b.astype(BF16), dims, preferred_element_type=F32)


def _split_bf16(x, terms):
    parts = []
    rem = x
    for _ in range(terms):
        part = rem.astype(BF16)
        rem = rem - part.astype(F32)
        parts.append(part)
    return parts


def _cumsum_rows(ltri01, parts):
    return lax.dot_general(jnp.concatenate([ltri01] * len(parts), axis=1),
                           jnp.concatenate(parts, axis=0), _NN, preferred_element_type=F32)


def _head_sum_parts(parts, seg01):
    cols = []
    for q in range(parts[0].shape[1] // HEAD_BLOCK):
        acc = None
        for part in parts:
            d = lax.dot_general(part[:, HEAD_BLOCK * q:HEAD_BLOCK * (q + 1)], seg01, _NN,
                                preferred_element_type=F32)
            acc = d if acc is None else acc + d
        cols.append(acc)
    return jnp.concatenate(cols, axis=1)


def _head_sum(x, seg01, terms=TERMS_HEAD_SUM):
    return _head_sum_parts(_split_bf16(x, terms), seg01)


def _seg01():
    li = lax.broadcasted_iota(jnp.int32, (HEAD_BLOCK, HEAD_BLOCK), 0) >> 6
    lj = lax.broadcasted_iota(jnp.int32, (HEAD_BLOCK, HEAD_BLOCK), 1) >> 6
    return (li == lj).astype(BF16)


def _gelu_tanh(x):
    k1 = -2.0 * math.sqrt(2.0 / math.pi) * math.log2(math.e)
    return x / (1.0 + jnp.exp2(x * (k1 + (k1 * 0.044715) * (x * x))))


def _rmsnorm(x, g):
    return x * lax.rsqrt(jnp.mean(x * x, axis=-1, keepdims=True) + RMS_EPS) * g


def _full(shape):
    n = len(shape)
    return pl.BlockSpec(shape, lambda i: (0,) * n, pipeline_mode=pl.Buffered(1))


def _params(sem="arbitrary"):
    return pltpu.CompilerParams(dimension_semantics=(sem,), vmem_limit_bytes=VMEM_LIMIT)


def _run_interleaved(stages, rounds):
    for r in range(rounds):
        for gen, n in stages:
            for _ in range((r + 1) * n // rounds - r * n // rounds):
                next(gen, None)
    for gen, _ in stages:
        assert next(gen, StopIteration) is StopIteration, "piece count too small"


def _run_wavefront(gens, stagger):
    live = list(enumerate(gens))
    r = 0
    while live:
        for entry in list(live):
            g, gen = entry
            if r >= g * stagger and next(gen, StopIteration) is StopIteration:
                live.remove(entry)
        r += 1


def _mem_kv_kernel(mem_ref, g_ref, wk_ref, wv_ref, k_ref, v_ref):
    m = _rmsnorm(mem_ref[...], g_ref[...]).astype(BF16)
    k_ref[...] = jnp.dot(m, wk_ref[...].astype(BF16), preferred_element_type=F32).astype(BF16)
    v_ref[...] = jnp.dot(m, wv_ref[...].astype(BF16), preferred_element_type=F32).astype(BF16)


def _mem_kv(mem, g, wk, wv):
    return pl.pallas_call(
        _mem_kv_kernel,
        out_shape=(jax.ShapeDtypeStruct((MEM_LEN, D_MODEL), BF16),) * 2,
        grid=(1,),
        in_specs=[_full((MEM_LEN, D_MODEL)), _full((1, D_MODEL)),
                  _full((D_MODEL, D_MODEL)), _full((D_MODEL, D_MODEL))],
        out_specs=(_full((MEM_LEN, D_MODEL)),) * 2,
        compiler_params=_params(),
        name="mem_kv",
    )(mem, g, wk, wv)


_PREP_F32 = ("at", "bonus")
_PREP_BF16 = ("rt", "bh", "kh", "bc", "kc", "v")
_PREP_NAMES = _PREP_F32 + _PREP_BF16 + ("wl", "gds")


def _in_proj_pieces(x_ref, g_ref, w_ref, z_ref):
    h = _rmsnorm(x_ref[...], g_ref[...]).astype(BF16)
    yield
    for j in range(IN_WIDTH // IN_PROJ_COLS):
        cols = slice(j * IN_PROJ_COLS, (j + 1) * IN_PROJ_COLS)
        z_ref[Z_PAD:, cols] = jnp.dot(h, w_ref[:, cols], preferred_element_type=F32)
        yield


def _rwkv_prep_pieces(z_ref, mu_ref, w0_ref, waup_ref, a0_ref, kk_ref, ka_ref, rk_ref, prep,
                      chunks):
    seg01 = _seg01()
    lane = lax.broadcasted_iota(jnp.int32, (1, LORA_WA), 1)
    row = lax.broadcasted_iota(jnp.int32, (PREP_ROWS, 1), 0)
    ti = lax.broadcasted_iota(jnp.int32, (PREP_ROWS, PREP_ROWS), 0)
    tj = lax.broadcasted_iota(jnp.int32, (PREP_ROWS, PREP_ROWS), 1)
    ltri01 = ((tj <= ti) & ((ti >> 6) == (tj >> 6))).astype(BF16)
    chunks_per_scan_tile = TB_SCAN // CHUNK
    chunks_per_unit = PREP_ROWS // CHUNK
    half_w = RWKV_WIDTH // 2

    def shifted(c, cols):
        z = z_ref[Z_PAD + c * PREP_ROWS:Z_PAD + (c + 1) * PREP_ROWS, cols]
        before = z_ref[Z_PAD + c * PREP_ROWS - 1:Z_PAD + c * PREP_ROWS, cols]
        zprev = jnp.where(row == 0, before, pltpu.roll(z, 1, axis=0))
        return z + (zprev - z) * mu_ref[:, cols]

    for c in chunks:
        rows = slice(c * PREP_ROWS, (c + 1) * PREP_ROWS)
        wa_in = shifted(c, slice(3 * RWKV_WIDTH, 3 * RWKV_WIDTH + LORA_WA))
        wa_in = jnp.where(lane < LORA_WA // 2, jnp.tanh(wa_in), wa_in).astype(BF16)
        prep["gds"][rows, :] = jax.nn.sigmoid(
            shifted(c, slice(3 * RWKV_WIDTH + LORA_WA, RWKV_IN))).astype(BF16)
        for q in range(2):
            hc = slice(q * half_w, (q + 1) * half_w)
            r = shifted(c, hc)
            k = shifted(c, slice(RWKV_WIDTH + q * half_w, RWKV_WIDTH + (q + 1) * half_w))
            v = shifted(c, slice(2 * RWKV_WIDTH + q * half_w, 2 * RWKV_WIDTH + (q + 1) * half_w))
            prep["v"][rows, hc] = v.astype(BF16)
            kk = k * kk_ref[:, hc]
            kk_sq = _split_bf16(kk * kk, TERMS_HEAD_SUM)
            yield
            w_pre = w0_ref[:, hc] + jnp.dot(wa_in, waup_ref[:, hc], preferred_element_type=F32)
            a_pre = a0_ref[:, hc] + jnp.dot(
                wa_in, waup_ref[:, RWKV_WIDTH + q * half_w:RWKV_WIDTH + (q + 1) * half_w],
                preferred_element_type=F32)
            kk_ss = _head_sum_parts(kk_sq, seg01)
            yield
            a = jax.nn.sigmoid(a_pre)
            lw = jax.nn.sigmoid(w_pre) * (-EXP_M05 * LOG2_E)
            lw_parts = _split_bf16(lw, TERMS_DECAY_CUMSUM)
            kk = kk * lax.rsqrt(jnp.maximum(kk_ss, 1e-24))
            kmod = k * ((1.0 - ka_ref[:, hc]) + a * ka_ref[:, hc])
            kka = kk * a
            rkk = _split_bf16(r * kmod * rk_ref[:, hc], TERMS_HEAD_SUM)
            yield
            cs = _cumsum_rows(ltri01, lw_parts)
            prep["bonus"][rows, hc] = _head_sum_parts(rkk, seg01) * v
            yield
            w_inv = jnp.exp2(-cs)
            w_last = [jnp.exp2(cs[(j + 1) * CHUNK - 1:(j + 1) * CHUNK, :])
                      for j in range(chunks_per_unit)]
            w_tail = jnp.concatenate(
                [w_last[j] * w_inv[j * CHUNK:(j + 1) * CHUNK] for j in range(chunks_per_unit)],
                axis=0)
            prep["rt"][rows, hc] = (r * jnp.exp2(cs)).astype(BF16)
            prep["at"][rows, hc] = -kk * jnp.exp2(cs - lw)
            prep["bh"][rows, hc] = (kka * w_inv).astype(BF16)
            prep["kh"][rows, hc] = (kmod * w_inv).astype(BF16)
            prep["bc"][rows, hc] = (kka * w_tail).astype(BF16)
            prep["kc"][rows, hc] = (kmod * w_tail).astype(BF16)
            for j in range(chunks_per_unit):
                cq, cr = divmod(c * chunks_per_unit + j, chunks_per_scan_tile)
                prep["wl"][cq, cr:cr + 1, hc] = w_last[j]
            yield


def _copy_pieces(z_ref, o_ref, *, tm):
    for b in range(tm // SGU_BLOCK):
        o_ref[b * SGU_BLOCK:(b + 1) * SGU_BLOCK, :] = (
            z_ref[Z_PAD + b * SGU_BLOCK:Z_PAD + (b + 1) * SGU_BLOCK, RWKV_IN:])
        yield


def _front_kernel(x_ref, g1_ref, win_ref, mu_ref, w0_ref, waup_ref, a0_ref, kk_ref, ka_ref, rk_ref,
                  *rest, tm):
    n = len(_PREP_NAMES)
    prep = dict(zip(_PREP_NAMES, rest[:n]))
    zs_ref = rest[n]
    wbf_ref, z0_ref, z1_ref = rest[n + 1:]
    i = pl.program_id(0)

    @pl.when(i == 0)
    def _():
        z1_ref[...] = jnp.zeros_like(z1_ref)
        wbf_ref[...] = win_ref[...].astype(BF16)

    def step(z_write, z_read):
        n_chunks = tm // PREP_ROWS
        prep_pieces = [
            _rwkv_prep_pieces(z_read, mu_ref, w0_ref, waup_ref, a0_ref, kk_ref, ka_ref, rk_ref,
                              prep, range(k, n_chunks, PREP_STREAMS))
            for k in range(PREP_STREAMS)]
        n_dot = 1 + IN_WIDTH // IN_PROJ_COLS + 1
        _run_interleaved(
            [(_in_proj_pieces(x_ref, g1_ref, wbf_ref, z_write), n_dot)]
            + [(gen, PREP_PIECES_PER_CHUNK * n_chunks // PREP_STREAMS + 1) for gen in prep_pieces]
            + [(_copy_pieces(z_read, zs_ref, tm=tm), tm // SGU_BLOCK + 1)],
            rounds=n_dot)
        if TB_SCAN // CHUNK < WL_ROWS:
            prep["wl"][:, TB_SCAN // CHUNK:, :] = jnp.zeros(
                (tm // TB_SCAN, WL_ROWS - TB_SCAN // CHUNK, RWKV_WIDTH), F32)
        z_write[Z_PAD - 1:Z_PAD, :] = z_read[Z_PAD + tm - 1:Z_PAD + tm, :]

    @pl.when((i & 1) == 0)
    def _():
        step(z0_ref, z1_ref)

    @pl.when((i & 1) == 1)
    def _():
        step(z1_ref, z0_ref)


def _front(x, g1, w_in, mu, w0, waup, a0, k_k, k_a, r_k, tm):
    t = x.shape[0]
    n_tiles = t // tm
    vec = _full((1, RWKV_WIDTH))
    out_tile = lambda i: (jnp.maximum(i - 1, 0), 0)
    out_shapes = ([jax.ShapeDtypeStruct((t, RWKV_WIDTH), F32)] * len(_PREP_F32)
                  + [jax.ShapeDtypeStruct((t, RWKV_WIDTH), BF16)] * len(_PREP_BF16)
                  + [jax.ShapeDtypeStruct((t // TB_SCAN, WL_ROWS, RWKV_WIDTH), F32),
                     jax.ShapeDtypeStruct((t, GATE_LORA), BF16),
                     jax.ShapeDtypeStruct((t, 2 * SGU_WIDTH), F32)])
    out_specs = ([pl.BlockSpec((tm, RWKV_WIDTH), out_tile)] * (len(_PREP_F32) + len(_PREP_BF16))
                 + [pl.BlockSpec((tm // TB_SCAN, WL_ROWS, RWKV_WIDTH),
                                 lambda i: (jnp.maximum(i - 1, 0), 0, 0)),
                    pl.BlockSpec((tm, GATE_LORA), out_tile),
                    pl.BlockSpec((tm, 2 * SGU_WIDTH), out_tile)])
    return pl.pallas_call(
        functools.partial(_front_kernel, tm=tm),
        out_shape=tuple(out_shapes),
        grid=(n_tiles + 1,),
        in_specs=[pl.BlockSpec((tm, D_MODEL), lambda i: (jnp.minimum(i, n_tiles - 1), 0)),
                  _full((1, D_MODEL)), _full((D_MODEL, IN_WIDTH)), _full((1, RWKV_IN)), vec,
                  _full((LORA_WA, 2 * RWKV_WIDTH)), vec, vec, vec, vec],
        out_specs=tuple(out_specs),
        scratch_shapes=[pltpu.VMEM((D_MODEL, IN_WIDTH), BF16),
                        pltpu.VMEM((Z_PAD + tm, IN_WIDTH), F32),
                        pltpu.VMEM((Z_PAD + tm, IN_WIDTH), F32)],
        compiler_params=_params(),
        name="front",
    )(x, g1, w_in, mu, w0, waup, a0, k_k, k_a, r_k)


def _pair_masks():
    t = lax.broadcasted_iota(jnp.int32, (CHUNK, PAIR), 0)
    j = lax.broadcasted_iota(jnp.int32, (CHUNK, PAIR), 1) & (CHUNK - 1)
    strict = j < t
    incl = j <= t
    blk16 = (t >> 4) == (j >> 4)
    blk32 = (t >> 5) == (j >> 5)
    return strict, incl, blk16, blk32


def _bd(x, bd_mask):
    x = x.astype(BF16)
    return jnp.where(bd_mask, jnp.concatenate([x, x], axis=0), 0.0).astype(BF16)


def _staged(fn, items, parts=2):
    out = []
    n = len(items) // parts
    for k in range(parts):
        out += [fn(*item) for item in items[k * n:(k + 1) * n]]
        yield
    return out


def _unit_lower_inverse_minus_identity(a_list, masks, bd_mask):
    _, _, blk16, blk32 = masks
    ad = [jnp.where(blk16, a, 0.0) for a in a_list]
    ap = yield from _staged(lambda x: _mm(x, _bd(x, bd_mask)), [(x,) for x in ad])
    tp = ad
    for _ in range(2):
        both = yield from _staged(
            lambda p, t: _mm(p, jnp.concatenate([_bd(p, bd_mask), _bd(t, bd_mask)], axis=1)),
            list(zip(ap, tp)))
        tp = [t + p + b[:, PAIR:] for t, p, b in zip(tp, ap, both)]
        ap = [b[:, :PAIR] for b in both]
    last = yield from _staged(lambda p, t: _mm(p, _bd(t, bd_mask)), list(zip(ap, tp)))
    tp = [t + p + x for t, p, x in zip(tp, ap, last)]
    for off_mask in (blk32 & ~blk16, ~blk32):
        off = [jnp.where(off_mask, a, 0.0) for a in a_list]
        x = yield from _staged(lambda o, t: o + _mm(t, _bd(o, bd_mask)), list(zip(off, tp)))
        tp = yield from _staged(lambda t, xx: t + xx + _mm(xx, _bd(t, bd_mask)), list(zip(tp, x)))
    return tp


_TERM_STAGES = 11
_TERM_NAMES = ("achk", "uv", "bb")


def _bd_masks():
    bi = lax.broadcasted_iota(jnp.int32, (PAIR, PAIR), 0) >> 6
    bj = lax.broadcasted_iota(jnp.int32, (PAIR, PAIR), 1) >> 6
    bd1 = bi == bj
    return bd1, jnp.concatenate([bd1, bd1], axis=1)


def _scan_terms_pieces(rt_ref, at_ref, bh_ref, kh_ref, v_ref, terms, *, tb):
    masks = _pair_masks()
    strict, incl = masks[0], masks[1]
    bd1, bd2 = _bd_masks()
    probs = [(c, p) for c in range(tb // CHUNK) for p in range(N_PAIRS)]
    cut = lambda ref: [ref[c * CHUNK:(c + 1) * CHUNK, p * PAIR:(p + 1) * PAIR] for c, p in probs]
    rt_p, at_p, bh_p, kh_p, v_p = map(cut, (rt_ref, at_ref, bh_ref, kh_ref, v_ref))
    gram = yield from _staged(
        lambda a_, r_, b_, k_: _mm(jnp.concatenate([a_.astype(BF16), r_], axis=0),
                                   jnp.concatenate([_bd(b_, bd1), _bd(k_, bd1)], axis=0), _NT),
        list(zip(at_p, rt_p, bh_p, kh_p)))
    a_ab = [jnp.where(strict, g_[:CHUNK, :PAIR], 0.0) for g_ in gram]
    a_ak = [jnp.where(strict, g_[:CHUNK, PAIR:], 0.0) for g_ in gram]
    incl2 = jnp.concatenate([incl, incl], axis=1)
    for i, g_ in enumerate(gram):
        terms["bb"][i] = jnp.where(incl2, g_[CHUNK:], 0.0).astype(BF16)
    rhs = yield from _staged(
        lambda m_, x_, a_: jnp.concatenate([_mm(m_, _bd(x_, bd1)), a_], axis=1),
        list(zip(a_ak, v_p, at_p)))
    tp = yield from _unit_lower_inverse_minus_identity(a_ab, masks, bd1)
    sol = yield from _staged(lambda x_, t_: x_ + _mm(t_, _bd(x_, bd2)), list(zip(rhs, tp)))
    for i, x_ in enumerate(sol):
        terms["uv"][i] = x_[:, :PAIR]
        terms["achk"][i] = x_[:, PAIR:].astype(BF16)


def _scan_state_pieces(terms, rt_ref, v_ref, bc_ref, kc_ref, wl_ref, gds_ref, gup_ref, bonus_ref,
                       lnw_ref, lnb_ref, o_ref, s_ref, y_ref, *, tb):
    bd1, _ = _bd_masks()
    bd_upd = jnp.logical_and(bd1, pl.program_id(0) > 0)
    s = [s_ref[p] for p in range(N_PAIRS)]
    for c in range(tb // CHUNK):
        rows = slice(c * CHUNK, (c + 1) * CHUNK)
        lanes = [slice(p * PAIR, (p + 1) * PAIR) for p in range(N_PAIRS)]
        idx = [c * N_PAIRS + p for p in range(N_PAIRS)]
        on_s = [_mm(jnp.concatenate([terms["achk"][i], rt_ref[rows, lanes[p]].astype(BF16)], axis=0),
                    s[p], _NT) for p, i in enumerate(idx)]
        u = [x[:CHUNK] + terms["uv"][i] for x, i in zip(on_s, idx)]
        yield
        w_last = wl_ref[0, c:c + 1, :]
        upd = []
        for p, i in enumerate(idx):
            v_i = v_ref[rows, lanes[p]]
            y_ref[rows, lanes[p]] = (
                on_s[p][CHUNK:]
                + _mm(terms["bb"][i], jnp.concatenate([_bd(u[p], bd1), _bd(v_i, bd1)], axis=0)))
            upd.append(_mm(jnp.concatenate([u[p].astype(BF16), v_i], axis=0),
                           jnp.concatenate([bc_ref[rows, lanes[p]], kc_ref[rows, lanes[p]]], axis=0),
                           _TN))
        s = [s[p] * w_last[:, lanes[p]] + jnp.where(bd_upd, upd[p], 0.0) for p in range(N_PAIRS)]
        yield
    for p in range(N_PAIRS):
        s_ref[p] = s[p]
    seg01 = _seg01()
    y = y_ref[...]
    mean = _head_sum(y, seg01, TERMS_GROUP_MEAN) * (1.0 / RWKV_HEAD)
    yield
    d = y - mean
    var = _head_sum(d * d, seg01) * (1.0 / RWKV_HEAD)
    yield
    yn = d * lax.rsqrt(var + LNX_EPS) * lnw_ref[...] + lnb_ref[...]
    gate = jnp.dot(gds_ref[...], gup_ref[...], preferred_element_type=F32)
    o_ref[...] = ((yn + bonus_ref[...]) * gate).astype(BF16)


def _scan_kernel(rt_ref, at_ref, bh_ref, kh_ref, v_ref, rtp_ref, vp_ref, bcp_ref, kcp_ref, bonus_ref,
                 gds_ref, wl_ref, gup_ref, lnw_ref, lnb_ref, wg_ref, wu_ref, wd_ref, o_ref, wg_bf_ref,
                 wu_bf_ref, wd_bf_ref, s_ref, y_ref, *term_refs, tb):
    n = len(_TERM_NAMES)
    slots = [dict(zip(_TERM_NAMES, term_refs[k * n:(k + 1) * n])) for k in range(2)]
    i = pl.program_id(0)

    @pl.when(i == 0)
    def _():
        s_ref[...] = jnp.zeros_like(s_ref)
        for ref in slots[1].values():
            ref[...] = jnp.zeros_like(ref)

    wg_bf_ref[...] = wg_ref[...].astype(BF16)
    wu_bf_ref[...] = wu_ref[...].astype(BF16)
    wd_bf_ref[...] = wd_ref[...].astype(BF16)

    def step(write, read):
        n_terms = 2 * _TERM_STAGES + 1
        _run_interleaved(
            [(_scan_terms_pieces(rt_ref, at_ref, bh_ref, kh_ref, v_ref, write, tb=tb), n_terms),
             (_scan_state_pieces(read, rtp_ref, vp_ref, bcp_ref, kcp_ref, wl_ref, gds_ref, gup_ref,
                                 bonus_ref, lnw_ref, lnb_ref, o_ref, s_ref, y_ref, tb=tb),
              2 * (tb // CHUNK) + 3)],
            rounds=n_terms)

    @pl.when((i & 1) == 0)
    def _():
        step(slots[0], slots[1])

    @pl.when((i & 1) == 1)
    def _():
        step(slots[1], slots[0])


def _scan(prep, gup, lnw, lnb, w_gate, w_up, w_down, tb):
    t = prep["rt"].shape[0]
    n_tiles = t // tb
    assert tb // CHUNK <= WL_ROWS
    last = n_tiles - 1
    cur = pl.BlockSpec((tb, RWKV_WIDTH), lambda i: (jnp.minimum(i, last), 0))
    prev = pl.BlockSpec((tb, RWKV_WIDTH), lambda i: (jnp.maximum(i - 1, 0), 0))
    vec = _full((1, RWKV_WIDTH))
    assert D_MODEL % n_tiles == 0 and (D_MODEL // n_tiles) % 16 == 0
    assert n_tiles % 2 == 0 and D_FF % (n_tiles // 2) == 0 and (D_FF // (n_tiles // 2)) % 16 == 0
    up_rows = pl.BlockSpec((D_MODEL // n_tiles, D_FF), lambda i: (jnp.minimum(i, last), 0))
    down_blk = pl.BlockSpec((D_FF // (n_tiles // 2), D_MODEL // 2),
                            lambda i: (jnp.minimum(i, last) // 2, jnp.minimum(i, last) % 2))
    n_prob = (tb // CHUNK) * N_PAIRS
    term_shapes = [pltpu.VMEM((n_prob, CHUNK, PAIR), BF16), pltpu.VMEM((n_prob, CHUNK, PAIR), F32),
                   pltpu.VMEM((n_prob, CHUNK, 2 * PAIR), BF16)]
    return pl.pallas_call(
        functools.partial(_scan_kernel, tb=tb),
        out_shape=(jax.ShapeDtypeStruct((t, RWKV_WIDTH), BF16),
                   jax.ShapeDtypeStruct((D_MODEL, D_FF), BF16),
                   jax.ShapeDtypeStruct((D_MODEL, D_FF), BF16),
                   jax.ShapeDtypeStruct((D_FF, D_MODEL), BF16)),
        grid=(n_tiles + 1,),
        in_specs=[cur] * 5 + [prev] * 5
        + [pl.BlockSpec((tb, GATE_LORA), lambda i: (jnp.maximum(i - 1, 0), 0)),
           pl.BlockSpec((1, WL_ROWS, RWKV_WIDTH), lambda i: (jnp.maximum(i - 1, 0), 0, 0)),
           _full((GATE_LORA, RWKV_WIDTH)), vec, vec, up_rows, up_rows, down_blk],
        out_specs=(prev, up_rows, up_rows, down_blk),
        scratch_shapes=[pltpu.VMEM((N_PAIRS, PAIR, PAIR), F32), pltpu.VMEM((tb, RWKV_WIDTH), F32)]
        + term_shapes * 2,
        compiler_params=_params(),
        name="scan",
    )(prep["rt"], prep["at"], prep["bh"], prep["kh"], prep["v"],
      prep["rt"], prep["v"], prep["bc"], prep["kc"], prep["bonus"], prep["gds"], prep["wl"],
      gup, lnw, lnb, w_gate, w_up, w_down)


def _sgu_block_prepare(z, lnw, lnb, sel):
    hz = _gelu_tanh(z)
    u = hz[:, :SGU_WIDTH]
    vf = hz[:, SGU_WIDTH:]
    mu = jnp.mean(vf, axis=-1, keepdims=True)
    d = vf - mu
    var = jnp.mean(d * d, axis=-1, keepdims=True)
    vn = d * lax.rsqrt(var + LN_EPS) * lnw + lnb
    stacks = []
    for p in range(SGU_WIDTH // PAIR):
        vb = vn[:, p * PAIR:(p + 1) * PAIR]
        stacks.append(jnp.where(sel, jnp.concatenate([vb, vb], axis=0), 0.0).astype(BF16))
    return u, stacks


def _sgu_block_mix(u, stacks, wcat, bias):
    return jnp.concatenate(
        [u[:, p * PAIR:(p + 1) * PAIR]
         * (jnp.dot(wcat[p], stacks[p], preferred_element_type=F32) + bias[:, p * PAIR:(p + 1) * PAIR])
         for p in range(SGU_WIDTH // PAIR)], axis=1)


def _mix_attn_group(r, x_ref, yr_ref, zs_ref, slnw_ref, slnb_ref, sbias_ref, wo1_ref, wo2_ref,
                    g2_ref, wq_ref, k_ref, v_ref, wo_ref, o_ref, wcat, sel):
    heads = [slice(hd * XA_HEAD_DIM, (hd + 1) * XA_HEAD_DIM) for hd in range(XA_HEADS)]
    prepared = [_sgu_block_prepare(zs_ref[b:b + SGU_BLOCK, :], slnw_ref[...], slnb_ref[...], sel)
                for b in range(r.start, r.stop, SGU_BLOCK)]
    yield
    x1 = x_ref[r, :] + jnp.dot(yr_ref[r, :].astype(BF16), wo1_ref[...],
                               preferred_element_type=F32)
    y_sgu = jnp.concatenate([_sgu_block_mix(u, st, wcat, sbias_ref[...]) for u, st in prepared],
                            axis=0)
    yield
    x1 = x1 + jnp.dot(y_sgu.astype(BF16), wo2_ref[...], preferred_element_type=F32)
    yield
    h = _rmsnorm(x1, g2_ref[...]).astype(BF16)
    yield
    q = jnp.dot(h, wq_ref[...], preferred_element_type=F32).astype(BF16)
    s = [lax.dot_general(q[:, hl], k_ref[:, hl], _NT, preferred_element_type=F32)
         * (XA_HEAD_DIM ** -0.5) for hl in heads]
    yield
    p = []
    for s_h in s:
        e = jnp.exp(s_h - jnp.max(s_h, axis=-1, keepdims=True))
        p.append((e / jnp.sum(e, axis=-1, keepdims=True)).astype(BF16))
    yield
    o = jnp.concatenate([jnp.dot(p_h, v_ref[:, hl], preferred_element_type=F32)
                         for p_h, hl in zip(p, heads)], axis=1).astype(BF16)
    o_ref[r, :] = x1 + jnp.dot(o, wo_ref[...], preferred_element_type=F32)


def _mix_attn_kernel(x_ref, yr_ref, zs_ref, slnw_ref, slnb_ref, ws_ref, sbias_ref, wout_f32_ref,
                     g2_ref, wq_f32_ref, k_ref, v_ref, wo_f32_ref, o_ref, wout_ref, wq_ref, wo_ref):
    @pl.when(pl.program_id(0) == 0)
    def _():
        wout_ref[...] = wout_f32_ref[...].astype(BF16)
        wq_ref[...] = wq_f32_ref[...].astype(BF16)
        wo_ref[...] = wo_f32_ref[...].astype(BF16)

    wo1_ref = wout_ref.at[:RWKV_WIDTH]
    wo2_ref = wout_ref.at[RWKV_WIDTH:]
    tm = x_ref.shape[0]
    ti = lax.broadcasted_iota(jnp.int32, (SGU_BLOCK, SGU_BLOCK), 0)
    tj = lax.broadcasted_iota(jnp.int32, (SGU_BLOCK, SGU_BLOCK), 1)
    tril = tj <= ti
    wcat = [jnp.concatenate([jnp.where(tril, ws_ref[2 * p], 0.0),
                             jnp.where(tril, ws_ref[2 * p + 1], 0.0)], axis=1).astype(BF16)
            for p in range(SGU_WIDTH // PAIR)]
    bi = lax.broadcasted_iota(jnp.int32, (2 * SGU_BLOCK, PAIR), 0) >> 7
    bj = lax.broadcasted_iota(jnp.int32, (2 * SGU_BLOCK, PAIR), 1) >> 6
    sel = bi == bj
    gens = [_mix_attn_group(slice(r, r + ATTN_ROW_GROUP), x_ref, yr_ref, zs_ref, slnw_ref, slnb_ref,
                            sbias_ref, wo1_ref, wo2_ref, g2_ref, wq_ref, k_ref, v_ref, wo_ref,
                            o_ref, wcat, sel)
            for r in range(0, tm, ATTN_ROW_GROUP)]
    _run_wavefront(gens, ATTN_STAGGER)


def _mix_attn(x, yr, zs, slnw, slnb, ws, sbias, w_out, g2, wq, k, v, wo, tm):
    t = x.shape[0]
    sq = _full((D_MODEL, D_MODEL))
    return pl.pallas_call(
        _mix_attn_kernel,
        out_shape=jax.ShapeDtypeStruct((t, D_MODEL), F32),
        grid=(t // tm,),
        in_specs=[pl.BlockSpec((tm, D_MODEL), lambda i: (i, 0)),
                  pl.BlockSpec((tm, RWKV_WIDTH), lambda i: (i, 0)),
                  pl.BlockSpec((tm, 2 * SGU_WIDTH), lambda i: (i, 0)),
                  _full((1, SGU_WIDTH)), _full((1, SGU_WIDTH)),
                  _full((SGU_GROUPS, SGU_BLOCK, SGU_BLOCK)), _full((SGU_BLOCK, SGU_WIDTH)),
                  sq, _full((1, D_MODEL)), sq,
                  _full((MEM_LEN, D_MODEL)), _full((MEM_LEN, D_MODEL)), sq],
        out_specs=pl.BlockSpec((tm, D_MODEL), lambda i: (i, 0)),
        scratch_shapes=[pltpu.VMEM((D_MODEL, D_MODEL), BF16)] * 3,
        compiler_params=_params(),
        name="mix_attn",
    )(x, yr, zs, slnw, slnb, ws, sbias, w_out, g2, wq, k, v, wo)


def _ffn_kernel(x_ref, g3_ref, wg_ref, wu_ref, wd_ref, gf_ref, o_ref):
    tm = x_ref.shape[0]
    groups = [slice(r, r + FFN_ROW_GROUP) for r in range(0, tm, FFN_ROW_GROUP)]
    x2 = [x_ref[r, :] for r in groups]
    h = [_rmsnorm(x, g3_ref[...]).astype(BF16) for x in x2]
    gate = [jnp.dot(h_, wg_ref[...], preferred_element_type=F32) for h_ in h]
    up = [jnp.dot(h_, wu_ref[...], preferred_element_type=F32) for h_ in h]
    act = [(jax.nn.silu(g_) * u_).astype(BF16) for g_, u_ in zip(gate, up)]
    x3 = [x + jnp.dot(a_, wd_ref[...], preferred_element_type=F32) for x, a_ in zip(x2, act)]
    for r, x in zip(groups, x3):
        o_ref[r, :] = _rmsnorm(x, gf_ref[...])


def _ffn(x, g3, wg, wu, wd, gf, tm):
    t = x.shape[0]
    return pl.pallas_call(
        _ffn_kernel,
        out_shape=jax.ShapeDtypeStruct((t, D_MODEL), F32),
        grid=(t // tm,),
        in_specs=[pl.BlockSpec((tm, D_MODEL), lambda i: (i, 0)), _full((1, D_MODEL)),
                  _full((D_MODEL, D_FF)), _full((D_MODEL, D_FF)), _full((D_FF, D_MODEL)),
                  _full((1, D_MODEL))],
        out_specs=pl.BlockSpec((tm, D_MODEL), lambda i: (i, 0)),
        compiler_params=_params("parallel"),
        name="ffn",
    )(x, g3, wg, wu, wd, gf)


def kernel(x, mem, norm1_g, w_in, shift_mu, w0, w_lora_up, a0, a_lora_up, g_lora_up, k_k, k_a, r_k,
           lnx_w, lnx_b, sgu_ln_w, sgu_ln_b, w_spatial, b_spatial, w_out, norm2_g, mem_norm_g,
           wq_x, wk_x, wv_x, wo_x, norm3_g, w_gate, w_up, w_down, norm_f_g):
    b, t, _ = x.shape
    depth = w_in.shape[0]
    assert depth == 1, "the final RMSNorm is fused into the (single) layer's ffn call"
    assert t % TM_ATTN == 0 and t % TM_FFN == 0
    assert t % TM_DENSE == 0 and TM_DENSE % TB_SCAN == 0 and TB_SCAN % CHUNK == 0
    row = lambda p: p.reshape(1, -1)
    bf = lambda p: p.astype(BF16)
    outs = []
    for bi in range(b):
        xb = x[bi]
        for l in range(depth):
            lora = w_lora_up.shape[1]
            zeros = jnp.zeros((lora, RWKV_WIDTH), F32)
            waup = jnp.concatenate(
                [jnp.concatenate([w_lora_up[l], zeros], axis=1),
                 jnp.concatenate([zeros, a_lora_up[l]], axis=1)], axis=0)
            bias = jnp.repeat(b_spatial[l].T, SGU_WIDTH // SGU_GROUPS, axis=1)

            front = _front(xb, row(norm1_g[l]), w_in[l], row(shift_mu[l]), row(w0[l]), bf(waup),
                           row(a0[l]), row(k_k[l]), row(k_a[l]), row(r_k[l]), TM_DENSE)
            prep = dict(zip(_PREP_NAMES, front[:len(_PREP_NAMES)]))
            z_sgu = front[len(_PREP_NAMES)]
            y_rwkv, wg_bf, wu_bf, wd_bf = _scan(prep, bf(g_lora_up[l]), row(lnx_w[l]), row(lnx_b[l]),
                                                w_gate[l], w_up[l], w_down[l], TB_SCAN)
            k_mem, v_mem = _mem_kv(mem[bi], row(mem_norm_g[l]), wk_x[l], wv_x[l])
            x2 = _mix_attn(xb, y_rwkv, z_sgu, row(sgu_ln_w[l]), row(sgu_ln_b[l]), w_spatial[l], bias,
                           w_out[l], row(norm2_g[l]), wq_x[l], k_mem, v_mem, wo_x[l], TM_ATTN)
            xb = _ffn(x2, row(norm3_g[l]), wg_bf, wu_bf, wd_bf, row(norm_f_g), TM_FFN)
        outs.append(xb)
    return jnp.stack(outs, axis=0)
```

```python
import functools
import math

import jax
import jax.numpy as jnp
from jax import lax
from jax.experimental import pallas as pl
from jax.experimental.pallas import tpu as pltpu

F32 = jnp.float32
BF16 = jnp.bfloat16

D_MODEL = 1024
RWKV_WIDTH = 512
RWKV_HEAD = 64
LORA_WA = 128
GATE_LORA = 128
RWKV_IN = 3 * RWKV_WIDTH + LORA_WA + GATE_LORA
SGU_WIDTH = 512
SGU_GROUPS = 8
SGU_BLOCK = 128
IN_WIDTH = RWKV_IN + 2 * SGU_WIDTH
MEM_LEN = 256
XA_HEADS = 4
XA_HEAD_DIM = D_MODEL // XA_HEADS
D_FF = 2816
RMS_EPS = 1e-6
LN_EPS = 1e-5
LNX_EPS = 64e-5
EXP_M05 = 0.6065306597126334
LOG2_E = 1.4426950408889634

CHUNK = 64
PAIR = 2 * RWKV_HEAD
N_PAIRS = RWKV_WIDTH // PAIR
HEAD_BLOCK = 2 * PAIR
TM_DENSE = 512
TM_ATTN = 1024
TM_FFN = 1024
TB_SCAN = 512
Z_PAD = 8
WL_ROWS = 8
ATTN_ROW_GROUP = 256
ATTN_STAGGER = 2
FFN_ROW_GROUP = 256
IN_PROJ_COLS = 256
PREP_ROWS = 128
PREP_STREAMS = 4
PREP_PIECES_PER_CHUNK = 10
TERMS_DECAY_CUMSUM = 2
TERMS_HEAD_SUM = 1
TERMS_GROUP_MEAN = 2
VMEM_LIMIT = 56 * 1024 * 1024

_NN = (((1,), (0,)), ((), ()))
_NT = (((1,), (1,)), ((), ()))
_TN = (((0,), (0,)), ((), ()))


def _mm(a, b, dims=_NN):
    return lax.dot_general(a.astype(BF16), b.astype(BF16), dims, preferred_element_type=F32)


def _split_bf16(x, terms):
    parts = []
    rem = x
    for _ in range(terms):
        part = rem.astype(BF16)
        rem = rem - part.astype(F32)
        parts.append(part)
    return parts


def _cumsum_rows(ltri01, parts):
    return lax.dot_general(jnp.concatenate([ltri01] * len(parts), axis=1),
                           jnp.concatenate(parts, axis=0), _NN, preferred_element_type=F32)


def _head_sum_parts(parts, seg01):
    cols = []
    for q in range(parts[0].shape[1] // HEAD_BLOCK):
        acc = None
        for part in parts:
            d = lax.dot_general(part[:, HEAD_BLOCK * q:HEAD_BLOCK * (q + 1)], seg01, _NN,
                                preferred_element_type=F32)
            acc = d if acc is None else acc + d
        cols.append(acc)
    return jnp.concatenate(cols, axis=1)


def _head_sum(x, seg01, terms=TERMS_HEAD_SUM):
    return _head_sum_parts(_split_bf16(x, terms), seg01)


def _seg01():
    li = lax.broadcasted_iota(jnp.int32, (HEAD_BLOCK, HEAD_BLOCK), 0) >> 6
    lj = lax.broadcasted_iota(jnp.int32, (HEAD_BLOCK, HEAD_BLOCK), 1) >> 6
    return (li == lj).astype(BF16)


def _gelu_tanh(x):
    k1 = -2.0 * math.sqrt(2.0 / math.pi) * math.log2(math.e)
    return x / (1.0 + jnp.exp2(x * (k1 + (k1 * 0.044715) * (x * x))))


def _rmsnorm(x, g):
    return x * lax.rsqrt(jnp.mean(x * x, axis=-1, keepdims=True) + RMS_EPS) * g


def _full(shape):
    n = len(shape)
    return pl.BlockSpec(shape, lambda i: (0,) * n, pipeline_mode=pl.Buffered(1))


def _params(sem="arbitrary"):
    return pltpu.CompilerParams(dimension_semantics=(sem,), vmem_limit_bytes=VMEM_LIMIT)


def _run_interleaved(stages, rounds):
    for r in range(rounds):
        for gen, n in stages:
            for _ in range((r + 1) * n // rounds - r * n // rounds):
                next(gen, None)
    for gen, _ in stages:
        assert next(gen, StopIteration) is StopIteration, "piece count too small"


def _run_wavefront(gens, stagger):
    live = list(enumerate(gens))
    r = 0
    while live:
        for entry in list(live):
            g, gen = entry
            if r >= g * stagger and next(gen, StopIteration) is StopIteration:
                live.remove(entry)
        r += 1


def _mem_kv_kernel(mem_ref, g_ref, wk_ref, wv_ref, k_ref, v_ref):
    m = _rmsnorm(mem_ref[...], g_ref[...]).astype(BF16)
    k_ref[...] = jnp.dot(m, wk_ref[...].astype(BF16), preferred_element_type=F32).astype(BF16)
    v_ref[...] = jnp.dot(m, wv_ref[...].astype(BF16), preferred_element_type=F32).astype(BF16)


def _mem_kv(mem, g, wk, wv):
    return pl.pallas_call(
        _mem_kv_kernel,
        out_shape=(jax.ShapeDtypeStruct((MEM_LEN, D_MODEL), BF16),) * 2,
        grid=(1,),
        in_specs=[_full((MEM_LEN, D_MODEL)), _full((1, D_MODEL)),
                  _full((D_MODEL, D_MODEL)), _full((D_MODEL, D_MODEL))],
        out_specs=(_full((MEM_LEN, D_MODEL)),) * 2,
        compiler_params=_params(),
        name="mem_kv",
    )(mem, g, wk, wv)


_PREP_BF16 = ("at", "bonus", "rt", "bh", "kh", "bc", "kc", "v")
_PREP_NAMES = _PREP_BF16 + ("wl", "gds")


def _in_proj_pieces(x_ref, g_ref, w_ref, z_ref):
    h = _rmsnorm(x_ref[...], g_ref[...]).astype(BF16)
    yield
    for j in range(IN_WIDTH // IN_PROJ_COLS):
        cols = slice(j * IN_PROJ_COLS, (j + 1) * IN_PROJ_COLS)
        z_ref[Z_PAD:, cols] = jnp.dot(h, w_ref[:, cols], preferred_element_type=F32)
        yield


def _rwkv_prep_pieces(z_ref, mu_ref, w0_ref, waup_ref, a0_ref, kk_ref, ka_ref, rk_ref, prep,
                      chunks):
    seg01 = _seg01()
    lane = lax.broadcasted_iota(jnp.int32, (1, LORA_WA), 1)
    row = lax.broadcasted_iota(jnp.int32, (PREP_ROWS, 1), 0)
    ti = lax.broadcasted_iota(jnp.int32, (PREP_ROWS, PREP_ROWS), 0)
    tj = lax.broadcasted_iota(jnp.int32, (PREP_ROWS, PREP_ROWS), 1)
    ltri01 = ((tj <= ti) & ((ti >> 6) == (tj >> 6))).astype(BF16)
    chunks_per_scan_tile = TB_SCAN // CHUNK
    chunks_per_unit = PREP_ROWS // CHUNK
    half_w = RWKV_WIDTH // 2

    def shifted(c, cols):
        z = z_ref[Z_PAD + c * PREP_ROWS:Z_PAD + (c + 1) * PREP_ROWS, cols]
        before = z_ref[Z_PAD + c * PREP_ROWS - 1:Z_PAD + c * PREP_ROWS, cols]
        zprev = jnp.where(row == 0, before, pltpu.roll(z, 1, axis=0))
        return z + (zprev - z) * mu_ref[:, cols]

    for c in chunks:
        rows = slice(c * PREP_ROWS, (c + 1) * PREP_ROWS)
        wa_in = shifted(c, slice(3 * RWKV_WIDTH, 3 * RWKV_WIDTH + LORA_WA))
        wa_in = jnp.where(lane < LORA_WA // 2, jnp.tanh(wa_in), wa_in).astype(BF16)
        prep["gds"][rows, :] = jax.nn.sigmoid(
            shifted(c, slice(3 * RWKV_WIDTH + LORA_WA, RWKV_IN))).astype(BF16)
        for q in range(2):
            hc = slice(q * half_w, (q + 1) * half_w)
            r = shifted(c, hc)
            k = shifted(c, slice(RWKV_WIDTH + q * half_w, RWKV_WIDTH + (q + 1) * half_w))
            v = shifted(c, slice(2 * RWKV_WIDTH + q * half_w, 2 * RWKV_WIDTH + (q + 1) * half_w))
            prep["v"][rows, hc] = v.astype(BF16)
            kk = k * kk_ref[:, hc]
            kk_sq = _split_bf16(kk * kk, TERMS_HEAD_SUM)
            yield
            w_pre = w0_ref[:, hc] + jnp.dot(wa_in, waup_ref[:, hc], preferred_element_type=F32)
            a_pre = a0_ref[:, hc] + jnp.dot(
                wa_in, waup_ref[:, RWKV_WIDTH + q * half_w:RWKV_WIDTH + (q + 1) * half_w],
                preferred_element_type=F32)
            kk_ss = _head_sum_parts(kk_sq, seg01)
            yield
            a = jax.nn.sigmoid(a_pre)
            lw = jax.nn.sigmoid(w_pre) * (-EXP_M05 * LOG2_E)
            lw_parts = _split_bf16(lw, TERMS_DECAY_CUMSUM)
            kk = kk * lax.rsqrt(jnp.maximum(kk_ss, 1e-24))
            kmod = k * ((1.0 - ka_ref[:, hc]) + a * ka_ref[:, hc])
            kka = kk * a
            rkk = _split_bf16(r * kmod * rk_ref[:, hc], TERMS_HEAD_SUM)
            yield
            cs = _cumsum_rows(ltri01, lw_parts)
            prep["bonus"][rows, hc] = (_head_sum_parts(rkk, seg01) * v).astype(BF16)
            yield
            w_inv = jnp.exp2(-cs)
            w_last = [jnp.exp2(cs[(j + 1) * CHUNK - 1:(j + 1) * CHUNK, :])
                      for j in range(chunks_per_unit)]
            w_tail = jnp.concatenate(
                [w_last[j] * w_inv[j * CHUNK:(j + 1) * CHUNK] for j in range(chunks_per_unit)],
                axis=0)
            prep["rt"][rows, hc] = (r * jnp.exp2(cs)).astype(BF16)
            prep["at"][rows, hc] = (-kk * jnp.exp2(cs - lw)).astype(BF16)
            prep["bh"][rows, hc] = (kka * w_inv).astype(BF16)
            prep["kh"][rows, hc] = (kmod * w_inv).astype(BF16)
            prep["bc"][rows, hc] = (kka * w_tail).astype(BF16)
            prep["kc"][rows, hc] = (kmod * w_tail).astype(BF16)
            for j in range(chunks_per_unit):
                cq, cr = divmod(c * chunks_per_unit + j, chunks_per_scan_tile)
                prep["wl"][cq, cr:cr + 1, hc] = w_last[j]
            yield


def _copy_pieces(z_ref, o_ref, *, tm):
    for b in range(tm // SGU_BLOCK):
        o_ref[b * SGU_BLOCK:(b + 1) * SGU_BLOCK, :] = (
            z_ref[Z_PAD + b * SGU_BLOCK:Z_PAD + (b + 1) * SGU_BLOCK, RWKV_IN:])
        yield


def _front_kernel(x_ref, g1_ref, win_ref, mu_ref, w0_ref, waup_ref, a0_ref, kk_ref, ka_ref, rk_ref,
                  *rest, tm):
    n = len(_PREP_NAMES)
    prep = dict(zip(_PREP_NAMES, rest[:n]))
    zs_ref = rest[n]
    wbf_ref, z0_ref, z1_ref = rest[n + 1:]
    i = pl.program_id(0)

    @pl.when(i == 0)
    def _():
        z1_ref[...] = jnp.zeros_like(z1_ref)
        wbf_ref[...] = win_ref[...].astype(BF16)

    def step(z_write, z_read):
        n_chunks = tm // PREP_ROWS
        prep_pieces = [
            _rwkv_prep_pieces(z_read, mu_ref, w0_ref, waup_ref, a0_ref, kk_ref, ka_ref, rk_ref,
                              prep, range(k, n_chunks, PREP_STREAMS))
            for k in range(PREP_STREAMS)]
        n_dot = 1 + IN_WIDTH // IN_PROJ_COLS + 1
        _run_interleaved(
            [(_in_proj_pieces(x_ref, g1_ref, wbf_ref, z_write), n_dot)]
            + [(gen, PREP_PIECES_PER_CHUNK * n_chunks // PREP_STREAMS + 1) for gen in prep_pieces]
            + [(_copy_pieces(z_read, zs_ref, tm=tm), tm // SGU_BLOCK + 1)],
            rounds=n_dot)
        if TB_SCAN // CHUNK < WL_ROWS:
            prep["wl"][:, TB_SCAN // CHUNK:, :] = jnp.zeros(
                (tm // TB_SCAN, WL_ROWS - TB_SCAN // CHUNK, RWKV_WIDTH), F32)
        z_write[Z_PAD - 1:Z_PAD, :] = z_read[Z_PAD + tm - 1:Z_PAD + tm, :]

    @pl.when((i & 1) == 0)
    def _():
        step(z0_ref, z1_ref)

    @pl.when((i & 1) == 1)
    def _():
        step(z1_ref, z0_ref)


def _front(x, g1, w_in, mu, w0, waup, a0, k_k, k_a, r_k, tm):
    t = x.shape[0]
    n_tiles = t // tm
    vec = _full((1, RWKV_WIDTH))
    out_tile = lambda i: (jnp.maximum(i - 1, 0), 0)
    out_shapes = ([jax.ShapeDtypeStruct((t, RWKV_WIDTH), BF16)] * len(_PREP_BF16)
                  + [jax.ShapeDtypeStruct((t // TB_SCAN, WL_ROWS, RWKV_WIDTH), F32),
                     jax.ShapeDtypeStruct((t, GATE_LORA), BF16),
                     jax.ShapeDtypeStruct((t, 2 * SGU_WIDTH), F32)])
    out_specs = ([pl.BlockSpec((tm, RWKV_WIDTH), out_tile)] * len(_PREP_BF16)
                 + [pl.BlockSpec((tm // TB_SCAN, WL_ROWS, RWKV_WIDTH),
                                 lambda i: (jnp.maximum(i - 1, 0), 0, 0)),
                    pl.BlockSpec((tm, GATE_LORA), out_tile),
                    pl.BlockSpec((tm, 2 * SGU_WIDTH), out_tile)])
    return pl.pallas_call(
        functools.partial(_front_kernel, tm=tm),
        out_shape=tuple(out_shapes),
        grid=(n_tiles + 1,),
        in_specs=[pl.BlockSpec((tm, D_MODEL), lambda i: (jnp.minimum(i, n_tiles - 1), 0)),
                  _full((1, D_MODEL)), _full((D_MODEL, IN_WIDTH)), _full((1, RWKV_IN)), vec,
                  _full((LORA_WA, 2 * RWKV_WIDTH)), vec, vec, vec, vec],
        out_specs=tuple(out_specs),
        scratch_shapes=[pltpu.VMEM((D_MODEL, IN_WIDTH), BF16),
                        pltpu.VMEM((Z_PAD + tm, IN_WIDTH), F32),
                        pltpu.VMEM((Z_PAD + tm, IN_WIDTH), F32)],
        compiler_params=_params(),
        name="front",
    )(x, g1, w_in, mu, w0, waup, a0, k_k, k_a, r_k)


def _pair_masks():
    t = lax.broadcasted_iota(jnp.int32, (CHUNK, PAIR), 0)
    j = lax.broadcasted_iota(jnp.int32, (CHUNK, PAIR), 1) & (CHUNK - 1)
    strict = j < t
    incl = j <= t
    blk16 = (t >> 4) == (j >> 4)
    blk32 = (t >> 5) == (j >> 5)
    return strict, incl, blk16, blk32


def _bd(x, bd_mask):
    x = x.astype(BF16)
    return jnp.where(bd_mask, jnp.concatenate([x, x], axis=0), 0.0).astype(BF16)


def _staged(fn, items, parts=2):
    out = []
    n = len(items) // parts
    for k in range(parts):
        out += [fn(*item) for item in items[k * n:(k + 1) * n]]
        yield
    return out


def _unit_lower_inverse_minus_identity(a_list, masks, bd_mask):
    _, _, blk16, blk32 = masks
    ad = [jnp.where(blk16, a, 0.0) for a in a_list]
    ap = yield from _staged(lambda x: _mm(x, _bd(x, bd_mask)), [(x,) for x in ad])
    tp = ad
    for _ in range(2):
        both = yield from _staged(
            lambda p, t: _mm(p, jnp.concatenate([_bd(p, bd_mask), _bd(t, bd_mask)], axis=1)),
            list(zip(ap, tp)))
        tp = [t + p + b[:, PAIR:] for t, p, b in zip(tp, ap, both)]
        ap = [b[:, :PAIR] for b in both]
    last = yield from _staged(lambda p, t: _mm(p, _bd(t, bd_mask)), list(zip(ap, tp)))
    tp = [t + p + x for t, p, x in zip(tp, ap, last)]
    for off_mask in (blk32 & ~blk16, ~blk32):
        off = [jnp.where(off_mask, a, 0.0) for a in a_list]
        x = yield from _staged(lambda o, t: o + _mm(t, _bd(o, bd_mask)), list(zip(off, tp)))
        tp = yield from _staged(lambda t, xx: t + xx + _mm(xx, _bd(t, bd_mask)), list(zip(tp, x)))
    return tp


_TERM_STAGES = 11
_TERM_NAMES = ("achk", "uv", "bb")


def _bd_masks():
    bi = lax.broadcasted_iota(jnp.int32, (PAIR, PAIR), 0) >> 6
    bj = lax.broadcasted_iota(jnp.int32, (PAIR, PAIR), 1) >> 6
    bd1 = bi == bj
    return bd1, jnp.concatenate([bd1, bd1], axis=1)


def _scan_terms_pieces(rt_ref, at_ref, bh_ref, kh_ref, v_ref, terms, *, tb):
    masks = _pair_masks()
    strict, incl = masks[0], masks[1]
    bd1, bd2 = _bd_masks()
    probs = [(c, p) for c in range(tb // CHUNK) for p in range(N_PAIRS)]
    cut = lambda ref: [ref[c * CHUNK:(c + 1) * CHUNK, p * PAIR:(p + 1) * PAIR] for c, p in probs]
    rt_p, at_p, bh_p, kh_p, v_p = map(cut, (rt_ref, at_ref, bh_ref, kh_ref, v_ref))
    gram = yield from _staged(
        lambda a_, r_, b_, k_: _mm(jnp.concatenate([a_, r_], axis=0),
                                   jnp.concatenate([_bd(b_, bd1), _bd(k_, bd1)], axis=0), _NT),
        list(zip(at_p, rt_p, bh_p, kh_p)))
    a_ab = [jnp.where(strict, g_[:CHUNK, :PAIR], 0.0) for g_ in gram]
    a_ak = [jnp.where(strict, g_[:CHUNK, PAIR:], 0.0) for g_ in gram]
    incl2 = jnp.concatenate([incl, incl], axis=1)
    for i, g_ in enumerate(gram):
        terms["bb"][i] = jnp.where(incl2, g_[CHUNK:], 0.0).astype(BF16)
    rhs = yield from _staged(
        lambda m_, x_, a_: jnp.concatenate([_mm(m_, _bd(x_, bd1)), a_.astype(F32)], axis=1),
        list(zip(a_ak, v_p, at_p)))
    tp = yield from _unit_lower_inverse_minus_identity(a_ab, masks, bd1)
    sol = yield from _staged(lambda x_, t_: x_ + _mm(t_, _bd(x_, bd2)), list(zip(rhs, tp)))
    for i, x_ in enumerate(sol):
        terms["uv"][i] = x_[:, :PAIR]
        terms["achk"][i] = x_[:, PAIR:].astype(BF16)


def _scan_state_pieces(terms, rt_ref, v_ref, bc_ref, kc_ref, wl_ref, gds_ref, gup_ref, bonus_ref,
                       lnw_ref, lnb_ref, o_ref, s_ref, y_ref, *, tb):
    bd1, _ = _bd_masks()
    bd_upd = jnp.logical_and(bd1, pl.program_id(0) > 0)
    s = [s_ref[p] for p in range(N_PAIRS)]
    for c in range(tb // CHUNK):
        rows = slice(c * CHUNK, (c + 1) * CHUNK)
        lanes = [slice(p * PAIR, (p + 1) * PAIR) for p in range(N_PAIRS)]
        idx = [c * N_PAIRS + p for p in range(N_PAIRS)]
        on_s = [_mm(jnp.concatenate([terms["achk"][i], rt_ref[rows, lanes[p]].astype(BF16)], axis=0),
                    s[p], _NT) for p, i in enumerate(idx)]
        u = [x[:CHUNK] + terms["uv"][i] for x, i in zip(on_s, idx)]
        yield
        w_last = wl_ref[0, c:c + 1, :]
        upd = []
        for p, i in enumerate(idx):
            v_i = v_ref[rows, lanes[p]]
            y_ref[rows, lanes[p]] = (
                on_s[p][CHUNK:]
                + _mm(terms["bb"][i], jnp.concatenate([_bd(u[p], bd1), _bd(v_i, bd1)], axis=0)))
            upd.append(_mm(jnp.concatenate([u[p].astype(BF16), v_i], axis=0),
                           jnp.concatenate([bc_ref[rows, lanes[p]], kc_ref[rows, lanes[p]]], axis=0),
                           _TN))
        s = [s[p] * w_last[:, lanes[p]] + jnp.where(bd_upd, upd[p], 0.0) for p in range(N_PAIRS)]
        yield
    for p in range(N_PAIRS):
        s_ref[p] = s[p]
    seg01 = _seg01()
    y = y_ref[...]
    mean = _head_sum(y, seg01, TERMS_GROUP_MEAN) * (1.0 / RWKV_HEAD)
    yield
    d = y - mean
    var = _head_sum(d * d, seg01) * (1.0 / RWKV_HEAD)
    yield
    yn = d * lax.rsqrt(var + LNX_EPS) * lnw_ref[...] + lnb_ref[...]
    gate = jnp.dot(gds_ref[...], gup_ref[...], preferred_element_type=F32)
    o_ref[...] = ((yn + bonus_ref[...]) * gate).astype(BF16)


def _scan_kernel(rt_ref, at_ref, bh_ref, kh_ref, v_ref, rtp_ref, vp_ref, bcp_ref, kcp_ref, bonus_ref,
                 gds_ref, wl_ref, gup_ref, lnw_ref, lnb_ref, wg_ref, wu_ref, wd_ref, o_ref, wg_bf_ref,
                 wu_bf_ref, wd_bf_ref, s_ref, y_ref, *term_refs, tb):
    n = len(_TERM_NAMES)
    slots = [dict(zip(_TERM_NAMES, term_refs[k * n:(k + 1) * n])) for k in range(2)]
    i = pl.program_id(0)

    @pl.when(i == 0)
    def _():
        s_ref[...] = jnp.zeros_like(s_ref)
        for ref in slots[1].values():
            ref[...] = jnp.zeros_like(ref)

    wg_bf_ref[...] = wg_ref[...].astype(BF16)
    wu_bf_ref[...] = wu_ref[...].astype(BF16)
    wd_bf_ref[...] = wd_ref[...].astype(BF16)

    def step(write, read):
        n_terms = 2 * _TERM_STAGES + 1
        _run_interleaved(
            [(_scan_terms_pieces(rt_ref, at_ref, bh_ref, kh_ref, v_ref, write, tb=tb), n_terms),
             (_scan_state_pieces(read, rtp_ref, vp_ref, bcp_ref, kcp_ref, wl_ref, gds_ref, gup_ref,
                                 bonus_ref, lnw_ref, lnb_ref, o_ref, s_ref, y_ref, tb=tb),
              2 * (tb // CHUNK) + 3)],
            rounds=n_terms)

    @pl.when((i & 1) == 0)
    def _():
        step(slots[0], slots[1])

    @pl.when((i & 1) == 1)
    def _():
        step(slots[1], slots[0])


def _scan(prep, gup, lnw, lnb, w_gate, w_up, w_down, tb):
    t = prep["rt"].shape[0]
    n_tiles = t // tb
    assert tb // CHUNK <= WL_ROWS
    last = n_tiles - 1
    cur = pl.BlockSpec((tb, RWKV_WIDTH), lambda i: (jnp.minimum(i, last), 0))
    prev = pl.BlockSpec((tb, RWKV_WIDTH), lambda i: (jnp.maximum(i - 1, 0), 0))
    vec = _full((1, RWKV_WIDTH))
    assert D_MODEL % n_tiles == 0 and (D_MODEL // n_tiles) % 16 == 0
    assert n_tiles % 2 == 0 and D_FF % (n_tiles // 2) == 0 and (D_FF // (n_tiles // 2)) % 16 == 0
    up_rows = pl.BlockSpec((D_MODEL // n_tiles, D_FF), lambda i: (jnp.minimum(i, last), 0))
    down_blk = pl.BlockSpec((D_FF // (n_tiles // 2), D_MODEL // 2),
                            lambda i: (jnp.minimum(i, last) // 2, jnp.minimum(i, last) % 2))
    n_prob = (tb // CHUNK) * N_PAIRS
    term_shapes = [pltpu.VMEM((n_prob, CHUNK, PAIR), BF16), pltpu.VMEM((n_prob, CHUNK, PAIR), F32),
                   pltpu.VMEM((n_prob, CHUNK, 2 * PAIR), BF16)]
    return pl.pallas_call(
        functools.partial(_scan_kernel, tb=tb),
        out_shape=(jax.ShapeDtypeStruct((t, RWKV_WIDTH), BF16),
                   jax.ShapeDtypeStruct((D_MODEL, D_FF), BF16),
                   jax.ShapeDtypeStruct((D_MODEL, D_FF), BF16),
                   jax.ShapeDtypeStruct((D_FF, D_MODEL), BF16)),
        grid=(n_tiles + 1,),
        in_specs=[cur] * 5 + [prev] * 5
        + [pl.BlockSpec((tb, GATE_LORA), lambda i: (jnp.maximum(i - 1, 0), 0)),
           pl.BlockSpec((1, WL_ROWS, RWKV_WIDTH), lambda i: (jnp.maximum(i - 1, 0), 0, 0)),
           _full((GATE_LORA, RWKV_WIDTH)), vec, vec, up_rows, up_rows, down_blk],
        out_specs=(prev, up_rows, up_rows, down_blk),
        scratch_shapes=[pltpu.VMEM((N_PAIRS, PAIR, PAIR), F32), pltpu.VMEM((tb, RWKV_WIDTH), F32)]
        + term_shapes * 2,
        compiler_params=_params(),
        name="scan",
    )(prep["rt"], prep["at"], prep["bh"], prep["kh"], prep["v"],
      prep["rt"], prep["v"], prep["bc"], prep["kc"], prep["bonus"], prep["gds"], prep["wl"],
      gup, lnw, lnb, w_gate, w_up, w_down)


def _sgu_block_prepare(z, lnw, lnb, sel):
    hz = _gelu_tanh(z)
    u = hz[:, :SGU_WIDTH]
    vf = hz[:, SGU_WIDTH:]
    mu = jnp.mean(vf, axis=-1, keepdims=True)
    d = vf - mu
    var = jnp.mean(d * d, axis=-1, keepdims=True)
    vn = d * lax.rsqrt(var + LN_EPS) * lnw + lnb
    stacks = []
    for p in range(SGU_WIDTH // PAIR):
        vb = vn[:, p * PAIR:(p + 1) * PAIR]
        stacks.append(jnp.where(sel, jnp.concatenate([vb, vb], axis=0), 0.0).astype(BF16))
    return u, stacks


def _sgu_block_mix(u, stacks, wcat, bias):
    return jnp.concatenate(
        [u[:, p * PAIR:(p + 1) * PAIR]
         * (jnp.dot(wcat[p], stacks[p], preferred_element_type=F32) + bias[:, p * PAIR:(p + 1) * PAIR])
         for p in range(SGU_WIDTH // PAIR)], axis=1)


def _mix_attn_group(r, x_ref, yr_ref, zs_ref, slnw_ref, slnb_ref, sbias_ref, wo1_ref, wo2_ref,
                    g2_ref, wq_ref, k_ref, v_ref, wo_ref, o_ref, wcat, sel):
    heads = [slice(hd * XA_HEAD_DIM, (hd + 1) * XA_HEAD_DIM) for hd in range(XA_HEADS)]
    prepared = [_sgu_block_prepare(zs_ref[b:b + SGU_BLOCK, :], slnw_ref[...], slnb_ref[...], sel)
                for b in range(r.start, r.stop, SGU_BLOCK)]
    yield
    x1 = x_ref[r, :] + jnp.dot(yr_ref[r, :].astype(BF16), wo1_ref[...],
                               preferred_element_type=F32)
    y_sgu = jnp.concatenate([_sgu_block_mix(u, st, wcat, sbias_ref[...]) for u, st in prepared],
                            axis=0)
    yield
    x1 = x1 + jnp.dot(y_sgu.astype(BF16), wo2_ref[...], preferred_element_type=F32)
    yield
    h = _rmsnorm(x1, g2_ref[...]).astype(BF16)
    yield
    q = jnp.dot(h, wq_ref[...], preferred_element_type=F32).astype(BF16)
    s = [lax.dot_general(q[:, hl], k_ref[:, hl], _NT, preferred_element_type=F32)
         * (XA_HEAD_DIM ** -0.5) for hl in heads]
    yield
    p = []
    for s_h in s:
        e = jnp.exp(s_h - jnp.max(s_h, axis=-1, keepdims=True))
        p.append((e / jnp.sum(e, axis=-1, keepdims=True)).astype(BF16))
    yield
    o = jnp.concatenate([jnp.dot(p_h, v_ref[:, hl], preferred_element_type=F32)
                         for p_h, hl in zip(p, heads)], axis=1).astype(BF16)
    o_ref[r, :] = x1 + jnp.dot(o, wo_ref[...], preferred_element_type=F32)


def _mix_attn_kernel(x_ref, yr_ref, zs_ref, slnw_ref, slnb_ref, ws_ref, sbias_ref, wout_f32_ref,
                     g2_ref, wq_f32_ref, k_ref, v_ref, wo_f32_ref, o_ref, wout_ref, wq_ref, wo_ref):
    @pl.when(pl.program_id(0) == 0)
    def _():
        wout_ref[...] = wout_f32_ref[...].astype(BF16)
        wq_ref[...] = wq_f32_ref[...].astype(BF16)
        wo_ref[...] = wo_f32_ref[...].astype(BF16)

    wo1_ref = wout_ref.at[:RWKV_WIDTH]
    wo2_ref = wout_ref.at[RWKV_WIDTH:]
    tm = x_ref.shape[0]
    ti = lax.broadcasted_iota(jnp.int32, (SGU_BLOCK, SGU_BLOCK), 0)
    tj = lax.broadcasted_iota(jnp.int32, (SGU_BLOCK, SGU_BLOCK), 1)
    tril = tj <= ti
    wcat = [jnp.concatenate([jnp.where(tril, ws_ref[2 * p], 0.0),
                             jnp.where(tril, ws_ref[2 * p + 1], 0.0)], axis=1).astype(BF16)
            for p in range(SGU_WIDTH // PAIR)]
    bi = lax.broadcasted_iota(jnp.int32, (2 * SGU_BLOCK, PAIR), 0) >> 7
    bj = lax.broadcasted_iota(jnp.int32, (2 * SGU_BLOCK, PAIR), 1) >> 6
    sel = bi == bj
    gens = [_mix_attn_group(slice(r, r + ATTN_ROW_GROUP), x_ref, yr_ref, zs_ref, slnw_ref, slnb_ref,
                            sbias_ref, wo1_ref, wo2_ref, g2_ref, wq_ref, k_ref, v_ref, wo_ref,
                            o_ref, wcat, sel)
            for r in range(0, tm, ATTN_ROW_GROUP)]
    _run_wavefront(gens, ATTN_STAGGER)


def _mix_attn(x, yr, zs, slnw, slnb, ws, sbias, w_out, g2, wq, k, v, wo, tm):
    t = x.shape[0]
    sq = _full((D_MODEL, D_MODEL))
    return pl.pallas_call(
        _mix_attn_kernel,
        out_shape=jax.ShapeDtypeStruct((t, D_MODEL), F32),
        grid=(t // tm,),
        in_specs=[pl.BlockSpec((tm, D_MODEL), lambda i: (i, 0)),
                  pl.BlockSpec((tm, RWKV_WIDTH), lambda i: (i, 0)),
                  pl.BlockSpec((tm, 2 * SGU_WIDTH), lambda i: (i, 0)),
                  _full((1, SGU_WIDTH)), _full((1, SGU_WIDTH)),
                  _full((SGU_GROUPS, SGU_BLOCK, SGU_BLOCK)), _full((SGU_BLOCK, SGU_WIDTH)),
                  sq, _full((1, D_MODEL)), sq,
                  _full((MEM_LEN, D_MODEL)), _full((MEM_LEN, D_MODEL)), sq],
        out_specs=pl.BlockSpec((tm, D_MODEL), lambda i: (i, 0)),
        scratch_shapes=[pltpu.VMEM((D_MODEL, D_MODEL), BF16)] * 3,
        compiler_params=_params(),
        name="mix_attn",
    )(x, yr, zs, slnw, slnb, ws, sbias, w_out, g2, wq, k, v, wo)


def _ffn_kernel(x_ref, g3_ref, wg_ref, wu_ref, wd_ref, gf_ref, o_ref):
    tm = x_ref.shape[0]
    groups = [slice(r, r + FFN_ROW_GROUP) for r in range(0, tm, FFN_ROW_GROUP)]
    x2 = [x_ref[r, :] for r in groups]
    h = [_rmsnorm(x, g3_ref[...]).astype(BF16) for x in x2]
    gate = [jnp.dot(h_, wg_ref[...], preferred_element_type=F32) for h_ in h]
    up = [jnp.dot(h_, wu_ref[...], preferred_element_type=F32) for h_ in h]
    act = [(jax.nn.silu(g_) * u_).astype(BF16) for g_, u_ in zip(gate, up)]
    x3 = [x + jnp.dot(a_, wd_ref[...], preferred_element_type=F32) for x, a_ in zip(x2, act)]
    for r, x in zip(groups, x3):
        o_ref[r, :] = _rmsnorm(x, gf_ref[...])


def _ffn(x, g3, wg, wu, wd, gf, tm):
    t = x.shape[0]
    return pl.pallas_call(
        _ffn_kernel,
        out_shape=jax.ShapeDtypeStruct((t, D_MODEL), F32),
        grid=(t // tm,),
        in_specs=[pl.BlockSpec((tm, D_MODEL), lambda i: (i, 0)), _full((1, D_MODEL)),
                  _full((D_MODEL, D_FF)), _full((D_MODEL, D_FF)), _full((D_FF, D_MODEL)),
                  _full((1, D_MODEL))],
        out_specs=pl.BlockSpec((tm, D_MODEL), lambda i: (i, 0)),
        compiler_params=_params("parallel"),
        name="ffn",
    )(x, g3, wg, wu, wd, gf)


def kernel(x, mem, norm1_g, w_in, shift_mu, w0, w_lora_up, a0, a_lora_up, g_lora_up, k_k, k_a, r_k,
           lnx_w, lnx_b, sgu_ln_w, sgu_ln_b, w_spatial, b_spatial, w_out, norm2_g, mem_norm_g,
           wq_x, wk_x, wv_x, wo_x, norm3_g, w_gate, w_up, w_down, norm_f_g):
    b, t, _ = x.shape
    depth = w_in.shape[0]
    assert depth == 1, "the final RMSNorm is fused into the (single) layer's ffn call"
    assert t % TM_ATTN == 0 and t % TM_FFN == 0
    assert t % TM_DENSE == 0 and TM_DENSE % TB_SCAN == 0 and TB_SCAN % CHUNK == 0
    row = lambda p: p.reshape(1, -1)
    bf = lambda p: p.astype(BF16)
    outs = []
    for bi in range(b):
        xb = x[bi]
        for l in range(depth):
            lora = w_lora_up.shape[1]
            zeros = jnp.zeros((lora, RWKV_WIDTH), F32)
            waup = jnp.concatenate(
                [jnp.concatenate([w_lora_up[l], zeros], axis=1),
                 jnp.concatenate([zeros, a_lora_up[l]], axis=1)], axis=0)
            bias = jnp.repeat(b_spatial[l].T, SGU_WIDTH // SGU_GROUPS, axis=1)

            front = _front(xb, row(norm1_g[l]), w_in[l], row(shift_mu[l]), row(w0[l]), bf(waup),
                           row(a0[l]), row(k_k[l]), row(k_a[l]), row(r_k[l]), TM_DENSE)
            prep = dict(zip(_PREP_NAMES, front[:len(_PREP_NAMES)]))
            z_sgu = front[len(_PREP_NAMES)]
            y_rwkv, wg_bf, wu_bf, wd_bf = _scan(prep, bf(g_lora_up[l]), row(lnx_w[l]), row(lnx_b[l]),
                                                w_gate[l], w_up[l], w_down[l], TB_SCAN)
            k_mem, v_mem = _mem_kv(mem[bi], row(mem_norm_g[l]), wk_x[l], wv_x[l])
            x2 = _mix_attn(xb, y_rwkv, z_sgu, row(sgu_ln_w[l]), row(sgu_ln_b[l]), w_spatial[l], bias,
                           w_out[l], row(norm2_g[l]), wq_x[l], k_mem, v_mem, wo_x[l], TM_ATTN)
            xb = _ffn(x2, row(norm3_g[l]), wg_bf, wu_bf, wd_bf, row(norm_f_g), TM_FFN)
        outs.append(xb)
    return jnp.stack(outs, axis=0)
```

```python
import functools
import math

import jax
import jax.numpy as jnp
from jax import lax
from jax.experimental import pallas as pl
from jax.experimental.pallas import tpu as pltpu

F32 = jnp.float32
BF16 = jnp.bfloat16

D_MODEL = 1024
RWKV_WIDTH = 512
RWKV_HEAD = 64
LORA_WA = 128
GATE_LORA = 128
RWKV_IN = 3 * RWKV_WIDTH + LORA_WA + GATE_LORA
SGU_WIDTH = 512
SGU_GROUPS = 8
SGU_BLOCK = 128
IN_WIDTH = RWKV_IN + 2 * SGU_WIDTH
MEM_LEN = 256
XA_HEADS = 4
XA_HEAD_DIM = D_MODEL // XA_HEADS
D_FF = 2816
RMS_EPS = 1e-6
LN_EPS = 1e-5
LNX_EPS = 64e-5
EXP_M05 = 0.6065306597126334
LOG2_E = 1.4426950408889634

CHUNK = 64
PAIR = 2 * RWKV_HEAD
N_PAIRS = RWKV_WIDTH // PAIR
HEAD_BLOCK = 2 * PAIR
TM_DENSE = 512
TM_ATTN = 1024
TM_FFN = 1024
TB_SCAN = 512
Z_PAD = 8
WL_ROWS = 8
ATTN_ROW_GROUP = 256
ATTN_STAGGER = 2
FFN_ROW_GROUP = 256
IN_PROJ_COLS = 256
PREP_ROWS = 128
PREP_STREAMS = 4
PREP_PIECES_PER_CHUNK = 10
TERMS_DECAY_CUMSUM = 2
TERMS_HEAD_SUM = 1
TERMS_GROUP_MEAN = 2
VMEM_LIMIT = 56 * 1024 * 1024

_NN = (((1,), (0,)), ((), ()))
_NT = (((1,), (1,)), ((), ()))
_TN = (((0,), (0,)), ((), ()))


def _mm(a, b, dims=_NN):
    return lax.dot_general(a.astype(BF16), b.astype(BF16), dims, preferred_element_type=F32)


def _split_bf16(x, terms):
    parts = []
    rem = x
    for _ in range(terms):
        part = rem.astype(BF16)
        rem = rem - part.astype(F32)
        parts.append(part)
    return parts


def _cumsum_rows(ltri01, parts):
    return lax.dot_general(jnp.concatenate([ltri01] * len(parts), axis=1),
                           jnp.concatenate(parts, axis=0), _NN, preferred_element_type=F32)


def _head_sum_parts(parts, seg01):
    cols = []
    for q in range(parts[0].shape[1] // HEAD_BLOCK):
        acc = None
        for part in parts:
            d = lax.dot_general(part[:, HEAD_BLOCK * q:HEAD_BLOCK * (q + 1)], seg01, _NN,
                                preferred_element_type=F32)
            acc = d if acc is None else acc + d
        cols.append(acc)
    return jnp.concatenate(cols, axis=1)


def _head_sum(x, seg01, terms=TERMS_HEAD_SUM):
    return _head_sum_parts(_split_bf16(x, terms), seg01)


def _seg01():
    li = lax.broadcasted_iota(jnp.int32, (HEAD_BLOCK, HEAD_BLOCK), 0) >> 6
    lj = lax.broadcasted_iota(jnp.int32, (HEAD_BLOCK, HEAD_BLOCK), 1) >> 6
    return (li == lj).astype(BF16)


def _gelu_tanh(x):
    k1 = -2.0 * math.sqrt(2.0 / math.pi) * math.log2(math.e)
    return x / (1.0 + jnp.exp2(x * (k1 + (k1 * 0.044715) * (x * x))))


def _rmsnorm(x, g):
    return x * lax.rsqrt(jnp.mean(x * x, axis=-1, keepdims=True) + RMS_EPS) * g


def _full(shape):
    n = len(shape)
    return pl.BlockSpec(shape, lambda i: (0,) * n, pipeline_mode=pl.Buffered(1))


def _params(sem="arbitrary"):
    return pltpu.CompilerParams(dimension_semantics=(sem,), vmem_limit_bytes=VMEM_LIMIT)


def _run_interleaved(stages, rounds):
    for r in range(rounds):
        for gen, n in stages:
            for _ in range((r + 1) * n // rounds - r * n // rounds):
                next(gen, None)
    for gen, _ in stages:
        assert next(gen, StopIteration) is StopIteration, "piece count too small"


def _run_wavefront(gens, stagger):
    live = list(enumerate(gens))
    r = 0
    while live:
        for entry in list(live):
            g, gen = entry
            if r >= g * stagger and next(gen, StopIteration) is StopIteration:
                live.remove(entry)
        r += 1


def _mem_kv_kernel(mem_ref, g_ref, wk_ref, wv_ref, k_ref, v_ref):
    m = _rmsnorm(mem_ref[...], g_ref[...]).astype(BF16)
    k_ref[...] = jnp.dot(m, wk_ref[...].astype(BF16), preferred_element_type=F32).astype(BF16)
    v_ref[...] = jnp.dot(m, wv_ref[...].astype(BF16), preferred_element_type=F32).astype(BF16)


def _mem_kv(mem, g, wk, wv):
    return pl.pallas_call(
        _mem_kv_kernel,
        out_shape=(jax.ShapeDtypeStruct((MEM_LEN, D_MODEL), BF16),) * 2,
        grid=(1,),
        in_specs=[_full((MEM_LEN, D_MODEL)), _full((1, D_MODEL)),
                  _full((D_MODEL, D_MODEL)), _full((D_MODEL, D_MODEL))],
        out_specs=(_full((MEM_LEN, D_MODEL)),) * 2,
        compiler_params=_params(),
        name="mem_kv",
    )(mem, g, wk, wv)


_PREP_BF16 = ("at", "bonus", "rt", "bh", "kh", "v")
_PREP_NAMES = _PREP_BF16 + ("wl", "gds")


def _in_proj_pieces(x_ref, g_ref, w_ref, z_ref):
    h = _rmsnorm(x_ref[...], g_ref[...]).astype(BF16)
    yield
    for j in range(IN_WIDTH // IN_PROJ_COLS):
        cols = slice(j * IN_PROJ_COLS, (j + 1) * IN_PROJ_COLS)
        z_ref[Z_PAD:, cols] = jnp.dot(h, w_ref[:, cols], preferred_element_type=F32)
        yield


def _rwkv_prep_pieces(z_ref, mu_ref, w0_ref, waup_ref, a0_ref, kk_ref, ka_ref, rk_ref, prep,
                      chunks):
    seg01 = _seg01()
    lane = lax.broadcasted_iota(jnp.int32, (1, LORA_WA), 1)
    row = lax.broadcasted_iota(jnp.int32, (PREP_ROWS, 1), 0)
    ti = lax.broadcasted_iota(jnp.int32, (PREP_ROWS, PREP_ROWS), 0)
    tj = lax.broadcasted_iota(jnp.int32, (PREP_ROWS, PREP_ROWS), 1)
    ltri01 = ((tj <= ti) & ((ti >> 6) == (tj >> 6))).astype(BF16)
    chunks_per_scan_tile = TB_SCAN // CHUNK
    chunks_per_unit = PREP_ROWS // CHUNK
    half_w = RWKV_WIDTH // 2

    def shifted(c, cols):
        z = z_ref[Z_PAD + c * PREP_ROWS:Z_PAD + (c + 1) * PREP_ROWS, cols]
        before = z_ref[Z_PAD + c * PREP_ROWS - 1:Z_PAD + c * PREP_ROWS, cols]
        zprev = jnp.where(row == 0, before, pltpu.roll(z, 1, axis=0))
        return z + (zprev - z) * mu_ref[:, cols]

    for c in chunks:
        rows = slice(c * PREP_ROWS, (c + 1) * PREP_ROWS)
        wa_in = shifted(c, slice(3 * RWKV_WIDTH, 3 * RWKV_WIDTH + LORA_WA))
        wa_in = jnp.where(lane < LORA_WA // 2, jnp.tanh(wa_in), wa_in).astype(BF16)
        prep["gds"][rows, :] = jax.nn.sigmoid(
            shifted(c, slice(3 * RWKV_WIDTH + LORA_WA, RWKV_IN))).astype(BF16)
        for q in range(2):
            hc = slice(q * half_w, (q + 1) * half_w)
            r = shifted(c, hc)
            k = shifted(c, slice(RWKV_WIDTH + q * half_w, RWKV_WIDTH + (q + 1) * half_w))
            v = shifted(c, slice(2 * RWKV_WIDTH + q * half_w, 2 * RWKV_WIDTH + (q + 1) * half_w))
            prep["v"][rows, hc] = v.astype(BF16)
            kk = k * kk_ref[:, hc]
            kk_sq = _split_bf16(kk * kk, TERMS_HEAD_SUM)
            yield
            w_pre = w0_ref[:, hc] + jnp.dot(wa_in, waup_ref[:, hc], preferred_element_type=F32)
            a_pre = a0_ref[:, hc] + jnp.dot(
                wa_in, waup_ref[:, RWKV_WIDTH + q * half_w:RWKV_WIDTH + (q + 1) * half_w],
                preferred_element_type=F32)
            kk_ss = _head_sum_parts(kk_sq, seg01)
            yield
            a = jax.nn.sigmoid(a_pre)
            lw = jax.nn.sigmoid(w_pre) * (-EXP_M05 * LOG2_E)
            lw_parts = _split_bf16(lw, TERMS_DECAY_CUMSUM)
            kk = kk * lax.rsqrt(jnp.maximum(kk_ss, 1e-24))
            kmod = k * ((1.0 - ka_ref[:, hc]) + a * ka_ref[:, hc])
            kka = kk * a
            rkk = _split_bf16(r * kmod * rk_ref[:, hc], TERMS_HEAD_SUM)
            yield
            cs = _cumsum_rows(ltri01, lw_parts)
            prep["bonus"][rows, hc] = (_head_sum_parts(rkk, seg01) * v).astype(BF16)
            yield
            w_inv = jnp.exp2(-cs)
            w_last = [jnp.exp2(cs[(j + 1) * CHUNK - 1:(j + 1) * CHUNK, :])
                      for j in range(chunks_per_unit)]
            prep["rt"][rows, hc] = (r * jnp.exp2(cs)).astype(BF16)
            prep["at"][rows, hc] = (-kk * jnp.exp2(cs - lw)).astype(BF16)
            prep["bh"][rows, hc] = (kka * w_inv).astype(BF16)
            prep["kh"][rows, hc] = (kmod * w_inv).astype(BF16)
            for j in range(chunks_per_unit):
                cq, cr = divmod(c * chunks_per_unit + j, chunks_per_scan_tile)
                prep["wl"][cq, cr:cr + 1, hc] = w_last[j]
            yield


def _copy_pieces(z_ref, o_ref, *, tm):
    for b in range(tm // SGU_BLOCK):
        o_ref[b * SGU_BLOCK:(b + 1) * SGU_BLOCK, :] = (
            z_ref[Z_PAD + b * SGU_BLOCK:Z_PAD + (b + 1) * SGU_BLOCK, RWKV_IN:])
        yield


def _front_kernel(x_ref, g1_ref, win_ref, mu_ref, w0_ref, waup_ref, a0_ref, kk_ref, ka_ref, rk_ref,
                  *rest, tm):
    n = len(_PREP_NAMES)
    prep = dict(zip(_PREP_NAMES, rest[:n]))
    zs_ref = rest[n]
    wbf_ref, z0_ref, z1_ref = rest[n + 1:]
    i = pl.program_id(0)

    @pl.when(i == 0)
    def _():
        z1_ref[...] = jnp.zeros_like(z1_ref)
        wbf_ref[...] = win_ref[...].astype(BF16)

    def step(z_write, z_read):
        n_chunks = tm // PREP_ROWS
        prep_pieces = [
            _rwkv_prep_pieces(z_read, mu_ref, w0_ref, waup_ref, a0_ref, kk_ref, ka_ref, rk_ref,
                              prep, range(k, n_chunks, PREP_STREAMS))
            for k in range(PREP_STREAMS)]
        n_dot = 1 + IN_WIDTH // IN_PROJ_COLS + 1
        _run_interleaved(
            [(_in_proj_pieces(x_ref, g1_ref, wbf_ref, z_write), n_dot)]
            + [(gen, PREP_PIECES_PER_CHUNK * n_chunks // PREP_STREAMS + 1) for gen in prep_pieces]
            + [(_copy_pieces(z_read, zs_ref, tm=tm), tm // SGU_BLOCK + 1)],
            rounds=n_dot)
        if TB_SCAN // CHUNK < WL_ROWS:
            prep["wl"][:, TB_SCAN // CHUNK:, :] = jnp.zeros(
                (tm // TB_SCAN, WL_ROWS - TB_SCAN // CHUNK, RWKV_WIDTH), F32)
        z_write[Z_PAD - 1:Z_PAD, :] = z_read[Z_PAD + tm - 1:Z_PAD + tm, :]

    @pl.when((i & 1) == 0)
    def _():
        step(z0_ref, z1_ref)

    @pl.when((i & 1) == 1)
    def _():
        step(z1_ref, z0_ref)


def _front(x, g1, w_in, mu, w0, waup, a0, k_k, k_a, r_k, tm):
    t = x.shape[0]
    n_tiles = t // tm
    vec = _full((1, RWKV_WIDTH))
    out_tile = lambda i: (jnp.maximum(i - 1, 0), 0)
    out_shapes = ([jax.ShapeDtypeStruct((t, RWKV_WIDTH), BF16)] * len(_PREP_BF16)
                  + [jax.ShapeDtypeStruct((t // TB_SCAN, WL_ROWS, RWKV_WIDTH), F32),
                     jax.ShapeDtypeStruct((t, GATE_LORA), BF16),
                     jax.ShapeDtypeStruct((t, 2 * SGU_WIDTH), F32)])
    out_specs = ([pl.BlockSpec((tm, RWKV_WIDTH), out_tile)] * len(_PREP_BF16)
                 + [pl.BlockSpec((tm // TB_SCAN, WL_ROWS, RWKV_WIDTH),
                                 lambda i: (jnp.maximum(i - 1, 0), 0, 0)),
                    pl.BlockSpec((tm, GATE_LORA), out_tile),
                    pl.BlockSpec((tm, 2 * SGU_WIDTH), out_tile)])
    return pl.pallas_call(
        functools.partial(_front_kernel, tm=tm),
        out_shape=tuple(out_shapes),
        grid=(n_tiles + 1,),
        in_specs=[pl.BlockSpec((tm, D_MODEL), lambda i: (jnp.minimum(i, n_tiles - 1), 0)),
                  _full((1, D_MODEL)), _full((D_MODEL, IN_WIDTH)), _full((1, RWKV_IN)), vec,
                  _full((LORA_WA, 2 * RWKV_WIDTH)), vec, vec, vec, vec],
        out_specs=tuple(out_specs),
        scratch_shapes=[pltpu.VMEM((D_MODEL, IN_WIDTH), BF16),
                        pltpu.VMEM((Z_PAD + tm, IN_WIDTH), F32),
                        pltpu.VMEM((Z_PAD + tm, IN_WIDTH), F32)],
        compiler_params=_params(),
        name="front",
    )(x, g1, w_in, mu, w0, waup, a0, k_k, k_a, r_k)


def _pair_masks():
    t = lax.broadcasted_iota(jnp.int32, (CHUNK, PAIR), 0)
    j = lax.broadcasted_iota(jnp.int32, (CHUNK, PAIR), 1) & (CHUNK - 1)
    strict = j < t
    incl = j <= t
    blk16 = (t >> 4) == (j >> 4)
    blk32 = (t >> 5) == (j >> 5)
    return strict, incl, blk16, blk32


def _bd(x, bd_mask):
    x = x.astype(BF16)
    return jnp.where(bd_mask, jnp.concatenate([x, x], axis=0), 0.0).astype(BF16)


def _staged(fn, items, parts=2):
    out = []
    n = len(items) // parts
    for k in range(parts):
        out += [fn(*item) for item in items[k * n:(k + 1) * n]]
        yield
    return out


def _unit_lower_inverse_minus_identity(a_list, masks, bd_mask):
    _, _, blk16, blk32 = masks
    ad = [jnp.where(blk16, a, 0.0) for a in a_list]
    ap = yield from _staged(lambda x: _mm(x, _bd(x, bd_mask)), [(x,) for x in ad])
    tp = ad
    for _ in range(2):
        both = yield from _staged(
            lambda p, t: _mm(p, jnp.concatenate([_bd(p, bd_mask), _bd(t, bd_mask)], axis=1)),
            list(zip(ap, tp)))
        tp = [t + p + b[:, PAIR:] for t, p, b in zip(tp, ap, both)]
        ap = [b[:, :PAIR] for b in both]
    last = yield from _staged(lambda p, t: _mm(p, _bd(t, bd_mask)), list(zip(ap, tp)))
    tp = [t + p + x for t, p, x in zip(tp, ap, last)]
    for off_mask in (blk32 & ~blk16, ~blk32):
        off = [jnp.where(off_mask, a, 0.0) for a in a_list]
        x = yield from _staged(lambda o, t: o + _mm(t, _bd(o, bd_mask)), list(zip(off, tp)))
        tp = yield from _staged(lambda t, xx: t + xx + _mm(xx, _bd(t, bd_mask)), list(zip(tp, x)))
    return tp


_TERM_STAGES = 11
_TERM_NAMES = ("achk", "uv", "bb")


def _bd_masks():
    bi = lax.broadcasted_iota(jnp.int32, (PAIR, PAIR), 0) >> 6
    bj = lax.broadcasted_iota(jnp.int32, (PAIR, PAIR), 1) >> 6
    bd1 = bi == bj
    return bd1, jnp.concatenate([bd1, bd1], axis=1)


def _scan_terms_pieces(rt_ref, at_ref, bh_ref, kh_ref, v_ref, terms, *, tb):
    masks = _pair_masks()
    strict, incl = masks[0], masks[1]
    bd1, bd2 = _bd_masks()
    probs = [(c, p) for c in range(tb // CHUNK) for p in range(N_PAIRS)]
    cut = lambda ref: [ref[c * CHUNK:(c + 1) * CHUNK, p * PAIR:(p + 1) * PAIR] for c, p in probs]
    rt_p, at_p, bh_p, kh_p, v_p = map(cut, (rt_ref, at_ref, bh_ref, kh_ref, v_ref))
    gram = yield from _staged(
        lambda a_, r_, b_, k_: _mm(jnp.concatenate([a_, r_], axis=0),
                                   jnp.concatenate([_bd(b_, bd1), _bd(k_, bd1)], axis=0), _NT),
        list(zip(at_p, rt_p, bh_p, kh_p)))
    a_ab = [jnp.where(strict, g_[:CHUNK, :PAIR], 0.0) for g_ in gram]
    a_ak = [jnp.where(strict, g_[:CHUNK, PAIR:], 0.0) for g_ in gram]
    incl2 = jnp.concatenate([incl, incl], axis=1)
    for i, g_ in enumerate(gram):
        terms["bb"][i] = jnp.where(incl2, g_[CHUNK:], 0.0).astype(BF16)
    rhs = yield from _staged(
        lambda m_, x_, a_: jnp.concatenate([_mm(m_, _bd(x_, bd1)), a_.astype(F32)], axis=1),
        list(zip(a_ak, v_p, at_p)))
    tp = yield from _unit_lower_inverse_minus_identity(a_ab, masks, bd1)
    sol = yield from _staged(lambda x_, t_: x_ + _mm(t_, _bd(x_, bd2)), list(zip(rhs, tp)))
    for i, x_ in enumerate(sol):
        terms["uv"][i] = x_[:, :PAIR]
        terms["achk"][i] = x_[:, PAIR:].astype(BF16)


def _scan_state_pieces(terms, rt_ref, v_ref, bh_ref, kh_ref, wl_ref, gds_ref, gup_ref, bonus_ref,
                       lnw_ref, lnb_ref, o_ref, s_ref, y_ref, *, tb):
    bd1, _ = _bd_masks()
    bd_upd = jnp.logical_and(bd1, pl.program_id(0) > 0)
    s = [s_ref[p] for p in range(N_PAIRS)]
    for c in range(tb // CHUNK):
        rows = slice(c * CHUNK, (c + 1) * CHUNK)
        lanes = [slice(p * PAIR, (p + 1) * PAIR) for p in range(N_PAIRS)]
        idx = [c * N_PAIRS + p for p in range(N_PAIRS)]
        on_s = [_mm(jnp.concatenate([terms["achk"][i], rt_ref[rows, lanes[p]]], axis=0),
                    s[p], _NT) for p, i in enumerate(idx)]
        u = [x[:CHUNK] + terms["uv"][i] for x, i in zip(on_s, idx)]
        yield
        w_last = wl_ref[0, c:c + 1, :]
        upd = []
        for p, i in enumerate(idx):
            v_i = v_ref[rows, lanes[p]]
            y_ref[rows, lanes[p]] = (
                on_s[p][CHUNK:]
                + _mm(terms["bb"][i], jnp.concatenate([_bd(u[p], bd1), _bd(v_i, bd1)], axis=0)))
            upd.append(_mm(jnp.concatenate([u[p].astype(BF16), v_i], axis=0),
                           jnp.concatenate([bh_ref[rows, lanes[p]], kh_ref[rows, lanes[p]]], axis=0),
                           _TN))
        s = [s[p] * w_last[:, lanes[p]] + jnp.where(bd_upd, upd[p], 0.0) * w_last[:, lanes[p]]
             for p in range(N_PAIRS)]
        yield
    for p in range(N_PAIRS):
        s_ref[p] = s[p]
    seg01 = _seg01()
    y = y_ref[...]
    mean = _head_sum(y, seg01, TERMS_GROUP_MEAN) * (1.0 / RWKV_HEAD)
    yield
    d = y - mean
    var = _head_sum(d * d, seg01) * (1.0 / RWKV_HEAD)
    yield
    yn = d * lax.rsqrt(var + LNX_EPS) * lnw_ref[...] + lnb_ref[...]
    gate = jnp.dot(gds_ref[...], gup_ref[...], preferred_element_type=F32)
    o_ref[...] = ((yn + bonus_ref[...]) * gate).astype(BF16)


def _scan_kernel(rt_ref, at_ref, bh_ref, kh_ref, v_ref, rtp_ref, vp_ref, bhp_ref, khp_ref, bonus_ref,
                 gds_ref, wl_ref, gup_ref, lnw_ref, lnb_ref, wg_ref, wu_ref, wd_ref, o_ref, wg_bf_ref,
                 wu_bf_ref, wd_bf_ref, s_ref, y_ref, *term_refs, tb):
    n = len(_TERM_NAMES)
    slots = [dict(zip(_TERM_NAMES, term_refs[k * n:(k + 1) * n])) for k in range(2)]
    i = pl.program_id(0)

    @pl.when(i == 0)
    def _():
        s_ref[...] = jnp.zeros_like(s_ref)
        for ref in slots[1].values():
            ref[...] = jnp.zeros_like(ref)

    wg_bf_ref[...] = wg_ref[...].astype(BF16)
    wu_bf_ref[...] = wu_ref[...].astype(BF16)
    wd_bf_ref[...] = wd_ref[...].astype(BF16)

    def step(write, read):
        n_terms = 2 * _TERM_STAGES + 1
        _run_interleaved(
            [(_scan_terms_pieces(rt_ref, at_ref, bh_ref, kh_ref, v_ref, write, tb=tb), n_terms),
             (_scan_state_pieces(read, rtp_ref, vp_ref, bhp_ref, khp_ref, wl_ref, gds_ref, gup_ref,
                                 bonus_ref, lnw_ref, lnb_ref, o_ref, s_ref, y_ref, tb=tb),
              2 * (tb // CHUNK) + 3)],
            rounds=n_terms)

    @pl.when((i & 1) == 0)
    def _():
        step(slots[0], slots[1])

    @pl.when((i & 1) == 1)
    def _():
        step(slots[1], slots[0])


def _scan(prep, gup, lnw, lnb, w_gate, w_up, w_down, tb):
    t = prep["rt"].shape[0]
    n_tiles = t // tb
    assert tb // CHUNK <= WL_ROWS
    last = n_tiles - 1
    cur = pl.BlockSpec((tb, RWKV_WIDTH), lambda i: (jnp.minimum(i, last), 0))
    prev = pl.BlockSpec((tb, RWKV_WIDTH), lambda i: (jnp.maximum(i - 1, 0), 0))
    vec = _full((1, RWKV_WIDTH))
    assert D_MODEL % n_tiles == 0 and (D_MODEL // n_tiles) % 16 == 0
    assert n_tiles % 2 == 0 and D_FF % (n_tiles // 2) == 0 and (D_FF // (n_tiles // 2)) % 16 == 0
    up_rows = pl.BlockSpec((D_MODEL // n_tiles, D_FF), lambda i: (jnp.minimum(i, last), 0))
    down_blk = pl.BlockSpec((D_FF // (n_tiles // 2), D_MODEL // 2),
                            lambda i: (jnp.minimum(i, last) // 2, jnp.minimum(i, last) % 2))
    n_prob = (tb // CHUNK) * N_PAIRS
    term_shapes = [pltpu.VMEM((n_prob, CHUNK, PAIR), BF16), pltpu.VMEM((n_prob, CHUNK, PAIR), F32),
                   pltpu.VMEM((n_prob, CHUNK, 2 * PAIR), BF16)]
    return pl.pallas_call(
        functools.partial(_scan_kernel, tb=tb),
        out_shape=(jax.ShapeDtypeStruct((t, RWKV_WIDTH), BF16),
                   jax.ShapeDtypeStruct((D_MODEL, D_FF), BF16),
                   jax.ShapeDtypeStruct((D_MODEL, D_FF), BF16),
                   jax.ShapeDtypeStruct((D_FF, D_MODEL), BF16)),
        grid=(n_tiles + 1,),
        in_specs=[cur] * 5 + [prev] * 5
        + [pl.BlockSpec((tb, GATE_LORA), lambda i: (jnp.maximum(i - 1, 0), 0)),
           pl.BlockSpec((1, WL_ROWS, RWKV_WIDTH), lambda i: (jnp.maximum(i - 1, 0), 0, 0)),
           _full((GATE_LORA, RWKV_WIDTH)), vec, vec, up_rows, up_rows, down_blk],
        out_specs=(prev, up_rows, up_rows, down_blk),
        scratch_shapes=[pltpu.VMEM((N_PAIRS, PAIR, PAIR), F32), pltpu.VMEM((tb, RWKV_WIDTH), F32)]
        + term_shapes * 2,
        compiler_params=_params(),
        name="scan",
    )(prep["rt"], prep["at"], prep["bh"], prep["kh"], prep["v"],
      prep["rt"], prep["v"], prep["bh"], prep["kh"], prep["bonus"], prep["gds"], prep["wl"],
      gup, lnw, lnb, w_gate, w_up, w_down)


def _sgu_block_prepare(z, lnw, lnb, sel):
    hz = _gelu_tanh(z)
    u = hz[:, :SGU_WIDTH]
    vf = hz[:, SGU_WIDTH:]
    mu = jnp.mean(vf, axis=-1, keepdims=True)
    d = vf - mu
    var = jnp.mean(d * d, axis=-1, keepdims=True)
    vn = d * lax.rsqrt(var + LN_EPS) * lnw + lnb
    stacks = []
    for p in range(SGU_WIDTH // PAIR):
        vb = vn[:, p * PAIR:(p + 1) * PAIR]
        stacks.append(jnp.where(sel, jnp.concatenate([vb, vb], axis=0), 0.0).astype(BF16))
    return u, stacks


def _sgu_block_mix(u, stacks, wcat, bias):
    return jnp.concatenate(
        [u[:, p * PAIR:(p + 1) * PAIR]
         * (jnp.dot(wcat[p], stacks[p], preferred_element_type=F32) + bias[:, p * PAIR:(p + 1) * PAIR])
         for p in range(SGU_WIDTH // PAIR)], axis=1)


def _mix_attn_group(r, x_ref, yr_ref, zs_ref, slnw_ref, slnb_ref, sbias_ref, wo1_ref, wo2_ref,
                    g2_ref, wq_ref, k_ref, v_ref, wo_ref, o_ref, wcat, sel):
    heads = [slice(hd * XA_HEAD_DIM, (hd + 1) * XA_HEAD_DIM) for hd in range(XA_HEADS)]
    prepared = [_sgu_block_prepare(zs_ref[b:b + SGU_BLOCK, :], slnw_ref[...], slnb_ref[...], sel)
                for b in range(r.start, r.stop, SGU_BLOCK)]
    yield
    x1 = x_ref[r, :] + jnp.dot(yr_ref[r, :].astype(BF16), wo1_ref[...],
                               preferred_element_type=F32)
    y_sgu = jnp.concatenate([_sgu_block_mix(u, st, wcat, sbias_ref[...]) for u, st in prepared],
                            axis=0)
    yield
    x1 = x1 + jnp.dot(y_sgu.astype(BF16), wo2_ref[...], preferred_element_type=F32)
    yield
    h = _rmsnorm(x1, g2_ref[...]).astype(BF16)
    yield
    q = jnp.dot(h, wq_ref[...], preferred_element_type=F32).astype(BF16)
    s = [lax.dot_general(q[:, hl], k_ref[:, hl], _NT, preferred_element_type=F32)
         * (XA_HEAD_DIM ** -0.5) for hl in heads]
    yield
    p = []
    for s_h in s:
        e = jnp.exp(s_h - jnp.max(s_h, axis=-1, keepdims=True))
        p.append((e / jnp.sum(e, axis=-1, keepdims=True)).astype(BF16))
    yield
    o = jnp.concatenate([jnp.dot(p_h, v_ref[:, hl], preferred_element_type=F32)
                         for p_h, hl in zip(p, heads)], axis=1).astype(BF16)
    o_ref[r, :] = x1 + jnp.dot(o, wo_ref[...], preferred_element_type=F32)


def _mix_attn_kernel(x_ref, yr_ref, zs_ref, slnw_ref, slnb_ref, ws_ref, sbias_ref, wout_f32_ref,
                     g2_ref, wq_f32_ref, k_ref, v_ref, wo_f32_ref, o_ref, wout_ref, wq_ref, wo_ref):
    @pl.when(pl.program_id(0) == 0)
    def _():
        wout_ref[...] = wout_f32_ref[...].astype(BF16)
        wq_ref[...] = wq_f32_ref[...].astype(BF16)
        wo_ref[...] = wo_f32_ref[...].astype(BF16)

    wo1_ref = wout_ref.at[:RWKV_WIDTH]
    wo2_ref = wout_ref.at[RWKV_WIDTH:]
    tm = x_ref.shape[0]
    ti = lax.broadcasted_iota(jnp.int32, (SGU_BLOCK, SGU_BLOCK), 0)
    tj = lax.broadcasted_iota(jnp.int32, (SGU_BLOCK, SGU_BLOCK), 1)
    tril = tj <= ti
    wcat = [jnp.concatenate([jnp.where(tril, ws_ref[2 * p], 0.0),
                             jnp.where(tril, ws_ref[2 * p + 1], 0.0)], axis=1).astype(BF16)
            for p in range(SGU_WIDTH // PAIR)]
    bi = lax.broadcasted_iota(jnp.int32, (2 * SGU_BLOCK, PAIR), 0) >> 7
    bj = lax.broadcasted_iota(jnp.int32, (2 * SGU_BLOCK, PAIR), 1) >> 6
    sel = bi == bj
    gens = [_mix_attn_group(slice(r, r + ATTN_ROW_GROUP), x_ref, yr_ref, zs_ref, slnw_ref, slnb_ref,
                            sbias_ref, wo1_ref, wo2_ref, g2_ref, wq_ref, k_ref, v_ref, wo_ref,
                            o_ref, wcat, sel)
            for r in range(0, tm, ATTN_ROW_GROUP)]
    _run_wavefront(gens, ATTN_STAGGER)


def _mix_attn(x, yr, zs, slnw, slnb, ws, sbias, w_out, g2, wq, k, v, wo, tm):
    t = x.shape[0]
    sq = _full((D_MODEL, D_MODEL))
    return pl.pallas_call(
        _mix_attn_kernel,
        out_shape=jax.ShapeDtypeStruct((t, D_MODEL), F32),
        grid=(t // tm,),
        in_specs=[pl.BlockSpec((tm, D_MODEL), lambda i: (i, 0)),
                  pl.BlockSpec((tm, RWKV_WIDTH), lambda i: (i, 0)),
                  pl.BlockSpec((tm, 2 * SGU_WIDTH), lambda i: (i, 0)),
                  _full((1, SGU_WIDTH)), _full((1, SGU_WIDTH)),
                  _full((SGU_GROUPS, SGU_BLOCK, SGU_BLOCK)), _full((SGU_BLOCK, SGU_WIDTH)),
                  sq, _full((1, D_MODEL)), sq,
                  _full((MEM_LEN, D_MODEL)), _full((MEM_LEN, D_MODEL)), sq],
        out_specs=pl.BlockSpec((tm, D_MODEL), lambda i: (i, 0)),
        scratch_shapes=[pltpu.VMEM((D_MODEL, D_MODEL), BF16)] * 3,
        compiler_params=_params(),
        name="mix_attn",
    )(x, yr, zs, slnw, slnb, ws, sbias, w_out, g2, wq, k, v, wo)


def _ffn_kernel(x_ref, g3_ref, wg_ref, wu_ref, wd_ref, gf_ref, o_ref):
    tm = x_ref.shape[0]
    groups = [slice(r, r + FFN_ROW_GROUP) for r in range(0, tm, FFN_ROW_GROUP)]
    x2 = [x_ref[r, :] for r in groups]
    h = [_rmsnorm(x, g3_ref[...]).astype(BF16) for x in x2]
    gate = [jnp.dot(h_, wg_ref[...], preferred_element_type=F32) for h_ in h]
    up = [jnp.dot(h_, wu_ref[...], preferred_element_type=F32) for h_ in h]
    act = [(jax.nn.silu(g_) * u_).astype(BF16) for g_, u_ in zip(gate, up)]
    x3 = [x + jnp.dot(a_, wd_ref[...], preferred_element_type=F32) for x, a_ in zip(x2, act)]
    for r, x in zip(groups, x3):
        o_ref[r, :] = _rmsnorm(x, gf_ref[...])


def _ffn(x, g3, wg, wu, wd, gf, tm):
    t = x.shape[0]
    return pl.pallas_call(
        _ffn_kernel,
        out_shape=jax.ShapeDtypeStruct((t, D_MODEL), F32),
        grid=(t // tm,),
        in_specs=[pl.BlockSpec((tm, D_MODEL), lambda i: (i, 0)), _full((1, D_MODEL)),
                  _full((D_MODEL, D_FF)), _full((D_MODEL, D_FF)), _full((D_FF, D_MODEL)),
                  _full((1, D_MODEL))],
        out_specs=pl.BlockSpec((tm, D_MODEL), lambda i: (i, 0)),
        compiler_params=_params("parallel"),
        name="ffn",
    )(x, g3, wg, wu, wd, gf)


def kernel(x, mem, norm1_g, w_in, shift_mu, w0, w_lora_up, a0, a_lora_up, g_lora_up, k_k, k_a, r_k,
           lnx_w, lnx_b, sgu_ln_w, sgu_ln_b, w_spatial, b_spatial, w_out, norm2_g, mem_norm_g,
           wq_x, wk_x, wv_x, wo_x, norm3_g, w_gate, w_up, w_down, norm_f_g):
    b, t, _ = x.shape
    depth = w_in.shape[0]
    assert depth == 1, "the final RMSNorm is fused into the (single) layer's ffn call"
    assert t % TM_ATTN == 0 and t % TM_FFN == 0
    assert t % TM_DENSE == 0 and TM_DENSE % TB_SCAN == 0 and TB_SCAN % CHUNK == 0
    row = lambda p: p.reshape(1, -1)
    bf = lambda p: p.astype(BF16)
    outs = []
    for bi in range(b):
        xb = x[bi]
        for l in range(depth):
            lora = w_lora_up.shape[1]
            zeros = jnp.zeros((lora, RWKV_WIDTH), F32)
            waup = jnp.concatenate(
                [jnp.concatenate([w_lora_up[l], zeros], axis=1),
                 jnp.concatenate([zeros, a_lora_up[l]], axis=1)], axis=0)
            bias = jnp.repeat(b_spatial[l].T, SGU_WIDTH // SGU_GROUPS, axis=1)

            front = _front(xb, row(norm1_g[l]), w_in[l], row(shift_mu[l]), row(w0[l]), bf(waup),
                           row(a0[l]), row(k_k[l]), row(k_a[l]), row(r_k[l]), TM_DENSE)
            prep = dict(zip(_PREP_NAMES, front[:len(_PREP_NAMES)]))
            z_sgu = front[len(_PREP_NAMES)]
            y_rwkv, wg_bf, wu_bf, wd_bf = _scan(prep, bf(g_lora_up[l]), row(lnx_w[l]), row(lnx_b[l]),
                                                w_gate[l], w_up[l], w_down[l], TB_SCAN)
            k_mem, v_mem = _mem_kv(mem[bi], row(mem_norm_g[l]), wk_x[l], wv_x[l])
            x2 = _mix_attn(xb, y_rwkv, z_sgu, row(sgu_ln_w[l]), row(sgu_ln_b[l]), w_spatial[l], bias,
                           w_out[l], row(norm2_g[l]), wq_x[l], k_mem, v_mem, wo_x[l], TM_ATTN)
            xb = _ffn(x2, row(norm3_g[l]), wg_bf, wu_bf, wd_bf, row(norm_f_g), TM_FFN)
        outs.append(xb)
    return jnp.stack(outs, axis=0)
```

```python
import functools
import math

import jax
import jax.numpy as jnp
from jax import lax
from jax.experimental import pallas as pl
from jax.experimental.pallas import tpu as pltpu

F32 = jnp.float32
BF16 = jnp.bfloat16

D_MODEL = 1024
RWKV_WIDTH = 512
RWKV_HEAD = 64
LORA_WA = 128
GATE_LORA = 128
RWKV_IN = 3 * RWKV_WIDTH + LORA_WA + GATE_LORA
SGU_WIDTH = 512
SGU_GROUPS = 8
SGU_BLOCK = 128
IN_WIDTH = RWKV_IN + 2 * SGU_WIDTH
MEM_LEN = 256
XA_HEADS = 4
XA_HEAD_DIM = D_MODEL // XA_HEADS
D_FF = 2816
RMS_EPS = 1e-6
LN_EPS = 1e-5
LNX_EPS = 64e-5
EXP_M05 = 0.6065306597126334
LOG2_E = 1.4426950408889634

CHUNK = 64
PAIR = 2 * RWKV_HEAD
N_PAIRS = RWKV_WIDTH // PAIR
HEAD_BLOCK = 2 * PAIR
TM_DENSE = 512
TM_ATTN = 1024
TM_FFN = 1024
TB_SCAN = 512
Z_PAD = 8
WL_ROWS = 8
ATTN_ROW_GROUP = 256
ATTN_STAGGER = 2
FFN_ROW_GROUP = 256
IN_PROJ_COLS = 256
PREP_ROWS = 128
PREP_STREAMS = 4
PREP_PIECES_PER_CHUNK = 10
TERMS_DECAY_CUMSUM = 2
TERMS_HEAD_SUM = 1
TERMS_GROUP_MEAN = 2
VMEM_LIMIT = 56 * 1024 * 1024

_NN = (((1,), (0,)), ((), ()))
_NT = (((1,), (1,)), ((), ()))
_TN = (((0,), (0,)), ((), ()))


def _mm(a, b, dims=_NN):
    return lax.dot_general(a.astype(BF16), b.astype(BF16), dims, preferred_element_type=F32)


def _split_bf16(x, terms):
    parts = []
    rem = x
    for _ in range(terms):
        part = rem.astype(BF16)
        rem = rem - part.astype(F32)
        parts.append(part)
    return parts


def _cumsum_rows(ltri01, parts):
    return lax.dot_general(jnp.concatenate([ltri01] * len(parts), axis=1),
                           jnp.concatenate(parts, axis=0), _NN, preferred_element_type=F32)


def _head_sum_parts(parts, seg01):
    cols = []
    for q in range(parts[0].shape[1] // HEAD_BLOCK):
        acc = None
        for part in parts:
            d = lax.dot_general(part[:, HEAD_BLOCK * q:HEAD_BLOCK * (q + 1)], seg01, _NN,
                                preferred_element_type=F32)
            acc = d if acc is None else acc + d
        cols.append(acc)
    return jnp.concatenate(cols, axis=1)


def _head_sum(x, seg01, terms=TERMS_HEAD_SUM):
    return _head_sum_parts(_split_bf16(x, terms), seg01)


def _seg01():
    li = lax.broadcasted_iota(jnp.int32, (HEAD_BLOCK, HEAD_BLOCK), 0) >> 6
    lj = lax.broadcasted_iota(jnp.int32, (HEAD_BLOCK, HEAD_BLOCK), 1) >> 6
    return (li == lj).astype(BF16)


def _gelu_tanh(x):
    k1 = -2.0 * math.sqrt(2.0 / math.pi) * math.log2(math.e)
    return x / (1.0 + jnp.exp2(x * (k1 + (k1 * 0.044715) * (x * x))))


def _rmsnorm(x, g):
    return x * lax.rsqrt(jnp.mean(x * x, axis=-1, keepdims=True) + RMS_EPS) * g


def _full(shape):
    n = len(shape)
    return pl.BlockSpec(shape, lambda i: (0,) * n, pipeline_mode=pl.Buffered(1))


def _params(sem="arbitrary"):
    return pltpu.CompilerParams(dimension_semantics=(sem,), vmem_limit_bytes=VMEM_LIMIT)


def _run_interleaved(stages, rounds):
    for r in range(rounds):
        for gen, n in stages:
            for _ in range((r + 1) * n // rounds - r * n // rounds):
                next(gen, None)
    for gen, _ in stages:
        assert next(gen, StopIteration) is StopIteration, "piece count too small"


def _run_wavefront(gens, stagger):
    live = list(enumerate(gens))
    r = 0
    while live:
        for entry in list(live):
            g, gen = entry
            if r >= g * stagger and next(gen, StopIteration) is StopIteration:
                live.remove(entry)
        r += 1


def _mem_kv_kernel(mem_ref, g_ref, wk_ref, wv_ref, k_ref, v_ref):
    m = _rmsnorm(mem_ref[...], g_ref[...]).astype(BF16)
    k_ref[...] = jnp.dot(m, wk_ref[...].astype(BF16), preferred_element_type=F32).astype(BF16)
    v_ref[...] = jnp.dot(m, wv_ref[...].astype(BF16), preferred_element_type=F32).astype(BF16)


def _mem_kv(mem, g, wk, wv):
    return pl.pallas_call(
        _mem_kv_kernel,
        out_shape=(jax.ShapeDtypeStruct((MEM_LEN, D_MODEL), BF16),) * 2,
        grid=(1,),
        in_specs=[_full((MEM_LEN, D_MODEL)), _full((1, D_MODEL)),
                  _full((D_MODEL, D_MODEL)), _full((D_MODEL, D_MODEL))],
        out_specs=(_full((MEM_LEN, D_MODEL)),) * 2,
        compiler_params=_params(),
        name="mem_kv",
    )(mem, g, wk, wv)


_PREP_BF16 = ("at", "bonus", "rt", "bh", "kh", "v")
_PREP_NAMES = _PREP_BF16 + ("wl", "gds")


def _in_proj_pieces(x_ref, g_ref, w_ref, z_ref, zs_ref):
    assert RWKV_IN % IN_PROJ_COLS == 0 and IN_WIDTH % IN_PROJ_COLS == 0
    h = _rmsnorm(x_ref[...], g_ref[...]).astype(BF16)
    yield
    for j in range(IN_WIDTH // IN_PROJ_COLS):
        lo = j * IN_PROJ_COLS
        zj = jnp.dot(h, w_ref[:, lo:lo + IN_PROJ_COLS], preferred_element_type=F32)
        if lo < RWKV_IN:
            z_ref[Z_PAD:, lo:lo + IN_PROJ_COLS] = zj
        else:
            zs_ref[:, lo - RWKV_IN:lo - RWKV_IN + IN_PROJ_COLS] = zj
        yield


def _rwkv_prep_pieces(z_ref, mu_ref, w0_ref, waup_ref, a0_ref, kk_ref, ka_ref, rk_ref, prep,
                      chunks):
    seg01 = _seg01()
    lane = lax.broadcasted_iota(jnp.int32, (1, LORA_WA), 1)
    row = lax.broadcasted_iota(jnp.int32, (PREP_ROWS, 1), 0)
    ti = lax.broadcasted_iota(jnp.int32, (PREP_ROWS, PREP_ROWS), 0)
    tj = lax.broadcasted_iota(jnp.int32, (PREP_ROWS, PREP_ROWS), 1)
    ltri01 = ((tj <= ti) & ((ti >> 6) == (tj >> 6))).astype(BF16)
    chunks_per_scan_tile = TB_SCAN // CHUNK
    chunks_per_unit = PREP_ROWS // CHUNK
    half_w = RWKV_WIDTH // 2

    def shifted(c, cols):
        z = z_ref[Z_PAD + c * PREP_ROWS:Z_PAD + (c + 1) * PREP_ROWS, cols]
        before = z_ref[Z_PAD + c * PREP_ROWS - 1:Z_PAD + c * PREP_ROWS, cols]
        zprev = jnp.where(row == 0, before, pltpu.roll(z, 1, axis=0))
        return z + (zprev - z) * mu_ref[:, cols]

    for c in chunks:
        rows = slice(c * PREP_ROWS, (c + 1) * PREP_ROWS)
        wa_in = shifted(c, slice(3 * RWKV_WIDTH, 3 * RWKV_WIDTH + LORA_WA))
        wa_in = jnp.where(lane < LORA_WA // 2, jnp.tanh(wa_in), wa_in).astype(BF16)
        prep["gds"][rows, :] = jax.nn.sigmoid(
            shifted(c, slice(3 * RWKV_WIDTH + LORA_WA, RWKV_IN))).astype(BF16)
        for q in range(2):
            hc = slice(q * half_w, (q + 1) * half_w)
            r = shifted(c, hc)
            k = shifted(c, slice(RWKV_WIDTH + q * half_w, RWKV_WIDTH + (q + 1) * half_w))
            v = shifted(c, slice(2 * RWKV_WIDTH + q * half_w, 2 * RWKV_WIDTH + (q + 1) * half_w))
            prep["v"][rows, hc] = v.astype(BF16)
            kk = k * kk_ref[:, hc]
            kk_sq = _split_bf16(kk * kk, TERMS_HEAD_SUM)
            yield
            w_pre = w0_ref[:, hc] + jnp.dot(wa_in, waup_ref[:, hc], preferred_element_type=F32)
            a_pre = a0_ref[:, hc] + jnp.dot(
                wa_in, waup_ref[:, RWKV_WIDTH + q * half_w:RWKV_WIDTH + (q + 1) * half_w],
                preferred_element_type=F32)
            kk_ss = _head_sum_parts(kk_sq, seg01)
            yield
            a = jax.nn.sigmoid(a_pre)
            lw = jax.nn.sigmoid(w_pre) * (-EXP_M05 * LOG2_E)
            lw_parts = _split_bf16(lw, TERMS_DECAY_CUMSUM)
            kk = kk * lax.rsqrt(jnp.maximum(kk_ss, 1e-24))
            kmod = k * ((1.0 - ka_ref[:, hc]) + a * ka_ref[:, hc])
            kka = kk * a
            rkk = _split_bf16(r * kmod * rk_ref[:, hc], TERMS_HEAD_SUM)
            yield
            cs = _cumsum_rows(ltri01, lw_parts)
            prep["bonus"][rows, hc] = (_head_sum_parts(rkk, seg01) * v).astype(BF16)
            yield
            w_inv = jnp.exp2(-cs)
            w_last = [jnp.exp2(cs[(j + 1) * CHUNK - 1:(j + 1) * CHUNK, :])
                      for j in range(chunks_per_unit)]
            prep["rt"][rows, hc] = (r * jnp.exp2(cs)).astype(BF16)
            prep["at"][rows, hc] = (-kk * jnp.exp2(cs - lw)).astype(BF16)
            prep["bh"][rows, hc] = (kka * w_inv).astype(BF16)
            prep["kh"][rows, hc] = (kmod * w_inv).astype(BF16)
            for j in range(chunks_per_unit):
                cq, cr = divmod(c * chunks_per_unit + j, chunks_per_scan_tile)
                prep["wl"][cq, cr:cr + 1, hc] = w_last[j]
            yield


def _front_kernel(x_ref, g1_ref, win_ref, mu_ref, w0_ref, waup_ref, a0_ref, kk_ref, ka_ref, rk_ref,
                  *rest, tm):
    n = len(_PREP_NAMES)
    prep = dict(zip(_PREP_NAMES, rest[:n]))
    zs_ref = rest[n]
    wbf_ref, z0_ref, z1_ref = rest[n + 1:]
    i = pl.program_id(0)

    @pl.when(i == 0)
    def _():
        z1_ref[...] = jnp.zeros_like(z1_ref)
        wbf_ref[...] = win_ref[...].astype(BF16)

    def step(z_write, z_read):
        n_chunks = tm // PREP_ROWS
        prep_pieces = [
            _rwkv_prep_pieces(z_read, mu_ref, w0_ref, waup_ref, a0_ref, kk_ref, ka_ref, rk_ref,
                              prep, range(k, n_chunks, PREP_STREAMS))
            for k in range(PREP_STREAMS)]
        n_dot = 1 + IN_WIDTH // IN_PROJ_COLS + 1
        _run_interleaved(
            [(_in_proj_pieces(x_ref, g1_ref, wbf_ref, z_write, zs_ref), n_dot)]
            + [(gen, PREP_PIECES_PER_CHUNK * n_chunks // PREP_STREAMS + 1) for gen in prep_pieces],
            rounds=n_dot)
        if TB_SCAN // CHUNK < WL_ROWS:
            prep["wl"][:, TB_SCAN // CHUNK:, :] = jnp.zeros(
                (tm // TB_SCAN, WL_ROWS - TB_SCAN // CHUNK, RWKV_WIDTH), F32)
        z_write[Z_PAD - 1:Z_PAD, :] = z_read[Z_PAD + tm - 1:Z_PAD + tm, :]

    @pl.when((i & 1) == 0)
    def _():
        step(z0_ref, z1_ref)

    @pl.when((i & 1) == 1)
    def _():
        step(z1_ref, z0_ref)


def _front(x, g1, w_in, mu, w0, waup, a0, k_k, k_a, r_k, tm):
    t = x.shape[0]
    n_tiles = t // tm
    vec = _full((1, RWKV_WIDTH))
    out_tile = lambda i: (jnp.maximum(i - 1, 0), 0)
    out_shapes = ([jax.ShapeDtypeStruct((t, RWKV_WIDTH), BF16)] * len(_PREP_BF16)
                  + [jax.ShapeDtypeStruct((t // TB_SCAN, WL_ROWS, RWKV_WIDTH), F32),
                     jax.ShapeDtypeStruct((t, GATE_LORA), BF16),
                     jax.ShapeDtypeStruct((t, 2 * SGU_WIDTH), F32)])
    out_specs = ([pl.BlockSpec((tm, RWKV_WIDTH), out_tile)] * len(_PREP_BF16)
                 + [pl.BlockSpec((tm // TB_SCAN, WL_ROWS, RWKV_WIDTH),
                                 lambda i: (jnp.maximum(i - 1, 0), 0, 0)),
                    pl.BlockSpec((tm, GATE_LORA), out_tile),
                    pl.BlockSpec((tm, 2 * SGU_WIDTH), lambda i: (jnp.minimum(i, n_tiles - 1), 0))])
    return pl.pallas_call(
        functools.partial(_front_kernel, tm=tm),
        out_shape=tuple(out_shapes),
        grid=(n_tiles + 1,),
        in_specs=[pl.BlockSpec((tm, D_MODEL), lambda i: (jnp.minimum(i, n_tiles - 1), 0)),
                  _full((1, D_MODEL)), _full((D_MODEL, IN_WIDTH)), _full((1, RWKV_IN)), vec,
                  _full((LORA_WA, 2 * RWKV_WIDTH)), vec, vec, vec, vec],
        out_specs=tuple(out_specs),
        scratch_shapes=[pltpu.VMEM((D_MODEL, IN_WIDTH), BF16),
                        pltpu.VMEM((Z_PAD + tm, RWKV_IN), F32),
                        pltpu.VMEM((Z_PAD + tm, RWKV_IN), F32)],
        compiler_params=_params(),
        name="front",
    )(x, g1, w_in, mu, w0, waup, a0, k_k, k_a, r_k)


def _pair_masks():
    t = lax.broadcasted_iota(jnp.int32, (CHUNK, PAIR), 0)
    j = lax.broadcasted_iota(jnp.int32, (CHUNK, PAIR), 1) & (CHUNK - 1)
    strict = j < t
    incl = j <= t
    blk16 = (t >> 4) == (j >> 4)
    blk32 = (t >> 5) == (j >> 5)
    return strict, incl, blk16, blk32


def _bd(x, bd_mask):
    x = x.astype(BF16)
    return jnp.where(bd_mask, jnp.concatenate([x, x], axis=0), 0.0).astype(BF16)


def _staged(fn, items, parts=2):
    out = []
    n = len(items) // parts
    for k in range(parts):
        out += [fn(*item) for item in items[k * n:(k + 1) * n]]
        yield
    return out


def _unit_lower_inverse_minus_identity(a_list, masks, bd_mask):
    _, _, blk16, blk32 = masks
    ad = [jnp.where(blk16, a, 0.0) for a in a_list]
    ap = yield from _staged(lambda x: _mm(x, _bd(x, bd_mask)), [(x,) for x in ad])
    tp = ad
    for _ in range(2):
        both = yield from _staged(
            lambda p, t: _mm(p, jnp.concatenate([_bd(p, bd_mask), _bd(t, bd_mask)], axis=1)),
            list(zip(ap, tp)))
        tp = [t + p + b[:, PAIR:] for t, p, b in zip(tp, ap, both)]
        ap = [b[:, :PAIR] for b in both]
    last = yield from _staged(lambda p, t: _mm(p, _bd(t, bd_mask)), list(zip(ap, tp)))
    tp = [t + p + x for t, p, x in zip(tp, ap, last)]
    for off_mask in (blk32 & ~blk16, ~blk32):
        off = [jnp.where(off_mask, a, 0.0) for a in a_list]
        x = yield from _staged(lambda o, t: o + _mm(t, _bd(o, bd_mask)), list(zip(off, tp)))
        tp = yield from _staged(lambda t, xx: t + xx + _mm(xx, _bd(t, bd_mask)), list(zip(tp, x)))
    return tp


_TERM_STAGES = 11
_TERM_NAMES = ("achk", "uv", "bb")


def _bd_masks():
    bi = lax.broadcasted_iota(jnp.int32, (PAIR, PAIR), 0) >> 6
    bj = lax.broadcasted_iota(jnp.int32, (PAIR, PAIR), 1) >> 6
    bd1 = bi == bj
    return bd1, jnp.concatenate([bd1, bd1], axis=1)


def _scan_terms_pieces(rt_ref, at_ref, bh_ref, kh_ref, v_ref, terms, *, tb):
    masks = _pair_masks()
    strict, incl = masks[0], masks[1]
    bd1, bd2 = _bd_masks()
    probs = [(c, p) for c in range(tb // CHUNK) for p in range(N_PAIRS)]
    cut = lambda ref: [ref[c * CHUNK:(c + 1) * CHUNK, p * PAIR:(p + 1) * PAIR] for c, p in probs]
    rt_p, at_p, bh_p, kh_p, v_p = map(cut, (rt_ref, at_ref, bh_ref, kh_ref, v_ref))
    gram = yield from _staged(
        lambda a_, r_, b_, k_: _mm(jnp.concatenate([a_, r_], axis=0),
                                   jnp.concatenate([_bd(b_, bd1), _bd(k_, bd1)], axis=0), _NT),
        list(zip(at_p, rt_p, bh_p, kh_p)))
    a_ab = [jnp.where(strict, g_[:CHUNK, :PAIR], 0.0) for g_ in gram]
    a_ak = [jnp.where(strict, g_[:CHUNK, PAIR:], 0.0) for g_ in gram]
    incl2 = jnp.concatenate([incl, incl], axis=1)
    for i, g_ in enumerate(gram):
        terms["bb"][i] = jnp.where(incl2, g_[CHUNK:], 0.0).astype(BF16)
    rhs = yield from _staged(
        lambda m_, x_, a_: jnp.concatenate([_mm(m_, _bd(x_, bd1)), a_.astype(F32)], axis=1),
        list(zip(a_ak, v_p, at_p)))
    tp = yield from _unit_lower_inverse_minus_identity(a_ab, masks, bd1)
    sol = yield from _staged(lambda x_, t_: x_ + _mm(t_, _bd(x_, bd2)), list(zip(rhs, tp)))
    for i, x_ in enumerate(sol):
        terms["uv"][i] = x_[:, :PAIR]
        terms["achk"][i] = x_[:, PAIR:].astype(BF16)


def _scan_state_pieces(terms, rt_ref, v_ref, bh_ref, kh_ref, wl_ref, gds_ref, gup_ref, bonus_ref,
                       lnw_ref, lnb_ref, o_ref, s_ref, y_ref, *, tb):
    bd1, _ = _bd_masks()
    bd_upd = jnp.logical_and(bd1, pl.program_id(0) > 0)
    s = [s_ref[p] for p in range(N_PAIRS)]
    for c in range(tb // CHUNK):
        rows = slice(c * CHUNK, (c + 1) * CHUNK)
        lanes = [slice(p * PAIR, (p + 1) * PAIR) for p in range(N_PAIRS)]
        idx = [c * N_PAIRS + p for p in range(N_PAIRS)]
        on_s = [_mm(jnp.concatenate([terms["achk"][i], rt_ref[rows, lanes[p]]], axis=0),
                    s[p], _NT) for p, i in enumerate(idx)]
        u = [x[:CHUNK] + terms["uv"][i] for x, i in zip(on_s, idx)]
        yield
        w_last = wl_ref[0, c:c + 1, :]
        upd = []
        for p, i in enumerate(idx):
            v_i = v_ref[rows, lanes[p]]
            y_ref[rows, lanes[p]] = (
                on_s[p][CHUNK:]
                + _mm(terms["bb"][i], jnp.concatenate([_bd(u[p], bd1), _bd(v_i, bd1)], axis=0)))
            upd.append(_mm(jnp.concatenate([u[p].astype(BF16), v_i], axis=0),
                           jnp.concatenate([bh_ref[rows, lanes[p]], kh_ref[rows, lanes[p]]], axis=0),
                           _TN))
        s = [s[p] * w_last[:, lanes[p]] + jnp.where(bd_upd, upd[p], 0.0) * w_last[:, lanes[p]]
             for p in range(N_PAIRS)]
        yield
    for p in range(N_PAIRS):
        s_ref[p] = s[p]
    seg01 = _seg01()
    y = y_ref[...]
    mean = _head_sum(y, seg01, TERMS_GROUP_MEAN) * (1.0 / RWKV_HEAD)
    yield
    d = y - mean
    var = _head_sum(d * d, seg01) * (1.0 / RWKV_HEAD)
    yield
    yn = d * lax.rsqrt(var + LNX_EPS) * lnw_ref[...] + lnb_ref[...]
    gate = jnp.dot(gds_ref[...], gup_ref[...], preferred_element_type=F32)
    o_ref[...] = ((yn + bonus_ref[...]) * gate).astype(BF16)


def _scan_kernel(rt_ref, at_ref, bh_ref, kh_ref, v_ref, rtp_ref, vp_ref, bhp_ref, khp_ref, bonus_ref,
                 gds_ref, wl_ref, gup_ref, lnw_ref, lnb_ref, wg_ref, wu_ref, wd_ref, o_ref, wg_bf_ref,
                 wu_bf_ref, wd_bf_ref, s_ref, y_ref, *term_refs, tb):
    n = len(_TERM_NAMES)
    slots = [dict(zip(_TERM_NAMES, term_refs[k * n:(k + 1) * n])) for k in range(2)]
    i = pl.program_id(0)

    @pl.when(i == 0)
    def _():
        s_ref[...] = jnp.zeros_like(s_ref)
        for ref in slots[1].values():
            ref[...] = jnp.zeros_like(ref)

    wg_bf_ref[...] = wg_ref[...].astype(BF16)
    wu_bf_ref[...] = wu_ref[...].astype(BF16)
    wd_bf_ref[...] = wd_ref[...].astype(BF16)

    def step(write, read):
        n_terms = 2 * _TERM_STAGES + 1
        _run_interleaved(
            [(_scan_terms_pieces(rt_ref, at_ref, bh_ref, kh_ref, v_ref, write, tb=tb), n_terms),
             (_scan_state_pieces(read, rtp_ref, vp_ref, bhp_ref, khp_ref, wl_ref, gds_ref, gup_ref,
                                 bonus_ref, lnw_ref, lnb_ref, o_ref, s_ref, y_ref, tb=tb),
              2 * (tb // CHUNK) + 3)],
            rounds=n_terms)

    @pl.when((i & 1) == 0)
    def _():
        step(slots[0], slots[1])

    @pl.when((i & 1) == 1)
    def _():
        step(slots[1], slots[0])


def _scan(prep, gup, lnw, lnb, w_gate, w_up, w_down, tb):
    t = prep["rt"].shape[0]
    n_tiles = t // tb
    assert tb // CHUNK <= WL_ROWS
    last = n_tiles - 1
    cur = pl.BlockSpec((tb, RWKV_WIDTH), lambda i: (jnp.minimum(i, last), 0))
    prev = pl.BlockSpec((tb, RWKV_WIDTH), lambda i: (jnp.maximum(i - 1, 0), 0))
    vec = _full((1, RWKV_WIDTH))
    assert D_MODEL % n_tiles == 0 and (D_MODEL // n_tiles) % 16 == 0
    assert n_tiles % 2 == 0 and D_FF % (n_tiles // 2) == 0 and (D_FF // (n_tiles // 2)) % 16 == 0
    up_rows = pl.BlockSpec((D_MODEL // n_tiles, D_FF), lambda i: (jnp.minimum(i, last), 0))
    down_blk = pl.BlockSpec((D_FF // (n_tiles // 2), D_MODEL // 2),
                            lambda i: (jnp.minimum(i, last) // 2, jnp.minimum(i, last) % 2))
    n_prob = (tb // CHUNK) * N_PAIRS
    term_shapes = [pltpu.VMEM((n_prob, CHUNK, PAIR), BF16), pltpu.VMEM((n_prob, CHUNK, PAIR), F32),
                   pltpu.VMEM((n_prob, CHUNK, 2 * PAIR), BF16)]
    return pl.pallas_call(
        functools.partial(_scan_kernel, tb=tb),
        out_shape=(jax.ShapeDtypeStruct((t, RWKV_WIDTH), BF16),
                   jax.ShapeDtypeStruct((D_MODEL, D_FF), BF16),
                   jax.ShapeDtypeStruct((D_MODEL, D_FF), BF16),
                   jax.ShapeDtypeStruct((D_FF, D_MODEL), BF16)),
        grid=(n_tiles + 1,),
        in_specs=[cur] * 5 + [prev] * 5
        + [pl.BlockSpec((tb, GATE_LORA), lambda i: (jnp.maximum(i - 1, 0), 0)),
           pl.BlockSpec((1, WL_ROWS, RWKV_WIDTH), lambda i: (jnp.maximum(i - 1, 0), 0, 0)),
           _full((GATE_LORA, RWKV_WIDTH)), vec, vec, up_rows, up_rows, down_blk],
        out_specs=(prev, up_rows, up_rows, down_blk),
        scratch_shapes=[pltpu.VMEM((N_PAIRS, PAIR, PAIR), F32), pltpu.VMEM((tb, RWKV_WIDTH), F32)]
        + term_shapes * 2,
        compiler_params=_params(),
        name="scan",
    )(prep["rt"], prep["at"], prep["bh"], prep["kh"], prep["v"],
      prep["rt"], prep["v"], prep["bh"], prep["kh"], prep["bonus"], prep["gds"], prep["wl"],
      gup, lnw, lnb, w_gate, w_up, w_down)


def _sgu_block_prepare(z, lnw, lnb, sel):
    hz = _gelu_tanh(z)
    u = hz[:, :SGU_WIDTH]
    vf = hz[:, SGU_WIDTH:]
    mu = jnp.mean(vf, axis=-1, keepdims=True)
    d = vf - mu
    var = jnp.mean(d * d, axis=-1, keepdims=True)
    vn = d * lax.rsqrt(var + LN_EPS) * lnw + lnb
    stacks = []
    for p in range(SGU_WIDTH // PAIR):
        vb = vn[:, p * PAIR:(p + 1) * PAIR]
        stacks.append(jnp.where(sel, jnp.concatenate([vb, vb], axis=0), 0.0).astype(BF16))
    return u, stacks


def _sgu_block_mix(u, stacks, wcat, bias):
    return jnp.concatenate(
        [u[:, p * PAIR:(p + 1) * PAIR]
         * (jnp.dot(wcat[p], stacks[p], preferred_element_type=F32) + bias[:, p * PAIR:(p + 1) * PAIR])
         for p in range(SGU_WIDTH // PAIR)], axis=1)


def _mix_attn_group(r, x_ref, yr_ref, zs_ref, slnw_ref, slnb_ref, sbias_ref, wo1_ref, wo2_ref,
                    g2_ref, wq_ref, k_ref, v_ref, wo_ref, o_ref, wcat, sel):
    heads = [slice(hd * XA_HEAD_DIM, (hd + 1) * XA_HEAD_DIM) for hd in range(XA_HEADS)]
    prepared = [_sgu_block_prepare(zs_ref[b:b + SGU_BLOCK, :], slnw_ref[...], slnb_ref[...], sel)
                for b in range(r.start, r.stop, SGU_BLOCK)]
    yield
    x1 = x_ref[r, :] + jnp.dot(yr_ref[r, :].astype(BF16), wo1_ref[...],
                               preferred_element_type=F32)
    y_sgu = jnp.concatenate([_sgu_block_mix(u, st, wcat, sbias_ref[...]) for u, st in prepared],
                            axis=0)
    yield
    x1 = x1 + jnp.dot(y_sgu.astype(BF16), wo2_ref[...], preferred_element_type=F32)
    yield
    h = _rmsnorm(x1, g2_ref[...]).astype(BF16)
    yield
    q = jnp.dot(h, wq_ref[...], preferred_element_type=F32).astype(BF16)
    s = [lax.dot_general(q[:, hl], k_ref[:, hl], _NT, preferred_element_type=F32)
         * (XA_HEAD_DIM ** -0.5) for hl in heads]
    yield
    p = []
    for s_h in s:
        e = jnp.exp(s_h - jnp.max(s_h, axis=-1, keepdims=True))
        p.append((e / jnp.sum(e, axis=-1, keepdims=True)).astype(BF16))
    yield
    o = jnp.concatenate([jnp.dot(p_h, v_ref[:, hl], preferred_element_type=F32)
                         for p_h, hl in zip(p, heads)], axis=1).astype(BF16)
    o_ref[r, :] = x1 + jnp.dot(o, wo_ref[...], preferred_element_type=F32)


def _mix_attn_kernel(x_ref, yr_ref, zs_ref, slnw_ref, slnb_ref, ws_ref, sbias_ref, wout_f32_ref,
                     g2_ref, wq_f32_ref, k_ref, v_ref, wo_f32_ref, o_ref, wout_ref, wq_ref, wo_ref):
    @pl.when(pl.program_id(0) == 0)
    def _():
        wout_ref[...] = wout_f32_ref[...].astype(BF16)
        wq_ref[...] = wq_f32_ref[...].astype(BF16)
        wo_ref[...] = wo_f32_ref[...].astype(BF16)

    wo1_ref = wout_ref.at[:RWKV_WIDTH]
    wo2_ref = wout_ref.at[RWKV_WIDTH:]
    tm = x_ref.shape[0]
    ti = lax.broadcasted_iota(jnp.int32, (SGU_BLOCK, SGU_BLOCK), 0)
    tj = lax.broadcasted_iota(jnp.int32, (SGU_BLOCK, SGU_BLOCK), 1)
    tril = tj <= ti
    wcat = [jnp.concatenate([jnp.where(tril, ws_ref[2 * p], 0.0),
                             jnp.where(tril, ws_ref[2 * p + 1], 0.0)], axis=1).astype(BF16)
            for p in range(SGU_WIDTH // PAIR)]
    bi = lax.broadcasted_iota(jnp.int32, (2 * SGU_BLOCK, PAIR), 0) >> 7
    bj = lax.broadcasted_iota(jnp.int32, (2 * SGU_BLOCK, PAIR), 1) >> 6
    sel = bi == bj
    gens = [_mix_attn_group(slice(r, r + ATTN_ROW_GROUP), x_ref, yr_ref, zs_ref, slnw_ref, slnb_ref,
                            sbias_ref, wo1_ref, wo2_ref, g2_ref, wq_ref, k_ref, v_ref, wo_ref,
                            o_ref, wcat, sel)
            for r in range(0, tm, ATTN_ROW_GROUP)]
    _run_wavefront(gens, ATTN_STAGGER)


def _mix_attn(x, yr, zs, slnw, slnb, ws, sbias, w_out, g2, wq, k, v, wo, tm):
    t = x.shape[0]
    sq = _full((D_MODEL, D_MODEL))
    return pl.pallas_call(
        _mix_attn_kernel,
        out_shape=jax.ShapeDtypeStruct((t, D_MODEL), F32),
        grid=(t // tm,),
        in_specs=[pl.BlockSpec((tm, D_MODEL), lambda i: (i, 0)),
                  pl.BlockSpec((tm, RWKV_WIDTH), lambda i: (i, 0)),
                  pl.BlockSpec((tm, 2 * SGU_WIDTH), lambda i: (i, 0)),
                  _full((1, SGU_WIDTH)), _full((1, SGU_WIDTH)),
                  _full((SGU_GROUPS, SGU_BLOCK, SGU_BLOCK)), _full((SGU_BLOCK, SGU_WIDTH)),
                  sq, _full((1, D_MODEL)), sq,
                  _full((MEM_LEN, D_MODEL)), _full((MEM_LEN, D_MODEL)), sq],
        out_specs=pl.BlockSpec((tm, D_MODEL), lambda i: (i, 0)),
        scratch_shapes=[pltpu.VMEM((D_MODEL, D_MODEL), BF16)] * 3,
        compiler_params=_params(),
        name="mix_attn",
    )(x, yr, zs, slnw, slnb, ws, sbias, w_out, g2, wq, k, v, wo)


def _ffn_kernel(x_ref, g3_ref, wg_ref, wu_ref, wd_ref, gf_ref, o_ref):
    tm = x_ref.shape[0]
    groups = [slice(r, r + FFN_ROW_GROUP) for r in range(0, tm, FFN_ROW_GROUP)]
    x2 = [x_ref[r, :] for r in groups]
    h = [_rmsnorm(x, g3_ref[...]).astype(BF16) for x in x2]
    gate = [jnp.dot(h_, wg_ref[...], preferred_element_type=F32) for h_ in h]
    up = [jnp.dot(h_, wu_ref[...], preferred_element_type=F32) for h_ in h]
    act = [(jax.nn.silu(g_) * u_).astype(BF16) for g_, u_ in zip(gate, up)]
    x3 = [x + jnp.dot(a_, wd_ref[...], preferred_element_type=F32) for x, a_ in zip(x2, act)]
    for r, x in zip(groups, x3):
        o_ref[r, :] = _rmsnorm(x, gf_ref[...])


def _ffn(x, g3, wg, wu, wd, gf, tm):
    t = x.shape[0]
    return pl.pallas_call(
        _ffn_kernel,
        out_shape=jax.ShapeDtypeStruct((t, D_MODEL), F32),
        grid=(t // tm,),
        in_specs=[pl.BlockSpec((tm, D_MODEL), lambda i: (i, 0)), _full((1, D_MODEL)),
                  _full((D_MODEL, D_FF)), _full((D_MODEL, D_FF)), _full((D_FF, D_MODEL)),
                  _full((1, D_MODEL))],
        out_specs=pl.BlockSpec((tm, D_MODEL), lambda i: (i, 0)),
        compiler_params=_params("parallel"),
        name="ffn",
    )(x, g3, wg, wu, wd, gf)


def kernel(x, mem, norm1_g, w_in, shift_mu, w0, w_lora_up, a0, a_lora_up, g_lora_up, k_k, k_a, r_k,
           lnx_w, lnx_b, sgu_ln_w, sgu_ln_b, w_spatial, b_spatial, w_out, norm2_g, mem_norm_g,
           wq_x, wk_x, wv_x, wo_x, norm3_g, w_gate, w_up, w_down, norm_f_g):
    b, t, _ = x.shape
    depth = w_in.shape[0]
    assert depth == 1, "the final RMSNorm is fused into the (single) layer's ffn call"
    assert t % TM_ATTN == 0 and t % TM_FFN == 0
    assert t % TM_DENSE == 0 and TM_DENSE % TB_SCAN == 0 and TB_SCAN % CHUNK == 0
    row = lambda p: p.reshape(1, -1)
    bf = lambda p: p.astype(BF16)
    outs = []
    for bi in range(b):
        xb = x[bi]
        for l in range(depth):
            lora = w_lora_up.shape[1]
            zeros = jnp.zeros((lora, RWKV_WIDTH), F32)
            waup = jnp.concatenate(
                [jnp.concatenate([w_lora_up[l], zeros], axis=1),
                 jnp.concatenate([zeros, a_lora_up[l]], axis=1)], axis=0)
            bias = jnp.repeat(b_spatial[l].T, SGU_WIDTH // SGU_GROUPS, axis=1)

            front = _front(xb, row(norm1_g[l]), w_in[l], row(shift_mu[l]), row(w0[l]), bf(waup),
                           row(a0[l]), row(k_k[l]), row(k_a[l]), row(r_k[l]), TM_DENSE)
            prep = dict(zip(_PREP_NAMES, front[:len(_PREP_NAMES)]))
            z_sgu = front[len(_PREP_NAMES)]
            y_rwkv, wg_bf, wu_bf, wd_bf = _scan(prep, bf(g_lora_up[l]), row(lnx_w[l]), row(lnx_b[l]),
                                                w_gate[l], w_up[l], w_down[l], TB_SCAN)
            k_mem, v_mem = _mem_kv(mem[bi], row(mem_norm_g[l]), wk_x[l], wv_x[l])
            x2 = _mix_attn(xb, y_rwkv, z_sgu, row(sgu_ln_w[l]), row(sgu_ln_b[l]), w_spatial[l], bias,
                           w_out[l], row(norm2_g[l]), wq_x[l], k_mem, v_mem, wo_x[l], TM_ATTN)
            xb = _ffn(x2, row(norm3_g[l]), wg_bf, wu_bf, wd_bf, row(norm_f_g), TM_FFN)
        outs.append(xb)
    return jnp.stack(outs, axis=0)
```

```python
import functools
import math

import jax
import jax.numpy as jnp
from jax import lax
from jax.experimental import pallas as pl
from jax.experimental.pallas import tpu as pltpu

F32 = jnp.float32
BF16 = jnp.bfloat16

D_MODEL = 1024
RWKV_WIDTH = 512
RWKV_HEAD = 64
LORA_WA = 128
GATE_LORA = 128
RWKV_IN = 3 * RWKV_WIDTH + LORA_WA + GATE_LORA
SGU_WIDTH = 512
SGU_GROUPS = 8
SGU_BLOCK = 128
IN_WIDTH = RWKV_IN + 2 * SGU_WIDTH
MEM_LEN = 256
XA_HEADS = 4
XA_HEAD_DIM = D_MODEL // XA_HEADS
D_FF = 2816
RMS_EPS = 1e-6
LN_EPS = 1e-5
LNX_EPS = 64e-5
EXP_M05 = 0.6065306597126334
LOG2_E = 1.4426950408889634

CHUNK = 64
PAIR = 2 * RWKV_HEAD
N_PAIRS = RWKV_WIDTH // PAIR
HEAD_BLOCK = 2 * PAIR
TM_DENSE = 512
TM_ATTN = 1024
TM_FFN = 1024
TB_SCAN = 512
Z_PAD = 8
WL_ROWS = 8
ATTN_ROW_GROUP = 256
ATTN_STAGGER = 2
FFN_ROW_GROUP = 256
IN_PROJ_COLS = 256
PREP_ROWS = 128
PREP_STREAMS = 4
PREP_PIECES_PER_CHUNK = 10
TERMS_DECAY_CUMSUM = 2
TERMS_HEAD_SUM = 1
TERMS_GROUP_MEAN = 2
VMEM_LIMIT = 56 * 1024 * 1024

_NN = (((1,), (0,)), ((), ()))
_NT = (((1,), (1,)), ((), ()))
_TN = (((0,), (0,)), ((), ()))


def _mm(a, b, dims=_NN):
    return lax.dot_general(a.astype(BF16), b.astype(BF16), dims, preferred_element_type=F32)


def _split_bf16(x, terms):
    parts = []
    rem = x
    for _ in range(terms):
        part = rem.astype(BF16)
        rem = rem - part.astype(F32)
        parts.append(part)
    return parts


def _cumsum_rows(ltri01, parts):
    return lax.dot_general(jnp.concatenate([ltri01] * len(parts), axis=1),
                           jnp.concatenate(parts, axis=0), _NN, preferred_element_type=F32)


def _head_sum_parts(parts, seg01):
    cols = []
    for q in range(parts[0].shape[1] // HEAD_BLOCK):
        acc = None
        for part in parts:
            d = lax.dot_general(part[:, HEAD_BLOCK * q:HEAD_BLOCK * (q + 1)], seg01, _NN,
                                preferred_element_type=F32)
            acc = d if acc is None else acc + d
        cols.append(acc)
    return jnp.concatenate(cols, axis=1)


def _head_sum(x, seg01, terms=TERMS_HEAD_SUM):
    return _head_sum_parts(_split_bf16(x, terms), seg01)


def _seg01():
    li = lax.broadcasted_iota(jnp.int32, (HEAD_BLOCK, HEAD_BLOCK), 0) >> 6
    lj = lax.broadcasted_iota(jnp.int32, (HEAD_BLOCK, HEAD_BLOCK), 1) >> 6
    return (li == lj).astype(BF16)


def _gelu_tanh(x):
    k1 = -2.0 * math.sqrt(2.0 / math.pi) * math.log2(math.e)
    return x / (1.0 + jnp.exp2(x * (k1 + (k1 * 0.044715) * (x * x))))


def _rmsnorm(x, g):
    return x * lax.rsqrt(jnp.mean(x * x, axis=-1, keepdims=True) + RMS_EPS) * g


def _full(shape):
    n = len(shape)
    return pl.BlockSpec(shape, lambda i: (0,) * n, pipeline_mode=pl.Buffered(1))


def _params(sem="arbitrary"):
    return pltpu.CompilerParams(dimension_semantics=(sem,), vmem_limit_bytes=VMEM_LIMIT)


def _run_interleaved(stages, rounds):
    for r in range(rounds):
        for gen, n in stages:
            for _ in range((r + 1) * n // rounds - r * n // rounds):
                next(gen, None)
    for gen, _ in stages:
        assert next(gen, StopIteration) is StopIteration, "piece count too small"


def _run_wavefront(gens, stagger):
    live = list(enumerate(gens))
    r = 0
    while live:
        for entry in list(live):
            g, gen = entry
            if r >= g * stagger and next(gen, StopIteration) is StopIteration:
                live.remove(entry)
        r += 1


def _mem_kv_kernel(mem_ref, g_ref, wk_ref, wv_ref, k_ref, v_ref):
    m = _rmsnorm(mem_ref[...], g_ref[...]).astype(BF16)
    k_ref[...] = jnp.dot(m, wk_ref[...].astype(BF16), preferred_element_type=F32).astype(BF16)
    v_ref[...] = jnp.dot(m, wv_ref[...].astype(BF16), preferred_element_type=F32).astype(BF16)


def _mem_kv(mem, g, wk, wv):
    return pl.pallas_call(
        _mem_kv_kernel,
        out_shape=(jax.ShapeDtypeStruct((MEM_LEN, D_MODEL), BF16),) * 2,
        grid=(1,),
        in_specs=[_full((MEM_LEN, D_MODEL)), _full((1, D_MODEL)),
                  _full((D_MODEL, D_MODEL)), _full((D_MODEL, D_MODEL))],
        out_specs=(_full((MEM_LEN, D_MODEL)),) * 2,
        compiler_params=_params(),
        name="mem_kv",
    )(mem, g, wk, wv)


_PREP_BF16 = ("at", "bonus", "rt", "bh", "kh", "v")
_PREP_NAMES = _PREP_BF16 + ("wl", "gds")


def _in_proj_pieces(x_ref, g_ref, w_ref, z_ref, zs_ref):
    assert RWKV_IN % IN_PROJ_COLS == 0 and IN_WIDTH % IN_PROJ_COLS == 0
    h = _rmsnorm(x_ref[...], g_ref[...]).astype(BF16)
    yield
    for j in range(IN_WIDTH // IN_PROJ_COLS):
        lo = j * IN_PROJ_COLS
        zj = jnp.dot(h, w_ref[:, lo:lo + IN_PROJ_COLS], preferred_element_type=F32)
        if lo < RWKV_IN:
            z_ref[Z_PAD:, lo:lo + IN_PROJ_COLS] = zj
        else:
            zs_ref[:, lo - RWKV_IN:lo - RWKV_IN + IN_PROJ_COLS] = zj
        yield


def _rwkv_prep_pieces(z_ref, mu_ref, w0_ref, waup_ref, a0_ref, kk_ref, ka_ref, rk_ref, prep,
                      chunks):
    seg01 = _seg01()
    lane = lax.broadcasted_iota(jnp.int32, (1, LORA_WA), 1)
    row = lax.broadcasted_iota(jnp.int32, (PREP_ROWS, 1), 0)
    ti = lax.broadcasted_iota(jnp.int32, (PREP_ROWS, PREP_ROWS), 0)
    tj = lax.broadcasted_iota(jnp.int32, (PREP_ROWS, PREP_ROWS), 1)
    ltri01 = ((tj <= ti) & ((ti >> 6) == (tj >> 6))).astype(BF16)
    chunks_per_scan_tile = TB_SCAN // CHUNK
    chunks_per_unit = PREP_ROWS // CHUNK
    half_w = RWKV_WIDTH // 2

    def shifted(c, cols):
        z = z_ref[Z_PAD + c * PREP_ROWS:Z_PAD + (c + 1) * PREP_ROWS, cols]
        before = z_ref[Z_PAD + c * PREP_ROWS - 1:Z_PAD + c * PREP_ROWS, cols]
        zprev = jnp.where(row == 0, before, pltpu.roll(z, 1, axis=0))
        return z + (zprev - z) * mu_ref[:, cols]

    for c in chunks:
        rows = slice(c * PREP_ROWS, (c + 1) * PREP_ROWS)
        wa_in = shifted(c, slice(3 * RWKV_WIDTH, 3 * RWKV_WIDTH + LORA_WA))
        wa_in = jnp.where(lane < LORA_WA // 2, jnp.tanh(wa_in), wa_in).astype(BF16)
        prep["gds"][rows, :] = jax.nn.sigmoid(
            shifted(c, slice(3 * RWKV_WIDTH + LORA_WA, RWKV_IN))).astype(BF16)
        for q in range(2):
            hc = slice(q * half_w, (q + 1) * half_w)
            r = shifted(c, hc)
            k = shifted(c, slice(RWKV_WIDTH + q * half_w, RWKV_WIDTH + (q + 1) * half_w))
            v = shifted(c, slice(2 * RWKV_WIDTH + q * half_w, 2 * RWKV_WIDTH + (q + 1) * half_w))
            prep["v"][rows, hc] = v.astype(BF16)
            kk = k * kk_ref[:, hc]
            kk_sq = _split_bf16(kk * kk, TERMS_HEAD_SUM)
            yield
            w_pre = w0_ref[:, hc] + jnp.dot(wa_in, waup_ref[:, hc], preferred_element_type=F32)
            a_pre = a0_ref[:, hc] + jnp.dot(
                wa_in, waup_ref[:, RWKV_WIDTH + q * half_w:RWKV_WIDTH + (q + 1) * half_w],
                preferred_element_type=F32)
            kk_ss = _head_sum_parts(kk_sq, seg01)
            yield
            a = jax.nn.sigmoid(a_pre)
            lw = jax.nn.sigmoid(w_pre) * (-EXP_M05 * LOG2_E)
            lw_parts = _split_bf16(lw, TERMS_DECAY_CUMSUM)
            kk = kk * lax.rsqrt(jnp.maximum(kk_ss, 1e-24))
            kmod = k * ((1.0 - ka_ref[:, hc]) + a * ka_ref[:, hc])
            kka = kk * a
            rkk = _split_bf16(r * kmod * rk_ref[:, hc], TERMS_HEAD_SUM)
            yield
            cs = _cumsum_rows(ltri01, lw_parts)
            prep["bonus"][rows, hc] = (_head_sum_parts(rkk, seg01) * v).astype(BF16)
            yield
            w_inv = jnp.exp2(-cs)
            w_last = [jnp.exp2(cs[(j + 1) * CHUNK - 1:(j + 1) * CHUNK, :])
                      for j in range(chunks_per_unit)]
            prep["rt"][rows, hc] = (r * jnp.exp2(cs)).astype(BF16)
            prep["at"][rows, hc] = (-kk * jnp.exp2(cs - lw)).astype(BF16)
            prep["bh"][rows, hc] = (kka * w_inv).astype(BF16)
            prep["kh"][rows, hc] = (kmod * w_inv).astype(BF16)
            for j in range(chunks_per_unit):
                cq, cr = divmod(c * chunks_per_unit + j, chunks_per_scan_tile)
                prep["wl"][cq, cr:cr + 1, hc] = w_last[j]
            yield


def _front_kernel(x_ref, g1_ref, win_ref, mu_ref, w0_ref, waup_ref, a0_ref, kk_ref, ka_ref, rk_ref,
                  *rest, tm):
    n = len(_PREP_NAMES)
    prep = dict(zip(_PREP_NAMES, rest[:n]))
    zs_ref = rest[n]
    wbf_ref, z0_ref, z1_ref = rest[n + 1:]
    i = pl.program_id(0)

    last = pl.num_programs(0) - 1
    interior = jnp.logical_and(i > 0, i < last)

    def step(z_write, z_read):
        n_chunks = tm // PREP_ROWS
        n_dot = 1 + IN_WIDTH // IN_PROJ_COLS + 1
        n_prep = PREP_PIECES_PER_CHUNK * n_chunks // PREP_STREAMS + 1
        stages = []
        if z_write is not None:
            stages.append((_in_proj_pieces(x_ref, g1_ref, wbf_ref, z_write, zs_ref), n_dot))
        if z_read is not None:
            stages += [
                (_rwkv_prep_pieces(z_read, mu_ref, w0_ref, waup_ref, a0_ref, kk_ref, ka_ref,
                                   rk_ref, prep, range(k, n_chunks, PREP_STREAMS)), n_prep)
                for k in range(PREP_STREAMS)]
            if TB_SCAN // CHUNK < WL_ROWS:
                prep["wl"][:, TB_SCAN // CHUNK:, :] = jnp.zeros(
                    (tm // TB_SCAN, WL_ROWS - TB_SCAN // CHUNK, RWKV_WIDTH), F32)
        _run_interleaved(stages, rounds=n_dot)
        if z_write is not None and z_read is not None:
            z_write[Z_PAD - 1:Z_PAD, :] = z_read[Z_PAD + tm - 1:Z_PAD + tm, :]

    @pl.when(i == 0)
    def _():
        wbf_ref[...] = win_ref[...].astype(BF16)
        z0_ref[Z_PAD - 1:Z_PAD, :] = jnp.zeros((1, RWKV_IN), F32)
        step(z0_ref, None)

    @pl.when(jnp.logical_and(interior, (i & 1) == 0))
    def _():
        step(z0_ref, z1_ref)

    @pl.when(jnp.logical_and(interior, (i & 1) == 1))
    def _():
        step(z1_ref, z0_ref)

    @pl.when(i == last)
    def _():
        step(None, z1_ref)


def _front(x, g1, w_in, mu, w0, waup, a0, k_k, k_a, r_k, tm):
    t = x.shape[0]
    n_tiles = t // tm
    assert n_tiles % 2 == 0
    vec = _full((1, RWKV_WIDTH))
    out_tile = lambda i: (jnp.maximum(i - 1, 0), 0)
    out_shapes = ([jax.ShapeDtypeStruct((t, RWKV_WIDTH), BF16)] * len(_PREP_BF16)
                  + [jax.ShapeDtypeStruct((t // TB_SCAN, WL_ROWS, RWKV_WIDTH), F32),
                     jax.ShapeDtypeStruct((t, GATE_LORA), BF16),
                     jax.ShapeDtypeStruct((t, 2 * SGU_WIDTH), F32)])
    out_specs = ([pl.BlockSpec((tm, RWKV_WIDTH), out_tile)] * len(_PREP_BF16)
                 + [pl.BlockSpec((tm // TB_SCAN, WL_ROWS, RWKV_WIDTH),
                                 lambda i: (jnp.maximum(i - 1, 0), 0, 0)),
                    pl.BlockSpec((tm, GATE_LORA), out_tile),
                    pl.BlockSpec((tm, 2 * SGU_WIDTH), lambda i: (jnp.minimum(i, n_tiles - 1), 0))])
    return pl.pallas_call(
        functools.partial(_front_kernel, tm=tm),
        out_shape=tuple(out_shapes),
        grid=(n_tiles + 1,),
        in_specs=[pl.BlockSpec((tm, D_MODEL), lambda i: (jnp.minimum(i, n_tiles - 1), 0)),
                  _full((1, D_MODEL)), _full((D_MODEL, IN_WIDTH)), _full((1, RWKV_IN)), vec,
                  _full((LORA_WA, 2 * RWKV_WIDTH)), vec, vec, vec, vec],
        out_specs=tuple(out_specs),
        scratch_shapes=[pltpu.VMEM((D_MODEL, IN_WIDTH), BF16),
                        pltpu.VMEM((Z_PAD + tm, RWKV_IN), F32),
                        pltpu.VMEM((Z_PAD + tm, RWKV_IN), F32)],
        compiler_params=_params(),
        name="front",
    )(x, g1, w_in, mu, w0, waup, a0, k_k, k_a, r_k)


def _pair_masks():
    t = lax.broadcasted_iota(jnp.int32, (CHUNK, PAIR), 0)
    j = lax.broadcasted_iota(jnp.int32, (CHUNK, PAIR), 1) & (CHUNK - 1)
    strict = j < t
    incl = j <= t
    blk16 = (t >> 4) == (j >> 4)
    blk32 = (t >> 5) == (j >> 5)
    return strict, incl, blk16, blk32


def _bd(x, bd_mask):
    x = x.astype(BF16)
    return jnp.where(bd_mask, jnp.concatenate([x, x], axis=0), 0.0).astype(BF16)


def _staged(fn, items, parts=2):
    out = []
    n = len(items) // parts
    for k in range(parts):
        out += [fn(*item) for item in items[k * n:(k + 1) * n]]
        yield
    return out


def _unit_lower_inverse_minus_identity(a_list, masks, bd_mask):
    _, _, blk16, blk32 = masks
    ad = [jnp.where(blk16, a, 0.0) for a in a_list]
    ap = yield from _staged(lambda x: _mm(x, _bd(x, bd_mask)), [(x,) for x in ad])
    tp = ad
    for _ in range(2):
        both = yield from _staged(
            lambda p, t: _mm(p, jnp.concatenate([_bd(p, bd_mask), _bd(t, bd_mask)], axis=1)),
            list(zip(ap, tp)))
        tp = [t + p + b[:, PAIR:] for t, p, b in zip(tp, ap, both)]
        ap = [b[:, :PAIR] for b in both]
    last = yield from _staged(lambda p, t: _mm(p, _bd(t, bd_mask)), list(zip(ap, tp)))
    tp = [t + p + x for t, p, x in zip(tp, ap, last)]
    for off_mask in (blk32 & ~blk16, ~blk32):
        off = [jnp.where(off_mask, a, 0.0) for a in a_list]
        x = yield from _staged(lambda o, t: o + _mm(t, _bd(o, bd_mask)), list(zip(off, tp)))
        tp = yield from _staged(lambda t, xx: t + xx + _mm(xx, _bd(t, bd_mask)), list(zip(tp, x)))
    return tp


_TERM_STAGES = 11
_TERM_NAMES = ("achk", "uv", "bb")


def _bd_masks():
    bi = lax.broadcasted_iota(jnp.int32, (PAIR, PAIR), 0) >> 6
    bj = lax.broadcasted_iota(jnp.int32, (PAIR, PAIR), 1) >> 6
    bd1 = bi == bj
    return bd1, jnp.concatenate([bd1, bd1], axis=1)


def _scan_terms_pieces(rt_ref, at_ref, bh_ref, kh_ref, v_ref, terms, *, tb):
    masks = _pair_masks()
    strict, incl = masks[0], masks[1]
    bd1, bd2 = _bd_masks()
    probs = [(c, p) for c in range(tb // CHUNK) for p in range(N_PAIRS)]
    cut = lambda ref: [ref[c * CHUNK:(c + 1) * CHUNK, p * PAIR:(p + 1) * PAIR] for c, p in probs]
    rt_p, at_p, bh_p, kh_p, v_p = map(cut, (rt_ref, at_ref, bh_ref, kh_ref, v_ref))
    gram = yield from _staged(
        lambda a_, r_, b_, k_: _mm(jnp.concatenate([a_, r_], axis=0),
                                   jnp.concatenate([_bd(b_, bd1), _bd(k_, bd1)], axis=0), _NT),
        list(zip(at_p, rt_p, bh_p, kh_p)))
    a_ab = [jnp.where(strict, g_[:CHUNK, :PAIR], 0.0) for g_ in gram]
    a_ak = [jnp.where(strict, g_[:CHUNK, PAIR:], 0.0) for g_ in gram]
    incl2 = jnp.concatenate([incl, incl], axis=1)
    for i, g_ in enumerate(gram):
        terms["bb"][i] = jnp.where(incl2, g_[CHUNK:], 0.0).astype(BF16)
    rhs = yield from _staged(
        lambda m_, x_, a_: jnp.concatenate([_mm(m_, _bd(x_, bd1)), a_.astype(F32)], axis=1),
        list(zip(a_ak, v_p, at_p)))
    tp = yield from _unit_lower_inverse_minus_identity(a_ab, masks, bd1)
    sol = yield from _staged(lambda x_, t_: x_ + _mm(t_, _bd(x_, bd2)), list(zip(rhs, tp)))
    for i, x_ in enumerate(sol):
        terms["uv"][i] = x_[:, :PAIR]
        terms["achk"][i] = x_[:, PAIR:].astype(BF16)


def _scan_state_pieces(terms, rt_ref, v_ref, bh_ref, kh_ref, wl_ref, gds_ref, gup_ref, bonus_ref,
                       lnw_ref, lnb_ref, o_ref, s_ref, y_ref, *, tb):
    bd1, _ = _bd_masks()
    s = [s_ref[p] for p in range(N_PAIRS)]
    for c in range(tb // CHUNK):
        rows = slice(c * CHUNK, (c + 1) * CHUNK)
        lanes = [slice(p * PAIR, (p + 1) * PAIR) for p in range(N_PAIRS)]
        idx = [c * N_PAIRS + p for p in range(N_PAIRS)]
        on_s = [_mm(jnp.concatenate([terms["achk"][i], rt_ref[rows, lanes[p]]], axis=0),
                    s[p], _NT) for p, i in enumerate(idx)]
        u = [x[:CHUNK] + terms["uv"][i] for x, i in zip(on_s, idx)]
        yield
        w_last = wl_ref[0, c:c + 1, :]
        upd = []
        for p, i in enumerate(idx):
            v_i = v_ref[rows, lanes[p]]
            y_ref[rows, lanes[p]] = (
                on_s[p][CHUNK:]
                + _mm(terms["bb"][i], jnp.concatenate([_bd(u[p], bd1), _bd(v_i, bd1)], axis=0)))
            upd.append(_mm(jnp.concatenate([u[p].astype(BF16), v_i], axis=0),
                           jnp.concatenate([bh_ref[rows, lanes[p]], kh_ref[rows, lanes[p]]], axis=0),
                           _TN))
        s = [s[p] * w_last[:, lanes[p]] + jnp.where(bd1, upd[p], 0.0) * w_last[:, lanes[p]]
             for p in range(N_PAIRS)]
        yield
    for p in range(N_PAIRS):
        s_ref[p] = s[p]
    seg01 = _seg01()
    y = y_ref[...]
    mean = _head_sum(y, seg01, TERMS_GROUP_MEAN) * (1.0 / RWKV_HEAD)
    yield
    d = y - mean
    var = _head_sum(d * d, seg01) * (1.0 / RWKV_HEAD)
    yield
    yn = d * lax.rsqrt(var + LNX_EPS) * lnw_ref[...] + lnb_ref[...]
    gate = jnp.dot(gds_ref[...], gup_ref[...], preferred_element_type=F32)
    o_ref[...] = ((yn + bonus_ref[...]) * gate).astype(BF16)


def _scan_kernel(rt_ref, at_ref, bh_ref, kh_ref, v_ref, rtp_ref, vp_ref, bhp_ref, khp_ref, bonus_ref,
                 gds_ref, wl_ref, gup_ref, lnw_ref, lnb_ref, wg_ref, wu_ref, wd_ref, o_ref, wg_bf_ref,
                 wu_bf_ref, wd_bf_ref, s_ref, y_ref, *term_refs, tb):
    n = len(_TERM_NAMES)
    slots = [dict(zip(_TERM_NAMES, term_refs[k * n:(k + 1) * n])) for k in range(2)]
    i = pl.program_id(0)

    last = pl.num_programs(0) - 1
    interior = jnp.logical_and(i > 0, i < last)

    wg_bf_ref[...] = wg_ref[...].astype(BF16)
    wu_bf_ref[...] = wu_ref[...].astype(BF16)
    wd_bf_ref[...] = wd_ref[...].astype(BF16)

    def step(write, read):
        n_terms = 2 * _TERM_STAGES + 1
        n_state = 2 * (tb // CHUNK) + 3
        stages = []
        if write is not None:
            stages.append(
                (_scan_terms_pieces(rt_ref, at_ref, bh_ref, kh_ref, v_ref, write, tb=tb), n_terms))
        if read is not None:
            stages.append(
                (_scan_state_pieces(read, rtp_ref, vp_ref, bhp_ref, khp_ref, wl_ref, gds_ref,
                                    gup_ref, bonus_ref, lnw_ref, lnb_ref, o_ref, s_ref, y_ref,
                                    tb=tb), n_state))
        _run_interleaved(stages, rounds=n_terms if write is not None else n_state)

    @pl.when(i == 0)
    def _():
        s_ref[...] = jnp.zeros_like(s_ref)
        step(slots[0], None)

    @pl.when(jnp.logical_and(interior, (i & 1) == 0))
    def _():
        step(slots[0], slots[1])

    @pl.when(jnp.logical_and(interior, (i & 1) == 1))
    def _():
        step(slots[1], slots[0])

    @pl.when(i == last)
    def _():
        step(None, slots[1])


def _scan(prep, gup, lnw, lnb, w_gate, w_up, w_down, tb):
    t = prep["rt"].shape[0]
    n_tiles = t // tb
    assert tb // CHUNK <= WL_ROWS
    last = n_tiles - 1
    cur = pl.BlockSpec((tb, RWKV_WIDTH), lambda i: (jnp.minimum(i, last), 0))
    prev = pl.BlockSpec((tb, RWKV_WIDTH), lambda i: (jnp.maximum(i - 1, 0), 0))
    vec = _full((1, RWKV_WIDTH))
    assert D_MODEL % n_tiles == 0 and (D_MODEL // n_tiles) % 16 == 0
    assert n_tiles % 2 == 0 and D_FF % (n_tiles // 2) == 0 and (D_FF // (n_tiles // 2)) % 16 == 0
    up_rows = pl.BlockSpec((D_MODEL // n_tiles, D_FF), lambda i: (jnp.minimum(i, last), 0))
    down_blk = pl.BlockSpec((D_FF // (n_tiles // 2), D_MODEL // 2),
                            lambda i: (jnp.minimum(i, last) // 2, jnp.minimum(i, last) % 2))
    n_prob = (tb // CHUNK) * N_PAIRS
    term_shapes = [pltpu.VMEM((n_prob, CHUNK, PAIR), BF16), pltpu.VMEM((n_prob, CHUNK, PAIR), F32),
                   pltpu.VMEM((n_prob, CHUNK, 2 * PAIR), BF16)]
    return pl.pallas_call(
        functools.partial(_scan_kernel, tb=tb),
        out_shape=(jax.ShapeDtypeStruct((t, RWKV_WIDTH), BF16),
                   jax.ShapeDtypeStruct((D_MODEL, D_FF), BF16),
                   jax.ShapeDtypeStruct((D_MODEL, D_FF), BF16),
                   jax.ShapeDtypeStruct((D_FF, D_MODEL), BF16)),
        grid=(n_tiles + 1,),
        in_specs=[cur] * 5 + [prev] * 5
        + [pl.BlockSpec((tb, GATE_LORA), lambda i: (jnp.maximum(i - 1, 0), 0)),
           pl.BlockSpec((1, WL_ROWS, RWKV_WIDTH), lambda i: (jnp.maximum(i - 1, 0), 0, 0)),
           _full((GATE_LORA, RWKV_WIDTH)), vec, vec, up_rows, up_rows, down_blk],
        out_specs=(prev, up_rows, up_rows, down_blk),
        scratch_shapes=[pltpu.VMEM((N_PAIRS, PAIR, PAIR), F32), pltpu.VMEM((tb, RWKV_WIDTH), F32)]
        + term_shapes * 2,
        compiler_params=_params(),
        name="scan",
    )(prep["rt"], prep["at"], prep["bh"], prep["kh"], prep["v"],
      prep["rt"], prep["v"], prep["bh"], prep["kh"], prep["bonus"], prep["gds"], prep["wl"],
      gup, lnw, lnb, w_gate, w_up, w_down)


def _sgu_block_prepare(z, lnw, lnb, sel):
    hz = _gelu_tanh(z)
    u = hz[:, :SGU_WIDTH]
    vf = hz[:, SGU_WIDTH:]
    mu = jnp.mean(vf, axis=-1, keepdims=True)
    d = vf - mu
    var = jnp.mean(d * d, axis=-1, keepdims=True)
    vn = d * lax.rsqrt(var + LN_EPS) * lnw + lnb
    stacks = []
    for p in range(SGU_WIDTH // PAIR):
        vb = vn[:, p * PAIR:(p + 1) * PAIR]
        stacks.append(jnp.where(sel, jnp.concatenate([vb, vb], axis=0), 0.0).astype(BF16))
    return u, stacks


def _sgu_block_mix(u, stacks, wcat, bias):
    return jnp.concatenate(
        [u[:, p * PAIR:(p + 1) * PAIR]
         * (jnp.dot(wcat[p], stacks[p], preferred_element_type=F32) + bias[:, p * PAIR:(p + 1) * PAIR])
         for p in range(SGU_WIDTH // PAIR)], axis=1)


def _mix_attn_group(r, x_ref, yr_ref, zs_ref, slnw_ref, slnb_ref, sbias_ref, wo1_ref, wo2_ref,
                    g2_ref, wq_ref, k_ref, v_ref, wo_ref, o_ref, wcat, sel):
    heads = [slice(hd * XA_HEAD_DIM, (hd + 1) * XA_HEAD_DIM) for hd in range(XA_HEADS)]
    prepared = [_sgu_block_prepare(zs_ref[b:b + SGU_BLOCK, :], slnw_ref[...], slnb_ref[...], sel)
                for b in range(r.start, r.stop, SGU_BLOCK)]
    yield
    x1 = x_ref[r, :] + jnp.dot(yr_ref[r, :].astype(BF16), wo1_ref[...],
                               preferred_element_type=F32)
    y_sgu = jnp.concatenate([_sgu_block_mix(u, st, wcat, sbias_ref[...]) for u, st in prepared],
                            axis=0)
    yield
    x1 = x1 + jnp.dot(y_sgu.astype(BF16), wo2_ref[...], preferred_element_type=F32)
    yield
    h = _rmsnorm(x1, g2_ref[...]).astype(BF16)
    yield
    q = jnp.dot(h, wq_ref[...], preferred_element_type=F32).astype(BF16)
    s = [lax.dot_general(q[:, hl], k_ref[:, hl], _NT, preferred_element_type=F32)
         * (XA_HEAD_DIM ** -0.5) for hl in heads]
    yield
    p = []
    for s_h in s:
        e = jnp.exp(s_h - jnp.max(s_h, axis=-1, keepdims=True))
        p.append((e / jnp.sum(e, axis=-1, keepdims=True)).astype(BF16))
    yield
    o = jnp.concatenate([jnp.dot(p_h, v_ref[:, hl], preferred_element_type=F32)
                         for p_h, hl in zip(p, heads)], axis=1).astype(BF16)
    o_ref[r, :] = x1 + jnp.dot(o, wo_ref[...], preferred_element_type=F32)


def _mix_attn_kernel(x_ref, yr_ref, zs_ref, slnw_ref, slnb_ref, ws_ref, sbias_ref, wout_f32_ref,
                     g2_ref, wq_f32_ref, k_ref, v_ref, wo_f32_ref, o_ref, wout_ref, wq_ref, wo_ref):
    @pl.when(pl.program_id(0) == 0)
    def _():
        wout_ref[...] = wout_f32_ref[...].astype(BF16)
        wq_ref[...] = wq_f32_ref[...].astype(BF16)
        wo_ref[...] = wo_f32_ref[...].astype(BF16)

    wo1_ref = wout_ref.at[:RWKV_WIDTH]
    wo2_ref = wout_ref.at[RWKV_WIDTH:]
    tm = x_ref.shape[0]
    ti = lax.broadcasted_iota(jnp.int32, (SGU_BLOCK, SGU_BLOCK), 0)
    tj = lax.broadcasted_iota(jnp.int32, (SGU_BLOCK, SGU_BLOCK), 1)
    tril = tj <= ti
    wcat = [jnp.concatenate([jnp.where(tril, ws_ref[2 * p], 0.0),
                             jnp.where(tril, ws_ref[2 * p + 1], 0.0)], axis=1).astype(BF16)
            for p in range(SGU_WIDTH // PAIR)]
    bi = lax.broadcasted_iota(jnp.int32, (2 * SGU_BLOCK, PAIR), 0) >> 7
    bj = lax.broadcasted_iota(jnp.int32, (2 * SGU_BLOCK, PAIR), 1) >> 6
    sel = bi == bj
    gens = [_mix_attn_group(slice(r, r + ATTN_ROW_GROUP), x_ref, yr_ref, zs_ref, slnw_ref, slnb_ref,
                            sbias_ref, wo1_ref, wo2_ref, g2_ref, wq_ref, k_ref, v_ref, wo_ref,
                            o_ref, wcat, sel)
            for r in range(0, tm, ATTN_ROW_GROUP)]
    _run_wavefront(gens, ATTN_STAGGER)


def _mix_attn(x, yr, zs, slnw, slnb, ws, sbias, w_out, g2, wq, k, v, wo, tm):
    t = x.shape[0]
    sq = _full((D_MODEL, D_MODEL))
    return pl.pallas_call(
        _mix_attn_kernel,
        out_shape=jax.ShapeDtypeStruct((t, D_MODEL), F32),
        grid=(t // tm,),
        in_specs=[pl.BlockSpec((tm, D_MODEL), lambda i: (i, 0)),
                  pl.BlockSpec((tm, RWKV_WIDTH), lambda i: (i, 0)),
                  pl.BlockSpec((tm, 2 * SGU_WIDTH), lambda i: (i, 0)),
                  _full((1, SGU_WIDTH)), _full((1, SGU_WIDTH)),
                  _full((SGU_GROUPS, SGU_BLOCK, SGU_BLOCK)), _full((SGU_BLOCK, SGU_WIDTH)),
                  sq, _full((1, D_MODEL)), sq,
                  _full((MEM_LEN, D_MODEL)), _full((MEM_LEN, D_MODEL)), sq],
        out_specs=pl.BlockSpec((tm, D_MODEL), lambda i: (i, 0)),
        scratch_shapes=[pltpu.VMEM((D_MODEL, D_MODEL), BF16)] * 3,
        compiler_params=_params(),
        name="mix_attn",
    )(x, yr, zs, slnw, slnb, ws, sbias, w_out, g2, wq, k, v, wo)


def _ffn_kernel(x_ref, g3_ref, wg_ref, wu_ref, wd_ref, gf_ref, o_ref):
    tm = x_ref.shape[0]
    groups = [slice(r, r + FFN_ROW_GROUP) for r in range(0, tm, FFN_ROW_GROUP)]
    x2 = [x_ref[r, :] for r in groups]
    h = [_rmsnorm(x, g3_ref[...]).astype(BF16) for x in x2]
    gate = [jnp.dot(h_, wg_ref[...], preferred_element_type=F32) for h_ in h]
    up = [jnp.dot(h_, wu_ref[...], preferred_element_type=F32) for h_ in h]
    act = [(jax.nn.silu(g_) * u_).astype(BF16) for g_, u_ in zip(gate, up)]
    x3 = [x + jnp.dot(a_, wd_ref[...], preferred_element_type=F32) for x, a_ in zip(x2, act)]
    for r, x in zip(groups, x3):
        o_ref[r, :] = _rmsnorm(x, gf_ref[...])


def _ffn(x, g3, wg, wu, wd, gf, tm):
    t = x.shape[0]
    return pl.pallas_call(
        _ffn_kernel,
        out_shape=jax.ShapeDtypeStruct((t, D_MODEL), F32),
        grid=(t // tm,),
        in_specs=[pl.BlockSpec((tm, D_MODEL), lambda i: (i, 0)), _full((1, D_MODEL)),
                  _full((D_MODEL, D_FF)), _full((D_MODEL, D_FF)), _full((D_FF, D_MODEL)),
                  _full((1, D_MODEL))],
        out_specs=pl.BlockSpec((tm, D_MODEL), lambda i: (i, 0)),
        compiler_params=_params("parallel"),
        name="ffn",
    )(x, g3, wg, wu, wd, gf)


def kernel(x, mem, norm1_g, w_in, shift_mu, w0, w_lora_up, a0, a_lora_up, g_lora_up, k_k, k_a, r_k,
           lnx_w, lnx_b, sgu_ln_w, sgu_ln_b, w_spatial, b_spatial, w_out, norm2_g, mem_norm_g,
           wq_x, wk_x, wv_x, wo_x, norm3_g, w_gate, w_up, w_down, norm_f_g):
    b, t, _ = x.shape
    depth = w_in.shape[0]
    assert depth == 1, "the final RMSNorm is fused into the (single) layer's ffn call"
    assert t % TM_ATTN == 0 and t % TM_FFN == 0
    assert t % TM_DENSE == 0 and TM_DENSE % TB_SCAN == 0 and TB_SCAN % CHUNK == 0
    row = lambda p: p.reshape(1, -1)
    bf = lambda p: p.astype(BF16)
    outs = []
    for bi in range(b):
        xb = x[bi]
        for l in range(depth):
            lora = w_lora_up.shape[1]
            zeros = jnp.zeros((lora, RWKV_WIDTH), F32)
            waup = jnp.concatenate(
                [jnp.concatenate([w_lora_up[l], zeros], axis=1),
                 jnp.concatenate([zeros, a_lora_up[l]], axis=1)], axis=0)
            bias = jnp.repeat(b_spatial[l].T, SGU_WIDTH // SGU_GROUPS, axis=1)

            front = _front(xb, row(norm1_g[l]), w_in[l], row(shift_mu[l]), row(w0[l]), bf(waup),
                           row(a0[l]), row(k_k[l]), row(k_a[l]), row(r_k[l]), TM_DENSE)
            prep = dict(zip(_PREP_NAMES, front[:len(_PREP_NAMES)]))
            z_sgu = front[len(_PREP_NAMES)]
            y_rwkv, wg_bf, wu_bf, wd_bf = _scan(prep, bf(g_lora_up[l]), row(lnx_w[l]), row(lnx_b[l]),
                                                w_gate[l], w_up[l], w_down[l], TB_SCAN)
            k_mem, v_mem = _mem_kv(mem[bi], row(mem_norm_g[l]), wk_x[l], wv_x[l])
            x2 = _mix_attn(xb, y_rwkv, z_sgu, row(sgu_ln_w[l]), row(sgu_ln_b[l]), w_spatial[l], bias,
                           w_out[l], row(norm2_g[l]), wq_x[l], k_mem, v_mem, wo_x[l], TM_ATTN)
            xb = _ffn(x2, row(norm3_g[l]), wg_bf, wu_bf, wd_bf, row(norm_f_g), TM_FFN)
        outs.append(xb)
    return jnp.stack(outs, axis=0)
```

```python
import functools
import math

import jax
import jax.numpy as jnp
from jax import lax
from jax.experimental import pallas as pl
from jax.experimental.pallas import tpu as pltpu

F32 = jnp.float32
BF16 = jnp.bfloat16

D_MODEL = 1024
RWKV_WIDTH = 512
RWKV_HEAD = 64
LORA_WA = 128
GATE_LORA = 128
RWKV_IN = 3 * RWKV_WIDTH + LORA_WA + GATE_LORA
SGU_WIDTH = 512
SGU_GROUPS = 8
SGU_BLOCK = 128
IN_WIDTH = RWKV_IN + 2 * SGU_WIDTH
MEM_LEN = 256
XA_HEADS = 4
XA_HEAD_DIM = D_MODEL // XA_HEADS
D_FF = 2816
RMS_EPS = 1e-6
LN_EPS = 1e-5
LNX_EPS = 64e-5
EXP_M05 = 0.6065306597126334
LOG2_E = 1.4426950408889634

CHUNK = 64
PAIR = 2 * RWKV_HEAD
N_PAIRS = RWKV_WIDTH // PAIR
HEAD_BLOCK = 2 * PAIR
TM_DENSE = 512
TM_ATTN = 1024
TM_FFN = 1024
TB_SCAN = 512
Z_PAD = 8
WL_ROWS = 8
ATTN_ROW_GROUP = 256
ATTN_STAGGER = 2
FFN_ROW_GROUP = 256
IN_PROJ_COLS = 256
SCAN_NORM_ROWS = 256
PREP_ROWS = 128
PREP_STREAMS = 4
PREP_PIECES_PER_CHUNK = 10
TERMS_DECAY_CUMSUM = 2
TERMS_HEAD_SUM = 1
TERMS_GROUP_MEAN = 2
VMEM_LIMIT = 56 * 1024 * 1024

_NN = (((1,), (0,)), ((), ()))
_NT = (((1,), (1,)), ((), ()))
_TN = (((0,), (0,)), ((), ()))


def _mm(a, b, dims=_NN):
    return lax.dot_general(a.astype(BF16), b.astype(BF16), dims, preferred_element_type=F32)


def _split_bf16(x, terms):
    parts = []
    rem = x
    for _ in range(terms):
        part = rem.astype(BF16)
        rem = rem - part.astype(F32)
        parts.append(part)
    return parts


def _cumsum_rows(ltri01, parts):
    return lax.dot_general(jnp.concatenate([ltri01] * len(parts), axis=1),
                           jnp.concatenate(parts, axis=0), _NN, preferred_element_type=F32)


def _head_sum_parts(parts, seg01):
    cols = []
    for q in range(parts[0].shape[1] // HEAD_BLOCK):
        acc = None
        for part in parts:
            d = lax.dot_general(part[:, HEAD_BLOCK * q:HEAD_BLOCK * (q + 1)], seg01, _NN,
                                preferred_element_type=F32)
            acc = d if acc is None else acc + d
        cols.append(acc)
    return jnp.concatenate(cols, axis=1)


def _head_sum(x, seg01, terms=TERMS_HEAD_SUM):
    return _head_sum_parts(_split_bf16(x, terms), seg01)


def _seg01():
    li = lax.broadcasted_iota(jnp.int32, (HEAD_BLOCK, HEAD_BLOCK), 0) >> 6
    lj = lax.broadcasted_iota(jnp.int32, (HEAD_BLOCK, HEAD_BLOCK), 1) >> 6
    return (li == lj).astype(BF16)


def _gelu_tanh(x):
    k1 = -2.0 * math.sqrt(2.0 / math.pi) * math.log2(math.e)
    return x / (1.0 + jnp.exp2(x * (k1 + (k1 * 0.044715) * (x * x))))


def _rmsnorm(x, g):
    return x * lax.rsqrt(jnp.mean(x * x, axis=-1, keepdims=True) + RMS_EPS) * g


def _full(shape):
    n = len(shape)
    return pl.BlockSpec(shape, lambda i: (0,) * n, pipeline_mode=pl.Buffered(1))


def _params(sem="arbitrary"):
    return pltpu.CompilerParams(dimension_semantics=(sem,), vmem_limit_bytes=VMEM_LIMIT)


def _run_interleaved(stages, rounds):
    for r in range(rounds):
        for gen, n in stages:
            for _ in range((r + 1) * n // rounds - r * n // rounds):
                next(gen, None)
    for gen, _ in stages:
        assert next(gen, StopIteration) is StopIteration, "piece count too small"


def _run_wavefront(gens, stagger):
    live = list(enumerate(gens))
    r = 0
    while live:
        for entry in list(live):
            g, gen = entry
            if r >= g * stagger and next(gen, StopIteration) is StopIteration:
                live.remove(entry)
        r += 1


def _mem_kv_kernel(mem_ref, g_ref, wk_ref, wv_ref, k_ref, v_ref):
    m = _rmsnorm(mem_ref[...], g_ref[...]).astype(BF16)
    k_ref[...] = jnp.dot(m, wk_ref[...].astype(BF16), preferred_element_type=F32).astype(BF16)
    v_ref[...] = jnp.dot(m, wv_ref[...].astype(BF16), preferred_element_type=F32).astype(BF16)


def _mem_kv(mem, g, wk, wv):
    return pl.pallas_call(
        _mem_kv_kernel,
        out_shape=(jax.ShapeDtypeStruct((MEM_LEN, D_MODEL), BF16),) * 2,
        grid=(1,),
        in_specs=[_full((MEM_LEN, D_MODEL)), _full((1, D_MODEL)),
                  _full((D_MODEL, D_MODEL)), _full((D_MODEL, D_MODEL))],
        out_specs=(_full((MEM_LEN, D_MODEL)),) * 2,
        compiler_params=_params(),
        name="mem_kv",
    )(mem, g, wk, wv)


_PREP_BF16 = ("at", "bonus", "rt", "bh", "kh", "v")
_PREP_NAMES = _PREP_BF16 + ("wl", "gds")


def _in_proj_pieces(x_ref, g_ref, w_ref, z_ref, zs_ref):
    assert RWKV_IN % IN_PROJ_COLS == 0 and IN_WIDTH % IN_PROJ_COLS == 0
    h = _rmsnorm(x_ref[...], g_ref[...]).astype(BF16)
    yield
    for j in range(IN_WIDTH // IN_PROJ_COLS):
        lo = j * IN_PROJ_COLS
        zj = jnp.dot(h, w_ref[:, lo:lo + IN_PROJ_COLS], preferred_element_type=F32)
        if lo < RWKV_IN:
            z_ref[Z_PAD:, lo:lo + IN_PROJ_COLS] = zj
        else:
            zs_ref[:, lo - RWKV_IN:lo - RWKV_IN + IN_PROJ_COLS] = zj
        yield


def _rwkv_prep_pieces(z_ref, mu_ref, w0_ref, waup_ref, a0_ref, kk_ref, ka_ref, rk_ref, prep,
                      chunks):
    seg01 = _seg01()
    lane = lax.broadcasted_iota(jnp.int32, (1, LORA_WA), 1)
    row = lax.broadcasted_iota(jnp.int32, (PREP_ROWS, 1), 0)
    ti = lax.broadcasted_iota(jnp.int32, (PREP_ROWS, PREP_ROWS), 0)
    tj = lax.broadcasted_iota(jnp.int32, (PREP_ROWS, PREP_ROWS), 1)
    ltri01 = ((tj <= ti) & ((ti >> 6) == (tj >> 6))).astype(BF16)
    chunks_per_scan_tile = TB_SCAN // CHUNK
    chunks_per_unit = PREP_ROWS // CHUNK
    half_w = RWKV_WIDTH // 2

    def shifted(c, cols):
        z = z_ref[Z_PAD + c * PREP_ROWS:Z_PAD + (c + 1) * PREP_ROWS, cols]
        before = z_ref[Z_PAD + c * PREP_ROWS - 1:Z_PAD + c * PREP_ROWS, cols]
        zprev = jnp.where(row == 0, before, pltpu.roll(z, 1, axis=0))
        return z + (zprev - z) * mu_ref[:, cols]

    for c in chunks:
        rows = slice(c * PREP_ROWS, (c + 1) * PREP_ROWS)
        wa_in = shifted(c, slice(3 * RWKV_WIDTH, 3 * RWKV_WIDTH + LORA_WA))
        wa_in = jnp.where(lane < LORA_WA // 2, jnp.tanh(wa_in), wa_in).astype(BF16)
        prep["gds"][rows, :] = jax.nn.sigmoid(
            shifted(c, slice(3 * RWKV_WIDTH + LORA_WA, RWKV_IN))).astype(BF16)
        for q in range(2):
            hc = slice(q * half_w, (q + 1) * half_w)
            r = shifted(c, hc)
            k = shifted(c, slice(RWKV_WIDTH + q * half_w, RWKV_WIDTH + (q + 1) * half_w))
            v = shifted(c, slice(2 * RWKV_WIDTH + q * half_w, 2 * RWKV_WIDTH + (q + 1) * half_w))
            prep["v"][rows, hc] = v.astype(BF16)
            kk = k * kk_ref[:, hc]
            kk_sq = _split_bf16(kk * kk, TERMS_HEAD_SUM)
            yield
            w_pre = w0_ref[:, hc] + jnp.dot(wa_in, waup_ref[:, hc], preferred_element_type=F32)
            a_pre = a0_ref[:, hc] + jnp.dot(
                wa_in, waup_ref[:, RWKV_WIDTH + q * half_w:RWKV_WIDTH + (q + 1) * half_w],
                preferred_element_type=F32)
            kk_ss = _head_sum_parts(kk_sq, seg01)
            yield
            a = jax.nn.sigmoid(a_pre)
            lw = jax.nn.sigmoid(w_pre) * (-EXP_M05 * LOG2_E)
            lw_parts = _split_bf16(lw, TERMS_DECAY_CUMSUM)
            kk = kk * lax.rsqrt(jnp.maximum(kk_ss, 1e-24))
            kmod = k * ((1.0 - ka_ref[:, hc]) + a * ka_ref[:, hc])
            kka = kk * a
            rkk = _split_bf16(r * kmod * rk_ref[:, hc], TERMS_HEAD_SUM)
            yield
            cs = _cumsum_rows(ltri01, lw_parts)
            prep["bonus"][rows, hc] = (_head_sum_parts(rkk, seg01) * v).astype(BF16)
            yield
            w_inv = jnp.exp2(-cs)
            w_last = [jnp.exp2(cs[(j + 1) * CHUNK - 1:(j + 1) * CHUNK, :])
                      for j in range(chunks_per_unit)]
            prep["rt"][rows, hc] = (r * jnp.exp2(cs)).astype(BF16)
            prep["at"][rows, hc] = (-kk * jnp.exp2(cs - lw)).astype(BF16)
            prep["bh"][rows, hc] = (kka * w_inv).astype(BF16)
            prep["kh"][rows, hc] = (kmod * w_inv).astype(BF16)
            for j in range(chunks_per_unit):
                cq, cr = divmod(c * chunks_per_unit + j, chunks_per_scan_tile)
                prep["wl"][cq, cr:cr + 1, hc] = w_last[j]
            yield


def _front_kernel(x_ref, g1_ref, win_ref, mu_ref, w0_ref, waup_ref, a0_ref, kk_ref, ka_ref, rk_ref,
                  *rest, tm):
    n = len(_PREP_NAMES)
    prep = dict(zip(_PREP_NAMES, rest[:n]))
    zs_ref = rest[n]
    wbf_ref, z0_ref, z1_ref = rest[n + 1:]
    i = pl.program_id(0)

    last = pl.num_programs(0) - 1
    interior = jnp.logical_and(i > 0, i < last)

    def step(z_write, z_read):
        n_chunks = tm // PREP_ROWS
        n_dot = 1 + IN_WIDTH // IN_PROJ_COLS + 1
        n_prep = PREP_PIECES_PER_CHUNK * n_chunks // PREP_STREAMS + 1
        stages = []
        if z_write is not None:
            stages.append((_in_proj_pieces(x_ref, g1_ref, wbf_ref, z_write, zs_ref), n_dot))
        if z_read is not None:
            stages += [
                (_rwkv_prep_pieces(z_read, mu_ref, w0_ref, waup_ref, a0_ref, kk_ref, ka_ref,
                                   rk_ref, prep, range(k, n_chunks, PREP_STREAMS)), n_prep)
                for k in range(PREP_STREAMS)]
            if TB_SCAN // CHUNK < WL_ROWS:
                prep["wl"][:, TB_SCAN // CHUNK:, :] = jnp.zeros(
                    (tm // TB_SCAN, WL_ROWS - TB_SCAN // CHUNK, RWKV_WIDTH), F32)
        _run_interleaved(stages, rounds=n_dot)
        if z_write is not None and z_read is not None:
            z_write[Z_PAD - 1:Z_PAD, :] = z_read[Z_PAD + tm - 1:Z_PAD + tm, :]

    @pl.when(i == 0)
    def _():
        wbf_ref[...] = win_ref[...].astype(BF16)
        z0_ref[Z_PAD - 1:Z_PAD, :] = jnp.zeros((1, RWKV_IN), F32)
        step(z0_ref, None)

    @pl.when(jnp.logical_and(interior, (i & 1) == 0))
    def _():
        step(z0_ref, z1_ref)

    @pl.when(jnp.logical_and(interior, (i & 1) == 1))
    def _():
        step(z1_ref, z0_ref)

    @pl.when(i == last)
    def _():
        step(None, z1_ref)


def _front(x, g1, w_in, mu, w0, waup, a0, k_k, k_a, r_k, tm):
    t = x.shape[0]
    n_tiles = t // tm
    assert n_tiles % 2 == 0
    vec = _full((1, RWKV_WIDTH))
    out_tile = lambda i: (jnp.maximum(i - 1, 0), 0)
    out_shapes = ([jax.ShapeDtypeStruct((t, RWKV_WIDTH), BF16)] * len(_PREP_BF16)
                  + [jax.ShapeDtypeStruct((t // TB_SCAN, WL_ROWS, RWKV_WIDTH), F32),
                     jax.ShapeDtypeStruct((t, GATE_LORA), BF16),
                     jax.ShapeDtypeStruct((t, 2 * SGU_WIDTH), F32)])
    out_specs = ([pl.BlockSpec((tm, RWKV_WIDTH), out_tile)] * len(_PREP_BF16)
                 + [pl.BlockSpec((tm // TB_SCAN, WL_ROWS, RWKV_WIDTH),
                                 lambda i: (jnp.maximum(i - 1, 0), 0, 0)),
                    pl.BlockSpec((tm, GATE_LORA), out_tile),
                    pl.BlockSpec((tm, 2 * SGU_WIDTH), lambda i: (jnp.minimum(i, n_tiles - 1), 0))])
    return pl.pallas_call(
        functools.partial(_front_kernel, tm=tm),
        out_shape=tuple(out_shapes),
        grid=(n_tiles + 1,),
        in_specs=[pl.BlockSpec((tm, D_MODEL), lambda i: (jnp.minimum(i, n_tiles - 1), 0)),
                  _full((1, D_MODEL)), _full((D_MODEL, IN_WIDTH)), _full((1, RWKV_IN)), vec,
                  _full((LORA_WA, 2 * RWKV_WIDTH)), vec, vec, vec, vec],
        out_specs=tuple(out_specs),
        scratch_shapes=[pltpu.VMEM((D_MODEL, IN_WIDTH), BF16),
                        pltpu.VMEM((Z_PAD + tm, RWKV_IN), F32),
                        pltpu.VMEM((Z_PAD + tm, RWKV_IN), F32)],
        compiler_params=_params(),
        name="front",
    )(x, g1, w_in, mu, w0, waup, a0, k_k, k_a, r_k)


def _pair_masks():
    t = lax.broadcasted_iota(jnp.int32, (CHUNK, PAIR), 0)
    j = lax.broadcasted_iota(jnp.int32, (CHUNK, PAIR), 1) & (CHUNK - 1)
    strict = j < t
    incl = j <= t
    blk16 = (t >> 4) == (j >> 4)
    blk32 = (t >> 5) == (j >> 5)
    return strict, incl, blk16, blk32


def _bd(x, bd_mask):
    x = x.astype(BF16)
    return jnp.where(bd_mask, jnp.concatenate([x, x], axis=0), 0.0).astype(BF16)


def _staged(fn, items, parts=2):
    out = []
    n = len(items) // parts
    for k in range(parts):
        out += [fn(*item) for item in items[k * n:(k + 1) * n]]
        yield
    return out


def _unit_lower_inverse_minus_identity(a_list, masks, bd_mask):
    _, _, blk16, blk32 = masks
    ad = [jnp.where(blk16, a, 0.0) for a in a_list]
    ap = yield from _staged(lambda x: _mm(x, _bd(x, bd_mask)), [(x,) for x in ad])
    tp = ad
    for _ in range(2):
        both = yield from _staged(
            lambda p, t: _mm(p, jnp.concatenate([_bd(p, bd_mask), _bd(t, bd_mask)], axis=1)),
            list(zip(ap, tp)))
        tp = [t + p + b[:, PAIR:] for t, p, b in zip(tp, ap, both)]
        ap = [b[:, :PAIR] for b in both]
    last = yield from _staged(lambda p, t: _mm(p, _bd(t, bd_mask)), list(zip(ap, tp)))
    tp = [t + p + x for t, p, x in zip(tp, ap, last)]
    for off_mask in (blk32 & ~blk16, ~blk32):
        off = [jnp.where(off_mask, a, 0.0) for a in a_list]
        x = yield from _staged(lambda o, t: o + _mm(t, _bd(o, bd_mask)), list(zip(off, tp)))
        tp = yield from _staged(lambda t, xx: t + xx + _mm(xx, _bd(t, bd_mask)), list(zip(tp, x)))
    return tp


_TERM_STAGES = 11
_TERM_NAMES = ("achk", "uv", "bb")


def _bd_masks():
    bi = lax.broadcasted_iota(jnp.int32, (PAIR, PAIR), 0) >> 6
    bj = lax.broadcasted_iota(jnp.int32, (PAIR, PAIR), 1) >> 6
    bd1 = bi == bj
    return bd1, jnp.concatenate([bd1, bd1], axis=1)


def _scan_terms_pieces(rt_ref, at_ref, bh_ref, kh_ref, v_ref, terms, *, tb):
    masks = _pair_masks()
    strict, incl = masks[0], masks[1]
    bd1, bd2 = _bd_masks()
    probs = [(c, p) for c in range(tb // CHUNK) for p in range(N_PAIRS)]
    cut = lambda ref: [ref[c * CHUNK:(c + 1) * CHUNK, p * PAIR:(p + 1) * PAIR] for c, p in probs]
    rt_p, at_p, bh_p, kh_p, v_p = map(cut, (rt_ref, at_ref, bh_ref, kh_ref, v_ref))
    gram = yield from _staged(
        lambda a_, r_, b_, k_: _mm(jnp.concatenate([a_, r_], axis=0),
                                   jnp.concatenate([_bd(b_, bd1), _bd(k_, bd1)], axis=0), _NT),
        list(zip(at_p, rt_p, bh_p, kh_p)))
    a_ab = [jnp.where(strict, g_[:CHUNK, :PAIR], 0.0) for g_ in gram]
    a_ak = [jnp.where(strict, g_[:CHUNK, PAIR:], 0.0) for g_ in gram]
    incl2 = jnp.concatenate([incl, incl], axis=1)
    for i, g_ in enumerate(gram):
        terms["bb"][i] = jnp.where(incl2, g_[CHUNK:], 0.0).astype(BF16)
    rhs = yield from _staged(
        lambda m_, x_, a_: jnp.concatenate([_mm(m_, _bd(x_, bd1)), a_.astype(F32)], axis=1),
        list(zip(a_ak, v_p, at_p)))
    tp = yield from _unit_lower_inverse_minus_identity(a_ab, masks, bd1)
    sol = yield from _staged(lambda x_, t_: x_ + _mm(t_, _bd(x_, bd2)), list(zip(rhs, tp)))
    for i, x_ in enumerate(sol):
        terms["uv"][i] = x_[:, :PAIR]
        terms["achk"][i] = x_[:, PAIR:].astype(BF16)


def _scan_state_pieces(terms, rt_ref, v_ref, bh_ref, kh_ref, wl_ref, gds_ref, gup_ref, bonus_ref,
                       lnw_ref, lnb_ref, o_ref, s_ref, y_ref, *, tb):
    bd1, _ = _bd_masks()
    seg01 = _seg01()
    chunks_per_norm = SCAN_NORM_ROWS // CHUNK
    assert (tb // CHUNK) % chunks_per_norm == 0

    def norm_pieces(rows):
        y = y_ref[rows, :]
        mean = _head_sum(y, seg01, TERMS_GROUP_MEAN) * (1.0 / RWKV_HEAD)
        yield
        d = y - mean
        var = _head_sum(d * d, seg01) * (1.0 / RWKV_HEAD)
        yield
        yn = d * lax.rsqrt(var + LNX_EPS) * lnw_ref[...] + lnb_ref[...]
        gate = jnp.dot(gds_ref[rows, :], gup_ref[...], preferred_element_type=F32)
        o_ref[rows, :] = ((yn + bonus_ref[rows, :]) * gate).astype(BF16)

    pending = []

    def advance():
        for gen in list(pending):
            if next(gen, StopIteration) is StopIteration:
                pending.remove(gen)

    s = [s_ref[p] for p in range(N_PAIRS)]
    for c in range(tb // CHUNK):
        rows = slice(c * CHUNK, (c + 1) * CHUNK)
        lanes = [slice(p * PAIR, (p + 1) * PAIR) for p in range(N_PAIRS)]
        idx = [c * N_PAIRS + p for p in range(N_PAIRS)]
        on_s = [_mm(jnp.concatenate([terms["achk"][i], rt_ref[rows, lanes[p]]], axis=0),
                    s[p], _NT) for p, i in enumerate(idx)]
        u = [x[:CHUNK] + terms["uv"][i] for x, i in zip(on_s, idx)]
        advance()
        yield
        w_last = wl_ref[0, c:c + 1, :]
        upd = []
        for p, i in enumerate(idx):
            v_i = v_ref[rows, lanes[p]]
            y_ref[rows, lanes[p]] = (
                on_s[p][CHUNK:]
                + _mm(terms["bb"][i], jnp.concatenate([_bd(u[p], bd1), _bd(v_i, bd1)], axis=0)))
            upd.append(_mm(jnp.concatenate([u[p].astype(BF16), v_i], axis=0),
                           jnp.concatenate([bh_ref[rows, lanes[p]], kh_ref[rows, lanes[p]]], axis=0),
                           _TN))
        s = [s[p] * w_last[:, lanes[p]] + jnp.where(bd1, upd[p], 0.0) * w_last[:, lanes[p]]
             for p in range(N_PAIRS)]
        advance()
        yield
        if (c + 1) % chunks_per_norm == 0:
            pending.append(norm_pieces(slice((c + 1 - chunks_per_norm) * CHUNK, (c + 1) * CHUNK)))
    for p in range(N_PAIRS):
        s_ref[p] = s[p]
    while pending:
        advance()
        if pending:
            yield


def _scan_kernel(rt_ref, at_ref, bh_ref, kh_ref, v_ref, rtp_ref, vp_ref, bhp_ref, khp_ref, bonus_ref,
                 gds_ref, wl_ref, gup_ref, lnw_ref, lnb_ref, wg_ref, wu_ref, wd_ref, o_ref, wg_bf_ref,
                 wu_bf_ref, wd_bf_ref, s_ref, y_ref, *term_refs, tb):
    n = len(_TERM_NAMES)
    slots = [dict(zip(_TERM_NAMES, term_refs[k * n:(k + 1) * n])) for k in range(2)]
    i = pl.program_id(0)

    last = pl.num_programs(0) - 1
    interior = jnp.logical_and(i > 0, i < last)

    wg_bf_ref[...] = wg_ref[...].astype(BF16)
    wu_bf_ref[...] = wu_ref[...].astype(BF16)
    wd_bf_ref[...] = wd_ref[...].astype(BF16)

    def step(write, read):
        n_terms = 2 * _TERM_STAGES + 1
        n_state = 2 * (tb // CHUNK) + 3
        stages = []
        if write is not None:
            stages.append(
                (_scan_terms_pieces(rt_ref, at_ref, bh_ref, kh_ref, v_ref, write, tb=tb), n_terms))
        if read is not None:
            stages.append(
                (_scan_state_pieces(read, rtp_ref, vp_ref, bhp_ref, khp_ref, wl_ref, gds_ref,
                                    gup_ref, bonus_ref, lnw_ref, lnb_ref, o_ref, s_ref, y_ref,
                                    tb=tb), n_state))
        _run_interleaved(stages, rounds=n_terms if write is not None else n_state)

    @pl.when(i == 0)
    def _():
        s_ref[...] = jnp.zeros_like(s_ref)
        step(slots[0], None)

    @pl.when(jnp.logical_and(interior, (i & 1) == 0))
    def _():
        step(slots[0], slots[1])

    @pl.when(jnp.logical_and(interior, (i & 1) == 1))
    def _():
        step(slots[1], slots[0])

    @pl.when(i == last)
    def _():
        step(None, slots[1])


def _scan(prep, gup, lnw, lnb, w_gate, w_up, w_down, tb):
    t = prep["rt"].shape[0]
    n_tiles = t // tb
    assert tb // CHUNK <= WL_ROWS
    last = n_tiles - 1
    cur = pl.BlockSpec((tb, RWKV_WIDTH), lambda i: (jnp.minimum(i, last), 0))
    prev = pl.BlockSpec((tb, RWKV_WIDTH), lambda i: (jnp.maximum(i - 1, 0), 0))
    vec = _full((1, RWKV_WIDTH))
    assert D_MODEL % n_tiles == 0 and (D_MODEL // n_tiles) % 16 == 0
    assert n_tiles % 2 == 0 and D_FF % (n_tiles // 2) == 0 and (D_FF // (n_tiles // 2)) % 16 == 0
    up_rows = pl.BlockSpec((D_MODEL // n_tiles, D_FF), lambda i: (jnp.minimum(i, last), 0))
    down_blk = pl.BlockSpec((D_FF // (n_tiles // 2), D_MODEL // 2),
                            lambda i: (jnp.minimum(i, last) // 2, jnp.minimum(i, last) % 2))
    n_prob = (tb // CHUNK) * N_PAIRS
    term_shapes = [pltpu.VMEM((n_prob, CHUNK, PAIR), BF16), pltpu.VMEM((n_prob, CHUNK, PAIR), F32),
                   pltpu.VMEM((n_prob, CHUNK, 2 * PAIR), BF16)]
    return pl.pallas_call(
        functools.partial(_scan_kernel, tb=tb),
        out_shape=(jax.ShapeDtypeStruct((t, RWKV_WIDTH), BF16),
                   jax.ShapeDtypeStruct((D_MODEL, D_FF), BF16),
                   jax.ShapeDtypeStruct((D_MODEL, D_FF), BF16),
                   jax.ShapeDtypeStruct((D_FF, D_MODEL), BF16)),
        grid=(n_tiles + 1,),
        in_specs=[cur] * 5 + [prev] * 5
        + [pl.BlockSpec((tb, GATE_LORA), lambda i: (jnp.maximum(i - 1, 0), 0)),
           pl.BlockSpec((1, WL_ROWS, RWKV_WIDTH), lambda i: (jnp.maximum(i - 1, 0), 0, 0)),
           _full((GATE_LORA, RWKV_WIDTH)), vec, vec, up_rows, up_rows, down_blk],
        out_specs=(prev, up_rows, up_rows, down_blk),
        scratch_shapes=[pltpu.VMEM((N_PAIRS, PAIR, PAIR), F32), pltpu.VMEM((tb, RWKV_WIDTH), F32)]
        + term_shapes * 2,
        compiler_params=_params(),
        name="scan",
    )(prep["rt"], prep["at"], prep["bh"], prep["kh"], prep["v"],
      prep["rt"], prep["v"], prep["bh"], prep["kh"], prep["bonus"], prep["gds"], prep["wl"],
      gup, lnw, lnb, w_gate, w_up, w_down)


def _sgu_block_prepare(z, lnw, lnb, sel):
    hz = _gelu_tanh(z)
    u = hz[:, :SGU_WIDTH]
    vf = hz[:, SGU_WIDTH:]
    mu = jnp.mean(vf, axis=-1, keepdims=True)
    d = vf - mu
    var = jnp.mean(d * d, axis=-1, keepdims=True)
    vn = d * lax.rsqrt(var + LN_EPS) * lnw + lnb
    stacks = []
    for p in range(SGU_WIDTH // PAIR):
        vb = vn[:, p * PAIR:(p + 1) * PAIR]
        stacks.append(jnp.where(sel, jnp.concatenate([vb, vb], axis=0), 0.0).astype(BF16))
    return u, stacks


def _sgu_block_mix(u, stacks, wcat, bias):
    return jnp.concatenate(
        [u[:, p * PAIR:(p + 1) * PAIR]
         * (jnp.dot(wcat[p], stacks[p], preferred_element_type=F32) + bias[:, p * PAIR:(p + 1) * PAIR])
         for p in range(SGU_WIDTH // PAIR)], axis=1)


def _mix_attn_group(r, x_ref, yr_ref, zs_ref, slnw_ref, slnb_ref, sbias_ref, wo1_ref, wo2_ref,
                    g2_ref, wq_ref, k_ref, v_ref, wo_ref, o_ref, wcat, sel):
    heads = [slice(hd * XA_HEAD_DIM, (hd + 1) * XA_HEAD_DIM) for hd in range(XA_HEADS)]
    prepared = [_sgu_block_prepare(zs_ref[b:b + SGU_BLOCK, :], slnw_ref[...], slnb_ref[...], sel)
                for b in range(r.start, r.stop, SGU_BLOCK)]
    yield
    x1 = x_ref[r, :] + jnp.dot(yr_ref[r, :].astype(BF16), wo1_ref[...],
                               preferred_element_type=F32)
    y_sgu = jnp.concatenate([_sgu_block_mix(u, st, wcat, sbias_ref[...]) for u, st in prepared],
                            axis=0)
    yield
    x1 = x1 + jnp.dot(y_sgu.astype(BF16), wo2_ref[...], preferred_element_type=F32)
    yield
    h = _rmsnorm(x1, g2_ref[...]).astype(BF16)
    yield
    q = jnp.dot(h, wq_ref[...], preferred_element_type=F32).astype(BF16)
    s = [lax.dot_general(q[:, hl], k_ref[:, hl], _NT, preferred_element_type=F32)
         * (XA_HEAD_DIM ** -0.5) for hl in heads]
    yield
    p = []
    for s_h in s:
        e = jnp.exp(s_h - jnp.max(s_h, axis=-1, keepdims=True))
        p.append((e / jnp.sum(e, axis=-1, keepdims=True)).astype(BF16))
    yield
    o = jnp.concatenate([jnp.dot(p_h, v_ref[:, hl], preferred_element_type=F32)
                         for p_h, hl in zip(p, heads)], axis=1).astype(BF16)
    o_ref[r, :] = x1 + jnp.dot(o, wo_ref[...], preferred_element_type=F32)


def _mix_attn_kernel(x_ref, yr_ref, zs_ref, slnw_ref, slnb_ref, ws_ref, sbias_ref, wout_f32_ref,
                     g2_ref, wq_f32_ref, k_ref, v_ref, wo_f32_ref, o_ref, wout_ref, wq_ref, wo_ref):
    @pl.when(pl.program_id(0) == 0)
    def _():
        wout_ref[...] = wout_f32_ref[...].astype(BF16)
        wq_ref[...] = wq_f32_ref[...].astype(BF16)
        wo_ref[...] = wo_f32_ref[...].astype(BF16)

    wo1_ref = wout_ref.at[:RWKV_WIDTH]
    wo2_ref = wout_ref.at[RWKV_WIDTH:]
    tm = x_ref.shape[0]
    ti = lax.broadcasted_iota(jnp.int32, (SGU_BLOCK, SGU_BLOCK), 0)
    tj = lax.broadcasted_iota(jnp.int32, (SGU_BLOCK, SGU_BLOCK), 1)
    tril = tj <= ti
    wcat = [jnp.concatenate([jnp.where(tril, ws_ref[2 * p], 0.0),
                             jnp.where(tril, ws_ref[2 * p + 1], 0.0)], axis=1).astype(BF16)
            for p in range(SGU_WIDTH // PAIR)]
    bi = lax.broadcasted_iota(jnp.int32, (2 * SGU_BLOCK, PAIR), 0) >> 7
    bj = lax.broadcasted_iota(jnp.int32, (2 * SGU_BLOCK, PAIR), 1) >> 6
    sel = bi == bj
    gens = [_mix_attn_group(slice(r, r + ATTN_ROW_GROUP), x_ref, yr_ref, zs_ref, slnw_ref, slnb_ref,
                            sbias_ref, wo1_ref, wo2_ref, g2_ref, wq_ref, k_ref, v_ref, wo_ref,
                            o_ref, wcat, sel)
            for r in range(0, tm, ATTN_ROW_GROUP)]
    _run_wavefront(gens, ATTN_STAGGER)


def _mix_attn(x, yr, zs, slnw, slnb, ws, sbias, w_out, g2, wq, k, v, wo, tm):
    t = x.shape[0]
    sq = _full((D_MODEL, D_MODEL))
    return pl.pallas_call(
        _mix_attn_kernel,
        out_shape=jax.ShapeDtypeStruct((t, D_MODEL), F32),
        grid=(t // tm,),
        in_specs=[pl.BlockSpec((tm, D_MODEL), lambda i: (i, 0)),
                  pl.BlockSpec((tm, RWKV_WIDTH), lambda i: (i, 0)),
                  pl.BlockSpec((tm, 2 * SGU_WIDTH), lambda i: (i, 0)),
                  _full((1, SGU_WIDTH)), _full((1, SGU_WIDTH)),
                  _full((SGU_GROUPS, SGU_BLOCK, SGU_BLOCK)), _full((SGU_BLOCK, SGU_WIDTH)),
                  sq, _full((1, D_MODEL)), sq,
                  _full((MEM_LEN, D_MODEL)), _full((MEM_LEN, D_MODEL)), sq],
        out_specs=pl.BlockSpec((tm, D_MODEL), lambda i: (i, 0)),
        scratch_shapes=[pltpu.VMEM((D_MODEL, D_MODEL), BF16)] * 3,
        compiler_params=_params(),
        name="mix_attn",
    )(x, yr, zs, slnw, slnb, ws, sbias, w_out, g2, wq, k, v, wo)


def _ffn_kernel(x_ref, g3_ref, wg_ref, wu_ref, wd_ref, gf_ref, o_ref):
    tm = x_ref.shape[0]
    groups = [slice(r, r + FFN_ROW_GROUP) for r in range(0, tm, FFN_ROW_GROUP)]
    x2 = [x_ref[r, :] for r in groups]
    h = [_rmsnorm(x, g3_ref[...]).astype(BF16) for x in x2]
    gate = [jnp.dot(h_, wg_ref[...], preferred_element_type=F32) for h_ in h]
    up = [jnp.dot(h_, wu_ref[...], preferred_element_type=F32) for h_ in h]
    act = [(jax.nn.silu(g_) * u_).astype(BF16) for g_, u_ in zip(gate, up)]
    x3 = [x + jnp.dot(a_, wd_ref[...], preferred_element_type=F32) for x, a_ in zip(x2, act)]
    for r, x in zip(groups, x3):
        o_ref[r, :] = _rmsnorm(x, gf_ref[...])


def _ffn(x, g3, wg, wu, wd, gf, tm):
    t = x.shape[0]
    return pl.pallas_call(
        _ffn_kernel,
        out_shape=jax.ShapeDtypeStruct((t, D_MODEL), F32),
        grid=(t // tm,),
        in_specs=[pl.BlockSpec((tm, D_MODEL), lambda i: (i, 0)), _full((1, D_MODEL)),
                  _full((D_MODEL, D_FF)), _full((D_MODEL, D_FF)), _full((D_FF, D_MODEL)),
                  _full((1, D_MODEL))],
        out_specs=pl.BlockSpec((tm, D_MODEL), lambda i: (i, 0)),
        compiler_params=_params("parallel"),
        name="ffn",
    )(x, g3, wg, wu, wd, gf)


def kernel(x, mem, norm1_g, w_in, shift_mu, w0, w_lora_up, a0, a_lora_up, g_lora_up, k_k, k_a, r_k,
           lnx_w, lnx_b, sgu_ln_w, sgu_ln_b, w_spatial, b_spatial, w_out, norm2_g, mem_norm_g,
           wq_x, wk_x, wv_x, wo_x, norm3_g, w_gate, w_up, w_down, norm_f_g):
    b, t, _ = x.shape
    depth = w_in.shape[0]
    assert depth == 1, "the final RMSNorm is fused into the (single) layer's ffn call"
    assert t % TM_ATTN == 0 and t % TM_FFN == 0
    assert t % TM_DENSE == 0 and TM_DENSE % TB_SCAN == 0 and TB_SCAN % CHUNK == 0
    row = lambda p: p.reshape(1, -1)
    bf = lambda p: p.astype(BF16)
    outs = []
    for bi in range(b):
        xb = x[bi]
        for l in range(depth):
            lora = w_lora_up.shape[1]
            zeros = jnp.zeros((lora, RWKV_WIDTH), F32)
            waup = jnp.concatenate(
                [jnp.concatenate([w_lora_up[l], zeros], axis=1),
                 jnp.concatenate([zeros, a_lora_up[l]], axis=1)], axis=0)
            bias = jnp.repeat(b_spatial[l].T, SGU_WIDTH // SGU_GROUPS, axis=1)

            front = _front(xb, row(norm1_g[l]), w_in[l], row(shift_mu[l]), row(w0[l]), bf(waup),
                           row(a0[l]), row(k_k[l]), row(k_a[l]), row(r_k[l]), TM_DENSE)
            prep = dict(zip(_PREP_NAMES, front[:len(_PREP_NAMES)]))
            z_sgu = front[len(_PREP_NAMES)]
            y_rwkv, wg_bf, wu_bf, wd_bf = _scan(prep, bf(g_lora_up[l]), row(lnx_w[l]), row(lnx_b[l]),
                                                w_gate[l], w_up[l], w_down[l], TB_SCAN)
            k_mem, v_mem = _mem_kv(mem[bi], row(mem_norm_g[l]), wk_x[l], wv_x[l])
            x2 = _mix_attn(xb, y_rwkv, z_sgu, row(sgu_ln_w[l]), row(sgu_ln_b[l]), w_spatial[l], bias,
                           w_out[l], row(norm2_g[l]), wq_x[l], k_mem, v_mem, wo_x[l], TM_ATTN)
            xb = _ffn(x2, row(norm3_g[l]), wg_bf, wu_bf, wd_bf, row(norm_f_g), TM_FFN)
        outs.append(xb)
    return jnp.stack(outs, axis=0)
```
